```python
import math
import jax
import jax.numpy as jnp
from jax import lax
import numpy as np

D_MODEL = 1024
BATCH = 2
SEQ = 8192
DEPTH = 1

N_HEADS = 8
HEAD_DIM = 64
ATTN_WIDTH = N_HEADS * HEAD_DIM
MOBA_BLOCK = 256
MOBA_TOPK = 3
MOBA_Q_CHUNK = 64
POOL_WINDOWS = (2, 4, 8, 16)
POOL_GROUPS = len(POOL_WINDOWS)
POOL_WIDTH = D_MODEL // 2
POOL_GROUP_WIDTH = POOL_WIDTH // POOL_GROUPS
N_BRANCHES = 2
IN_WIDTH = 3 * ATTN_WIDTH + POOL_WIDTH + N_BRANCHES * D_MODEL
REL_BUCKETS = 32
REL_MAX_DIST = 128
N_EXPERT_GROUPS = 4
EXPERTS_PER_GROUP = 8
N_EXPERTS = N_EXPERT_GROUPS * EXPERTS_PER_GROUP
EXPERT_TOPK = 2
D_EXPERT = D_MODEL // 2
N_MOD = 6
EPS = 1e-6
NEG_INF = -1e30

kernel_name = "hybrid_moba_pool_hmoe_adaln_block"


def rms_norm(x, g):
    xf = x.astype(jnp.float32)
    xf = xf * lax.rsqrt(jnp.mean(xf * xf, axis=-1, keepdims=True) + EPS)
    return (xf * g.astype(jnp.float32)).astype(x.dtype)


def t5_bucket(rel):
    n = jnp.maximum(rel, 0)
    max_exact = REL_BUCKETS // 2
    nf = jnp.maximum(n, 1).astype(jnp.float32)
    large = max_exact + (jnp.log(nf / max_exact) / math.log(REL_MAX_DIST / max_exact)
                         * (REL_BUCKETS - max_exact)).astype(jnp.int32)
    large = jnp.minimum(large, REL_BUCKETS - 1)
    return jnp.where(n < max_exact, n, large)


def moba_attention(q, k, v, rel_bias):
    B, S, H, Dh = q.shape
    nb = -(-S // MOBA_BLOCK)
    pad = nb * MOBA_BLOCK - S
    topk = min(MOBA_TOPK, nb)
    scale = Dh ** -0.5
    qh = q.transpose(0, 2, 1, 3)
    kp = jnp.pad(k.transpose(0, 2, 1, 3), ((0, 0), (0, 0), (0, pad), (0, 0)))
    vp = jnp.pad(v.transpose(0, 2, 1, 3), ((0, 0), (0, 0), (0, pad), (0, 0)))
    kb = kp.reshape(B, H, nb, MOBA_BLOCK, Dh)
    vb = vp.reshape(B, H, nb, MOBA_BLOCK, Dh)
    k_mean = jnp.mean(kb.astype(jnp.float32), axis=3)
    bias_t = rel_bias.T
    b_idx = jnp.arange(B)[:, None, None, None]
    h_idx = jnp.arange(H)[None, :, None, None]
    h_idx5 = jnp.arange(H)[None, :, None, None, None]
    blk_offsets = jnp.arange(MOBA_BLOCK)

    def attend_chunk(ci):
        q0 = ci * MOBA_Q_CHUNK
        blk = q0 // MOBA_BLOCK
        qc = lax.dynamic_slice_in_dim(qh, q0, MOBA_Q_CHUNK, axis=2)
        qpos = q0 + jnp.arange(MOBA_Q_CHUNK)
        gate = jnp.einsum('bhqd,bhnd->bhqn', qc.astype(jnp.float32), k_mean)
        gate = jnp.where(jnp.arange(nb) < blk, gate, -jnp.inf)
        _, sel = lax.top_k(gate, topk)
        sel_valid = sel < blk
        ks = kb[b_idx, h_idx, sel]
        vs = vb[b_idx, h_idx, sel]
        s_sel = jnp.einsum('bhqd,bhqjkd->bhqjk', qc, ks).astype(jnp.float32) * scale
        kpos_sel = sel[..., None] * MOBA_BLOCK + blk_offsets
        bucket_sel = t5_bucket(qpos[None, None, :, None, None] - kpos_sel)
        s_sel = s_sel + bias_t[h_idx5, bucket_sel].astype(jnp.float32)
        s_sel = jnp.where(sel_valid[..., None], s_sel, NEG_INF)
        ko = lax.dynamic_index_in_dim(kb, blk, axis=2, keepdims=False)
        vo = lax.dynamic_index_in_dim(vb, blk, axis=2, keepdims=False)
        s_own = jnp.einsum('bhqd,bhkd->bhqk', qc, ko).astype(jnp.float32) * scale
        rel_own = qpos[:, None] - (blk * MOBA_BLOCK + blk_offsets)[None, :]
        s_own = s_own + bias_t[:, t5_bucket(rel_own)][None].astype(jnp.float32)
        s_own = jnp.where(rel_own >= 0, s_own, NEG_INF)
        logits = jnp.concatenate(
            [s_sel.reshape(B, H, MOBA_Q_CHUNK, topk * MOBA_BLOCK), s_own], axis=-1)
        p = jax.nn.softmax(logits, axis=-1).astype(v.dtype)
        p_sel = p[..., :topk * MOBA_BLOCK].reshape(B, H, MOBA_Q_CHUNK, topk, MOBA_BLOCK)
        p_own = p[..., topk * MOBA_BLOCK:]
        return (jnp.einsum('bhqjk,bhqjkd->bhqd', p_sel, vs)
                + jnp.einsum('bhqk,bhkd->bhqd', p_own, vo))

    out = lax.map(attend_chunk, jnp.arange(S // MOBA_Q_CHUNK))
    return out.transpose(1, 0, 3, 2, 4).reshape(B, S, H * Dh)


def pool_mixer(u, pool_w, pool_scale):
    B, S, P = u.shape
    uf = u.astype(jnp.float32)
    cs = jnp.pad(jnp.cumsum(uf, axis=1), ((0, 0), (1, 0), (0, 0)))
    t = jnp.arange(S)
    outs = []
    for g, w in enumerate(POOL_WINDOWS):
        lo_c, hi_c = g * POOL_GROUP_WIDTH, (g + 1) * POOL_GROUP_WIDTH
        csg = cs[..., lo_c:hi_c]
        lo = jnp.maximum(t + 1 - w, 0)
        cnt = (t + 1 - lo).astype(jnp.float32)
        mean = (csg[:, 1:] - csg[:, lo]) / cnt[None, :, None]
        d = (mean - uf[..., lo_c:hi_c]).astype(u.dtype)
        outs.append(jnp.einsum('bsc,cd->bsd', d, pool_w[g]))
    return jnp.concatenate(outs, axis=-1) * pool_scale


def hier_moe(h, w_rg, b_rg, w_re, b_re, w_g, w_u, w_d):
    B, S, D = h.shape
    N = B * S
    hf = h.reshape(N, D)
    z_group = (hf @ w_rg).astype(jnp.float32) + b_rg.astype(jnp.float32)
    p_group = jax.nn.softmax(z_group, axis=-1)
    pg_top, g_idx = lax.top_k(p_group, 1)
    z_exp = ((hf @ w_re).astype(jnp.float32) + b_re.astype(jnp.float32)).reshape(
        N, N_EXPERT_GROUPS, EXPERTS_PER_GROUP)
    z_in_group = z_exp[jnp.arange(N), g_idx[:, 0]]
    ze_top, e_idx = lax.top_k(z_in_group, EXPERT_TOPK)
    weights = pg_top * jax.nn.softmax(ze_top, axis=-1)
    expert_id = g_idx * EXPERTS_PER_GROUP + e_idx
    combine = jnp.einsum('nk,nke->ne', weights,
                         jax.nn.one_hot(expert_id, N_EXPERTS, dtype=jnp.float32)).astype(h.dtype)
    y = jnp.zeros_like(hf)
    for e in range(N_EXPERTS):
        act = jax.nn.silu(hf @ w_g[e]) * (hf @ w_u[e])
        y = y + combine[:, e:e + 1] * (act @ w_d[e])
    return y.reshape(B, S, D)


def setup_inputs(seed: int = 0) -> dict:
    key = jax.random.key(seed)
    ks = jax.random.split(key, 22)
    f32 = jnp.float32
    D = D_MODEL
    PGW = POOL_GROUP_WIDTH

    def nrm(k, shape, s):
        return jax.random.normal(k, shape, f32) * s

    return {
        "x": nrm(ks[0], (BATCH, SEQ, D), 1.0),
        "c": nrm(ks[1], (BATCH, D), 1.0),
        "w_ada": nrm(ks[2], (DEPTH, D, N_MOD * D), 0.5 * D ** -0.5),
        "b_ada": nrm(ks[3], (DEPTH, N_MOD * D), 0.02),
        "norm1_g": 1.0 + nrm(ks[4], (DEPTH, D), 0.05),
        "w_in": nrm(ks[5], (DEPTH, D, IN_WIDTH), D ** -0.5),
        "b_gate": nrm(ks[6], (DEPTH, N_BRANCHES * D), 0.02),
        "rel_bias": nrm(ks[7], (REL_BUCKETS, N_HEADS), 0.5),
        "pool_w": nrm(ks[8], (DEPTH, POOL_GROUPS, PGW, PGW), PGW ** -0.5),
        "pool_scale": 1.0 + nrm(ks[9], (DEPTH, POOL_WIDTH), 0.1),
        "w_branch_attn": nrm(ks[10], (DEPTH, ATTN_WIDTH, D), ATTN_WIDTH ** -0.5),
        "w_branch_pool": nrm(ks[11], (DEPTH, POOL_WIDTH, D), POOL_WIDTH ** -0.5),
        "w_out": nrm(ks[12], (DEPTH, D, D), D ** -0.5),
        "norm2_g": 1.0 + nrm(ks[13], (DEPTH, D), 0.05),
        "w_router_group": nrm(ks[14], (DEPTH, D, N_EXPERT_GROUPS), D ** -0.5),
        "b_router_group": nrm(ks[15], (DEPTH, N_EXPERT_GROUPS), 0.01),
        "w_router_expert": nrm(ks[16], (DEPTH, D, N_EXPERTS), D ** -0.5),
        "b_router_expert": nrm(ks[17], (DEPTH, N_EXPERTS), 0.01),
        "w_expert_gate": nrm(ks[18], (DEPTH, N_EXPERTS, D, D_EXPERT), D ** -0.5),
        "w_expert_up": nrm(ks[19], (DEPTH, N_EXPERTS, D, D_EXPERT), D ** -0.5),
        "w_expert_down": nrm(ks[20], (DEPTH, N_EXPERTS, D_EXPERT, D), D_EXPERT ** -0.5),
        "norm_f_g": 1.0 + nrm(ks[21], (D,), 0.05),
    }


def reference(x, c, w_ada, b_ada, norm1_g, w_in, b_gate, rel_bias, pool_w, pool_scale,
              w_branch_attn, w_branch_pool, w_out, norm2_g, w_router_group, b_router_group,
              w_router_expert, b_router_expert, w_expert_gate, w_expert_up, w_expert_down,
              norm_f_g):
    B, S, D = x.shape
    c_act = jax.nn.silu(c)
    split_pts = [ATTN_WIDTH, 2 * ATTN_WIDTH, 3 * ATTN_WIDTH, 3 * ATTN_WIDTH + POOL_WIDTH]
    for l in range(DEPTH):
        mod = (c_act @ w_ada[l] + b_ada[l])[:, None, :]
        shift1, scale1, gate1, shift2, scale2, gate2 = jnp.split(mod, N_MOD, axis=-1)
        h = rms_norm(x, norm1_g[l]) * (1 + scale1) + shift1
        proj = h @ w_in[l]
        q, k, v, u, z_gate = jnp.split(proj, split_pts, axis=-1)
        q = q.reshape(B, S, N_HEADS, HEAD_DIM)
        k = k.reshape(B, S, N_HEADS, HEAD_DIM)
        v = v.reshape(B, S, N_HEADS, HEAD_DIM)
        y_attn = moba_attention(q, k, v, rel_bias) @ w_branch_attn[l]
        y_pool = pool_mixer(u, pool_w[l], pool_scale[l]) @ w_branch_pool[l]
        g_attn, g_pool = jnp.split(jax.nn.sigmoid(z_gate + b_gate[l]), N_BRANCHES, axis=-1)
        mixed = (g_attn * y_attn + g_pool * y_pool) @ w_out[l]
        x = x + gate1 * mixed
        h2 = rms_norm(x, norm2_g[l]) * (1 + scale2) + shift2
        y_moe = hier_moe(h2, w_router_group[l], b_router_group[l], w_router_expert[l],
                         b_router_expert[l], w_expert_gate[l], w_expert_up[l], w_expert_down[l])
        x = x + gate2 * y_moe
    return rms_norm(x, norm_f_g)
```

```python
import functools
import math

import numpy as np
import jax
import jax.numpy as jnp
from jax import lax
from jax.experimental import pallas as pl
from jax.experimental.pallas import tpu as pltpu

F32 = jnp.float32
BF16 = jnp.bfloat16

D_MODEL = 1024
N_HEADS = 8
HEAD_DIM = 64
ATTN_WIDTH = N_HEADS * HEAD_DIM
MOBA_BLOCK = 256
MOBA_TOPK = 3
MAX_BLOCKS = 32
POOL_WINDOWS = (2, 4, 8, 16)
POOL_WIDTH = 512
POOL_GROUP_WIDTH = 128
POOL_HALO = 16
REL_BUCKETS = 32
REL_MAX_DIST = 128
N_GROUPS = 4
EXPERTS_PER_GROUP = 8
N_EXPERTS = N_GROUPS * EXPERTS_PER_GROUP
D_EXPERT = 512
N_MOD = 6
EPS = 1e-6
NEG_INF = -1e30
LANES = 128
ROUTER_LANE0 = N_GROUPS
VMEM_LIMIT = 56 * 1024 * 1024

TM_PROJ = 512
TM_POST = 512
TM_MOE = 1024
TM_FINAL = 512


def _cparams(*sem):
    return pltpu.CompilerParams(dimension_semantics=sem, vmem_limit_bytes=VMEM_LIMIT)


def _dot(a, b):
    return jnp.dot(a, b, preferred_element_type=F32)


def _rms_mod(x, g, scale, shift):
    xn = x * lax.rsqrt(jnp.mean(x * x, axis=-1, keepdims=True) + EPS)
    return (xn * g) * (1.0 + scale) + shift


def _mod_kernel(c_ref, w_ref, b_ref, o_ref):
    c = c_ref[...]
    ca = c * jax.nn.sigmoid(c)
    o_ref[...] = _dot(ca.astype(BF16), w_ref[...].astype(BF16)) + b_ref[...]


def _modulation(c_pad, w_ada, b_ada):
    rows, d = c_pad.shape
    n_out = w_ada.shape[1]
    tn = 1024
    return pl.pallas_call(
        _mod_kernel,
        grid=(n_out // tn,),
        in_specs=[pl.BlockSpec((rows, d), lambda j: (0, 0)),
                  pl.BlockSpec((d, tn), lambda j: (0, j)),
                  pl.BlockSpec((1, tn), lambda j: (0, j))],
        out_specs=pl.BlockSpec((rows, tn), lambda j: (0, j)),
        out_shape=jax.ShapeDtypeStruct((rows, n_out), F32),
        compiler_params=_cparams("arbitrary"),
        name="adaln_mod",
    )(c_pad, w_ada, b_ada)


def _inproj_kernel(x_ref, mod_ref, g_ref, w_ref, qT_ref, k_ref, vT_ref, u_ref, km_ref, *, tm):
    d = D_MODEL
    mod = mod_ref[0]
    h = _rms_mod(x_ref[...], g_ref[...], mod[:, d:2 * d], mod[:, 0:d])
    proj = _dot(h.astype(BF16), w_ref[...])
    aw = ATTN_WIDTH
    q = proj[:, 0:aw] * (HEAD_DIM ** -0.5)
    k = proj[:, aw:2 * aw]
    v = proj[:, 2 * aw:3 * aw]
    u_ref[...] = proj[:, 3 * aw:3 * aw + POOL_WIDTH]
    k_ref[...] = k.astype(BF16)
    qT = q.T.astype(BF16)
    vT = v.T.astype(BF16)
    for c in range(tm // MOBA_BLOCK):
        sl = slice(c * MOBA_BLOCK, (c + 1) * MOBA_BLOCK)
        qT_ref[0, c] = qT[:, sl]
        vT_ref[0, c] = vT[:, sl]
        km_ref[0, c:c + 1, :] = jnp.mean(k[sl, :], axis=0, keepdims=True)


def _in_projection(x2, mod3, g1, w_qkvu, *, batch, seq):
    n, d = x2.shape
    tm = TM_PROJ
    tps = seq // tm
    nb = seq // MOBA_BLOCK
    cb = tm // MOBA_BLOCK
    wcols = w_qkvu.shape[1]
    kern = functools.partial(_inproj_kernel, tm=tm)
    t_shape = jax.ShapeDtypeStruct((batch, nb, ATTN_WIDTH, MOBA_BLOCK), BF16)
    t_spec = pl.BlockSpec((1, cb, ATTN_WIDTH, MOBA_BLOCK), lambda i: (i // tps, i % tps, 0, 0))
    return pl.pallas_call(
        kern,
        grid=(n // tm,),
        in_specs=[pl.BlockSpec((tm, d), lambda i: (i, 0)),
                  pl.BlockSpec((1, 1, N_MOD * d), lambda i: (i // tps, 0, 0)),
                  pl.BlockSpec((1, d), lambda i: (0, 0)),
                  pl.BlockSpec((d, wcols), lambda i: (0, 0))],
        out_specs=[t_spec,
                   pl.BlockSpec((tm, ATTN_WIDTH), lambda i: (i, 0)),
                   t_spec,
                   pl.BlockSpec((tm, POOL_WIDTH), lambda i: (i, 0)),
                   pl.BlockSpec((1, cb, ATTN_WIDTH), lambda i: (i, 0, 0))],
        out_shape=[t_shape,
                   jax.ShapeDtypeStruct((n, ATTN_WIDTH), BF16),
                   t_shape,
                   jax.ShapeDtypeStruct((n, POOL_WIDTH), F32),
                   jax.ShapeDtypeStruct((n // tm, cb, ATTN_WIDTH), F32)],
        compiler_params=_cparams("arbitrary"),
        name="norm1_inproj",
    )(x2, mod3, g1, w_qkvu)


def _attn_kernel(qT_ref, k_ref, vT_ref, km_ref, bown_ref, bprev_ref, o_ref, kaug_ref, *, nb):
    blk = MOBA_BLOCK
    hd = HEAD_DIM
    lane = lax.broadcasted_iota(jnp.int32, (blk, LANES), 1)

    def build_kaug(j, carry):
        rows = pl.ds(pl.multiple_of(j * blk, blk), blk)
        kp = k_ref[0, rows, :]
        kaug_ref[0, rows, :] = jnp.where(lane < hd, kp, jnp.where(lane == hd + j, 1.0, 0.0).astype(BF16))
        kaug_ref[1, rows, :] = jnp.where(lane >= hd, kp, jnp.where(lane == j, 1.0, 0.0).astype(BF16))
        return carry

    lax.fori_loop(0, nb, build_kaug, 0)

    rowi = lax.broadcasted_iota(jnp.int32, (MAX_BLOCKS, blk), 0)
    km_lane = lax.broadcasted_iota(jnp.int32, (MAX_BLOCKS, LANES), 1)
    zpad = jnp.zeros((LANES - hd - MAX_BLOCKS, blk), BF16)

    for hh in range(2):
        in_head = (km_lane >= hh * hd) & (km_lane < (hh + 1) * hd)
        km_h = jnp.where(in_head, km_ref[0], 0.0)
        km_hi = km_h.astype(BF16)
        km_lo = (km_h - km_hi.astype(F32)).astype(BF16)
        bown = bown_ref.at[hh]
        bprev = bprev_ref.at[hh]

        def kblock(j):
            rows = pl.ds(pl.multiple_of(j * blk, blk), blk)
            return kaug_ref[hh, rows, :], vT_ref[0, j, hh * hd:(hh + 1) * hd, :]

        def qtile(i, carry, hh=hh, km_hi=km_hi, km_lo=km_lo, bown=bown, bprev=bprev, kblock=kblock):
            qp = qT_ref[0, i]
            gate = _dot(km_hi, qp) + _dot(km_lo, qp)
            g = jnp.where(rowi < i, gate, -jnp.inf)
            sel = rowi == i
            for _ in range(MOBA_TOPK):
                m = jnp.max(g, axis=0, keepdims=True)
                hit = (g == m) & (m > -jnp.inf)
                idx = jnp.min(jnp.where(hit, rowi, MAX_BLOCKS), axis=0, keepdims=True)
                pick = rowi == idx
                sel = sel | pick
                g = jnp.where(pick, -jnp.inf, g)
            maskT = jnp.where(sel, 0.0, NEG_INF).astype(BF16)
            qh = qp[hh * hd:(hh + 1) * hd, :]
            if hh == 0:
                qaug = jnp.concatenate([qh, maskT, zpad], axis=0)
            else:
                qaug = jnp.concatenate([maskT, zpad, qh], axis=0)

            ka, va = kblock(i)
            s = _dot(ka, qaug) + bown[...]
            m0 = jnp.max(s, axis=0, keepdims=True)
            p = jnp.exp(s - m0)
            l0 = jnp.sum(p, axis=0, keepdims=True)
            acc0 = _dot(va, p.astype(BF16))

            def online(s, m, l, acc, vb):
                m_new = jnp.maximum(m, jnp.max(s, axis=0, keepdims=True))
                alpha = jnp.exp(m - m_new)
                p = jnp.exp(s - m_new)
                l = alpha * l + jnp.sum(p, axis=0, keepdims=True)
                acc = alpha * acc + _dot(vb, p.astype(BF16))
                return m_new, l, acc

            def with_prev(args):
                m, l, acc = args
                kb, vb = kblock(i - 1)
                return online(_dot(kb, qaug) + bprev[...], m, l, acc, vb)

            m1, l1, acc1 = lax.cond(i >= 1, with_prev, lambda a: a, (m0, l0, acc0))

            def far(j, args):
                m, l, acc = args
                kb, vb = kblock(j)
                return online(_dot(kb, qaug), m, l, acc, vb)

            m2, l2, acc2 = lax.fori_loop(0, jnp.maximum(i - 1, 0), far, (m1, l1, acc1))
            o_ref[0, i, hh * hd:(hh + 1) * hd, :] = (acc2 / l2).astype(BF16)
            return carry

        lax.fori_loop(0, nb, qtile, 0)


def _rel_bucket_table(max_rel):
    n = np.arange(max_rel)
    max_exact = REL_BUCKETS // 2
    nf = np.maximum(n, 1).astype(np.float32)
    large = max_exact + (np.log(nf / np.float32(max_exact)) / np.float32(math.log(REL_MAX_DIST / max_exact))
                         * np.float32(REL_BUCKETS - max_exact)).astype(np.int32)
    large = np.minimum(large, REL_BUCKETS - 1)
    return np.where(n < max_exact, n, large)


def _bias_tiles(rel_bias):
    blk = MOBA_BLOCK
    table = _rel_bucket_table(2 * blk)
    ki = np.arange(blk)[:, None]
    qi = np.arange(blk)[None, :]
    rel_own = qi - ki
    bt = rel_bias.T - rel_bias[REL_BUCKETS - 1][:, None]
    own = bt[:, table[np.maximum(rel_own, 0)]]
    own = jnp.where(jnp.asarray(rel_own >= 0)[None], own, NEG_INF)
    prev = bt[:, table[rel_own + blk]]
    return own.astype(F32), prev.astype(F32)


def _moba_attention(qT, k3, vT, km_pad, bias_own, bias_prev, *, batch, seq):
    nb = seq // MOBA_BLOCK
    blk = MOBA_BLOCK
    kern = functools.partial(_attn_kernel, nb=nb)
    t_spec = pl.BlockSpec((1, nb, LANES, blk), lambda b, p: (b, 0, p, 0))
    b_spec = pl.BlockSpec((2, blk, blk), lambda b, p: (p, 0, 0))
    return pl.pallas_call(
        kern,
        grid=(batch, N_HEADS // 2),
        in_specs=[t_spec,
                  pl.BlockSpec((1, seq, LANES), lambda b, p: (b, 0, p)),
                  t_spec,
                  pl.BlockSpec((1, MAX_BLOCKS, LANES), lambda b, p: (b, 0, p)),
                  b_spec, b_spec],
        out_specs=t_spec,
        out_shape=jax.ShapeDtypeStruct((batch, nb, ATTN_WIDTH, blk), BF16),
        scratch_shapes=[pltpu.VMEM((2, seq, LANES), BF16)],
        compiler_params=_cparams("arbitrary", "arbitrary"),
        name="moba_attention",
    )(qT, k3, vT, km_pad, bias_own, bias_prev)


def _post_kernel(x_ref, mod_ref, g1_ref, wz_ref, bz_ref, at_ref, u_ref, uh_ref, pw_ref, ps_ref,
                 wba_ref, wbp_ref, wo_ref, g2_ref, wrh_ref, wrl_ref, br_ref,
                 x1_ref, h2_ref, comb_ref, *, tm, seq):
    d = D_MODEL
    i = pl.program_id(0)
    x = x_ref[...]
    mod = mod_ref[0]
    shift1, scale1, gate1 = mod[:, 0:d], mod[:, d:2 * d], mod[:, 2 * d:3 * d]
    shift2, scale2 = mod[:, 3 * d:4 * d], mod[:, 4 * d:5 * d]

    h = _rms_mod(x, g1_ref[...], scale1, shift1)
    zg = jax.nn.sigmoid(_dot(h.astype(BF16), wz_ref[...]) + bz_ref[...])

    ya = jnp.concatenate(
        [lax.dot_general(at_ref[0, c], wba_ref[...], (((0,), (0,)), ((), ())),
                         preferred_element_type=F32) for c in range(tm // MOBA_BLOCK)], axis=0)

    first = (i * tm) % seq == 0
    halo = jnp.where(first, 0.0, uh_ref[...])
    ubuf = jnp.concatenate([halo, u_ref[...]], axis=0)
    rows = tm + POOL_HALO
    pos = (lax.broadcasted_iota(jnp.int32, (tm, POOL_GROUP_WIDTH), 0) + i * tm) % seq
    outs = []
    for gi, w in enumerate(POOL_WINDOWS):
        e = ubuf[:, gi * POOL_GROUP_WIDTH:(gi + 1) * POOL_GROUP_WIDTH]
        s = e
        step = 1
        while step < w:
            s = s + pltpu.roll(s, step, 0)
            step *= 2
        cnt = jnp.minimum(pos + 1, w).astype(F32)
        dlt = s[POOL_HALO:rows] / cnt - e[POOL_HALO:rows]
        outs.append(_dot(dlt.astype(BF16), pw_ref[gi]))
    pooled = jnp.concatenate(outs, axis=-1) * ps_ref[...]
    yp = _dot(pooled.astype(BF16), wbp_ref[...])

    merged = zg[:, 0:d] * ya + zg[:, d:2 * d] * yp
    x1 = x + gate1 * _dot(merged.astype(BF16), wo_ref[...])
    x1_ref[...] = x1

    h2 = _rms_mod(x1, g2_ref[...], scale2, shift2)
    h2_hi = h2.astype(BF16)
    h2_ref[...] = h2_hi
    h2_lo = (h2 - h2_hi.astype(F32)).astype(BF16)
    z = _dot(h2_hi, wrh_ref[...]) + _dot(h2_lo, wrh_ref[...]) + _dot(h2_hi, wrl_ref[...]) + br_ref[...]

    lane = lax.broadcasted_iota(jnp.int32, (tm, LANES), 1)
    zgrp = jnp.where(lane < N_GROUPS, z, -jnp.inf)
    mg = jnp.max(zgrp, axis=-1, keepdims=True)
    pg_top = 1.0 / jnp.sum(jnp.exp(zgrp - mg), axis=-1, keepdims=True)
    g_idx = jnp.min(jnp.where(zgrp == mg, lane, LANES), axis=-1, keepdims=True)
    e_lane = lane - ROUTER_LANE0
    in_grp = (e_lane >= 0) & (e_lane < N_EXPERTS) & ((e_lane >> 3) == g_idx)
    ze = jnp.where(in_grp, z, -jnp.inf)
    m1 = jnp.max(ze, axis=-1, keepdims=True)
    i1 = jnp.min(jnp.where(ze == m1, lane, LANES), axis=-1, keepdims=True)
    ze2 = jnp.where(lane == i1, -jnp.inf, ze)
    m2 = jnp.max(ze2, axis=-1, keepdims=True)
    i2 = jnp.min(jnp.where(ze2 == m2, lane, LANES), axis=-1, keepdims=True)
    e2 = jnp.exp(m2 - m1)
    w1 = pg_top / (1.0 + e2)
    w2 = pg_top * e2 / (1.0 + e2)
    comb_ref[...] = jnp.where(lane == i1, w1, 0.0) + jnp.where(lane == i2, w2, 0.0)


def _post_attention(x2, mod3, g1, w_z, b_z, attnT, u, pool_w, pool_scale, w_ba, w_bp, w_out, g2,
                    wr_hi, wr_lo, b_r, *, batch, seq):
    n, d = x2.shape
    tm = TM_POST
    tps = seq // tm
    cb = tm // MOBA_BLOCK
    hb = tm // POOL_HALO
    kern = functools.partial(_post_kernel, tm=tm, seq=seq)
    const2 = lambda i: (0, 0)
    return pl.pallas_call(
        kern,
        grid=(n // tm,),
        in_specs=[pl.BlockSpec((tm, d), lambda i: (i, 0)),
                  pl.BlockSpec((1, 1, N_MOD * d), lambda i: (i // tps, 0, 0)),
                  pl.BlockSpec((1, d), const2),
                  pl.BlockSpec((d, 2 * d), const2),
                  pl.BlockSpec((1, 2 * d), const2),
                  pl.BlockSpec((1, cb, ATTN_WIDTH, MOBA_BLOCK), lambda i: (i // tps, i % tps, 0, 0)),
                  pl.BlockSpec((tm, POOL_WIDTH), lambda i: (i, 0)),
                  pl.BlockSpec((POOL_HALO, POOL_WIDTH), lambda i: (jnp.maximum(i * hb - 1, 0), 0)),
                  pl.BlockSpec((len(POOL_WINDOWS), POOL_GROUP_WIDTH, POOL_GROUP_WIDTH), lambda i: (0, 0, 0)),
                  pl.BlockSpec((1, POOL_WIDTH), const2),
                  pl.BlockSpec((ATTN_WIDTH, d), const2),
                  pl.BlockSpec((POOL_WIDTH, d), const2),
                  pl.BlockSpec((d, d), const2),
                  pl.BlockSpec((1, d), const2),
                  pl.BlockSpec((d, LANES), const2),
                  pl.BlockSpec((d, LANES), const2),
                  pl.BlockSpec((1, LANES), const2)],
        out_specs=[pl.BlockSpec((tm, d), lambda i: (i, 0)),
                   pl.BlockSpec((tm, d), lambda i: (i, 0)),
                   pl.BlockSpec((tm, LANES), lambda i: (i, 0))],
        out_shape=[jax.ShapeDtypeStruct((n, d), F32),
                   jax.ShapeDtypeStruct((n, d), BF16),
                   jax.ShapeDtypeStruct((n, LANES), F32)],
        compiler_params=_cparams("arbitrary"),
        name="merge_outproj_router",
    )(x2, mod3, g1, w_z, b_z, attnT, u, u, pool_w, pool_scale, w_ba, w_bp, w_out, g2, wr_hi, wr_lo, b_r)


def _moe_dense_kernel(h_ref, comb_ref, wg_ref, wu_ref, wd_ref, y_ref):
    e = pl.program_id(1)

    @pl.when(e == 0)
    def _():
        y_ref[...] = jnp.zeros_like(y_ref)

    x = h_ref[...]
    g = _dot(x, wg_ref[0])
    u = _dot(x, wu_ref[0])
    act = (g * jax.nn.sigmoid(g)) * u
    ye = _dot(act.astype(BF16), wd_ref[0])
    lane = lax.broadcasted_iota(jnp.int32, comb_ref.shape, 1)
    col = jnp.sum(jnp.where(lane == e + ROUTER_LANE0, comb_ref[...], 0.0), axis=-1, keepdims=True)
    y_ref[...] += col * ye


def _moe_dense(h2, comb, wg, wu, wd):
    n, d = h2.shape
    tm = TM_MOE
    return pl.pallas_call(
        _moe_dense_kernel,
        grid=(n // tm, N_EXPERTS),
        in_specs=[pl.BlockSpec((tm, d), lambda i, e: (i, 0)),
                  pl.BlockSpec((tm, LANES), lambda i, e: (i, 0)),
                  pl.BlockSpec((1, d, D_EXPERT), lambda i, e: (e, 0, 0)),
                  pl.BlockSpec((1, d, D_EXPERT), lambda i, e: (e, 0, 0)),
                  pl.BlockSpec((1, D_EXPERT, d), lambda i, e: (e, 0, 0))],
        out_specs=pl.BlockSpec((tm, d), lambda i, e: (i, 0)),
        out_shape=jax.ShapeDtypeStruct((n, d), F32),
        compiler_params=_cparams("arbitrary", "arbitrary"),
        name="moe_dense",
    )(h2, comb, wg, wu, wd)


def _final_kernel(x1_ref, y_ref, mod_ref, g_ref, o_ref, *, final_norm):
    d = D_MODEL
    gate2 = mod_ref[0][:, 5 * d:6 * d]
    x = x1_ref[...] + gate2 * y_ref[...]
    if final_norm:
        x = (x * lax.rsqrt(jnp.mean(x * x, axis=-1, keepdims=True) + EPS)) * g_ref[...]
    o_ref[...] = x


def _final(x1, y, mod3, gf, *, seq, final_norm):
    n, d = x1.shape
    tm = TM_FINAL
    tps = seq // tm
    return pl.pallas_call(
        functools.partial(_final_kernel, final_norm=final_norm),
        grid=(n // tm,),
        in_specs=[pl.BlockSpec((tm, d), lambda i: (i, 0)),
                  pl.BlockSpec((tm, d), lambda i: (i, 0)),
                  pl.BlockSpec((1, 1, N_MOD * d), lambda i: (i // tps, 0, 0)),
                  pl.BlockSpec((1, d), lambda i: (0, 0))],
        out_specs=pl.BlockSpec((tm, d), lambda i: (i, 0)),
        out_shape=jax.ShapeDtypeStruct((n, d), F32),
        compiler_params=_cparams("arbitrary"),
        name="residual_final_norm",
    )(x1, y, mod3, gf)


def _split_bf16(w):
    hi = w.astype(BF16)
    return hi, (w - hi.astype(F32)).astype(BF16)


def kernel(x, c, w_ada, b_ada, norm1_g, w_in, b_gate, rel_bias, pool_w, pool_scale, w_branch_attn,
           w_branch_pool, w_out, norm2_g, w_router_group, b_router_group, w_router_expert,
           b_router_expert, w_expert_gate, w_expert_up, w_expert_down, norm_f_g):
    batch, seq, d = x.shape
    depth = w_ada.shape[0]
    n = batch * seq
    nb = seq // MOBA_BLOCK
    qkvu = 3 * ATTN_WIDTH + POOL_WIDTH
    bias_own, bias_prev = _bias_tiles(rel_bias)
    c_pad = jnp.pad(c, ((0, 8 - batch), (0, 0)))

    xc = x.reshape(n, d)
    for l in range(depth):
        mod3 = _modulation(c_pad, w_ada[l], b_ada[l][None])[:batch].reshape(batch, 1, N_MOD * d)
        w_in_b = w_in[l].astype(BF16)
        g1 = norm1_g[l][None]

        qT, k2, vT, u, km = _in_projection(xc, mod3, g1, w_in_b[:, :qkvu], batch=batch, seq=seq)
        km_pad = jnp.pad(km.reshape(batch, nb, ATTN_WIDTH), ((0, 0), (0, MAX_BLOCKS - nb), (0, 0)))
        attnT = _moba_attention(qT, k2.reshape(batch, seq, ATTN_WIDTH), vT, km_pad, bias_own, bias_prev,
                                batch=batch, seq=seq)

        w_r = jnp.concatenate([w_router_group[l], w_router_expert[l]], axis=1)
        w_r = jnp.pad(w_r, ((0, 0), (0, LANES - w_r.shape[1])))
        b_r = jnp.concatenate([b_router_group[l], b_router_expert[l]])
        b_r = jnp.pad(b_r, (0, LANES - b_r.shape[0]))[None]
        wr_hi, wr_lo = _split_bf16(w_r)
        x1, h2, comb = _post_attention(
            xc, mod3, g1, w_in_b[:, qkvu:], b_gate[l][None], attnT, u, pool_w[l].astype(BF16),
            pool_scale[l][None], w_branch_attn[l].astype(BF16), w_branch_pool[l].astype(BF16),
            w_out[l].astype(BF16), norm2_g[l][None], wr_hi, wr_lo, b_r, batch=batch, seq=seq)

        y = _moe_dense(h2, comb, w_expert_gate[l].astype(BF16), w_expert_up[l].astype(BF16),
                       w_expert_down[l].astype(BF16))
        xc = _final(x1, y, mod3, norm_f_g[None], seq=seq, final_norm=(l == depth - 1))
    return xc.reshape(batch, seq, d)
```

```python
import functools
import math

import numpy as np
import jax
import jax.numpy as jnp
from jax import lax
from jax.experimental import pallas as pl
from jax.experimental.pallas import tpu as pltpu

F32 = jnp.float32
BF16 = jnp.bfloat16

D_MODEL = 1024
N_HEADS = 8
HEAD_DIM = 64
ATTN_WIDTH = N_HEADS * HEAD_DIM
MOBA_BLOCK = 256
MOBA_TOPK = 3
MAX_BLOCKS = 32
FAR_GROUP = 4
POOL_WINDOWS = (2, 4, 8, 16)
POOL_WIDTH = 512
POOL_GROUP_WIDTH = 128
POOL_HALO = 16
REL_BUCKETS = 32
REL_MAX_DIST = 128
N_GROUPS = 4
EXPERTS_PER_GROUP = 8
N_EXPERTS = N_GROUPS * EXPERTS_PER_GROUP
D_EXPERT = 512
N_MOD = 6
EPS = 1e-6
NEG_INF = -1e30
LANES = 128
ROUTER_LANE0 = N_GROUPS
VMEM_LIMIT = 56 * 1024 * 1024

TM_PROJ = 512
TM_POST = 512
TM_MOE = 1024
TM_FINAL = 512


def _cparams(*sem):
    return pltpu.CompilerParams(dimension_semantics=sem, vmem_limit_bytes=VMEM_LIMIT)


def _dot(a, b):
    return jnp.dot(a, b, preferred_element_type=F32)


def _rms_mod(x, g, scale, shift):
    xn = x * lax.rsqrt(jnp.mean(x * x, axis=-1, keepdims=True) + EPS)
    return (xn * g) * (1.0 + scale) + shift


def _mod_kernel(c_ref, w_ref, b_ref, o_ref):
    c = c_ref[...]
    ca = c * jax.nn.sigmoid(c)
    o_ref[...] = _dot(ca.astype(BF16), w_ref[...].astype(BF16)) + b_ref[...]


def _modulation(c_pad, w_ada, b_ada):
    rows, d = c_pad.shape
    n_out = w_ada.shape[1]
    tn = 1024
    return pl.pallas_call(
        _mod_kernel,
        grid=(n_out // tn,),
        in_specs=[pl.BlockSpec((rows, d), lambda j: (0, 0)),
                  pl.BlockSpec((d, tn), lambda j: (0, j)),
                  pl.BlockSpec((1, tn), lambda j: (0, j))],
        out_specs=pl.BlockSpec((rows, tn), lambda j: (0, j)),
        out_shape=jax.ShapeDtypeStruct((rows, n_out), F32),
        compiler_params=_cparams("arbitrary"),
        name="adaln_mod",
    )(c_pad, w_ada, b_ada)


def _inproj_kernel(x_ref, mod_ref, g_ref, w_ref, qT_ref, k_ref, vT_ref, u_ref, km_ref, *, tm):
    d = D_MODEL
    mod = mod_ref[0]
    h = _rms_mod(x_ref[...], g_ref[...], mod[:, d:2 * d], mod[:, 0:d])
    proj = _dot(h.astype(BF16), w_ref[...])
    aw = ATTN_WIDTH
    q = proj[:, 0:aw] * (HEAD_DIM ** -0.5)
    k = proj[:, aw:2 * aw]
    v = proj[:, 2 * aw:3 * aw]
    u_ref[...] = proj[:, 3 * aw:3 * aw + POOL_WIDTH]
    k_ref[...] = k.astype(BF16)
    qT = q.T.astype(BF16)
    vT = v.T.astype(BF16)
    for c in range(tm // MOBA_BLOCK):
        sl = slice(c * MOBA_BLOCK, (c + 1) * MOBA_BLOCK)
        qT_ref[0, c] = qT[:, sl]
        vT_ref[0, c] = vT[:, sl]
        km_ref[0, c:c + 1, :] = jnp.mean(k[sl, :], axis=0, keepdims=True)


def _in_projection(x2, mod3, g1, w_qkvu, *, batch, seq):
    n, d = x2.shape
    tm = TM_PROJ
    tps = seq // tm
    nb = seq // MOBA_BLOCK
    cb = tm // MOBA_BLOCK
    wcols = w_qkvu.shape[1]
    kern = functools.partial(_inproj_kernel, tm=tm)
    t_shape = jax.ShapeDtypeStruct((batch, nb, ATTN_WIDTH, MOBA_BLOCK), BF16)
    t_spec = pl.BlockSpec((1, cb, ATTN_WIDTH, MOBA_BLOCK), lambda i: (i // tps, i % tps, 0, 0))
    return pl.pallas_call(
        kern,
        grid=(n // tm,),
        in_specs=[pl.BlockSpec((tm, d), lambda i: (i, 0)),
                  pl.BlockSpec((1, 1, N_MOD * d), lambda i: (i // tps, 0, 0)),
                  pl.BlockSpec((1, d), lambda i: (0, 0)),
                  pl.BlockSpec((d, wcols), lambda i: (0, 0))],
        out_specs=[t_spec,
                   pl.BlockSpec((tm, ATTN_WIDTH), lambda i: (i, 0)),
                   t_spec,
                   pl.BlockSpec((tm, POOL_WIDTH), lambda i: (i, 0)),
                   pl.BlockSpec((1, cb, ATTN_WIDTH), lambda i: (i, 0, 0))],
        out_shape=[t_shape,
                   jax.ShapeDtypeStruct((n, ATTN_WIDTH), BF16),
                   t_shape,
                   jax.ShapeDtypeStruct((n, POOL_WIDTH), F32),
                   jax.ShapeDtypeStruct((n // tm, cb, ATTN_WIDTH), F32)],
        compiler_params=_cparams("arbitrary"),
        name="norm1_inproj",
    )(x2, mod3, g1, w_qkvu)


def _attn_kernel(qT_ref, k_ref, vT_ref, km_ref, bw_ref, o_ref, kaug_ref, bias_ref, qnear_ref, qfar_ref,
                 s_ref, *, nb):
    blk = MOBA_BLOCK
    hd = HEAD_DIM
    lane = lax.broadcasted_iota(jnp.int32, (blk, LANES), 1)

    def build_kaug(j, carry):
        rows = pl.ds(pl.multiple_of(j * blk, blk), blk)
        kp = k_ref[0, rows, :]
        kaug_ref[0, rows, :] = jnp.where(lane < hd, kp, jnp.where(lane == hd + j, 1.0, 0.0).astype(BF16))
        kaug_ref[1, rows, :] = jnp.where(lane >= hd, kp, jnp.where(lane == j, 1.0, 0.0).astype(BF16))
        return carry

    lax.fori_loop(0, nb, build_kaug, 0)

    for hh in range(2):
        for t in range(2):
            wide = jnp.broadcast_to(bw_ref[hh, t:t + 1, :], (blk, 2 * blk))
            bias_ref[hh, t] = pltpu.roll(wide, 0, 1, stride=1, stride_axis=0)[:, blk:2 * blk]

    rowi = lax.broadcasted_iota(jnp.int32, (MAX_BLOCKS, blk), 0)
    km_lane = lax.broadcasted_iota(jnp.int32, (MAX_BLOCKS, LANES), 1)
    zpad = jnp.zeros((LANES - hd - MAX_BLOCKS, blk), BF16)
    km_split = []
    for hh in range(2):
        in_head = (km_lane >= hh * hd) & (km_lane < (hh + 1) * hd)
        km_h = jnp.where(in_head, km_ref[0], 0.0)
        km_hi = km_h.astype(BF16)
        km_split.append((km_hi, (km_h - km_hi.astype(F32)).astype(BF16)))

    def head_rows(hh):
        return slice(hh * hd, (hh + 1) * hd)

    grp = FAR_GROUP

    def consume(hh, m, l, acc, mx, vidx):
        m_new = jnp.maximum(m, mx)
        alpha = jnp.exp(m - m_new)
        p = jnp.exp(s_ref[hh] - m_new)
        vcat = jnp.concatenate([vT_ref[0, j, head_rows(hh), :] for j in vidx], axis=1)
        l = alpha * l + jnp.sum(p, axis=0, keepdims=True)
        return m_new, l, alpha * acc + _dot(vcat, p.astype(BF16))

    def qtile(i, carry):
        qp = qT_ref[0, i]
        near_idx = [jnp.where(i - (grp - 1) + g < 0, i + 1, i - (grp - 1) + g) for g in range(grp)]
        for hh in range(2):
            km_hi, km_lo = km_split[hh]
            gate = _dot(km_hi, qp) + _dot(km_lo, qp)
            g = jnp.where(rowi < i, gate, -jnp.inf)
            sel = rowi == i
            for _ in range(MOBA_TOPK):
                m = jnp.max(g, axis=0, keepdims=True)
                hit = (g == m) & (m > -jnp.inf)
                idx = jnp.min(jnp.where(hit, rowi, MAX_BLOCKS), axis=0, keepdims=True)
                pick = rowi == idx
                sel = sel | pick
                g = jnp.where(pick, -jnp.inf, g)
            near = jnp.where(sel, 0.0, NEG_INF)
            far = jnp.where(rowi > i - grp, NEG_INF, near)
            qh = qp[head_rows(hh), :]
            if hh == 0:
                qnear_ref[hh] = jnp.concatenate([qh, near.astype(BF16), zpad], axis=0)
                qfar_ref[hh] = jnp.concatenate([qh, far.astype(BF16), zpad], axis=0)
            else:
                qnear_ref[hh] = jnp.concatenate([near.astype(BF16), zpad, qh], axis=0)
                qfar_ref[hh] = jnp.concatenate([far.astype(BF16), zpad, qh], axis=0)

        heads = []
        for hh in range(2):
            mx = None
            for g in range(grp):
                rows = pl.ds(pl.multiple_of(near_idx[g] * blk, blk), blk)
                s = _dot(kaug_ref[hh, rows, :], qnear_ref[hh])
                if g == grp - 1:
                    s = s + bias_ref[hh, 0]
                elif g == grp - 2:
                    s = s + bias_ref[hh, 1]
                s_ref[hh, g * blk:(g + 1) * blk, :] = s
                cm = jnp.max(s, axis=0, keepdims=True)
                mx = cm if mx is None else jnp.maximum(mx, cm)
            heads.append((jnp.full((1, blk), -jnp.inf, F32), jnp.zeros((1, blk), F32),
                          jnp.zeros((hd, blk), F32), mx))

        def far_step(c, st):
            vidx, hs = st
            rows = pl.ds(pl.multiple_of(c * (grp * blk), grp * blk), grp * blk)
            new = []
            for hh in range(2):
                m, l, acc, mx = hs[hh]
                m, l, acc = consume(hh, m, l, acc, mx, vidx)
                s = _dot(kaug_ref[hh, rows, :], qfar_ref[hh])
                s_ref[hh] = s
                new.append((m, l, acc, jnp.max(s, axis=0, keepdims=True)))
            return tuple(c * grp + g for g in range(grp)), tuple(new)

        n_far = jnp.maximum(i - (grp - 1), 0)
        vidx, heads = lax.fori_loop(0, (n_far + grp - 1) // grp, far_step, (tuple(near_idx), tuple(heads)))
        for hh in range(2):
            m, l, acc, mx = heads[hh]
            _, l, acc = consume(hh, m, l, acc, mx, vidx)
            o_ref[0, i, head_rows(hh), :] = (acc / l).astype(BF16)
        return carry

    lax.fori_loop(0, nb, qtile, 0)


def _rel_bucket_table(max_rel):
    n = np.arange(max_rel)
    max_exact = REL_BUCKETS // 2
    nf = np.maximum(n, 1).astype(np.float32)
    large = max_exact + (np.log(nf / np.float32(max_exact)) / np.float32(math.log(REL_MAX_DIST / max_exact))
                         * np.float32(REL_BUCKETS - max_exact)).astype(np.int32)
    large = np.minimum(large, REL_BUCKETS - 1)
    return np.where(n < max_exact, n, large)


def _bias_vectors(rel_bias):
    blk = MOBA_BLOCK
    table = _rel_bucket_table(2 * blk)
    r = np.arange(-blk, blk)
    bt = rel_bias.T - rel_bias[REL_BUCKETS - 1][:, None]
    own = jnp.where(jnp.asarray(r >= 0)[None], bt[:, table[np.maximum(r, 0)]], NEG_INF)
    prev = bt[:, table[np.minimum(r + blk, 2 * blk - 1)]]
    return jnp.stack([own, prev], axis=1).astype(F32)


def _moba_attention(qT, k3, vT, km_pad, bias_w, *, batch, seq):
    nb = seq // MOBA_BLOCK
    blk = MOBA_BLOCK
    assert nb % FAR_GROUP == 0 and nb <= MAX_BLOCKS
    kern = functools.partial(_attn_kernel, nb=nb)
    t_spec = pl.BlockSpec((1, nb, LANES, blk), lambda b, p: (b, 0, p, 0))
    return pl.pallas_call(
        kern,
        grid=(batch, N_HEADS // 2),
        in_specs=[t_spec,
                  pl.BlockSpec((1, seq, LANES), lambda b, p: (b, 0, p)),
                  t_spec,
                  pl.BlockSpec((1, MAX_BLOCKS, LANES), lambda b, p: (b, 0, p)),
                  pl.BlockSpec((2, 2, 2 * blk), lambda b, p: (p, 0, 0))],
        out_specs=t_spec,
        out_shape=jax.ShapeDtypeStruct((batch, nb, ATTN_WIDTH, blk), BF16),
        scratch_shapes=[pltpu.VMEM((2, seq, LANES), BF16),
                        pltpu.VMEM((2, 2, blk, blk), F32),
                        pltpu.VMEM((2, LANES, blk), BF16),
                        pltpu.VMEM((2, LANES, blk), BF16),
                        pltpu.VMEM((2, FAR_GROUP * blk, blk), F32)],
        compiler_params=_cparams("arbitrary", "arbitrary"),
        name="moba_attention",
    )(qT, k3, vT, km_pad, bias_w)


def _post_kernel(x_ref, mod_ref, g1_ref, wz_ref, bz_ref, at_ref, u_ref, uh_ref, pw_ref, ps_ref,
                 wba_ref, wbp_ref, wo_ref, g2_ref, wrh_ref, wrl_ref, br_ref,
                 x1_ref, h2_ref, comb_ref, *, tm, seq):
    d = D_MODEL
    i = pl.program_id(0)
    x = x_ref[...]
    mod = mod_ref[0]
    shift1, scale1, gate1 = mod[:, 0:d], mod[:, d:2 * d], mod[:, 2 * d:3 * d]
    shift2, scale2 = mod[:, 3 * d:4 * d], mod[:, 4 * d:5 * d]

    h = _rms_mod(x, g1_ref[...], scale1, shift1)
    zg = jax.nn.sigmoid(_dot(h.astype(BF16), wz_ref[...]) + bz_ref[...])

    ya = jnp.concatenate(
        [lax.dot_general(at_ref[0, c], wba_ref[...], (((0,), (0,)), ((), ())),
                         preferred_element_type=F32) for c in range(tm // MOBA_BLOCK)], axis=0)

    first = (i * tm) % seq == 0
    halo = jnp.where(first, 0.0, uh_ref[...])
    ubuf = jnp.concatenate([halo, u_ref[...]], axis=0)
    rows = tm + POOL_HALO
    pos = (lax.broadcasted_iota(jnp.int32, (tm, POOL_GROUP_WIDTH), 0) + i * tm) % seq
    outs = []
    for gi, w in enumerate(POOL_WINDOWS):
        e = ubuf[:, gi * POOL_GROUP_WIDTH:(gi + 1) * POOL_GROUP_WIDTH]
        s = e
        step = 1
        while step < w:
            s = s + pltpu.roll(s, step, 0)
            step *= 2
        cnt = jnp.minimum(pos + 1, w).astype(F32)
        dlt = s[POOL_HALO:rows] / cnt - e[POOL_HALO:rows]
        outs.append(_dot(dlt.astype(BF16), pw_ref[gi]))
    pooled = jnp.concatenate(outs, axis=-1) * ps_ref[...]
    yp = _dot(pooled.astype(BF16), wbp_ref[...])

    merged = zg[:, 0:d] * ya + zg[:, d:2 * d] * yp
    x1 = x + gate1 * _dot(merged.astype(BF16), wo_ref[...])
    x1_ref[...] = x1

    h2 = _rms_mod(x1, g2_ref[...], scale2, shift2)
    h2_hi = h2.astype(BF16)
    h2_ref[...] = h2_hi
    h2_lo = (h2 - h2_hi.astype(F32)).astype(BF16)
    z = _dot(h2_hi, wrh_ref[...]) + _dot(h2_lo, wrh_ref[...]) + _dot(h2_hi, wrl_ref[...]) + br_ref[...]

    lane = lax.broadcasted_iota(jnp.int32, (tm, LANES), 1)
    zgrp = jnp.where(lane < N_GROUPS, z, -jnp.inf)
    mg = jnp.max(zgrp, axis=-1, keepdims=True)
    pg_top = 1.0 / jnp.sum(jnp.exp(zgrp - mg), axis=-1, keepdims=True)
    g_idx = jnp.min(jnp.where(zgrp == mg, lane, LANES), axis=-1, keepdims=True)
    e_lane = lane - ROUTER_LANE0
    in_grp = (e_lane >= 0) & (e_lane < N_EXPERTS) & ((e_lane >> 3) == g_idx)
    ze = jnp.where(in_grp, z, -jnp.inf)
    m1 = jnp.max(ze, axis=-1, keepdims=True)
    i1 = jnp.min(jnp.where(ze == m1, lane, LANES), axis=-1, keepdims=True)
    ze2 = jnp.where(lane == i1, -jnp.inf, ze)
    m2 = jnp.max(ze2, axis=-1, keepdims=True)
    i2 = jnp.min(jnp.where(ze2 == m2, lane, LANES), axis=-1, keepdims=True)
    e2 = jnp.exp(m2 - m1)
    w1 = pg_top / (1.0 + e2)
    w2 = pg_top * e2 / (1.0 + e2)
    comb_ref[...] = jnp.where(lane == i1, w1, 0.0) + jnp.where(lane == i2, w2, 0.0)


def _post_attention(x2, mod3, g1, w_z, b_z, attnT, u, pool_w, pool_scale, w_ba, w_bp, w_out, g2,
                    wr_hi, wr_lo, b_r, *, batch, seq):
    n, d = x2.shape
    tm = TM_POST
    tps = seq // tm
    cb = tm // MOBA_BLOCK
    hb = tm // POOL_HALO
    kern = functools.partial(_post_kernel, tm=tm, seq=seq)
    const2 = lambda i: (0, 0)
    return pl.pallas_call(
        kern,
        grid=(n // tm,),
        in_specs=[pl.BlockSpec((tm, d), lambda i: (i, 0)),
                  pl.BlockSpec((1, 1, N_MOD * d), lambda i: (i // tps, 0, 0)),
                  pl.BlockSpec((1, d), const2),
                  pl.BlockSpec((d, 2 * d), const2),
                  pl.BlockSpec((1, 2 * d), const2),
                  pl.BlockSpec((1, cb, ATTN_WIDTH, MOBA_BLOCK), lambda i: (i // tps, i % tps, 0, 0)),
                  pl.BlockSpec((tm, POOL_WIDTH), lambda i: (i, 0)),
                  pl.BlockSpec((POOL_HALO, POOL_WIDTH), lambda i: (jnp.maximum(i * hb - 1, 0), 0)),
                  pl.BlockSpec((len(POOL_WINDOWS), POOL_GROUP_WIDTH, POOL_GROUP_WIDTH), lambda i: (0, 0, 0)),
                  pl.BlockSpec((1, POOL_WIDTH), const2),
                  pl.BlockSpec((ATTN_WIDTH, d), const2),
                  pl.BlockSpec((POOL_WIDTH, d), const2),
                  pl.BlockSpec((d, d), const2),
                  pl.BlockSpec((1, d), const2),
                  pl.BlockSpec((d, LANES), const2),
                  pl.BlockSpec((d, LANES), const2),
                  pl.BlockSpec((1, LANES), const2)],
        out_specs=[pl.BlockSpec((tm, d), lambda i: (i, 0)),
                   pl.BlockSpec((tm, d), lambda i: (i, 0)),
                   pl.BlockSpec((tm, LANES), lambda i: (i, 0))],
        out_shape=[jax.ShapeDtypeStruct((n, d), F32),
                   jax.ShapeDtypeStruct((n, d), BF16),
                   jax.ShapeDtypeStruct((n, LANES), F32)],
        compiler_params=_cparams("arbitrary"),
        name="merge_outproj_router",
    )(x2, mod3, g1, w_z, b_z, attnT, u, u, pool_w, pool_scale, w_ba, w_bp, w_out, g2, wr_hi, wr_lo, b_r)


def _moe_dense_kernel(h_ref, comb_ref, wg_ref, wu_ref, wd_ref, y_ref):
    e = pl.program_id(1)

    @pl.when(e == 0)
    def _():
        y_ref[...] = jnp.zeros_like(y_ref)

    x = h_ref[...]
    g = _dot(x, wg_ref[0])
    u = _dot(x, wu_ref[0])
    act = (g * jax.nn.sigmoid(g)) * u
    ye = _dot(act.astype(BF16), wd_ref[0])
    lane = lax.broadcasted_iota(jnp.int32, comb_ref.shape, 1)
    col = jnp.sum(jnp.where(lane == e + ROUTER_LANE0, comb_ref[...], 0.0), axis=-1, keepdims=True)
    y_ref[...] += col * ye


def _moe_dense(h2, comb, wg, wu, wd):
    n, d = h2.shape
    tm = TM_MOE
    return pl.pallas_call(
        _moe_dense_kernel,
        grid=(n // tm, N_EXPERTS),
        in_specs=[pl.BlockSpec((tm, d), lambda i, e: (i, 0)),
                  pl.BlockSpec((tm, LANES), lambda i, e: (i, 0)),
                  pl.BlockSpec((1, d, D_EXPERT), lambda i, e: (e, 0, 0)),
                  pl.BlockSpec((1, d, D_EXPERT), lambda i, e: (e, 0, 0)),
                  pl.BlockSpec((1, D_EXPERT, d), lambda i, e: (e, 0, 0))],
        out_specs=pl.BlockSpec((tm, d), lambda i, e: (i, 0)),
        out_shape=jax.ShapeDtypeStruct((n, d), F32),
        compiler_params=_cparams("arbitrary", "arbitrary"),
        name="moe_dense",
    )(h2, comb, wg, wu, wd)


def _final_kernel(x1_ref, y_ref, mod_ref, g_ref, o_ref, *, final_norm):
    d = D_MODEL
    gate2 = mod_ref[0][:, 5 * d:6 * d]
    x = x1_ref[...] + gate2 * y_ref[...]
    if final_norm:
        x = (x * lax.rsqrt(jnp.mean(x * x, axis=-1, keepdims=True) + EPS)) * g_ref[...]
    o_ref[...] = x


def _final(x1, y, mod3, gf, *, seq, final_norm):
    n, d = x1.shape
    tm = TM_FINAL
    tps = seq // tm
    return pl.pallas_call(
        functools.partial(_final_kernel, final_norm=final_norm),
        grid=(n // tm,),
        in_specs=[pl.BlockSpec((tm, d), lambda i: (i, 0)),
                  pl.BlockSpec((tm, d), lambda i: (i, 0)),
                  pl.BlockSpec((1, 1, N_MOD * d), lambda i: (i // tps, 0, 0)),
                  pl.BlockSpec((1, d), lambda i: (0, 0))],
        out_specs=pl.BlockSpec((tm, d), lambda i: (i, 0)),
        out_shape=jax.ShapeDtypeStruct((n, d), F32),
        compiler_params=_cparams("arbitrary"),
        name="residual_final_norm",
    )(x1, y, mod3, gf)


def _split_bf16(w):
    hi = w.astype(BF16)
    return hi, (w - hi.astype(F32)).astype(BF16)


def kernel(x, c, w_ada, b_ada, norm1_g, w_in, b_gate, rel_bias, pool_w, pool_scale, w_branch_attn,
           w_branch_pool, w_out, norm2_g, w_router_group, b_router_group, w_router_expert,
           b_router_expert, w_expert_gate, w_expert_up, w_expert_down, norm_f_g):
    batch, seq, d = x.shape
    depth = w_ada.shape[0]
    n = batch * seq
    nb = seq // MOBA_BLOCK
    qkvu = 3 * ATTN_WIDTH + POOL_WIDTH
    bias_w = _bias_vectors(rel_bias)
    c_pad = jnp.pad(c, ((0, 8 - batch), (0, 0)))

    xc = x.reshape(n, d)
    for l in range(depth):
        mod3 = _modulation(c_pad, w_ada[l], b_ada[l][None])[:batch].reshape(batch, 1, N_MOD * d)
        w_in_b = w_in[l].astype(BF16)
        g1 = norm1_g[l][None]

        qT, k2, vT, u, km = _in_projection(xc, mod3, g1, w_in_b[:, :qkvu], batch=batch, seq=seq)
        km_pad = jnp.pad(km.reshape(batch, nb, ATTN_WIDTH), ((0, 0), (0, MAX_BLOCKS - nb), (0, 0)))
        attnT = _moba_attention(qT, k2.reshape(batch, seq, ATTN_WIDTH), vT, km_pad, bias_w,
                                batch=batch, seq=seq)

        w_r = jnp.concatenate([w_router_group[l], w_router_expert[l]], axis=1)
        w_r = jnp.pad(w_r, ((0, 0), (0, LANES - w_r.shape[1])))
        b_r = jnp.concatenate([b_router_group[l], b_router_expert[l]])
        b_r = jnp.pad(b_r, (0, LANES - b_r.shape[0]))[None]
        wr_hi, wr_lo = _split_bf16(w_r)
        x1, h2, comb = _post_attention(
            xc, mod3, g1, w_in_b[:, qkvu:], b_gate[l][None], attnT, u, pool_w[l].astype(BF16),
            pool_scale[l][None], w_branch_attn[l].astype(BF16), w_branch_pool[l].astype(BF16),
            w_out[l].astype(BF16), norm2_g[l][None], wr_hi, wr_lo, b_r, batch=batch, seq=seq)

        y = _moe_dense(h2, comb, w_expert_gate[l].astype(BF16), w_expert_up[l].astype(BF16),
                       w_expert_down[l].astype(BF16))
        xc = _final(x1, y, mod3, norm_f_g[None], seq=seq, final_norm=(l == depth - 1))
    return xc.reshape(batch, seq, d)
```

```python
import functools
import math

import numpy as np
import jax
import jax.numpy as jnp
from jax import lax
from jax.experimental import pallas as pl
from jax.experimental.pallas import tpu as pltpu

F32 = jnp.float32
BF16 = jnp.bfloat16

D_MODEL = 1024
N_HEADS = 8
HEAD_DIM = 64
ATTN_WIDTH = N_HEADS * HEAD_DIM
MOBA_BLOCK = 256
MOBA_TOPK = 3
MAX_BLOCKS = 32
FAR_GROUP = 4
POOL_WINDOWS = (2, 4, 8, 16)
POOL_WIDTH = 512
POOL_GROUP_WIDTH = 128
POOL_HALO = 16
REL_BUCKETS = 32
REL_MAX_DIST = 128
N_GROUPS = 4
EXPERTS_PER_GROUP = 8
N_EXPERTS = N_GROUPS * EXPERTS_PER_GROUP
D_EXPERT = 512
N_MOD = 6
EPS = 1e-6
NEG_INF = -1e30
LANES = 128
ROUTER_LANE0 = N_GROUPS
VMEM_LIMIT = 56 * 1024 * 1024

TM_PROJ = 512
TM_POST = 512
TM_DISPATCH = 512
TE_ROWS = 256
EXPERT_TOPK = 2
TM_FINAL = 256


def _cparams(*sem):
    return pltpu.CompilerParams(dimension_semantics=sem, vmem_limit_bytes=VMEM_LIMIT)


def _dot(a, b):
    return jnp.dot(a, b, preferred_element_type=F32)


def _rms_mod(x, g, scale, shift):
    xn = x * lax.rsqrt(jnp.mean(x * x, axis=-1, keepdims=True) + EPS)
    return (xn * g) * (1.0 + scale) + shift


def _pack_bf16_pairs(a, b):
    abits = lax.bitcast_convert_type(a.astype(F32), jnp.uint32)
    bbits = lax.bitcast_convert_type(b.astype(F32), jnp.uint32)
    return (abits >> 16) | (bbits & jnp.uint32(0xFFFF0000))


def _unpack_bf16_pairs(p):
    a = lax.bitcast_convert_type(p << 16, F32)
    b = lax.bitcast_convert_type(p & jnp.uint32(0xFFFF0000), F32)
    return a.astype(BF16), b.astype(BF16)


def _mod_kernel(c_ref, w_ref, b_ref, o_ref):
    c = c_ref[...]
    ca = c * jax.nn.sigmoid(c)
    o_ref[...] = _dot(ca.astype(BF16), w_ref[...].astype(BF16)) + b_ref[...]


def _modulation(c_pad, w_ada, b_ada):
    rows, d = c_pad.shape
    n_out = w_ada.shape[1]
    tn = 1024
    return pl.pallas_call(
        _mod_kernel,
        grid=(n_out // tn,),
        in_specs=[pl.BlockSpec((rows, d), lambda j: (0, 0)),
                  pl.BlockSpec((d, tn), lambda j: (0, j)),
                  pl.BlockSpec((1, tn), lambda j: (0, j))],
        out_specs=pl.BlockSpec((rows, tn), lambda j: (0, j)),
        out_shape=jax.ShapeDtypeStruct((rows, n_out), F32),
        compiler_params=_cparams("arbitrary"),
        name="adaln_mod",
    )(c_pad, w_ada, b_ada)


def _inproj_kernel(x_ref, mod_ref, g_ref, w_ref, qT_ref, k_ref, vT_ref, u_ref, km_ref, *, tm):
    d = D_MODEL
    mod = mod_ref[0]
    h = _rms_mod(x_ref[...], g_ref[...], mod[:, d:2 * d], mod[:, 0:d])
    proj = _dot(h.astype(BF16), w_ref[...])
    aw = ATTN_WIDTH
    q = proj[:, 0:aw] * (HEAD_DIM ** -0.5)
    k = proj[:, aw:2 * aw]
    v = proj[:, 2 * aw:3 * aw]
    u_ref[...] = proj[:, 3 * aw:3 * aw + POOL_WIDTH]
    k_ref[...] = k.astype(BF16)
    qT = q.T.astype(BF16)
    vT = v.T.astype(BF16)
    for c in range(tm // MOBA_BLOCK):
        sl = slice(c * MOBA_BLOCK, (c + 1) * MOBA_BLOCK)
        qT_ref[0, c] = qT[:, sl]
        vT_ref[0, c] = vT[:, sl]
        km_ref[0, c:c + 1, :] = jnp.mean(k[sl, :], axis=0, keepdims=True)


def _in_projection(x2, mod3, g1, w_qkvu, *, batch, seq):
    n, d = x2.shape
    tm = TM_PROJ
    tps = seq // tm
    nb = seq // MOBA_BLOCK
    cb = tm // MOBA_BLOCK
    wcols = w_qkvu.shape[1]
    kern = functools.partial(_inproj_kernel, tm=tm)
    t_shape = jax.ShapeDtypeStruct((batch, nb, ATTN_WIDTH, MOBA_BLOCK), BF16)
    t_spec = pl.BlockSpec((1, cb, ATTN_WIDTH, MOBA_BLOCK), lambda i: (i // tps, i % tps, 0, 0))
    return pl.pallas_call(
        kern,
        grid=(n // tm,),
        in_specs=[pl.BlockSpec((tm, d), lambda i: (i, 0)),
                  pl.BlockSpec((1, 1, N_MOD * d), lambda i: (i // tps, 0, 0)),
                  pl.BlockSpec((1, d), lambda i: (0, 0)),
                  pl.BlockSpec((d, wcols), lambda i: (0, 0))],
        out_specs=[t_spec,
                   pl.BlockSpec((tm, ATTN_WIDTH), lambda i: (i, 0)),
                   t_spec,
                   pl.BlockSpec((tm, POOL_WIDTH), lambda i: (i, 0)),
                   pl.BlockSpec((1, cb, ATTN_WIDTH), lambda i: (i, 0, 0))],
        out_shape=[t_shape,
                   jax.ShapeDtypeStruct((n, ATTN_WIDTH), BF16),
                   t_shape,
                   jax.ShapeDtypeStruct((n, POOL_WIDTH), F32),
                   jax.ShapeDtypeStruct((n // tm, cb, ATTN_WIDTH), F32)],
        compiler_params=_cparams("arbitrary"),
        name="norm1_inproj",
    )(x2, mod3, g1, w_qkvu)


def _attn_kernel(qT_ref, k_ref, vT_ref, km_ref, bw_ref, o_ref, kaug_ref, bias_ref, qnear_ref, qfar_ref,
                 s_ref, *, nb):
    blk = MOBA_BLOCK
    hd = HEAD_DIM
    lane = lax.broadcasted_iota(jnp.int32, (blk, LANES), 1)

    def build_kaug(j, carry):
        rows = pl.ds(pl.multiple_of(j * blk, blk), blk)
        kp = k_ref[0, rows, :]
        kaug_ref[0, rows, :] = jnp.where(lane < hd, kp, jnp.where(lane == hd + j, 1.0, 0.0).astype(BF16))
        kaug_ref[1, rows, :] = jnp.where(lane >= hd, kp, jnp.where(lane == j, 1.0, 0.0).astype(BF16))
        return carry

    lax.fori_loop(0, nb, build_kaug, 0)

    for hh in range(2):
        for t in range(2):
            wide = jnp.broadcast_to(bw_ref[hh, t:t + 1, :], (blk, 2 * blk))
            bias_ref[hh, t] = pltpu.roll(wide, 0, 1, stride=1, stride_axis=0)[:, blk:2 * blk]

    rowi = lax.broadcasted_iota(jnp.int32, (MAX_BLOCKS, blk), 0)
    km_lane = lax.broadcasted_iota(jnp.int32, (MAX_BLOCKS, LANES), 1)
    zpad = jnp.zeros((LANES - hd - MAX_BLOCKS, blk), BF16)
    km_split = []
    for hh in range(2):
        in_head = (km_lane >= hh * hd) & (km_lane < (hh + 1) * hd)
        km_h = jnp.where(in_head, km_ref[0], 0.0)
        km_hi = km_h.astype(BF16)
        km_split.append((km_hi, (km_h - km_hi.astype(F32)).astype(BF16)))

    def head_rows(hh):
        return slice(hh * hd, (hh + 1) * hd)

    grp = FAR_GROUP

    def consume(hh, m, l, acc, mx, vidx):
        m_new = jnp.maximum(m, mx)
        alpha = jnp.exp(m - m_new)
        p = jnp.exp(s_ref[hh] - m_new)
        vcat = jnp.concatenate([vT_ref[0, j, head_rows(hh), :] for j in vidx], axis=1)
        l = alpha * l + jnp.sum(p, axis=0, keepdims=True)
        return m_new, l, alpha * acc + _dot(vcat, p.astype(BF16))

    def qtile(i, carry):
        qp = qT_ref[0, i]
        near_idx = [jnp.where(i - (grp - 1) + g < 0, i + 1, i - (grp - 1) + g) for g in range(grp)]
        for hh in range(2):
            km_hi, km_lo = km_split[hh]
            gate = _dot(km_hi, qp) + _dot(km_lo, qp)
            g = jnp.where(rowi < i, gate, -jnp.inf)
            sel = rowi == i
            for _ in range(MOBA_TOPK):
                m = jnp.max(g, axis=0, keepdims=True)
                hit = (g == m) & (m > -jnp.inf)
                idx = jnp.min(jnp.where(hit, rowi, MAX_BLOCKS), axis=0, keepdims=True)
                pick = rowi == idx
                sel = sel | pick
                g = jnp.where(pick, -jnp.inf, g)
            near = jnp.where(sel, 0.0, NEG_INF)
            far = jnp.where(rowi > i - grp, NEG_INF, near)
            qh = qp[head_rows(hh), :]
            if hh == 0:
                qnear_ref[hh] = jnp.concatenate([qh, near.astype(BF16), zpad], axis=0)
                qfar_ref[hh] = jnp.concatenate([qh, far.astype(BF16), zpad], axis=0)
            else:
                qnear_ref[hh] = jnp.concatenate([near.astype(BF16), zpad, qh], axis=0)
                qfar_ref[hh] = jnp.concatenate([far.astype(BF16), zpad, qh], axis=0)

        heads = []
        for hh in range(2):
            mx = None
            for g in range(grp):
                rows = pl.ds(pl.multiple_of(near_idx[g] * blk, blk), blk)
                s = _dot(kaug_ref[hh, rows, :], qnear_ref[hh])
                if g == grp - 1:
                    s = s + bias_ref[hh, 0]
                elif g == grp - 2:
                    s = s + bias_ref[hh, 1]
                s_ref[hh, g * blk:(g + 1) * blk, :] = s
                cm = jnp.max(s, axis=0, keepdims=True)
                mx = cm if mx is None else jnp.maximum(mx, cm)
            heads.append((jnp.full((1, blk), -jnp.inf, F32), jnp.zeros((1, blk), F32),
                          jnp.zeros((hd, blk), F32), mx))

        def far_step(c, st):
            vidx, hs = st
            rows = pl.ds(pl.multiple_of(c * (grp * blk), grp * blk), grp * blk)
            new = []
            for hh in range(2):
                m, l, acc, mx = hs[hh]
                m, l, acc = consume(hh, m, l, acc, mx, vidx)
                s = _dot(kaug_ref[hh, rows, :], qfar_ref[hh])
                s_ref[hh] = s
                new.append((m, l, acc, jnp.max(s, axis=0, keepdims=True)))
            return tuple(c * grp + g for g in range(grp)), tuple(new)

        n_far = jnp.maximum(i - (grp - 1), 0)
        vidx, heads = lax.fori_loop(0, (n_far + grp - 1) // grp, far_step, (tuple(near_idx), tuple(heads)))
        for hh in range(2):
            m, l, acc, mx = heads[hh]
            _, l, acc = consume(hh, m, l, acc, mx, vidx)
            o_ref[0, i, head_rows(hh), :] = (acc / l).astype(BF16)
        return carry

    lax.fori_loop(0, nb, qtile, 0)


def _rel_bucket_table(max_rel):
    n = np.arange(max_rel)
    max_exact = REL_BUCKETS // 2
    nf = np.maximum(n, 1).astype(np.float32)
    large = max_exact + (np.log(nf / np.float32(max_exact)) / np.float32(math.log(REL_MAX_DIST / max_exact))
                         * np.float32(REL_BUCKETS - max_exact)).astype(np.int32)
    large = np.minimum(large, REL_BUCKETS - 1)
    return np.where(n < max_exact, n, large)


def _bias_vectors(rel_bias):
    blk = MOBA_BLOCK
    table = _rel_bucket_table(2 * blk)
    r = np.arange(-blk, blk)
    bt = rel_bias.T - rel_bias[REL_BUCKETS - 1][:, None]
    own = jnp.where(jnp.asarray(r >= 0)[None], bt[:, table[np.maximum(r, 0)]], NEG_INF)
    prev = bt[:, table[np.minimum(r + blk, 2 * blk - 1)]]
    return jnp.stack([own, prev], axis=1).astype(F32)


def _moba_attention(qT, k3, vT, km_pad, bias_w, *, batch, seq):
    nb = seq // MOBA_BLOCK
    blk = MOBA_BLOCK
    assert nb % FAR_GROUP == 0 and nb <= MAX_BLOCKS
    kern = functools.partial(_attn_kernel, nb=nb)
    t_spec = pl.BlockSpec((1, nb, LANES, blk), lambda b, p: (b, 0, p, 0))
    return pl.pallas_call(
        kern,
        grid=(batch, N_HEADS // 2),
        in_specs=[t_spec,
                  pl.BlockSpec((1, seq, LANES), lambda b, p: (b, 0, p)),
                  t_spec,
                  pl.BlockSpec((1, MAX_BLOCKS, LANES), lambda b, p: (b, 0, p)),
                  pl.BlockSpec((2, 2, 2 * blk), lambda b, p: (p, 0, 0))],
        out_specs=t_spec,
        out_shape=jax.ShapeDtypeStruct((batch, nb, ATTN_WIDTH, blk), BF16),
        scratch_shapes=[pltpu.VMEM((2, seq, LANES), BF16),
                        pltpu.VMEM((2, 2, blk, blk), F32),
                        pltpu.VMEM((2, LANES, blk), BF16),
                        pltpu.VMEM((2, LANES, blk), BF16),
                        pltpu.VMEM((2, FAR_GROUP * blk, blk), F32)],
        compiler_params=_cparams("arbitrary", "arbitrary"),
        name="moba_attention",
    )(qT, k3, vT, km_pad, bias_w)


def _post_kernel(x_ref, mod_ref, g1_ref, wz_ref, bz_ref, at_ref, u_ref, uh_ref, pw_ref, ps_ref,
                 wba_ref, wbp_ref, wo_ref, g2_ref, wrh_ref, wrl_ref, br_ref,
                 x1_ref, h2p_ref, route_ref, counts_ref, carry_ref, *, tm, seq):
    d = D_MODEL
    i = pl.program_id(0)
    x = x_ref[...]
    mod = mod_ref[0]
    shift1, scale1, gate1 = mod[:, 0:d], mod[:, d:2 * d], mod[:, 2 * d:3 * d]
    shift2, scale2 = mod[:, 3 * d:4 * d], mod[:, 4 * d:5 * d]

    h = _rms_mod(x, g1_ref[...], scale1, shift1)
    zg = jax.nn.sigmoid(_dot(h.astype(BF16), wz_ref[...]) + bz_ref[...])

    ya = jnp.concatenate(
        [lax.dot_general(at_ref[0, c], wba_ref[...], (((0,), (0,)), ((), ())),
                         preferred_element_type=F32) for c in range(tm // MOBA_BLOCK)], axis=0)

    first = (i * tm) % seq == 0
    halo = jnp.where(first, 0.0, uh_ref[...])
    ubuf = jnp.concatenate([halo, u_ref[...]], axis=0)
    rows = tm + POOL_HALO
    pos = (lax.broadcasted_iota(jnp.int32, (tm, POOL_GROUP_WIDTH), 0) + i * tm) % seq
    outs = []
    for gi, w in enumerate(POOL_WINDOWS):
        e = ubuf[:, gi * POOL_GROUP_WIDTH:(gi + 1) * POOL_GROUP_WIDTH]
        s = e
        step = 1
        while step < w:
            s = s + pltpu.roll(s, step, 0)
            step *= 2
        cnt = jnp.minimum(pos + 1, w).astype(F32)
        dlt = s[POOL_HALO:rows] / cnt - e[POOL_HALO:rows]
        outs.append(_dot(dlt.astype(BF16), pw_ref[gi]))
    pooled = jnp.concatenate(outs, axis=-1) * ps_ref[...]
    yp = _dot(pooled.astype(BF16), wbp_ref[...])

    merged = zg[:, 0:d] * ya + zg[:, d:2 * d] * yp
    x1 = x + gate1 * _dot(merged.astype(BF16), wo_ref[...])
    x1_ref[...] = x1

    h2 = _rms_mod(x1, g2_ref[...], scale2, shift2)
    h2_hi = h2.astype(BF16)
    h2p_ref[...] = _pack_bf16_pairs(h2_hi[:, 0:d // 2], h2_hi[:, d // 2:d])
    h2_lo = (h2 - h2_hi.astype(F32)).astype(BF16)
    z = _dot(h2_hi, wrh_ref[...]) + _dot(h2_lo, wrh_ref[...]) + _dot(h2_hi, wrl_ref[...]) + br_ref[...]

    lane = lax.broadcasted_iota(jnp.int32, (tm, LANES), 1)
    zgrp = jnp.where(lane < N_GROUPS, z, -jnp.inf)
    mg = jnp.max(zgrp, axis=-1, keepdims=True)
    pg_top = 1.0 / jnp.sum(jnp.exp(zgrp - mg), axis=-1, keepdims=True)
    g_idx = jnp.min(jnp.where(zgrp == mg, lane, LANES), axis=-1, keepdims=True)
    e_lane = lane - ROUTER_LANE0
    in_grp = (e_lane >= 0) & (e_lane < N_EXPERTS) & ((e_lane >> 3) == g_idx)
    ze = jnp.where(in_grp, z, -jnp.inf)
    m1 = jnp.max(ze, axis=-1, keepdims=True)
    i1 = jnp.min(jnp.where(ze == m1, lane, LANES), axis=-1, keepdims=True)
    ze2 = jnp.where(lane == i1, -jnp.inf, ze)
    m2 = jnp.max(ze2, axis=-1, keepdims=True)
    i2 = jnp.min(jnp.where(ze2 == m2, lane, LANES), axis=-1, keepdims=True)
    e2 = jnp.exp(m2 - m1)
    w1 = pg_top / (1.0 + e2)
    w2 = pg_top * e2 / (1.0 + e2)

    hot = jnp.where((lane == i1) | (lane == i2), 1.0, 0.0)
    tri = jnp.where(lax.broadcasted_iota(jnp.int32, (tm, tm), 1) <= lax.broadcasted_iota(jnp.int32, (tm, tm), 0),
                    1.0, 0.0).astype(BF16)
    prefix = _dot(tri, hot.astype(BF16))

    @pl.when(i == 0)
    def _():
        carry_ref[...] = jnp.zeros_like(carry_ref)

    base = carry_ref[...] + prefix - 1.0
    rank1 = jnp.sum(jnp.where(lane == i1, base, 0.0), axis=-1, keepdims=True)
    rank2 = jnp.sum(jnp.where(lane == i2, base, 0.0), axis=-1, keepdims=True)
    carry_ref[...] = carry_ref[...] + prefix[tm - 1:tm, :]
    counts_ref[...] = carry_ref[...]
    fields = ((i1 - ROUTER_LANE0).astype(F32), (i2 - ROUTER_LANE0).astype(F32), w1, w2, rank1, rank2)
    route = jnp.zeros((tm, LANES), F32)
    for col, val in enumerate(fields):
        route = jnp.where(lane == col, val, route)
    route_ref[...] = route


def _post_attention(x2, mod3, g1, w_z, b_z, attnT, u, pool_w, pool_scale, w_ba, w_bp, w_out, g2,
                    wr_hi, wr_lo, b_r, *, batch, seq):
    n, d = x2.shape
    tm = TM_POST
    tps = seq // tm
    cb = tm // MOBA_BLOCK
    hb = tm // POOL_HALO
    kern = functools.partial(_post_kernel, tm=tm, seq=seq)
    const2 = lambda i: (0, 0)
    return pl.pallas_call(
        kern,
        grid=(n // tm,),
        in_specs=[pl.BlockSpec((tm, d), lambda i: (i, 0)),
                  pl.BlockSpec((1, 1, N_MOD * d), lambda i: (i // tps, 0, 0)),
                  pl.BlockSpec((1, d), const2),
                  pl.BlockSpec((d, 2 * d), const2),
                  pl.BlockSpec((1, 2 * d), const2),
                  pl.BlockSpec((1, cb, ATTN_WIDTH, MOBA_BLOCK), lambda i: (i // tps, i % tps, 0, 0)),
                  pl.BlockSpec((tm, POOL_WIDTH), lambda i: (i, 0)),
                  pl.BlockSpec((POOL_HALO, POOL_WIDTH), lambda i: (jnp.maximum(i * hb - 1, 0), 0)),
                  pl.BlockSpec((len(POOL_WINDOWS), POOL_GROUP_WIDTH, POOL_GROUP_WIDTH), lambda i: (0, 0, 0)),
                  pl.BlockSpec((1, POOL_WIDTH), const2),
                  pl.BlockSpec((ATTN_WIDTH, d), const2),
                  pl.BlockSpec((POOL_WIDTH, d), const2),
                  pl.BlockSpec((d, d), const2),
                  pl.BlockSpec((1, d), const2),
                  pl.BlockSpec((d, LANES), const2),
                  pl.BlockSpec((d, LANES), const2),
                  pl.BlockSpec((1, LANES), const2)],
        out_specs=[pl.BlockSpec((tm, d), lambda i: (i, 0)),
                   pl.BlockSpec((tm, d // 2), lambda i: (i, 0)),
                   pl.BlockSpec((tm, LANES), lambda i: (i, 0)),
                   pl.BlockSpec((1, LANES), const2)],
        out_shape=[jax.ShapeDtypeStruct((n, d), F32),
                   jax.ShapeDtypeStruct((n, d // 2), jnp.uint32),
                   jax.ShapeDtypeStruct((n, LANES), F32),
                   jax.ShapeDtypeStruct((1, LANES), F32)],
        scratch_shapes=[pltpu.VMEM((1, LANES), F32)],
        compiler_params=_cparams("arbitrary"),
        name="merge_outproj_router",
    )(x2, mod3, g1, w_z, b_z, attnT, u, u, pool_w, pool_scale, w_ba, w_bp, w_out, g2, wr_hi, wr_lo, b_r)


def _routing_plan(route, counts, *, n):
    te = TE_ROWS
    n_tiles = _expert_tiles(n)
    cnt = counts[0, ROUTER_LANE0:ROUTER_LANE0 + N_EXPERTS].astype(jnp.int32)
    padded = ((cnt + te - 1) // te) * te
    ends = jnp.cumsum(padded)
    starts = ends - padded
    expert = route[:, 0:2].astype(jnp.int32)
    rank = route[:, 4:6].astype(jnp.int32)
    pos = jnp.take(starts, expert) + rank
    tile_expert = jnp.searchsorted(ends // te, jnp.arange(n_tiles, dtype=jnp.int32), side="right")
    tile_expert = jnp.minimum(tile_expert, N_EXPERTS - 1).astype(jnp.int32)
    n_used = (ends[N_EXPERTS - 1:] // te).astype(jnp.int32)
    return pos, tile_expert, n_used


def _expert_tiles(n):
    return (EXPERT_TOPK * n) // TE_ROWS + N_EXPERTS


def _dispatch_kernel(pos_ref, h_ref, xz_ref, xs_ref, sem, *, tm):
    del xz_ref
    base = pl.program_id(0) * tm

    def row_copy(r, k):
        return pltpu.make_async_copy(h_ref.at[pl.ds(base + r, 1)],
                                     xs_ref.at[pl.ds(pos_ref[0, 0, EXPERT_TOPK * r + k], 1)], sem)

    def issue(r, c):
        for k in range(EXPERT_TOPK):
            row_copy(r, k).start()
        return c

    def drain(r, c):
        for k in range(EXPERT_TOPK):
            row_copy(r, k).wait()
        return c

    lax.fori_loop(0, tm, issue, 0)
    lax.fori_loop(0, tm, drain, 0)


def _dispatch(pos, h2p, *, n):
    tm = TM_DISPATCH
    rows = _expert_tiles(n) * TE_ROWS
    half = h2p.shape[1]
    pos3 = pos.reshape(n // tm, 1, EXPERT_TOPK * tm)
    return pl.pallas_call(
        functools.partial(_dispatch_kernel, tm=tm),
        grid=(n // tm,),
        in_specs=[pl.BlockSpec((1, 1, EXPERT_TOPK * tm), lambda i: (i, 0, 0), memory_space=pltpu.SMEM),
                  pl.BlockSpec(memory_space=pl.ANY),
                  pl.BlockSpec(memory_space=pl.ANY)],
        out_specs=pl.BlockSpec(memory_space=pl.ANY),
        out_shape=jax.ShapeDtypeStruct((rows, half), jnp.uint32),
        scratch_shapes=[pltpu.SemaphoreType.DMA(())],
        input_output_aliases={2: 0},
        compiler_params=_cparams("arbitrary"),
        name="moe_dispatch",
    )(pos3, h2p, jnp.zeros((rows, half), jnp.uint32))


def _experts_kernel(te_ref, nu_ref, xs_ref, wg_ref, wu_ref, wd_ref, y_ref, wg_b, wu_b, wd_b):
    t = pl.program_id(0)

    @pl.when(t < nu_ref[0])
    def _():
        e = te_ref[t]
        e_prev = te_ref[jnp.maximum(t - 1, 0)]

        @pl.when((t == 0) | (e != e_prev))
        def _():
            wg_b[...] = wg_ref[0].astype(BF16)
            wu_b[...] = wu_ref[0].astype(BF16)
            wd_b[...] = wd_ref[0].astype(BF16)

        xa, xb = _unpack_bf16_pairs(xs_ref[...])
        x = jnp.concatenate([xa, xb], axis=-1)
        g = _dot(x, wg_b[...])
        u = _dot(x, wu_b[...])
        act = (g * jax.nn.sigmoid(g)) * u
        y_ref[...] = _dot(act.astype(BF16), wd_b[...])

    @pl.when(t >= nu_ref[0])
    def _():
        y_ref[...] = jnp.zeros_like(y_ref)


def _experts(tile_expert, n_used, xs, wg, wu, wd):
    rows, half = xs.shape
    d = 2 * half
    te = TE_ROWS

    def used(t, nu):
        return jnp.minimum(t, nu[0] - 1)

    grid_spec = pltpu.PrefetchScalarGridSpec(
        num_scalar_prefetch=2,
        grid=(rows // te,),
        in_specs=[pl.BlockSpec((te, half), lambda t, te_r, nu: (used(t, nu), 0)),
                  pl.BlockSpec((1, d, D_EXPERT), lambda t, te_r, nu: (te_r[used(t, nu)], 0, 0)),
                  pl.BlockSpec((1, d, D_EXPERT), lambda t, te_r, nu: (te_r[used(t, nu)], 0, 0)),
                  pl.BlockSpec((1, D_EXPERT, d), lambda t, te_r, nu: (te_r[used(t, nu)], 0, 0))],
        out_specs=pl.BlockSpec((te, d), lambda t, te_r, nu: (t, 0)),
        scratch_shapes=[pltpu.VMEM((d, D_EXPERT), BF16), pltpu.VMEM((d, D_EXPERT), BF16),
                        pltpu.VMEM((D_EXPERT, d), BF16)])
    return pl.pallas_call(
        _experts_kernel,
        grid_spec=grid_spec,
        out_shape=jax.ShapeDtypeStruct((rows, d), F32),
        compiler_params=_cparams("arbitrary"),
        name="moe_experts",
    )(tile_expert, n_used, xs, wg, wu, wd)


def _final_kernel(pos_ref, x1_ref, route_ref, mod_ref, g_ref, y_ref, o_ref, ybuf, sem, *, tm, final_norm):
    d = D_MODEL

    def row_copy(r, k):
        return pltpu.make_async_copy(y_ref.at[pl.ds(pos_ref[0, 0, EXPERT_TOPK * r + k], 1)],
                                     ybuf.at[k, pl.ds(r, 1)], sem)

    def issue(r, c):
        for k in range(EXPERT_TOPK):
            row_copy(r, k).start()
        return c

    def drain(r, c):
        for k in range(EXPERT_TOPK):
            row_copy(r, k).wait()
        return c

    lax.fori_loop(0, tm, issue, 0)
    lax.fori_loop(0, tm, drain, 0)

    route = route_ref[...]
    lane = lax.broadcasted_iota(jnp.int32, route.shape, 1)
    y = jnp.zeros((tm, d), F32)
    for k in range(EXPERT_TOPK):
        wk = jnp.sum(jnp.where(lane == 2 + k, route, 0.0), axis=-1, keepdims=True)
        y = y + wk * ybuf[k]
    gate2 = mod_ref[0][:, 5 * d:6 * d]
    x = x1_ref[...] + gate2 * y
    if final_norm:
        x = (x * lax.rsqrt(jnp.mean(x * x, axis=-1, keepdims=True) + EPS)) * g_ref[...]
    o_ref[...] = x


def _final(pos, x1, route, mod3, gf, y, *, seq, final_norm):
    n, d = x1.shape
    tm = TM_FINAL
    tps = seq // tm
    pos3 = pos.reshape(n // tm, 1, EXPERT_TOPK * tm)
    return pl.pallas_call(
        functools.partial(_final_kernel, tm=tm, final_norm=final_norm),
        grid=(n // tm,),
        in_specs=[pl.BlockSpec((1, 1, EXPERT_TOPK * tm), lambda i: (i, 0, 0), memory_space=pltpu.SMEM),
                  pl.BlockSpec((tm, d), lambda i: (i, 0)),
                  pl.BlockSpec((tm, LANES), lambda i: (i, 0)),
                  pl.BlockSpec((1, 1, N_MOD * d), lambda i: (i // tps, 0, 0)),
                  pl.BlockSpec((1, d), lambda i: (0, 0)),
                  pl.BlockSpec(memory_space=pl.ANY)],
        out_specs=pl.BlockSpec((tm, d), lambda i: (i, 0)),
        out_shape=jax.ShapeDtypeStruct((n, d), F32),
        scratch_shapes=[pltpu.VMEM((EXPERT_TOPK, tm, d), F32), pltpu.SemaphoreType.DMA(())],
        compiler_params=_cparams("arbitrary"),
        name="moe_combine_final_norm",
    )(pos3, x1, route, mod3, gf, y)


def _split_bf16(w):
    hi = w.astype(BF16)
    return hi, (w - hi.astype(F32)).astype(BF16)


def kernel(x, c, w_ada, b_ada, norm1_g, w_in, b_gate, rel_bias, pool_w, pool_scale, w_branch_attn,
           w_branch_pool, w_out, norm2_g, w_router_group, b_router_group, w_router_expert,
           b_router_expert, w_expert_gate, w_expert_up, w_expert_down, norm_f_g):
    batch, seq, d = x.shape
    depth = w_ada.shape[0]
    n = batch * seq
    nb = seq // MOBA_BLOCK
    qkvu = 3 * ATTN_WIDTH + POOL_WIDTH
    bias_w = _bias_vectors(rel_bias)
    c_pad = jnp.pad(c, ((0, 8 - batch), (0, 0)))

    xc = x.reshape(n, d)
    for l in range(depth):
        mod3 = _modulation(c_pad, w_ada[l], b_ada[l][None])[:batch].reshape(batch, 1, N_MOD * d)
        w_in_b = w_in[l].astype(BF16)
        g1 = norm1_g[l][None]

        qT, k2, vT, u, km = _in_projection(xc, mod3, g1, w_in_b[:, :qkvu], batch=batch, seq=seq)
        km_pad = jnp.pad(km.reshape(batch, nb, ATTN_WIDTH), ((0, 0), (0, MAX_BLOCKS - nb), (0, 0)))
        attnT = _moba_attention(qT, k2.reshape(batch, seq, ATTN_WIDTH), vT, km_pad, bias_w,
                                batch=batch, seq=seq)

        w_r = jnp.concatenate([w_router_group[l], w_router_expert[l]], axis=1)
        w_r = jnp.pad(w_r, ((0, 0), (0, LANES - w_r.shape[1])))
        b_r = jnp.concatenate([b_router_group[l], b_router_expert[l]])
        b_r = jnp.pad(b_r, (0, LANES - b_r.shape[0]))[None]
        wr_hi, wr_lo = _split_bf16(w_r)
        x1, h2p, route, counts = _post_attention(
            xc, mod3, g1, w_in_b[:, qkvu:], b_gate[l][None], attnT, u, pool_w[l].astype(BF16),
            pool_scale[l][None], w_branch_attn[l].astype(BF16), w_branch_pool[l].astype(BF16),
            w_out[l].astype(BF16), norm2_g[l][None], wr_hi, wr_lo, b_r, batch=batch, seq=seq)

        pos, tile_expert, n_used = _routing_plan(route, counts, n=n)
        xs = _dispatch(pos, h2p, n=n)
        y = _experts(tile_expert, n_used, xs, w_expert_gate[l], w_expert_up[l], w_expert_down[l])
        xc = _final(pos, x1, route, mod3, norm_f_g[None], y, seq=seq, final_norm=(l == depth - 1))
    return xc.reshape(batch, seq, d)
```

```python
import functools
import math

import numpy as np
import jax
import jax.numpy as jnp
from jax import lax
from jax.experimental import pallas as pl
from jax.experimental.pallas import tpu as pltpu

F32 = jnp.float32
BF16 = jnp.bfloat16

D_MODEL = 1024
N_HEADS = 8
HEAD_DIM = 64
ATTN_WIDTH = N_HEADS * HEAD_DIM
MOBA_BLOCK = 256
MOBA_TOPK = 3
MAX_BLOCKS = 32
FAR_GROUP = 4
POOL_WINDOWS = (2, 4, 8, 16)
POOL_WIDTH = 512
POOL_GROUP_WIDTH = 128
POOL_HALO = 16
REL_BUCKETS = 32
REL_MAX_DIST = 128
N_GROUPS = 4
EXPERTS_PER_GROUP = 8
N_EXPERTS = N_GROUPS * EXPERTS_PER_GROUP
D_EXPERT = 512
N_MOD = 6
EPS = 1e-6
NEG_INF = -1e30
LANES = 128
ROUTER_LANE0 = N_GROUPS
VMEM_LIMIT = 56 * 1024 * 1024

TM_PROJ = 512
TM_POST = 512
TM_DISPATCH = 512
TE_ROWS = 256
EXPERT_TOPK = 2
TM_FINAL = 256


def _cparams(*sem):
    return pltpu.CompilerParams(dimension_semantics=sem, vmem_limit_bytes=VMEM_LIMIT)


def _dot(a, b):
    return jnp.dot(a, b, preferred_element_type=F32)


def _rms_mod(x, g, scale, shift):
    xn = x * lax.rsqrt(jnp.mean(x * x, axis=-1, keepdims=True) + EPS)
    return (xn * g) * (1.0 + scale) + shift


def _pack_bf16_pairs(a, b):
    abits = lax.bitcast_convert_type(a.astype(F32), jnp.uint32)
    bbits = lax.bitcast_convert_type(b.astype(F32), jnp.uint32)
    return (abits >> 16) | (bbits & jnp.uint32(0xFFFF0000))


def _unpack_bf16_pairs(p):
    a = lax.bitcast_convert_type(p << 16, F32)
    b = lax.bitcast_convert_type(p & jnp.uint32(0xFFFF0000), F32)
    return a.astype(BF16), b.astype(BF16)


def _mod_kernel(c_ref, w_ref, b_ref, o_ref):
    c = c_ref[...]
    ca = c * jax.nn.sigmoid(c)
    o_ref[...] = _dot(ca.astype(BF16), w_ref[...].astype(BF16)) + b_ref[...]


def _modulation(c_pad, w_ada, b_ada):
    rows, d = c_pad.shape
    n_out = w_ada.shape[1]
    tn = 1024
    return pl.pallas_call(
        _mod_kernel,
        grid=(n_out // tn,),
        in_specs=[pl.BlockSpec((rows, d), lambda j: (0, 0)),
                  pl.BlockSpec((d, tn), lambda j: (0, j)),
                  pl.BlockSpec((1, tn), lambda j: (0, j))],
        out_specs=pl.BlockSpec((rows, tn), lambda j: (0, j)),
        out_shape=jax.ShapeDtypeStruct((rows, n_out), F32),
        compiler_params=_cparams("arbitrary"),
        name="adaln_mod",
    )(c_pad, w_ada, b_ada)


def _inproj_kernel(x_ref, mod_ref, g_ref, w_ref, qT_ref, k_ref, vT_ref, u_ref, km_ref, *, tm):
    d = D_MODEL
    mod = mod_ref[0]
    h = _rms_mod(x_ref[...], g_ref[...], mod[:, d:2 * d], mod[:, 0:d])
    proj = _dot(h.astype(BF16), w_ref[...])
    aw = ATTN_WIDTH
    q = proj[:, 0:aw] * (HEAD_DIM ** -0.5)
    k = proj[:, aw:2 * aw]
    v = proj[:, 2 * aw:3 * aw]
    u_ref[...] = proj[:, 3 * aw:3 * aw + POOL_WIDTH]
    k_ref[...] = k.astype(BF16)
    qT = q.T.astype(BF16)
    vT = v.T.astype(BF16)
    for c in range(tm // MOBA_BLOCK):
        sl = slice(c * MOBA_BLOCK, (c + 1) * MOBA_BLOCK)
        qT_ref[0, c] = qT[:, sl]
        vT_ref[0, c] = vT[:, sl]
        km_ref[0, c:c + 1, :] = jnp.mean(k[sl, :], axis=0, keepdims=True)


def _in_projection(x2, mod3, g1, w_qkvu, *, batch, seq):
    n, d = x2.shape
    tm = TM_PROJ
    tps = seq // tm
    nb = seq // MOBA_BLOCK
    cb = tm // MOBA_BLOCK
    wcols = w_qkvu.shape[1]
    kern = functools.partial(_inproj_kernel, tm=tm)
    t_shape = jax.ShapeDtypeStruct((batch, nb, ATTN_WIDTH, MOBA_BLOCK), BF16)
    t_spec = pl.BlockSpec((1, cb, ATTN_WIDTH, MOBA_BLOCK), lambda i: (i // tps, i % tps, 0, 0))
    return pl.pallas_call(
        kern,
        grid=(n // tm,),
        in_specs=[pl.BlockSpec((tm, d), lambda i: (i, 0)),
                  pl.BlockSpec((1, 1, N_MOD * d), lambda i: (i // tps, 0, 0)),
                  pl.BlockSpec((1, d), lambda i: (0, 0)),
                  pl.BlockSpec((d, wcols), lambda i: (0, 0))],
        out_specs=[t_spec,
                   pl.BlockSpec((tm, ATTN_WIDTH), lambda i: (i, 0)),
                   t_spec,
                   pl.BlockSpec((tm, POOL_WIDTH), lambda i: (i, 0)),
                   pl.BlockSpec((1, cb, ATTN_WIDTH), lambda i: (i, 0, 0))],
        out_shape=[t_shape,
                   jax.ShapeDtypeStruct((n, ATTN_WIDTH), BF16),
                   t_shape,
                   jax.ShapeDtypeStruct((n, POOL_WIDTH), F32),
                   jax.ShapeDtypeStruct((n // tm, cb, ATTN_WIDTH), F32)],
        compiler_params=_cparams("arbitrary"),
        name="norm1_inproj",
    )(x2, mod3, g1, w_qkvu)


def _attn_kernel(qT_ref, k_ref, vT_ref, km_ref, bw_ref, o_ref, kaug_ref, bias_ref, qnear_ref, qfar_ref,
                 s_ref, *, nb):
    blk = MOBA_BLOCK
    hd = HEAD_DIM
    lane = lax.broadcasted_iota(jnp.int32, (blk, LANES), 1)

    def build_kaug(j, carry):
        rows = pl.ds(pl.multiple_of(j * blk, blk), blk)
        kp = k_ref[0, rows, :]
        kaug_ref[0, rows, :] = jnp.where(lane < hd, kp, jnp.where(lane == hd + j, 1.0, 0.0).astype(BF16))
        kaug_ref[1, rows, :] = jnp.where(lane >= hd, kp, jnp.where(lane == j, 1.0, 0.0).astype(BF16))
        return carry

    lax.fori_loop(0, nb, build_kaug, 0)

    for hh in range(2):
        for t in range(2):
            wide = jnp.broadcast_to(bw_ref[hh, t:t + 1, :], (blk, 2 * blk))
            bias_ref[hh, t] = pltpu.roll(wide, 0, 1, stride=1, stride_axis=0)[:, blk:2 * blk]

    rowi = lax.broadcasted_iota(jnp.int32, (MAX_BLOCKS, blk), 0)
    km_lane = lax.broadcasted_iota(jnp.int32, (MAX_BLOCKS, LANES), 1)
    zpad = jnp.zeros((LANES - hd - MAX_BLOCKS, blk), BF16)
    km_split = []
    for hh in range(2):
        in_head = (km_lane >= hh * hd) & (km_lane < (hh + 1) * hd)
        km_h = jnp.where(in_head, km_ref[0], 0.0)
        km_hi = km_h.astype(BF16)
        km_split.append((km_hi, (km_h - km_hi.astype(F32)).astype(BF16)))

    def head_rows(hh):
        return slice(hh * hd, (hh + 1) * hd)

    grp = FAR_GROUP

    def consume(hh, m, l, acc, mx, vidx):
        m_new = jnp.maximum(m, mx)
        alpha = jnp.exp(m - m_new)
        p = jnp.exp(s_ref[hh] - m_new)
        vcat = jnp.concatenate([vT_ref[0, j, head_rows(hh), :] for j in vidx], axis=1)
        l = alpha * l + jnp.sum(p, axis=0, keepdims=True)
        return m_new, l, alpha * acc + _dot(vcat, p.astype(BF16))

    def qtile(i, carry):
        qp = qT_ref[0, i]
        near_idx = [jnp.where(i - (grp - 1) + g < 0, i + 1, i - (grp - 1) + g) for g in range(grp)]
        for hh in range(2):
            km_hi, km_lo = km_split[hh]
            gate = _dot(km_hi, qp) + _dot(km_lo, qp)
            g = jnp.where(rowi < i, gate, -jnp.inf)
            sel = rowi == i
            for _ in range(MOBA_TOPK):
                m = jnp.max(g, axis=0, keepdims=True)
                hit = (g == m) & (m > -jnp.inf)
                idx = jnp.min(jnp.where(hit, rowi, MAX_BLOCKS), axis=0, keepdims=True)
                pick = rowi == idx
                sel = sel | pick
                g = jnp.where(pick, -jnp.inf, g)
            near = jnp.where(sel, 0.0, NEG_INF)
            far = jnp.where(rowi > i - grp, NEG_INF, near)
            qh = qp[head_rows(hh), :]
            if hh == 0:
                qnear_ref[hh] = jnp.concatenate([qh, near.astype(BF16), zpad], axis=0)
                qfar_ref[hh] = jnp.concatenate([qh, far.astype(BF16), zpad], axis=0)
            else:
                qnear_ref[hh] = jnp.concatenate([near.astype(BF16), zpad, qh], axis=0)
                qfar_ref[hh] = jnp.concatenate([far.astype(BF16), zpad, qh], axis=0)

        heads = []
        for hh in range(2):
            mx = None
            for g in range(grp):
                rows = pl.ds(pl.multiple_of(near_idx[g] * blk, blk), blk)
                s = _dot(kaug_ref[hh, rows, :], qnear_ref[hh])
                if g == grp - 1:
                    s = s + bias_ref[hh, 0]
                elif g == grp - 2:
                    s = s + bias_ref[hh, 1]
                s_ref[hh, g * blk:(g + 1) * blk, :] = s
                cm = jnp.max(s, axis=0, keepdims=True)
                mx = cm if mx is None else jnp.maximum(mx, cm)
            heads.append((jnp.full((1, blk), -jnp.inf, F32), jnp.zeros((1, blk), F32),
                          jnp.zeros((hd, blk), F32), mx))

        def far_step(c, st):
            vidx, hs = st
            rows = pl.ds(pl.multiple_of(c * (grp * blk), grp * blk), grp * blk)
            new = []
            for hh in range(2):
                m, l, acc, mx = hs[hh]
                m, l, acc = consume(hh, m, l, acc, mx, vidx)
                s = _dot(kaug_ref[hh, rows, :], qfar_ref[hh])
                s_ref[hh] = s
                new.append((m, l, acc, jnp.max(s, axis=0, keepdims=True)))
            return tuple(c * grp + g for g in range(grp)), tuple(new)

        n_far = jnp.maximum(i - (grp - 1), 0)
        vidx, heads = lax.fori_loop(0, (n_far + grp - 1) // grp, far_step, (tuple(near_idx), tuple(heads)))
        for hh in range(2):
            m, l, acc, mx = heads[hh]
            _, l, acc = consume(hh, m, l, acc, mx, vidx)
            o_ref[0, i, head_rows(hh), :] = (acc / l).astype(BF16)
        return carry

    lax.fori_loop(0, nb, qtile, 0)


def _rel_bucket_table(max_rel):
    n = np.arange(max_rel)
    max_exact = REL_BUCKETS // 2
    nf = np.maximum(n, 1).astype(np.float32)
    large = max_exact + (np.log(nf / np.float32(max_exact)) / np.float32(math.log(REL_MAX_DIST / max_exact))
                         * np.float32(REL_BUCKETS - max_exact)).astype(np.int32)
    large = np.minimum(large, REL_BUCKETS - 1)
    return np.where(n < max_exact, n, large)


def _bias_vectors(rel_bias):
    blk = MOBA_BLOCK
    table = _rel_bucket_table(2 * blk)
    r = np.arange(-blk, blk)
    bt = rel_bias.T - rel_bias[REL_BUCKETS - 1][:, None]
    own = jnp.where(jnp.asarray(r >= 0)[None], bt[:, table[np.maximum(r, 0)]], NEG_INF)
    prev = bt[:, table[np.minimum(r + blk, 2 * blk - 1)]]
    return jnp.stack([own, prev], axis=1).astype(F32)


def _moba_attention(qT, k3, vT, km_pad, bias_w, *, batch, seq):
    nb = seq // MOBA_BLOCK
    blk = MOBA_BLOCK
    assert nb % FAR_GROUP == 0 and nb <= MAX_BLOCKS
    kern = functools.partial(_attn_kernel, nb=nb)
    t_spec = pl.BlockSpec((1, nb, LANES, blk), lambda b, p: (b, 0, p, 0))
    return pl.pallas_call(
        kern,
        grid=(batch, N_HEADS // 2),
        in_specs=[t_spec,
                  pl.BlockSpec((1, seq, LANES), lambda b, p: (b, 0, p)),
                  t_spec,
                  pl.BlockSpec((1, MAX_BLOCKS, LANES), lambda b, p: (b, 0, p)),
                  pl.BlockSpec((2, 2, 2 * blk), lambda b, p: (p, 0, 0))],
        out_specs=t_spec,
        out_shape=jax.ShapeDtypeStruct((batch, nb, ATTN_WIDTH, blk), BF16),
        scratch_shapes=[pltpu.VMEM((2, seq, LANES), BF16),
                        pltpu.VMEM((2, 2, blk, blk), F32),
                        pltpu.VMEM((2, LANES, blk), BF16),
                        pltpu.VMEM((2, LANES, blk), BF16),
                        pltpu.VMEM((2, FAR_GROUP * blk, blk), F32)],
        compiler_params=_cparams("arbitrary", "arbitrary"),
        name="moba_attention",
    )(qT, k3, vT, km_pad, bias_w)


def _post_kernel(x_ref, mod_ref, g1_ref, wz_ref, bz_ref, at_ref, u_ref, uh_ref, pw_ref, ps_ref,
                 wba_ref, wbp_ref, wo_ref, g2_ref, wrh_ref, wrl_ref, br_ref,
                 x1_ref, h2p_ref, route_ref, counts_ref, carry_ref, *, tm, seq):
    d = D_MODEL
    i = pl.program_id(0)
    x = x_ref[...]
    mod = mod_ref[0]
    shift1, scale1, gate1 = mod[:, 0:d], mod[:, d:2 * d], mod[:, 2 * d:3 * d]
    shift2, scale2 = mod[:, 3 * d:4 * d], mod[:, 4 * d:5 * d]

    h = _rms_mod(x, g1_ref[...], scale1, shift1)
    zg = jax.nn.sigmoid(_dot(h.astype(BF16), wz_ref[...]) + bz_ref[...])

    ya = jnp.concatenate(
        [lax.dot_general(at_ref[0, c], wba_ref[...], (((0,), (0,)), ((), ())),
                         preferred_element_type=F32) for c in range(tm // MOBA_BLOCK)], axis=0)

    first = (i * tm) % seq == 0
    halo = jnp.where(first, 0.0, uh_ref[...])
    ubuf = jnp.concatenate([halo, u_ref[...]], axis=0)
    rows = tm + POOL_HALO
    pos = (lax.broadcasted_iota(jnp.int32, (tm, POOL_GROUP_WIDTH), 0) + i * tm) % seq
    outs = []
    for gi, w in enumerate(POOL_WINDOWS):
        e = ubuf[:, gi * POOL_GROUP_WIDTH:(gi + 1) * POOL_GROUP_WIDTH]
        s = e
        step = 1
        while step < w:
            s = s + pltpu.roll(s, step, 0)
            step *= 2
        cnt = jnp.minimum(pos + 1, w).astype(F32)
        dlt = s[POOL_HALO:rows] / cnt - e[POOL_HALO:rows]
        outs.append(_dot(dlt.astype(BF16), pw_ref[gi]))
    pooled = jnp.concatenate(outs, axis=-1) * ps_ref[...]
    yp = _dot(pooled.astype(BF16), wbp_ref[...])

    merged = zg[:, 0:d] * ya + zg[:, d:2 * d] * yp
    x1 = x + gate1 * _dot(merged.astype(BF16), wo_ref[...])
    x1_ref[...] = x1

    h2 = _rms_mod(x1, g2_ref[...], scale2, shift2)
    h2_hi = h2.astype(BF16)
    h2p_ref[...] = _pack_bf16_pairs(h2_hi[:, 0:d // 2], h2_hi[:, d // 2:d])
    h2_lo = (h2 - h2_hi.astype(F32)).astype(BF16)
    z = _dot(h2_hi, wrh_ref[...]) + _dot(h2_lo, wrh_ref[...]) + _dot(h2_hi, wrl_ref[...]) + br_ref[...]

    lane = lax.broadcasted_iota(jnp.int32, (tm, LANES), 1)
    zgrp = jnp.where(lane < N_GROUPS, z, -jnp.inf)
    mg = jnp.max(zgrp, axis=-1, keepdims=True)
    pg_top = 1.0 / jnp.sum(jnp.exp(zgrp - mg), axis=-1, keepdims=True)
    g_idx = jnp.min(jnp.where(zgrp == mg, lane, LANES), axis=-1, keepdims=True)
    e_lane = lane - ROUTER_LANE0
    in_grp = (e_lane >= 0) & (e_lane < N_EXPERTS) & ((e_lane >> 3) == g_idx)
    ze = jnp.where(in_grp, z, -jnp.inf)
    m1 = jnp.max(ze, axis=-1, keepdims=True)
    i1 = jnp.min(jnp.where(ze == m1, lane, LANES), axis=-1, keepdims=True)
    ze2 = jnp.where(lane == i1, -jnp.inf, ze)
    m2 = jnp.max(ze2, axis=-1, keepdims=True)
    i2 = jnp.min(jnp.where(ze2 == m2, lane, LANES), axis=-1, keepdims=True)
    e2 = jnp.exp(m2 - m1)
    w1 = pg_top / (1.0 + e2)
    w2 = pg_top * e2 / (1.0 + e2)

    hot = jnp.where((lane == i1) | (lane == i2), 1.0, 0.0)
    tri = jnp.where(lax.broadcasted_iota(jnp.int32, (tm, tm), 1) <= lax.broadcasted_iota(jnp.int32, (tm, tm), 0),
                    1.0, 0.0).astype(BF16)
    prefix = _dot(tri, hot.astype(BF16))

    @pl.when(i == 0)
    def _():
        carry_ref[...] = jnp.zeros_like(carry_ref)

    base = carry_ref[...] + prefix - 1.0
    rank1 = jnp.sum(jnp.where(lane == i1, base, 0.0), axis=-1, keepdims=True)
    rank2 = jnp.sum(jnp.where(lane == i2, base, 0.0), axis=-1, keepdims=True)
    carry_ref[...] = carry_ref[...] + prefix[tm - 1:tm, :]
    counts_ref[...] = carry_ref[...]
    fields = ((i1 - ROUTER_LANE0).astype(F32), (i2 - ROUTER_LANE0).astype(F32), w1, w2, rank1, rank2)
    route = jnp.zeros((tm, LANES), F32)
    for col, val in enumerate(fields):
        route = jnp.where(lane == col, val, route)
    route_ref[...] = route


def _post_attention(x2, mod3, g1, w_z, b_z, attnT, u, pool_w, pool_scale, w_ba, w_bp, w_out, g2,
                    wr_hi, wr_lo, b_r, *, batch, seq):
    n, d = x2.shape
    tm = TM_POST
    tps = seq // tm
    cb = tm // MOBA_BLOCK
    hb = tm // POOL_HALO
    kern = functools.partial(_post_kernel, tm=tm, seq=seq)
    const2 = lambda i: (0, 0)
    return pl.pallas_call(
        kern,
        grid=(n // tm,),
        in_specs=[pl.BlockSpec((tm, d), lambda i: (i, 0)),
                  pl.BlockSpec((1, 1, N_MOD * d), lambda i: (i // tps, 0, 0)),
                  pl.BlockSpec((1, d), const2),
                  pl.BlockSpec((d, 2 * d), const2),
                  pl.BlockSpec((1, 2 * d), const2),
                  pl.BlockSpec((1, cb, ATTN_WIDTH, MOBA_BLOCK), lambda i: (i // tps, i % tps, 0, 0)),
                  pl.BlockSpec((tm, POOL_WIDTH), lambda i: (i, 0)),
                  pl.BlockSpec((POOL_HALO, POOL_WIDTH), lambda i: (jnp.maximum(i * hb - 1, 0), 0)),
                  pl.BlockSpec((len(POOL_WINDOWS), POOL_GROUP_WIDTH, POOL_GROUP_WIDTH), lambda i: (0, 0, 0)),
                  pl.BlockSpec((1, POOL_WIDTH), const2),
                  pl.BlockSpec((ATTN_WIDTH, d), const2),
                  pl.BlockSpec((POOL_WIDTH, d), const2),
                  pl.BlockSpec((d, d), const2),
                  pl.BlockSpec((1, d), const2),
                  pl.BlockSpec((d, LANES), const2),
                  pl.BlockSpec((d, LANES), const2),
                  pl.BlockSpec((1, LANES), const2)],
        out_specs=[pl.BlockSpec((tm, d), lambda i: (i, 0)),
                   pl.BlockSpec((tm, d // 2), lambda i: (i, 0)),
                   pl.BlockSpec((tm, LANES), lambda i: (i, 0)),
                   pl.BlockSpec((1, LANES), const2)],
        out_shape=[jax.ShapeDtypeStruct((n, d), F32),
                   jax.ShapeDtypeStruct((n, d // 2), jnp.uint32),
                   jax.ShapeDtypeStruct((n, LANES), F32),
                   jax.ShapeDtypeStruct((1, LANES), F32)],
        scratch_shapes=[pltpu.VMEM((1, LANES), F32)],
        compiler_params=_cparams("arbitrary"),
        name="merge_outproj_router",
    )(x2, mod3, g1, w_z, b_z, attnT, u, u, pool_w, pool_scale, w_ba, w_bp, w_out, g2, wr_hi, wr_lo, b_r)


def _routing_plan(route, counts, *, n):
    te = TE_ROWS
    n_tiles = _expert_tiles(n)
    cnt = counts[0, ROUTER_LANE0:ROUTER_LANE0 + N_EXPERTS].astype(jnp.int32)
    padded = ((cnt + te - 1) // te) * te
    ids = jnp.arange(N_EXPERTS, dtype=jnp.int32)
    ends = jnp.sum(jnp.where(ids[None, :] <= ids[:, None], padded[None, :], 0), axis=-1)
    starts = ends - padded
    expert = route[:, 0:2].astype(jnp.int32)
    rank = route[:, 4:6].astype(jnp.int32)
    pos = jnp.sum(jnp.where(expert[:, :, None] == ids, starts, 0), axis=-1) + rank
    tiles = jnp.arange(n_tiles, dtype=jnp.int32)
    tile_expert = jnp.sum((ends // te)[None, :] <= tiles[:, None], axis=-1)
    tile_expert = jnp.minimum(tile_expert, N_EXPERTS - 1).astype(jnp.int32)
    n_used = (ends[N_EXPERTS - 1:] // te).astype(jnp.int32)
    return pos, tile_expert, n_used


def _expert_tiles(n):
    return (EXPERT_TOPK * n) // TE_ROWS + N_EXPERTS


def _dispatch_kernel(pos_ref, h_ref, xz_ref, xs_ref, sem, *, tm):
    del xz_ref
    for r in range(tm):
        for k in range(EXPERT_TOPK):
            j = EXPERT_TOPK * r + k
            pltpu.make_async_copy(h_ref.at[pl.ds(r, 1)], xs_ref.at[pl.ds(pos_ref[0, 0, j], 1)],
                                  sem).start(priority=j % 2)
    for k in range(EXPERT_TOPK):
        pltpu.make_async_copy(h_ref, h_ref, sem).wait()


def _dispatch(pos, h2p, *, n):
    tm = TM_DISPATCH
    rows = _expert_tiles(n) * TE_ROWS
    half = h2p.shape[1]
    pos3 = pos.reshape(n // tm, 1, EXPERT_TOPK * tm)
    return pl.pallas_call(
        functools.partial(_dispatch_kernel, tm=tm),
        grid=(n // tm,),
        in_specs=[pl.BlockSpec((1, 1, EXPERT_TOPK * tm), lambda i: (i, 0, 0), memory_space=pltpu.SMEM),
                  pl.BlockSpec((tm, half), lambda i: (i, 0)),
                  pl.BlockSpec(memory_space=pl.ANY)],
        out_specs=pl.BlockSpec(memory_space=pl.ANY),
        out_shape=jax.ShapeDtypeStruct((rows, half), jnp.uint32),
        scratch_shapes=[pltpu.SemaphoreType.DMA(())],
        input_output_aliases={2: 0},
        compiler_params=_cparams("arbitrary"),
        name="moe_dispatch",
    )(pos3, h2p, jnp.zeros((rows, half), jnp.uint32))


def _experts_kernel(te_ref, nu_ref, xs_ref, wg_ref, wu_ref, wd_ref, y_ref, wg_b, wu_b, wd_b):
    t = pl.program_id(0)

    @pl.when(t < nu_ref[0])
    def _():
        e = te_ref[t]
        e_prev = te_ref[jnp.maximum(t - 1, 0)]

        @pl.when((t == 0) | (e != e_prev))
        def _():
            wg_b[...] = wg_ref[0].astype(BF16)
            wu_b[...] = wu_ref[0].astype(BF16)
            wd_b[...] = wd_ref[0].astype(BF16)

        xa, xb = _unpack_bf16_pairs(xs_ref[...])
        x = jnp.concatenate([xa, xb], axis=-1)
        g = _dot(x, wg_b[...])
        u = _dot(x, wu_b[...])
        act = (g * jax.nn.sigmoid(g)) * u
        y_ref[...] = _dot(act.astype(BF16), wd_b[...])

    @pl.when(t >= nu_ref[0])
    def _():
        y_ref[...] = jnp.zeros_like(y_ref)


def _experts(tile_expert, n_used, xs, wg, wu, wd):
    rows, half = xs.shape
    d = 2 * half
    te = TE_ROWS

    def used(t, nu):
        return jnp.minimum(t, nu[0] - 1)

    grid_spec = pltpu.PrefetchScalarGridSpec(
        num_scalar_prefetch=2,
        grid=(rows // te,),
        in_specs=[pl.BlockSpec((te, half), lambda t, te_r, nu: (used(t, nu), 0)),
                  pl.BlockSpec((1, d, D_EXPERT), lambda t, te_r, nu: (te_r[used(t, nu)], 0, 0)),
                  pl.BlockSpec((1, d, D_EXPERT), lambda t, te_r, nu: (te_r[used(t, nu)], 0, 0)),
                  pl.BlockSpec((1, D_EXPERT, d), lambda t, te_r, nu: (te_r[used(t, nu)], 0, 0))],
        out_specs=pl.BlockSpec((te, d), lambda t, te_r, nu: (t, 0)),
        scratch_shapes=[pltpu.VMEM((d, D_EXPERT), BF16), pltpu.VMEM((d, D_EXPERT), BF16),
                        pltpu.VMEM((D_EXPERT, d), BF16)])
    return pl.pallas_call(
        _experts_kernel,
        grid_spec=grid_spec,
        out_shape=jax.ShapeDtypeStruct((rows, d), F32),
        compiler_params=_cparams("arbitrary"),
        name="moe_experts",
    )(tile_expert, n_used, xs, wg, wu, wd)


def _final_kernel(pos_ref, x1_ref, route_ref, mod_ref, g_ref, y_ref, o_ref, ybuf, sem, *, tm, final_norm):
    d = D_MODEL

    for r in range(tm):
        for k in range(EXPERT_TOPK):
            j = EXPERT_TOPK * r + k
            pltpu.make_async_copy(y_ref.at[pl.ds(pos_ref[0, 0, j], 1)], ybuf.at[k, pl.ds(r, 1)],
                                  sem).start(priority=j % 2)
    for k in range(EXPERT_TOPK):
        pltpu.make_async_copy(ybuf.at[k], ybuf.at[k], sem).wait()

    route = route_ref[...]
    lane = lax.broadcasted_iota(jnp.int32, route.shape, 1)
    y = jnp.zeros((tm, d), F32)
    for k in range(EXPERT_TOPK):
        wk = jnp.sum(jnp.where(lane == 2 + k, route, 0.0), axis=-1, keepdims=True)
        y = y + wk * ybuf[k]
    gate2 = mod_ref[0][:, 5 * d:6 * d]
    x = x1_ref[...] + gate2 * y
    if final_norm:
        x = (x * lax.rsqrt(jnp.mean(x * x, axis=-1, keepdims=True) + EPS)) * g_ref[...]
    o_ref[...] = x


def _final(pos, x1, route, mod3, gf, y, *, seq, final_norm):
    n, d = x1.shape
    tm = TM_FINAL
    tps = seq // tm
    pos3 = pos.reshape(n // tm, 1, EXPERT_TOPK * tm)
    return pl.pallas_call(
        functools.partial(_final_kernel, tm=tm, final_norm=final_norm),
        grid=(n // tm,),
        in_specs=[pl.BlockSpec((1, 1, EXPERT_TOPK * tm), lambda i: (i, 0, 0), memory_space=pltpu.SMEM),
                  pl.BlockSpec((tm, d), lambda i: (i, 0)),
                  pl.BlockSpec((tm, LANES), lambda i: (i, 0)),
                  pl.BlockSpec((1, 1, N_MOD * d), lambda i: (i // tps, 0, 0)),
                  pl.BlockSpec((1, d), lambda i: (0, 0)),
                  pl.BlockSpec(memory_space=pl.ANY)],
        out_specs=pl.BlockSpec((tm, d), lambda i: (i, 0)),
        out_shape=jax.ShapeDtypeStruct((n, d), F32),
        scratch_shapes=[pltpu.VMEM((EXPERT_TOPK, tm, d), F32), pltpu.SemaphoreType.DMA(())],
        compiler_params=_cparams("arbitrary"),
        name="moe_combine_final_norm",
    )(pos3, x1, route, mod3, gf, y)


def _split_bf16(w):
    hi = w.astype(BF16)
    return hi, (w - hi.astype(F32)).astype(BF16)


def kernel(x, c, w_ada, b_ada, norm1_g, w_in, b_gate, rel_bias, pool_w, pool_scale, w_branch_attn,
           w_branch_pool, w_out, norm2_g, w_router_group, b_router_group, w_router_expert,
           b_router_expert, w_expert_gate, w_expert_up, w_expert_down, norm_f_g):
    batch, seq, d = x.shape
    depth = w_ada.shape[0]
    n = batch * seq
    nb = seq // MOBA_BLOCK
    qkvu = 3 * ATTN_WIDTH + POOL_WIDTH
    bias_w = _bias_vectors(rel_bias)
    c_pad = jnp.pad(c, ((0, 8 - batch), (0, 0)))

    xc = x.reshape(n, d)
    for l in range(depth):
        mod3 = _modulation(c_pad, w_ada[l], b_ada[l][None])[:batch].reshape(batch, 1, N_MOD * d)
        w_in_b = w_in[l].astype(BF16)
        g1 = norm1_g[l][None]

        qT, k2, vT, u, km = _in_projection(xc, mod3, g1, w_in_b[:, :qkvu], batch=batch, seq=seq)
        km_pad = jnp.pad(km.reshape(batch, nb, ATTN_WIDTH), ((0, 0), (0, MAX_BLOCKS - nb), (0, 0)))
        attnT = _moba_attention(qT, k2.reshape(batch, seq, ATTN_WIDTH), vT, km_pad, bias_w,
                                batch=batch, seq=seq)

        w_r = jnp.concatenate([w_router_group[l], w_router_expert[l]], axis=1)
        w_r = jnp.pad(w_r, ((0, 0), (0, LANES - w_r.shape[1])))
        b_r = jnp.concatenate([b_router_group[l], b_router_expert[l]])
        b_r = jnp.pad(b_r, (0, LANES - b_r.shape[0]))[None]
        wr_hi, wr_lo = _split_bf16(w_r)
        x1, h2p, route, counts = _post_attention(
            xc, mod3, g1, w_in_b[:, qkvu:], b_gate[l][None], attnT, u, pool_w[l].astype(BF16),
            pool_scale[l][None], w_branch_attn[l].astype(BF16), w_branch_pool[l].astype(BF16),
            w_out[l].astype(BF16), norm2_g[l][None], wr_hi, wr_lo, b_r, batch=batch, seq=seq)

        pos, tile_expert, n_used = _routing_plan(route, counts, n=n)
        xs = _dispatch(pos, h2p, n=n)
        y = _experts(tile_expert, n_used, xs, w_expert_gate[l], w_expert_up[l], w_expert_down[l])
        xc = _final(pos, x1, route, mod3, norm_f_g[None], y, seq=seq, final_norm=(l == depth - 1))
    return xc.reshape(batch, seq, d)
```

```python
import functools
import math

import numpy as np
import jax
import jax.numpy as jnp
from jax import lax
from jax.experimental import pallas as pl
from jax.experimental.pallas import tpu as pltpu

F32 = jnp.float32
BF16 = jnp.bfloat16

D_MODEL = 1024
N_HEADS = 8
HEAD_DIM = 64
ATTN_WIDTH = N_HEADS * HEAD_DIM
MOBA_BLOCK = 256
MOBA_TOPK = 3
MAX_BLOCKS = 32
FAR_GROUP = 4
ONES_ROWS = 16
LOG2E = math.log2(math.e)
POOL_WINDOWS = (2, 4, 8, 16)
POOL_WIDTH = 512
POOL_GROUP_WIDTH = 128
POOL_HALO = 16
REL_BUCKETS = 32
REL_MAX_DIST = 128
N_GROUPS = 4
EXPERTS_PER_GROUP = 8
N_EXPERTS = N_GROUPS * EXPERTS_PER_GROUP
D_EXPERT = 512
N_MOD = 6
EPS = 1e-6
NEG_INF = -1e30
LANES = 128
ROUTER_LANE0 = N_GROUPS
VMEM_LIMIT = 56 * 1024 * 1024

TM_PROJ = 512
TM_POST = 512
TM_DISPATCH = 512
TE_ROWS = 256
EXPERT_TOPK = 2
TM_FINAL = 256


def _cparams(*sem):
    return pltpu.CompilerParams(dimension_semantics=sem, vmem_limit_bytes=VMEM_LIMIT)


def _dot(a, b):
    return jnp.dot(a, b, preferred_element_type=F32)


def _rms_mod(x, g, scale, shift):
    xn = x * lax.rsqrt(jnp.mean(x * x, axis=-1, keepdims=True) + EPS)
    return (xn * g) * (1.0 + scale) + shift


def _pack_bf16_pairs(a, b):
    abits = lax.bitcast_convert_type(a.astype(F32), jnp.uint32)
    bbits = lax.bitcast_convert_type(b.astype(F32), jnp.uint32)
    return (abits >> 16) | (bbits & jnp.uint32(0xFFFF0000))


def _unpack_bf16_pairs(p):
    a = lax.bitcast_convert_type(p << 16, F32)
    b = lax.bitcast_convert_type(p & jnp.uint32(0xFFFF0000), F32)
    return a.astype(BF16), b.astype(BF16)


def _mod_kernel(c_ref, w_ref, b_ref, o_ref):
    c = c_ref[...]
    ca = c * jax.nn.sigmoid(c)
    o_ref[...] = _dot(ca.astype(BF16), w_ref[...].astype(BF16)) + b_ref[...]


def _modulation(c_pad, w_ada, b_ada):
    rows, d = c_pad.shape
    n_out = w_ada.shape[1]
    tn = 1024
    return pl.pallas_call(
        _mod_kernel,
        grid=(n_out // tn,),
        in_specs=[pl.BlockSpec((rows, d), lambda j: (0, 0)),
                  pl.BlockSpec((d, tn), lambda j: (0, j)),
                  pl.BlockSpec((1, tn), lambda j: (0, j))],
        out_specs=pl.BlockSpec((rows, tn), lambda j: (0, j)),
        out_shape=jax.ShapeDtypeStruct((rows, n_out), F32),
        compiler_params=_cparams("arbitrary"),
        name="adaln_mod",
    )(c_pad, w_ada, b_ada)


def _inproj_kernel(x_ref, mod_ref, g_ref, w_ref, qT_ref, k_ref, vT_ref, u_ref, km_ref, *, tm):
    d = D_MODEL
    mod = mod_ref[0]
    h = _rms_mod(x_ref[...], g_ref[...], mod[:, d:2 * d], mod[:, 0:d])
    proj = _dot(h.astype(BF16), w_ref[...])
    aw = ATTN_WIDTH
    q = proj[:, 0:aw] * (HEAD_DIM ** -0.5 * LOG2E)
    k = proj[:, aw:2 * aw]
    v = proj[:, 2 * aw:3 * aw]
    u_ref[...] = proj[:, 3 * aw:3 * aw + POOL_WIDTH]
    k_ref[...] = k.astype(BF16)
    qT = q.T.astype(BF16)
    vT = v.T.astype(BF16)
    for c in range(tm // MOBA_BLOCK):
        sl = slice(c * MOBA_BLOCK, (c + 1) * MOBA_BLOCK)
        qT_ref[0, c] = qT[:, sl]
        vT_ref[0, c] = vT[:, sl]
        km_ref[0, c:c + 1, :] = jnp.mean(k[sl, :], axis=0, keepdims=True)


def _in_projection(x2, mod3, g1, w_qkvu, *, batch, seq):
    n, d = x2.shape
    tm = TM_PROJ
    tps = seq // tm
    nb = seq // MOBA_BLOCK
    cb = tm // MOBA_BLOCK
    wcols = w_qkvu.shape[1]
    kern = functools.partial(_inproj_kernel, tm=tm)
    t_shape = jax.ShapeDtypeStruct((batch, nb, ATTN_WIDTH, MOBA_BLOCK), BF16)
    t_spec = pl.BlockSpec((1, cb, ATTN_WIDTH, MOBA_BLOCK), lambda i: (i // tps, i % tps, 0, 0))
    return pl.pallas_call(
        kern,
        grid=(n // tm,),
        in_specs=[pl.BlockSpec((tm, d), lambda i: (i, 0)),
                  pl.BlockSpec((1, 1, N_MOD * d), lambda i: (i // tps, 0, 0)),
                  pl.BlockSpec((1, d), lambda i: (0, 0)),
                  pl.BlockSpec((d, wcols), lambda i: (0, 0))],
        out_specs=[t_spec,
                   pl.BlockSpec((tm, ATTN_WIDTH), lambda i: (i, 0)),
                   t_spec,
                   pl.BlockSpec((tm, POOL_WIDTH), lambda i: (i, 0)),
                   pl.BlockSpec((1, cb, ATTN_WIDTH), lambda i: (i, 0, 0))],
        out_shape=[t_shape,
                   jax.ShapeDtypeStruct((n, ATTN_WIDTH), BF16),
                   t_shape,
                   jax.ShapeDtypeStruct((n, POOL_WIDTH), F32),
                   jax.ShapeDtypeStruct((n // tm, cb, ATTN_WIDTH), F32)],
        compiler_params=_cparams("arbitrary"),
        name="norm1_inproj",
    )(x2, mod3, g1, w_qkvu)


def _attn_kernel(qT_ref, k_ref, vT_ref, km_ref, bw_ref, o_ref, kaug_ref, vaug_ref, bias_ref, qnear_ref,
                 qfar_ref, sn_ref, sf_ref, *, nb):
    blk = MOBA_BLOCK
    hd = HEAD_DIM
    lane = lax.broadcasted_iota(jnp.int32, (blk, LANES), 1)

    ones = jnp.ones((ONES_ROWS, blk), BF16)

    def build_operands(j, carry):
        rows = pl.ds(pl.multiple_of(j * blk, blk), blk)
        kp = k_ref[0, rows, :]
        kaug_ref[0, rows, :] = jnp.where(lane < hd, kp, jnp.where(lane == hd + j, 1.0, 0.0).astype(BF16))
        kaug_ref[1, rows, :] = jnp.where(lane >= hd, kp, jnp.where(lane == j, 1.0, 0.0).astype(BF16))
        for hh in range(2):
            vaug_ref[j, hh] = jnp.concatenate([vT_ref[0, j, hh * hd:(hh + 1) * hd, :], ones], axis=0)
        return carry

    lax.fori_loop(0, nb, build_operands, 0)

    for hh in range(2):
        for t in range(2):
            wide = jnp.broadcast_to(bw_ref[hh, t:t + 1, :], (blk, 2 * blk))
            bias_ref[hh, t] = pltpu.roll(wide, 0, 1, stride=1, stride_axis=0)[:, blk:2 * blk]

    rowi = lax.broadcasted_iota(jnp.int32, (MAX_BLOCKS, blk), 0)
    km_lane = lax.broadcasted_iota(jnp.int32, (MAX_BLOCKS, LANES), 1)
    zpad = jnp.zeros((LANES - hd - MAX_BLOCKS, blk), BF16)
    km_split = []
    for hh in range(2):
        in_head = (km_lane >= hh * hd) & (km_lane < (hh + 1) * hd)
        km_h = jnp.where(in_head, km_ref[0], 0.0)
        km_hi = km_h.astype(BF16)
        km_split.append((km_hi, (km_h - km_hi.astype(F32)).astype(BF16)))

    def head_rows(hh):
        return slice(hh * hd, (hh + 1) * hd)

    grp = FAR_GROUP

    def retire_begin(state, mx):
        m, acc = state
        m_new = jnp.maximum(m, mx)
        return m_new, jnp.exp2(m - m_new) * acc

    def retire_slice(src_ref, hh, m_new, acc, g, j):
        p = jnp.exp2(src_ref[hh, g * blk:(g + 1) * blk, :] - m_new)
        return acc + _dot(vaug_ref[j, hh], p.astype(BF16))

    def fresh():
        return (jnp.full((1, blk), -jnp.inf, F32), jnp.zeros((hd + ONES_ROWS, blk), F32))

    def near_blocks(i):
        return tuple(jnp.where(i - (grp - 1) + g < 0, i + 1, i - (grp - 1) + g) for g in range(grp))

    def finish(i, states):
        for hh in range(2):
            _, acc = states[hh]
            o_ref[0, i, head_rows(hh), :] = (acc[0:hd] / acc[hd:hd + 1]).astype(BF16)

    def select(i):
        qp = qT_ref[0, i]
        for hh in range(2):
            km_hi, km_lo = km_split[hh]
            gate = _dot(km_hi, qp) + _dot(km_lo, qp)
            g = jnp.where(rowi < i, gate, -jnp.inf)
            sel = rowi == i
            for _ in range(MOBA_TOPK):
                m = jnp.max(g, axis=0, keepdims=True)
                hit = (g == m) & (m > -jnp.inf)
                idx = jnp.min(jnp.where(hit, rowi, MAX_BLOCKS), axis=0, keepdims=True)
                pick = rowi == idx
                sel = sel | pick
                g = jnp.where(pick, -jnp.inf, g)
            near = jnp.where(sel, 0.0, NEG_INF)
            far = jnp.where(rowi > i - grp, NEG_INF, near)
            qh = qp[head_rows(hh), :]
            if hh == 0:
                qnear_ref[hh] = jnp.concatenate([qh, near.astype(BF16), zpad], axis=0)
                qfar_ref[hh] = jnp.concatenate([qh, far.astype(BF16), zpad], axis=0)
            else:
                qnear_ref[hh] = jnp.concatenate([near.astype(BF16), zpad, qh], axis=0)
                qfar_ref[hh] = jnp.concatenate([far.astype(BF16), zpad, qh], axis=0)

    def park(dst_ref, q_ref, hh, g, j, bias=None):
        rows = pl.ds(pl.multiple_of(j * blk, blk), blk)
        s = _dot(kaug_ref[hh, rows, :], q_ref[hh])
        if bias is not None:
            s = s + bias
        dst_ref[hh, g * blk:(g + 1) * blk, :] = s
        return jnp.max(s, axis=0, keepdims=True)

    def park_near(i):
        idx = near_blocks(i)

        def quarter(g, hh):
            bias = None
            if g == grp - 1:
                bias = bias_ref[hh, 0]
            elif g == grp - 2:
                bias = bias_ref[hh, 1]
            return park(sn_ref, qnear_ref, hh, g, idx[g], bias)
        return quarter

    def park_far(c):
        return lambda g, hh: park(sf_ref, qfar_ref, hh, g, c * grp + g)

    def chunk_blocks(c):
        return tuple(c * grp + g for g in range(grp))

    def step(src_ref, states, maxes, vidx, quarter):
        begun = [retire_begin(states[hh], maxes[hh]) for hh in range(2)] if states is not None else None
        accs = [b[1] for b in begun] if begun else None
        new_max = [None, None]
        for g in range(grp):
            if begun:
                for hh in range(2):
                    accs[hh] = retire_slice(src_ref, hh, begun[hh][0], accs[hh], g, vidx[g])
            if quarter is not None:
                for hh in range(2):
                    cm = quarter(g, hh)
                    new_max[hh] = cm if new_max[hh] is None else jnp.maximum(new_max[hh], cm)
        new_states = tuple((begun[hh][0], accs[hh]) for hh in range(2)) if begun else None
        return new_states, tuple(new_max)

    def fresh_states():
        return (fresh(), fresh())

    def early_tile(i, near_max):
        nxt = jnp.minimum(i + 1, nb - 1)
        select(nxt)
        states, next_max = step(sn_ref, fresh_states(), near_max, near_blocks(i), park_near(nxt))
        finish(i, states)
        return next_max

    def late_tile(i, near_max):
        states, far_max = step(sn_ref, fresh_states(), near_max, near_blocks(i), park_far(0))

        def far_step(c, st):
            return step(sf_ref, st[0], st[1], chunk_blocks(c - 1), park_far(c))

        n_chunks = (i - (grp - 1) + grp - 1) // grp
        states, far_max = lax.fori_loop(1, n_chunks, far_step, (states, far_max))
        nxt = jnp.minimum(i + 1, nb - 1)
        select(nxt)
        states, next_max = step(sf_ref, states, far_max, chunk_blocks(n_chunks - 1), park_near(nxt))
        finish(i, states)
        return next_max

    n_early = min(grp, nb)
    select(0)
    _, near_max = step(None, None, None, None, park_near(0))
    near_max = lax.fori_loop(0, n_early, early_tile, near_max)
    lax.fori_loop(n_early, nb, late_tile, near_max)


def _rel_bucket_table(max_rel):
    n = np.arange(max_rel)
    max_exact = REL_BUCKETS // 2
    nf = np.maximum(n, 1).astype(np.float32)
    large = max_exact + (np.log(nf / np.float32(max_exact)) / np.float32(math.log(REL_MAX_DIST / max_exact))
                         * np.float32(REL_BUCKETS - max_exact)).astype(np.int32)
    large = np.minimum(large, REL_BUCKETS - 1)
    return np.where(n < max_exact, n, large)


def _bias_vectors(rel_bias):
    blk = MOBA_BLOCK
    table = _rel_bucket_table(2 * blk)
    r = np.arange(-blk, blk)
    bt = (rel_bias.T - rel_bias[REL_BUCKETS - 1][:, None]) * LOG2E
    own = jnp.where(jnp.asarray(r >= 0)[None], bt[:, table[np.maximum(r, 0)]], NEG_INF)
    prev = bt[:, table[np.minimum(r + blk, 2 * blk - 1)]]
    return jnp.stack([own, prev], axis=1).astype(F32)


def _moba_attention(qT, k3, vT, km_pad, bias_w, *, batch, seq):
    nb = seq // MOBA_BLOCK
    blk = MOBA_BLOCK
    assert nb % FAR_GROUP == 0 and nb <= MAX_BLOCKS
    kern = functools.partial(_attn_kernel, nb=nb)
    t_spec = pl.BlockSpec((1, nb, LANES, blk), lambda b, p: (b, 0, p, 0))
    return pl.pallas_call(
        kern,
        grid=(batch, N_HEADS // 2),
        in_specs=[t_spec,
                  pl.BlockSpec((1, seq, LANES), lambda b, p: (b, 0, p)),
                  t_spec,
                  pl.BlockSpec((1, MAX_BLOCKS, LANES), lambda b, p: (b, 0, p)),
                  pl.BlockSpec((2, 2, 2 * blk), lambda b, p: (p, 0, 0))],
        out_specs=t_spec,
        out_shape=jax.ShapeDtypeStruct((batch, nb, ATTN_WIDTH, blk), BF16),
        scratch_shapes=[pltpu.VMEM((2, seq, LANES), BF16),
                        pltpu.VMEM((nb, 2, HEAD_DIM + ONES_ROWS, blk), BF16),
                        pltpu.VMEM((2, 2, blk, blk), F32),
                        pltpu.VMEM((2, LANES, blk), BF16),
                        pltpu.VMEM((2, LANES, blk), BF16),
                        pltpu.VMEM((2, FAR_GROUP * blk, blk), F32),
                        pltpu.VMEM((2, FAR_GROUP * blk, blk), F32)],
        compiler_params=_cparams("arbitrary", "arbitrary"),
        name="moba_attention",
    )(qT, k3, vT, km_pad, bias_w)


def _post_kernel(x_ref, mod_ref, g1_ref, wz_ref, bz_ref, at_ref, u_ref, uh_ref, pw_ref, ps_ref,
                 wba_ref, wbp_ref, wo_ref, g2_ref, wrh_ref, wrl_ref, br_ref,
                 x1_ref, h2p_ref, route_ref, counts_ref, carry_ref, *, tm, seq):
    d = D_MODEL
    i = pl.program_id(0)
    x = x_ref[...]
    mod = mod_ref[0]
    shift1, scale1, gate1 = mod[:, 0:d], mod[:, d:2 * d], mod[:, 2 * d:3 * d]
    shift2, scale2 = mod[:, 3 * d:4 * d], mod[:, 4 * d:5 * d]

    h = _rms_mod(x, g1_ref[...], scale1, shift1)
    zg = jax.nn.sigmoid(_dot(h.astype(BF16), wz_ref[...]) + bz_ref[...])

    ya = jnp.concatenate(
        [lax.dot_general(at_ref[0, c], wba_ref[...], (((0,), (0,)), ((), ())),
                         preferred_element_type=F32) for c in range(tm // MOBA_BLOCK)], axis=0)

    first = (i * tm) % seq == 0
    halo = jnp.where(first, 0.0, uh_ref[...])
    ubuf = jnp.concatenate([halo, u_ref[...]], axis=0)
    rows = tm + POOL_HALO
    pos = (lax.broadcasted_iota(jnp.int32, (tm, POOL_GROUP_WIDTH), 0) + i * tm) % seq
    outs = []
    for gi, w in enumerate(POOL_WINDOWS):
        e = ubuf[:, gi * POOL_GROUP_WIDTH:(gi + 1) * POOL_GROUP_WIDTH]
        s = e
        step = 1
        while step < w:
            s = s + pltpu.roll(s, step, 0)
            step *= 2
        cnt = jnp.minimum(pos + 1, w).astype(F32)
        dlt = s[POOL_HALO:rows] / cnt - e[POOL_HALO:rows]
        outs.append(_dot(dlt.astype(BF16), pw_ref[gi]))
    pooled = jnp.concatenate(outs, axis=-1) * ps_ref[...]
    yp = _dot(pooled.astype(BF16), wbp_ref[...])

    merged = zg[:, 0:d] * ya + zg[:, d:2 * d] * yp
    x1 = x + gate1 * _dot(merged.astype(BF16), wo_ref[...])
    x1_ref[...] = x1

    h2 = _rms_mod(x1, g2_ref[...], scale2, shift2)
    h2_hi = h2.astype(BF16)
    h2p_ref[...] = _pack_bf16_pairs(h2_hi[:, 0:d // 2], h2_hi[:, d // 2:d])
    h2_lo = (h2 - h2_hi.astype(F32)).astype(BF16)
    z = _dot(h2_hi, wrh_ref[...]) + _dot(h2_lo, wrh_ref[...]) + _dot(h2_hi, wrl_ref[...]) + br_ref[...]

    lane = lax.broadcasted_iota(jnp.int32, (tm, LANES), 1)
    zgrp = jnp.where(lane < N_GROUPS, z, -jnp.inf)
    mg = jnp.max(zgrp, axis=-1, keepdims=True)
    pg_top = 1.0 / jnp.sum(jnp.exp(zgrp - mg), axis=-1, keepdims=True)
    g_idx = jnp.min(jnp.where(zgrp == mg, lane, LANES), axis=-1, keepdims=True)
    e_lane = lane - ROUTER_LANE0
    in_grp = (e_lane >= 0) & (e_lane < N_EXPERTS) & ((e_lane >> 3) == g_idx)
    ze = jnp.where(in_grp, z, -jnp.inf)
    m1 = jnp.max(ze, axis=-1, keepdims=True)
    i1 = jnp.min(jnp.where(ze == m1, lane, LANES), axis=-1, keepdims=True)
    ze2 = jnp.where(lane == i1, -jnp.inf, ze)
    m2 = jnp.max(ze2, axis=-1, keepdims=True)
    i2 = jnp.min(jnp.where(ze2 == m2, lane, LANES), axis=-1, keepdims=True)
    e2 = jnp.exp(m2 - m1)
    w1 = pg_top / (1.0 + e2)
    w2 = pg_top * e2 / (1.0 + e2)

    hot = jnp.where((lane == i1) | (lane == i2), 1.0, 0.0)
    tri = jnp.where(lax.broadcasted_iota(jnp.int32, (tm, tm), 1) <= lax.broadcasted_iota(jnp.int32, (tm, tm), 0),
                    1.0, 0.0).astype(BF16)
    prefix = _dot(tri, hot.astype(BF16))

    @pl.when(i == 0)
    def _():
        carry_ref[...] = jnp.zeros_like(carry_ref)

    base = carry_ref[...] + prefix - 1.0
    rank1 = jnp.sum(jnp.where(lane == i1, base, 0.0), axis=-1, keepdims=True)
    rank2 = jnp.sum(jnp.where(lane == i2, base, 0.0), axis=-1, keepdims=True)
    carry_ref[...] = carry_ref[...] + prefix[tm - 1:tm, :]
    counts_ref[...] = carry_ref[...]
    fields = ((i1 - ROUTER_LANE0).astype(F32), (i2 - ROUTER_LANE0).astype(F32), w1, w2, rank1, rank2)
    route = jnp.zeros((tm, LANES), F32)
    for col, val in enumerate(fields):
        route = jnp.where(lane == col, val, route)
    route_ref[...] = route


def _post_attention(x2, mod3, g1, w_z, b_z, attnT, u, pool_w, pool_scale, w_ba, w_bp, w_out, g2,
                    wr_hi, wr_lo, b_r, *, batch, seq):
    n, d = x2.shape
    tm = TM_POST
    tps = seq // tm
    cb = tm // MOBA_BLOCK
    hb = tm // POOL_HALO
    kern = functools.partial(_post_kernel, tm=tm, seq=seq)
    const2 = lambda i: (0, 0)
    return pl.pallas_call(
        kern,
        grid=(n // tm,),
        in_specs=[pl.BlockSpec((tm, d), lambda i: (i, 0)),
                  pl.BlockSpec((1, 1, N_MOD * d), lambda i: (i // tps, 0, 0)),
                  pl.BlockSpec((1, d), const2),
                  pl.BlockSpec((d, 2 * d), const2),
                  pl.BlockSpec((1, 2 * d), const2),
                  pl.BlockSpec((1, cb, ATTN_WIDTH, MOBA_BLOCK), lambda i: (i // tps, i % tps, 0, 0)),
                  pl.BlockSpec((tm, POOL_WIDTH), lambda i: (i, 0)),
                  pl.BlockSpec((POOL_HALO, POOL_WIDTH), lambda i: (jnp.maximum(i * hb - 1, 0), 0)),
                  pl.BlockSpec((len(POOL_WINDOWS), POOL_GROUP_WIDTH, POOL_GROUP_WIDTH), lambda i: (0, 0, 0)),
                  pl.BlockSpec((1, POOL_WIDTH), const2),
                  pl.BlockSpec((ATTN_WIDTH, d), const2),
                  pl.BlockSpec((POOL_WIDTH, d), const2),
                  pl.BlockSpec((d, d), const2),
                  pl.BlockSpec((1, d), const2),
                  pl.BlockSpec((d, LANES), const2),
                  pl.BlockSpec((d, LANES), const2),
                  pl.BlockSpec((1, LANES), const2)],
        out_specs=[pl.BlockSpec((tm, d), lambda i: (i, 0)),
                   pl.BlockSpec((tm, d // 2), lambda i: (i, 0)),
                   pl.BlockSpec((tm, LANES), lambda i: (i, 0)),
                   pl.BlockSpec((1, LANES), const2)],
        out_shape=[jax.ShapeDtypeStruct((n, d), F32),
                   jax.ShapeDtypeStruct((n, d // 2), jnp.uint32),
                   jax.ShapeDtypeStruct((n, LANES), F32),
                   jax.ShapeDtypeStruct((1, LANES), F32)],
        scratch_shapes=[pltpu.VMEM((1, LANES), F32)],
        compiler_params=_cparams("arbitrary"),
        name="merge_outproj_router",
    )(x2, mod3, g1, w_z, b_z, attnT, u, u, pool_w, pool_scale, w_ba, w_bp, w_out, g2, wr_hi, wr_lo, b_r)


def _routing_plan(route, counts, *, n):
    te = TE_ROWS
    n_tiles = _expert_tiles(n)
    cnt = counts[0, ROUTER_LANE0:ROUTER_LANE0 + N_EXPERTS].astype(jnp.int32)
    padded = ((cnt + te - 1) // te) * te
    ids = jnp.arange(N_EXPERTS, dtype=jnp.int32)
    ends = jnp.sum(jnp.where(ids[None, :] <= ids[:, None], padded[None, :], 0), axis=-1)
    starts = ends - padded
    expert = route[:, 0:2].astype(jnp.int32)
    rank = route[:, 4:6].astype(jnp.int32)
    pos = jnp.sum(jnp.where(expert[:, :, None] == ids, starts, 0), axis=-1) + rank
    tiles = jnp.arange(n_tiles, dtype=jnp.int32)
    tile_expert = jnp.sum((ends // te)[None, :] <= tiles[:, None], axis=-1)
    tile_expert = jnp.minimum(tile_expert, N_EXPERTS - 1).astype(jnp.int32)
    n_used = (ends[N_EXPERTS - 1:] // te).astype(jnp.int32)
    return pos, tile_expert, n_used


def _expert_tiles(n):
    return (EXPERT_TOPK * n) // TE_ROWS + N_EXPERTS


def _dispatch_kernel(pos_ref, h_ref, xz_ref, xs_ref, sem, *, tm):
    del xz_ref
    for r in range(tm):
        for k in range(EXPERT_TOPK):
            j = EXPERT_TOPK * r + k
            pltpu.make_async_copy(h_ref.at[pl.ds(r, 1)], xs_ref.at[pl.ds(pos_ref[0, 0, j], 1)],
                                  sem).start(priority=j % 2)
    for k in range(EXPERT_TOPK):
        pltpu.make_async_copy(h_ref, h_ref, sem).wait()


def _dispatch(pos, h2p, *, n):
    tm = TM_DISPATCH
    rows = _expert_tiles(n) * TE_ROWS
    half = h2p.shape[1]
    pos3 = pos.reshape(n // tm, 1, EXPERT_TOPK * tm)
    return pl.pallas_call(
        functools.partial(_dispatch_kernel, tm=tm),
        grid=(n // tm,),
        in_specs=[pl.BlockSpec((1, 1, EXPERT_TOPK * tm), lambda i: (i, 0, 0), memory_space=pltpu.SMEM),
                  pl.BlockSpec((tm, half), lambda i: (i, 0)),
                  pl.BlockSpec(memory_space=pl.ANY)],
        out_specs=pl.BlockSpec(memory_space=pl.ANY),
        out_shape=jax.ShapeDtypeStruct((rows, half), jnp.uint32),
        scratch_shapes=[pltpu.SemaphoreType.DMA(())],
        input_output_aliases={2: 0},
        compiler_params=_cparams("arbitrary"),
        name="moe_dispatch",
    )(pos3, h2p, jnp.zeros((rows, half), jnp.uint32))


def _experts_kernel(te_ref, nu_ref, xs_ref, wg_ref, wu_ref, wd_ref, y_ref, wg_b, wu_b, wd_b):
    t = pl.program_id(0)

    @pl.when(t < nu_ref[0])
    def _():
        e = te_ref[t]
        e_prev = te_ref[jnp.maximum(t - 1, 0)]

        @pl.when((t == 0) | (e != e_prev))
        def _():
            wg_b[...] = wg_ref[0].astype(BF16)
            wu_b[...] = wu_ref[0].astype(BF16)
            wd_b[...] = wd_ref[0].astype(BF16)

        xa, xb = _unpack_bf16_pairs(xs_ref[...])
        x = jnp.concatenate([xa, xb], axis=-1)
        g = _dot(x, wg_b[...])
        u = _dot(x, wu_b[...])
        act = (g * jax.nn.sigmoid(g)) * u
        y_ref[...] = _dot(act.astype(BF16), wd_b[...])

    @pl.when(t >= nu_ref[0])
    def _():
        y_ref[...] = jnp.zeros_like(y_ref)


def _experts(tile_expert, n_used, xs, wg, wu, wd):
    rows, half = xs.shape
    d = 2 * half
    te = TE_ROWS

    def used(t, nu):
        return jnp.minimum(t, nu[0] - 1)

    grid_spec = pltpu.PrefetchScalarGridSpec(
        num_scalar_prefetch=2,
        grid=(rows // te,),
        in_specs=[pl.BlockSpec((te, half), lambda t, te_r, nu: (used(t, nu), 0)),
                  pl.BlockSpec((1, d, D_EXPERT), lambda t, te_r, nu: (te_r[used(t, nu)], 0, 0)),
                  pl.BlockSpec((1, d, D_EXPERT), lambda t, te_r, nu: (te_r[used(t, nu)], 0, 0)),
                  pl.BlockSpec((1, D_EXPERT, d), lambda t, te_r, nu: (te_r[used(t, nu)], 0, 0))],
        out_specs=pl.BlockSpec((te, d), lambda t, te_r, nu: (t, 0)),
        scratch_shapes=[pltpu.VMEM((d, D_EXPERT), BF16), pltpu.VMEM((d, D_EXPERT), BF16),
                        pltpu.VMEM((D_EXPERT, d), BF16)])
    return pl.pallas_call(
        _experts_kernel,
        grid_spec=grid_spec,
        out_shape=jax.ShapeDtypeStruct((rows, d), F32),
        compiler_params=_cparams("arbitrary"),
        name="moe_experts",
    )(tile_expert, n_used, xs, wg, wu, wd)


def _final_kernel(pos_ref, x1_ref, route_ref, mod_ref, g_ref, y_ref, o_ref, ybuf, sem, *, tm, final_norm):
    d = D_MODEL

    for r in range(tm):
        for k in range(EXPERT_TOPK):
            j = EXPERT_TOPK * r + k
            pltpu.make_async_copy(y_ref.at[pl.ds(pos_ref[0, 0, j], 1)], ybuf.at[k, pl.ds(r, 1)],
                                  sem).start(priority=j % 2)
    for k in range(EXPERT_TOPK):
        pltpu.make_async_copy(ybuf.at[k], ybuf.at[k], sem).wait()

    route = route_ref[...]
    lane = lax.broadcasted_iota(jnp.int32, route.shape, 1)
    y = jnp.zeros((tm, d), F32)
    for k in range(EXPERT_TOPK):
        wk = jnp.sum(jnp.where(lane == 2 + k, route, 0.0), axis=-1, keepdims=True)
        y = y + wk * ybuf[k]
    gate2 = mod_ref[0][:, 5 * d:6 * d]
    x = x1_ref[...] + gate2 * y
    if final_norm:
        x = (x * lax.rsqrt(jnp.mean(x * x, axis=-1, keepdims=True) + EPS)) * g_ref[...]
    o_ref[...] = x


def _final(pos, x1, route, mod3, gf, y, *, seq, final_norm):
    n, d = x1.shape
    tm = TM_FINAL
    tps = seq // tm
    pos3 = pos.reshape(n // tm, 1, EXPERT_TOPK * tm)
    return pl.pallas_call(
        functools.partial(_final_kernel, tm=tm, final_norm=final_norm),
        grid=(n // tm,),
        in_specs=[pl.BlockSpec((1, 1, EXPERT_TOPK * tm), lambda i: (i, 0, 0), memory_space=pltpu.SMEM),
                  pl.BlockSpec((tm, d), lambda i: (i, 0)),
                  pl.BlockSpec((tm, LANES), lambda i: (i, 0)),
                  pl.BlockSpec((1, 1, N_MOD * d), lambda i: (i // tps, 0, 0)),
                  pl.BlockSpec((1, d), lambda i: (0, 0)),
                  pl.BlockSpec(memory_space=pl.ANY)],
        out_specs=pl.BlockSpec((tm, d), lambda i: (i, 0)),
        out_shape=jax.ShapeDtypeStruct((n, d), F32),
        scratch_shapes=[pltpu.VMEM((EXPERT_TOPK, tm, d), F32), pltpu.SemaphoreType.DMA(())],
        compiler_params=_cparams("arbitrary"),
        name="moe_combine_final_norm",
    )(pos3, x1, route, mod3, gf, y)


def _split_bf16(w):
    hi = w.astype(BF16)
    return hi, (w - hi.astype(F32)).astype(BF16)


def kernel(x, c, w_ada, b_ada, norm1_g, w_in, b_gate, rel_bias, pool_w, pool_scale, w_branch_attn,
           w_branch_pool, w_out, norm2_g, w_router_group, b_router_group, w_router_expert,
           b_router_expert, w_expert_gate, w_expert_up, w_expert_down, norm_f_g):
    batch, seq, d = x.shape
    depth = w_ada.shape[0]
    n = batch * seq
    nb = seq // MOBA_BLOCK
    qkvu = 3 * ATTN_WIDTH + POOL_WIDTH
    bias_w = _bias_vectors(rel_bias)
    c_pad = jnp.pad(c, ((0, 8 - batch), (0, 0)))

    xc = x.reshape(n, d)
    for l in range(depth):
        mod3 = _modulation(c_pad, w_ada[l], b_ada[l][None])[:batch].reshape(batch, 1, N_MOD * d)
        w_in_b = w_in[l].astype(BF16)
        g1 = norm1_g[l][None]

        qT, k2, vT, u, km = _in_projection(xc, mod3, g1, w_in_b[:, :qkvu], batch=batch, seq=seq)
        km_pad = jnp.pad(km.reshape(batch, nb, ATTN_WIDTH), ((0, 0), (0, MAX_BLOCKS - nb), (0, 0)))
        attnT = _moba_attention(qT, k2.reshape(batch, seq, ATTN_WIDTH), vT, km_pad, bias_w,
                                batch=batch, seq=seq)

        w_r = jnp.concatenate([w_router_group[l], w_router_expert[l]], axis=1)
        w_r = jnp.pad(w_r, ((0, 0), (0, LANES - w_r.shape[1])))
        b_r = jnp.concatenate([b_router_group[l], b_router_expert[l]])
        b_r = jnp.pad(b_r, (0, LANES - b_r.shape[0]))[None]
        wr_hi, wr_lo = _split_bf16(w_r)
        x1, h2p, route, counts = _post_attention(
            xc, mod3, g1, w_in_b[:, qkvu:], b_gate[l][None], attnT, u, pool_w[l].astype(BF16),
            pool_scale[l][None], w_branch_attn[l].astype(BF16), w_branch_pool[l].astype(BF16),
            w_out[l].astype(BF16), norm2_g[l][None], wr_hi, wr_lo, b_r, batch=batch, seq=seq)

        pos, tile_expert, n_used = _routing_plan(route, counts, n=n)
        xs = _dispatch(pos, h2p, n=n)
        y = _experts(tile_expert, n_used, xs, w_expert_gate[l], w_expert_up[l], w_expert_down[l])
        xc = _final(pos, x1, route, mod3, norm_f_g[None], y, seq=seq, final_norm=(l == depth - 1))
    return xc.reshape(batch, seq, d)
```

```python
import functools
import math

import numpy as np
import jax
import jax.numpy as jnp
from jax import lax
from jax.experimental import pallas as pl
from jax.experimental.pallas import tpu as pltpu

F32 = jnp.float32
BF16 = jnp.bfloat16

D_MODEL = 1024
N_HEADS = 8
HEAD_DIM = 64
ATTN_WIDTH = N_HEADS * HEAD_DIM
MOBA_BLOCK = 256
MOBA_TOPK = 3
MAX_BLOCKS = 32
FAR_GROUP = 4
ONES_ROWS = 16
LOG2E = math.log2(math.e)
POOL_WINDOWS = (2, 4, 8, 16)
POOL_WIDTH = 512
POOL_GROUP_WIDTH = 128
POOL_HALO = 16
REL_BUCKETS = 32
REL_MAX_DIST = 128
N_GROUPS = 4
EXPERTS_PER_GROUP = 8
N_EXPERTS = N_GROUPS * EXPERTS_PER_GROUP
D_EXPERT = 512
N_MOD = 6
EPS = 1e-6
NEG_INF = -1e30
LANES = 128
ROUTER_LANE0 = N_GROUPS
VMEM_LIMIT = 56 * 1024 * 1024

TM_PROJ = 512
TM_POST = 512
TM_DISPATCH = 512
TE_ROWS = 512
EXPERT_TOPK = 2
TM_FINAL = 256


def _cparams(*sem):
    return pltpu.CompilerParams(dimension_semantics=sem, vmem_limit_bytes=VMEM_LIMIT)


def _dot(a, b):
    return jnp.dot(a, b, preferred_element_type=F32)


def _rms_mod(x, g, scale, shift):
    xn = x * lax.rsqrt(jnp.mean(x * x, axis=-1, keepdims=True) + EPS)
    return (xn * g) * (1.0 + scale) + shift


def _mod_kernel(c_ref, w_ref, b_ref, o_ref):
    c = c_ref[...]
    ca = c * jax.nn.sigmoid(c)
    o_ref[...] = _dot(ca.astype(BF16), w_ref[...].astype(BF16)) + b_ref[...]


def _modulation(c_pad, w_ada, b_ada):
    rows, d = c_pad.shape
    n_out = w_ada.shape[1]
    tn = 1024
    return pl.pallas_call(
        _mod_kernel,
        grid=(n_out // tn,),
        in_specs=[pl.BlockSpec((rows, d), lambda j: (0, 0)),
                  pl.BlockSpec((d, tn), lambda j: (0, j)),
                  pl.BlockSpec((1, tn), lambda j: (0, j))],
        out_specs=pl.BlockSpec((rows, tn), lambda j: (0, j)),
        out_shape=jax.ShapeDtypeStruct((rows, n_out), F32),
        compiler_params=_cparams("arbitrary"),
        name="adaln_mod",
    )(c_pad, w_ada, b_ada)


def _inproj_kernel(x_ref, mod_ref, g_ref, w_ref, qT_ref, k_ref, vT_ref, u_ref, km_ref, *, tm):
    d = D_MODEL
    mod = mod_ref[0]
    h = _rms_mod(x_ref[...], g_ref[...], mod[:, d:2 * d], mod[:, 0:d])
    proj = _dot(h.astype(BF16), w_ref[...])
    aw = ATTN_WIDTH
    q = proj[:, 0:aw] * (HEAD_DIM ** -0.5 * LOG2E)
    k = proj[:, aw:2 * aw]
    v = proj[:, 2 * aw:3 * aw]
    u_ref[...] = proj[:, 3 * aw:3 * aw + POOL_WIDTH]
    k_ref[...] = k.astype(BF16)
    qT = q.T.astype(BF16)
    vT = v.T.astype(BF16)
    for c in range(tm // MOBA_BLOCK):
        sl = slice(c * MOBA_BLOCK, (c + 1) * MOBA_BLOCK)
        qT_ref[0, c] = qT[:, sl]
        vT_ref[0, c] = vT[:, sl]
        km_ref[0, c:c + 1, :] = jnp.mean(k[sl, :], axis=0, keepdims=True)


def _in_projection(x2, mod3, g1, w_qkvu, *, batch, seq):
    n, d = x2.shape
    tm = TM_PROJ
    tps = seq // tm
    nb = seq // MOBA_BLOCK
    cb = tm // MOBA_BLOCK
    wcols = 3 * ATTN_WIDTH + POOL_WIDTH
    assert w_qkvu.shape[1] % wcols == 0
    kern = functools.partial(_inproj_kernel, tm=tm)
    t_shape = jax.ShapeDtypeStruct((batch, nb, ATTN_WIDTH, MOBA_BLOCK), BF16)
    t_spec = pl.BlockSpec((1, cb, ATTN_WIDTH, MOBA_BLOCK), lambda i: (i // tps, i % tps, 0, 0))
    return pl.pallas_call(
        kern,
        grid=(n // tm,),
        in_specs=[pl.BlockSpec((tm, d), lambda i: (i, 0)),
                  pl.BlockSpec((1, 1, N_MOD * d), lambda i: (i // tps, 0, 0)),
                  pl.BlockSpec((1, d), lambda i: (0, 0)),
                  pl.BlockSpec((d, wcols), lambda i: (0, 0))],
        out_specs=[t_spec,
                   pl.BlockSpec((tm, ATTN_WIDTH), lambda i: (i, 0)),
                   t_spec,
                   pl.BlockSpec((tm, POOL_WIDTH), lambda i: (i, 0)),
                   pl.BlockSpec((1, cb, ATTN_WIDTH), lambda i: (i, 0, 0))],
        out_shape=[t_shape,
                   jax.ShapeDtypeStruct((n, ATTN_WIDTH), BF16),
                   t_shape,
                   jax.ShapeDtypeStruct((n, POOL_WIDTH), F32),
                   jax.ShapeDtypeStruct((n // tm, cb, ATTN_WIDTH), F32)],
        compiler_params=_cparams("arbitrary"),
        name="norm1_inproj",
    )(x2, mod3, g1, w_qkvu)


def _attn_kernel(qT_ref, k_ref, vT_ref, km_ref, bw_ref, o_ref, kaug_ref, vaug_ref, bias_ref, qnear_ref,
                 qfar_ref, sn_ref, sf_ref, *, nb):
    blk = MOBA_BLOCK
    hd = HEAD_DIM
    lane = lax.broadcasted_iota(jnp.int32, (blk, LANES), 1)

    ones = jnp.ones((ONES_ROWS, blk), BF16)

    def build_operands(j, carry):
        rows = pl.ds(pl.multiple_of(j * blk, blk), blk)
        kp = k_ref[0, rows, :]
        kaug_ref[0, rows, :] = jnp.where(lane < hd, kp, jnp.where(lane == hd + j, 1.0, 0.0).astype(BF16))
        kaug_ref[1, rows, :] = jnp.where(lane >= hd, kp, jnp.where(lane == j, 1.0, 0.0).astype(BF16))
        for hh in range(2):
            vaug_ref[j, hh] = jnp.concatenate([vT_ref[0, j, hh * hd:(hh + 1) * hd, :], ones], axis=0)
        return carry

    lax.fori_loop(0, nb, build_operands, 0)

    for hh in range(2):
        for t in range(2):
            wide = jnp.broadcast_to(bw_ref[hh, t:t + 1, :], (blk, 2 * blk))
            bias_ref[hh, t] = pltpu.roll(wide, 0, 1, stride=1, stride_axis=0)[:, blk:2 * blk]

    rowi = lax.broadcasted_iota(jnp.int32, (MAX_BLOCKS, blk), 0)
    km_lane = lax.broadcasted_iota(jnp.int32, (MAX_BLOCKS, LANES), 1)
    zpad = jnp.zeros((LANES - hd - MAX_BLOCKS, blk), BF16)
    km_split = []
    for hh in range(2):
        in_head = (km_lane >= hh * hd) & (km_lane < (hh + 1) * hd)
        km_h = jnp.where(in_head, km_ref[0], 0.0)
        km_hi = km_h.astype(BF16)
        km_split.append((km_hi, (km_h - km_hi.astype(F32)).astype(BF16)))

    def head_rows(hh):
        return slice(hh * hd, (hh + 1) * hd)

    grp = FAR_GROUP

    def retire_begin(state, mx):
        m, acc = state
        m_new = jnp.maximum(m, mx)
        return m_new, jnp.exp2(m - m_new) * acc

    def retire_slice(src_ref, hh, m_new, acc, g, j):
        p = jnp.exp2(src_ref[hh, g * blk:(g + 1) * blk, :] - m_new)
        return acc + _dot(vaug_ref[j, hh], p.astype(BF16))

    def fresh():
        return (jnp.full((1, blk), -jnp.inf, F32), jnp.zeros((hd + ONES_ROWS, blk), F32))

    def near_blocks(i):
        return tuple(jnp.where(i - (grp - 1) + g < 0, i + 1, i - (grp - 1) + g) for g in range(grp))

    def finish(i, states):
        for hh in range(2):
            _, acc = states[hh]
            o_ref[0, i, head_rows(hh), :] = (acc[0:hd] / acc[hd:hd + 1]).astype(BF16)

    def select(i):
        qp = qT_ref[0, i]
        for hh in range(2):
            km_hi, km_lo = km_split[hh]
            gate = _dot(km_hi, qp) + _dot(km_lo, qp)
            g = jnp.where(rowi < i, gate, -jnp.inf)
            sel = rowi == i
            for _ in range(MOBA_TOPK):
                m = jnp.max(g, axis=0, keepdims=True)
                hit = (g == m) & (m > -jnp.inf)
                idx = jnp.min(jnp.where(hit, rowi, MAX_BLOCKS), axis=0, keepdims=True)
                pick = rowi == idx
                sel = sel | pick
                g = jnp.where(pick, -jnp.inf, g)
            near = jnp.where(sel, 0.0, NEG_INF)
            far = jnp.where(rowi > i - grp, NEG_INF, near)
            qh = qp[head_rows(hh), :]
            if hh == 0:
                qnear_ref[hh] = jnp.concatenate([qh, near.astype(BF16), zpad], axis=0)
                qfar_ref[hh] = jnp.concatenate([qh, far.astype(BF16), zpad], axis=0)
            else:
                qnear_ref[hh] = jnp.concatenate([near.astype(BF16), zpad, qh], axis=0)
                qfar_ref[hh] = jnp.concatenate([far.astype(BF16), zpad, qh], axis=0)

    def park(dst_ref, q_ref, hh, g, j, bias=None):
        rows = pl.ds(pl.multiple_of(j * blk, blk), blk)
        s = _dot(kaug_ref[hh, rows, :], q_ref[hh])
        if bias is not None:
            s = s + bias
        dst_ref[hh, g * blk:(g + 1) * blk, :] = s
        return jnp.max(s, axis=0, keepdims=True)

    def park_near(i):
        idx = near_blocks(i)

        def quarter(g, hh):
            bias = None
            if g == grp - 1:
                bias = bias_ref[hh, 0]
            elif g == grp - 2:
                bias = bias_ref[hh, 1]
            return park(sn_ref, qnear_ref, hh, g, idx[g], bias)
        return quarter

    def park_far(c):
        return lambda g, hh: park(sf_ref, qfar_ref, hh, g, c * grp + g)

    def chunk_blocks(c):
        return tuple(c * grp + g for g in range(grp))

    def step(src_ref, states, maxes, vidx, quarter):
        begun = [retire_begin(states[hh], maxes[hh]) for hh in range(2)] if states is not None else None
        accs = [b[1] for b in begun] if begun else None
        new_max = [None, None]
        for g in range(grp):
            if begun:
                for hh in range(2):
                    accs[hh] = retire_slice(src_ref, hh, begun[hh][0], accs[hh], g, vidx[g])
            if quarter is not None:
                for hh in range(2):
                    cm = quarter(g, hh)
                    new_max[hh] = cm if new_max[hh] is None else jnp.maximum(new_max[hh], cm)
        new_states = tuple((begun[hh][0], accs[hh]) for hh in range(2)) if begun else None
        return new_states, tuple(new_max)

    def fresh_states():
        return (fresh(), fresh())

    def early_tile(i, near_max):
        nxt = jnp.minimum(i + 1, nb - 1)
        select(nxt)
        states, next_max = step(sn_ref, fresh_states(), near_max, near_blocks(i), park_near(nxt))
        finish(i, states)
        return next_max

    def late_tile(i, near_max):
        states, far_max = step(sn_ref, fresh_states(), near_max, near_blocks(i), park_far(0))

        def far_step(c, st):
            return step(sf_ref, st[0], st[1], chunk_blocks(c - 1), park_far(c))

        n_chunks = (i - (grp - 1) + grp - 1) // grp
        states, far_max = lax.fori_loop(1, n_chunks, far_step, (states, far_max))
        nxt = jnp.minimum(i + 1, nb - 1)
        select(nxt)
        states, next_max = step(sf_ref, states, far_max, chunk_blocks(n_chunks - 1), park_near(nxt))
        finish(i, states)
        return next_max

    n_early = min(grp, nb)
    select(0)
    _, near_max = step(None, None, None, None, park_near(0))
    near_max = lax.fori_loop(0, n_early, early_tile, near_max)
    lax.fori_loop(n_early, nb, late_tile, near_max)


def _rel_bucket_table(max_rel):
    n = np.arange(max_rel)
    max_exact = REL_BUCKETS // 2
    nf = np.maximum(n, 1).astype(np.float32)
    large = max_exact + (np.log(nf / np.float32(max_exact)) / np.float32(math.log(REL_MAX_DIST / max_exact))
                         * np.float32(REL_BUCKETS - max_exact)).astype(np.int32)
    large = np.minimum(large, REL_BUCKETS - 1)
    return np.where(n < max_exact, n, large)


def _bias_vectors(rel_bias):
    blk = MOBA_BLOCK
    table = _rel_bucket_table(2 * blk)
    r = np.arange(-blk, blk)
    bt = (rel_bias.T - rel_bias[REL_BUCKETS - 1][:, None]) * LOG2E
    own = jnp.where(jnp.asarray(r >= 0)[None], bt[:, table[np.maximum(r, 0)]], NEG_INF)
    prev = bt[:, table[np.minimum(r + blk, 2 * blk - 1)]]
    return jnp.stack([own, prev], axis=1).astype(F32)


def _moba_attention(qT, k3, vT, km_pad, bias_w, *, batch, seq):
    nb = seq // MOBA_BLOCK
    blk = MOBA_BLOCK
    assert nb % FAR_GROUP == 0 and nb <= MAX_BLOCKS
    kern = functools.partial(_attn_kernel, nb=nb)
    t_spec = pl.BlockSpec((1, nb, LANES, blk), lambda b, p: (b, 0, p, 0))
    return pl.pallas_call(
        kern,
        grid=(batch, N_HEADS // 2),
        in_specs=[t_spec,
                  pl.BlockSpec((1, seq, LANES), lambda b, p: (b, 0, p)),
                  t_spec,
                  pl.BlockSpec((1, MAX_BLOCKS, LANES), lambda b, p: (b, 0, p)),
                  pl.BlockSpec((2, 2, 2 * blk), lambda b, p: (p, 0, 0))],
        out_specs=t_spec,
        out_shape=jax.ShapeDtypeStruct((batch, nb, ATTN_WIDTH, blk), BF16),
        scratch_shapes=[pltpu.VMEM((2, seq, LANES), BF16),
                        pltpu.VMEM((nb, 2, HEAD_DIM + ONES_ROWS, blk), BF16),
                        pltpu.VMEM((2, 2, blk, blk), F32),
                        pltpu.VMEM((2, LANES, blk), BF16),
                        pltpu.VMEM((2, LANES, blk), BF16),
                        pltpu.VMEM((2, FAR_GROUP * blk, blk), F32),
                        pltpu.VMEM((2, FAR_GROUP * blk, blk), F32)],
        compiler_params=_cparams("arbitrary", "arbitrary"),
        name="moba_attention",
    )(qT, k3, vT, km_pad, bias_w)


def _post_kernel(x_ref, mod_ref, g1_ref, wz_ref, bz_ref, at_ref, u_ref, uh_ref, pw_ref, ps_ref,
                 wba_ref, wbp_ref, wo_ref, g2_ref, wrh_ref, wrl_ref, br_ref,
                 x1_ref, h2r_ref, route_ref, counts_ref, carry_ref, *, tm, seq):
    d = D_MODEL
    i = pl.program_id(0)
    x = x_ref[...]
    mod = mod_ref[0]
    shift1, scale1, gate1 = mod[:, 0:d], mod[:, d:2 * d], mod[:, 2 * d:3 * d]
    shift2, scale2 = mod[:, 3 * d:4 * d], mod[:, 4 * d:5 * d]

    h = _rms_mod(x, g1_ref[...], scale1, shift1)
    zg = jax.nn.sigmoid(_dot(h.astype(BF16), wz_ref[...]) + bz_ref[...])

    ya = jnp.concatenate(
        [lax.dot_general(at_ref[0, c], wba_ref[...], (((0,), (0,)), ((), ())),
                         preferred_element_type=F32) for c in range(tm // MOBA_BLOCK)], axis=0)

    first = (i * tm) % seq == 0
    halo = jnp.where(first, 0.0, uh_ref[...])
    ubuf = jnp.concatenate([halo, u_ref[...]], axis=0)
    rows = tm + POOL_HALO
    pos = (lax.broadcasted_iota(jnp.int32, (tm, POOL_GROUP_WIDTH), 0) + i * tm) % seq
    outs = []
    for gi, w in enumerate(POOL_WINDOWS):
        e = ubuf[:, gi * POOL_GROUP_WIDTH:(gi + 1) * POOL_GROUP_WIDTH]
        s = e
        step = 1
        while step < w:
            s = s + pltpu.roll(s, step, 0)
            step *= 2
        cnt = jnp.minimum(pos + 1, w).astype(F32)
        dlt = s[POOL_HALO:rows] / cnt - e[POOL_HALO:rows]
        outs.append(_dot(dlt.astype(BF16), pw_ref[gi]))
    pooled = jnp.concatenate(outs, axis=-1) * ps_ref[...]
    yp = _dot(pooled.astype(BF16), wbp_ref[...])

    merged = zg[:, 0:d] * ya + zg[:, d:2 * d] * yp
    x1 = x + gate1 * _dot(merged.astype(BF16), wo_ref[...])
    x1_ref[...] = x1

    h2 = _rms_mod(x1, g2_ref[...], scale2, shift2)
    h2_hi = h2.astype(BF16)
    h2r_ref[...] = h2_hi.astype(F32)
    h2_lo = (h2 - h2_hi.astype(F32)).astype(BF16)
    z = _dot(h2_hi, wrh_ref[...]) + _dot(h2_lo, wrh_ref[...]) + _dot(h2_hi, wrl_ref[...]) + br_ref[...]

    lane = lax.broadcasted_iota(jnp.int32, (tm, LANES), 1)
    zgrp = jnp.where(lane < N_GROUPS, z, -jnp.inf)
    mg = jnp.max(zgrp, axis=-1, keepdims=True)
    pg_top = 1.0 / jnp.sum(jnp.exp(zgrp - mg), axis=-1, keepdims=True)
    g_idx = jnp.min(jnp.where(zgrp == mg, lane, LANES), axis=-1, keepdims=True)
    e_lane = lane - ROUTER_LANE0
    in_grp = (e_lane >= 0) & (e_lane < N_EXPERTS) & ((e_lane >> 3) == g_idx)
    ze = jnp.where(in_grp, z, -jnp.inf)
    m1 = jnp.max(ze, axis=-1, keepdims=True)
    i1 = jnp.min(jnp.where(ze == m1, lane, LANES), axis=-1, keepdims=True)
    ze2 = jnp.where(lane == i1, -jnp.inf, ze)
    m2 = jnp.max(ze2, axis=-1, keepdims=True)
    i2 = jnp.min(jnp.where(ze2 == m2, lane, LANES), axis=-1, keepdims=True)
    e2 = jnp.exp(m2 - m1)
    w1 = pg_top / (1.0 + e2)
    w2 = pg_top * e2 / (1.0 + e2)

    hot = jnp.where((lane == i1) | (lane == i2), 1.0, 0.0)
    tri = jnp.where(lax.broadcasted_iota(jnp.int32, (tm, tm), 1) <= lax.broadcasted_iota(jnp.int32, (tm, tm), 0),
                    1.0, 0.0).astype(BF16)
    prefix = _dot(tri, hot.astype(BF16))

    @pl.when(i == 0)
    def _():
        carry_ref[...] = jnp.zeros_like(carry_ref)

    base = carry_ref[...] + prefix - 1.0
    rank1 = jnp.sum(jnp.where(lane == i1, base, 0.0), axis=-1, keepdims=True)
    rank2 = jnp.sum(jnp.where(lane == i2, base, 0.0), axis=-1, keepdims=True)
    carry_ref[...] = carry_ref[...] + prefix[tm - 1:tm, :]
    counts_ref[...] = carry_ref[...]
    fields = ((i1 - ROUTER_LANE0).astype(F32), (i2 - ROUTER_LANE0).astype(F32), w1, w2, rank1, rank2)
    route = jnp.zeros((tm, LANES), F32)
    for col, val in enumerate(fields):
        route = jnp.where(lane == col, val, route)
    route_ref[...] = route


def _post_attention(x2, mod3, g1, w_z, b_z, attnT, u, pool_w, pool_scale, w_ba, w_bp, w_out, g2,
                    wr_hi, wr_lo, b_r, *, batch, seq):
    n, d = x2.shape
    tm = TM_POST
    tps = seq // tm
    cb = tm // MOBA_BLOCK
    hb = tm // POOL_HALO
    kern = functools.partial(_post_kernel, tm=tm, seq=seq)
    const2 = lambda i: (0, 0)
    return pl.pallas_call(
        kern,
        grid=(n // tm,),
        in_specs=[pl.BlockSpec((tm, d), lambda i: (i, 0)),
                  pl.BlockSpec((1, 1, N_MOD * d), lambda i: (i // tps, 0, 0)),
                  pl.BlockSpec((1, d), const2),
                  pl.BlockSpec((d, 2 * d), lambda i: (0, 1)),
                  pl.BlockSpec((1, 2 * d), const2),
                  pl.BlockSpec((1, cb, ATTN_WIDTH, MOBA_BLOCK), lambda i: (i // tps, i % tps, 0, 0)),
                  pl.BlockSpec((tm, POOL_WIDTH), lambda i: (i, 0)),
                  pl.BlockSpec((POOL_HALO, POOL_WIDTH), lambda i: (jnp.maximum(i * hb - 1, 0), 0)),
                  pl.BlockSpec((len(POOL_WINDOWS), POOL_GROUP_WIDTH, POOL_GROUP_WIDTH), lambda i: (0, 0, 0)),
                  pl.BlockSpec((1, POOL_WIDTH), const2),
                  pl.BlockSpec((ATTN_WIDTH, d), const2),
                  pl.BlockSpec((POOL_WIDTH, d), const2),
                  pl.BlockSpec((d, d), const2),
                  pl.BlockSpec((1, d), const2),
                  pl.BlockSpec((d, LANES), const2),
                  pl.BlockSpec((d, LANES), const2),
                  pl.BlockSpec((1, LANES), const2)],
        out_specs=[pl.BlockSpec((tm, d), lambda i: (i, 0)),
                   pl.BlockSpec((tm, d), lambda i: (i, 0)),
                   pl.BlockSpec((tm, LANES), lambda i: (i, 0)),
                   pl.BlockSpec((1, LANES), const2)],
        out_shape=[jax.ShapeDtypeStruct((n, d), F32),
                   jax.ShapeDtypeStruct((n, d), F32),
                   jax.ShapeDtypeStruct((n, LANES), F32),
                   jax.ShapeDtypeStruct((1, LANES), F32)],
        scratch_shapes=[pltpu.VMEM((1, LANES), F32)],
        compiler_params=_cparams("arbitrary"),
        name="merge_outproj_router",
    )(x2, mod3, g1, w_z, b_z, attnT, u, u, pool_w, pool_scale, w_ba, w_bp, w_out, g2, wr_hi, wr_lo, b_r)


def _routing_plan(route, counts, *, n):
    te = TE_ROWS
    n_tiles = _expert_tiles(n)
    cnt = counts[0, ROUTER_LANE0:ROUTER_LANE0 + N_EXPERTS].astype(jnp.int32)
    padded = ((cnt + te - 1) // te) * te
    ids = jnp.arange(N_EXPERTS, dtype=jnp.int32)
    ends = jnp.sum(jnp.where(ids[None, :] <= ids[:, None], padded[None, :], 0), axis=-1)
    starts = ends - padded
    pos = []
    for k in range(EXPERT_TOPK):
        expert = route[:, k].astype(jnp.int32)
        rank = route[:, 4 + k].astype(jnp.int32)
        pos.append(jnp.sum(jnp.where(expert[:, None] == ids, starts, 0), axis=-1) + rank)
    tiles = jnp.arange(n_tiles, dtype=jnp.int32)
    tile_expert = jnp.sum((ends // te)[None, :] <= tiles[:, None], axis=-1)
    tile_expert = jnp.minimum(tile_expert, N_EXPERTS - 1).astype(jnp.int32)
    n_used = (ends[N_EXPERTS - 1:] // te).astype(jnp.int32)
    return pos, tile_expert, n_used


def _expert_tiles(n):
    return (EXPERT_TOPK * n) // TE_ROWS + N_EXPERTS


def _tile_pos(pos, tm):
    return jnp.concatenate([p.reshape(-1, 1, tm) for p in pos], axis=-1)


def _dispatch_kernel(pos_ref, h_ref, xz_ref, xs_ref, sem, *, tm):
    del xz_ref
    for r in range(tm):
        for k in range(EXPERT_TOPK):
            pltpu.make_async_copy(h_ref.at[pl.ds(r, 1)], xs_ref.at[pl.ds(pos_ref[0, 0, k * tm + r], 1)],
                                  sem).start(priority=(r + k) % 2)
    for k in range(EXPERT_TOPK):
        pltpu.make_async_copy(h_ref, h_ref, sem).wait()


def _dispatch(pos, h2r, *, n):
    tm = TM_DISPATCH
    rows = _expert_tiles(n) * TE_ROWS
    d = h2r.shape[1]
    pos3 = _tile_pos(pos, tm)
    return pl.pallas_call(
        functools.partial(_dispatch_kernel, tm=tm),
        grid=(n // tm,),
        in_specs=[pl.BlockSpec((1, 1, EXPERT_TOPK * tm), lambda i: (i, 0, 0), memory_space=pltpu.SMEM),
                  pl.BlockSpec((tm, d), lambda i: (i, 0)),
                  pl.BlockSpec(memory_space=pl.ANY)],
        out_specs=pl.BlockSpec(memory_space=pl.ANY),
        out_shape=jax.ShapeDtypeStruct((rows, d), F32),
        scratch_shapes=[pltpu.SemaphoreType.DMA(())],
        input_output_aliases={2: 0},
        compiler_params=_cparams("arbitrary"),
        name="moe_dispatch",
    )(pos3, h2r, jnp.zeros((rows, d), F32))


def _experts_kernel(te_ref, nu_ref, xs_ref, wg_ref, wu_ref, wd_ref, y_ref, wg_b, wu_b, wd_b):
    t = pl.program_id(0)

    @pl.when(t < nu_ref[0])
    def _():
        e = te_ref[t]
        e_prev = te_ref[jnp.maximum(t - 1, 0)]

        @pl.when((t == 0) | (e != e_prev))
        def _():
            wg_b[...] = wg_ref[0].astype(BF16)
            wu_b[...] = wu_ref[0].astype(BF16)
            wd_b[...] = wd_ref[0].astype(BF16)

        x = xs_ref[...].astype(BF16)
        g = _dot(x, wg_b[...])
        u = _dot(x, wu_b[...])
        act = (g * jax.nn.sigmoid(g)) * u
        y_ref[...] = _dot(act.astype(BF16), wd_b[...])

    @pl.when(t >= nu_ref[0])
    def _():
        y_ref[...] = jnp.zeros_like(y_ref)


def _experts(tile_expert, n_used, xs, wg, wu, wd):
    rows, d = xs.shape
    te = TE_ROWS

    def used(t, nu):
        return jnp.minimum(t, nu[0] - 1)

    grid_spec = pltpu.PrefetchScalarGridSpec(
        num_scalar_prefetch=2,
        grid=(rows // te,),
        in_specs=[pl.BlockSpec((te, d), lambda t, te_r, nu: (used(t, nu), 0)),
                  pl.BlockSpec((1, d, D_EXPERT), lambda t, te_r, nu: (te_r[used(t, nu)], 0, 0)),
                  pl.BlockSpec((1, d, D_EXPERT), lambda t, te_r, nu: (te_r[used(t, nu)], 0, 0)),
                  pl.BlockSpec((1, D_EXPERT, d), lambda t, te_r, nu: (te_r[used(t, nu)], 0, 0))],
        out_specs=pl.BlockSpec((te, d), lambda t, te_r, nu: (t, 0)),
        scratch_shapes=[pltpu.VMEM((d, D_EXPERT), BF16), pltpu.VMEM((d, D_EXPERT), BF16),
                        pltpu.VMEM((D_EXPERT, d), BF16)])
    return pl.pallas_call(
        _experts_kernel,
        grid_spec=grid_spec,
        out_shape=jax.ShapeDtypeStruct((rows, d), F32),
        compiler_params=_cparams("arbitrary"),
        name="moe_experts",
    )(tile_expert, n_used, xs, wg, wu, wd)


def _final_kernel(pos_ref, x1_ref, route_ref, mod_ref, g_ref, y_ref, o_ref, ybuf, sem, *, tm, final_norm):
    d = D_MODEL

    for r in range(tm):
        for k in range(EXPERT_TOPK):
            pltpu.make_async_copy(y_ref.at[pl.ds(pos_ref[0, 0, k * tm + r], 1)], ybuf.at[k, pl.ds(r, 1)],
                                  sem).start(priority=(r + k) % 2)
    for k in range(EXPERT_TOPK):
        pltpu.make_async_copy(ybuf.at[k], ybuf.at[k], sem).wait()

    route = route_ref[...]
    lane = lax.broadcasted_iota(jnp.int32, route.shape, 1)
    y = jnp.zeros((tm, d), F32)
    for k in range(EXPERT_TOPK):
        wk = jnp.sum(jnp.where(lane == 2 + k, route, 0.0), axis=-1, keepdims=True)
        y = y + wk * ybuf[k]
    gate2 = mod_ref[0][:, 5 * d:6 * d]
    x = x1_ref[...] + gate2 * y
    if final_norm:
        x = (x * lax.rsqrt(jnp.mean(x * x, axis=-1, keepdims=True) + EPS)) * g_ref[...]
    o_ref[...] = x


def _final(pos, x1, route, mod3, gf, y, *, seq, final_norm):
    n, d = x1.shape
    tm = TM_FINAL
    tps = seq // tm
    pos3 = _tile_pos(pos, tm)
    return pl.pallas_call(
        functools.partial(_final_kernel, tm=tm, final_norm=final_norm),
        grid=(n // tm,),
        in_specs=[pl.BlockSpec((1, 1, EXPERT_TOPK * tm), lambda i: (i, 0, 0), memory_space=pltpu.SMEM),
                  pl.BlockSpec((tm, d), lambda i: (i, 0)),
                  pl.BlockSpec((tm, LANES), lambda i: (i, 0)),
                  pl.BlockSpec((1, 1, N_MOD * d), lambda i: (i // tps, 0, 0)),
                  pl.BlockSpec((1, d), lambda i: (0, 0)),
                  pl.BlockSpec(memory_space=pl.ANY)],
        out_specs=pl.BlockSpec((tm, d), lambda i: (i, 0)),
        out_shape=jax.ShapeDtypeStruct((n, d), F32),
        scratch_shapes=[pltpu.VMEM((EXPERT_TOPK, tm, d), F32), pltpu.SemaphoreType.DMA(())],
        compiler_params=_cparams("arbitrary"),
        name="moe_combine_final_norm",
    )(pos3, x1, route, mod3, gf, y)


def _split_bf16(w):
    hi = w.astype(BF16)
    return hi, (w - hi.astype(F32)).astype(BF16)


def kernel(x, c, w_ada, b_ada, norm1_g, w_in, b_gate, rel_bias, pool_w, pool_scale, w_branch_attn,
           w_branch_pool, w_out, norm2_g, w_router_group, b_router_group, w_router_expert,
           b_router_expert, w_expert_gate, w_expert_up, w_expert_down, norm_f_g):
    batch, seq, d = x.shape
    depth = w_ada.shape[0]
    n = batch * seq
    nb = seq // MOBA_BLOCK
    qkvu = 3 * ATTN_WIDTH + POOL_WIDTH
    bias_w = _bias_vectors(rel_bias)
    c_pad = jnp.pad(c, ((0, 8 - batch), (0, 0)))

    xc = x.reshape(n, d)
    for l in range(depth):
        mod3 = _modulation(c_pad, w_ada[l], b_ada[l][None])[:batch].reshape(batch, 1, N_MOD * d)
        w_in_b = w_in[l].astype(BF16)
        g1 = norm1_g[l][None]

        assert w_in_b.shape[1] == 2 * qkvu == qkvu + 2 * d
        qT, k2, vT, u, km = _in_projection(xc, mod3, g1, w_in_b, batch=batch, seq=seq)
        km_pad = jnp.pad(km.reshape(batch, nb, ATTN_WIDTH), ((0, 0), (0, MAX_BLOCKS - nb), (0, 0)))
        attnT = _moba_attention(qT, k2.reshape(batch, seq, ATTN_WIDTH), vT, km_pad, bias_w,
                                batch=batch, seq=seq)

        w_r = jnp.concatenate([w_router_group[l], w_router_expert[l]], axis=1)
        w_r = jnp.pad(w_r, ((0, 0), (0, LANES - w_r.shape[1])))
        b_r = jnp.concatenate([b_router_group[l], b_router_expert[l]])
        b_r = jnp.pad(b_r, (0, LANES - b_r.shape[0]))[None]
        wr_hi, wr_lo = _split_bf16(w_r)
        x1, h2r, route, counts = _post_attention(
            xc, mod3, g1, w_in_b, b_gate[l][None], attnT, u, pool_w[l].astype(BF16),
            pool_scale[l][None], w_branch_attn[l].astype(BF16), w_branch_pool[l].astype(BF16),
            w_out[l].astype(BF16), norm2_g[l][None], wr_hi, wr_lo, b_r, batch=batch, seq=seq)

        pos, tile_expert, n_used = _routing_plan(route, counts, n=n)
        xs = _dispatch(pos, h2r, n=n)
        y = _experts(tile_expert, n_used, xs, w_expert_gate[l], w_expert_up[l], w_expert_down[l])
        xc = _final(pos, x1, route, mod3, norm_f_g[None], y, seq=seq, final_norm=(l == depth - 1))
    return xc.reshape(batch, seq, d)
```

```python
import functools
import math

import numpy as np
import jax
import jax.numpy as jnp
from jax import lax
from jax.experimental import pallas as pl
from jax.experimental.pallas import tpu as pltpu

F32 = jnp.float32
BF16 = jnp.bfloat16

D_MODEL = 1024
N_HEADS = 8
HEAD_DIM = 64
ATTN_WIDTH = N_HEADS * HEAD_DIM
MOBA_BLOCK = 256
MOBA_TOPK = 3
MAX_BLOCKS = 32
FAR_GROUP = 4
ONES_ROWS = 16
LOG2E = math.log2(math.e)
POOL_WINDOWS = (2, 4, 8, 16)
POOL_WIDTH = 512
POOL_GROUP_WIDTH = 128
POOL_HALO = 16
REL_BUCKETS = 32
REL_MAX_DIST = 128
N_GROUPS = 4
EXPERTS_PER_GROUP = 8
N_EXPERTS = N_GROUPS * EXPERTS_PER_GROUP
D_EXPERT = 512
N_MOD = 6
EPS = 1e-6
NEG_INF = -1e30
LANES = 128
ROUTER_LANE0 = N_GROUPS
ROUTE_FIELDS = 8
VMEM_LIMIT = 56 * 1024 * 1024

TM_PROJ = 512
TM_POST = 512
TM_DISPATCH = 512
TE_ROWS = 512
EXPERT_TOPK = 2
TM_FINAL = 256


def _cparams(*sem):
    return pltpu.CompilerParams(dimension_semantics=sem, vmem_limit_bytes=VMEM_LIMIT)


def _dot(a, b):
    return jnp.dot(a, b, preferred_element_type=F32)


def _rms_mod(x, g, scale, shift):
    xn = x * lax.rsqrt(jnp.mean(x * x, axis=-1, keepdims=True) + EPS)
    return (xn * g) * (1.0 + scale) + shift


def _mod_kernel(c_ref, w_ref, b_ref, o_ref):
    c = c_ref[...]
    ca = c * jax.nn.sigmoid(c)
    o_ref[...] = _dot(ca.astype(BF16), w_ref[...].astype(BF16)) + b_ref[...]


def _modulation(c_pad, w_ada, b_ada):
    rows, d = c_pad.shape
    n_out = w_ada.shape[1]
    tn = 1024
    return pl.pallas_call(
        _mod_kernel,
        grid=(n_out // tn,),
        in_specs=[pl.BlockSpec((rows, d), lambda j: (0, 0)),
                  pl.BlockSpec((d, tn), lambda j: (0, j)),
                  pl.BlockSpec((1, tn), lambda j: (0, j))],
        out_specs=pl.BlockSpec((rows, tn), lambda j: (0, j)),
        out_shape=jax.ShapeDtypeStruct((rows, n_out), F32),
        compiler_params=_cparams("arbitrary"),
        name="adaln_mod",
    )(c_pad, w_ada, b_ada)


def _inproj_kernel(x_ref, mod_ref, g_ref, w_ref, qT_ref, k_ref, vT_ref, u_ref, km_ref, *, tm):
    d = D_MODEL
    mod = mod_ref[0]
    h = _rms_mod(x_ref[...], g_ref[...], mod[:, d:2 * d], mod[:, 0:d])
    proj = _dot(h.astype(BF16), w_ref[...])
    aw = ATTN_WIDTH
    q = proj[:, 0:aw] * (HEAD_DIM ** -0.5 * LOG2E)
    k = proj[:, aw:2 * aw]
    v = proj[:, 2 * aw:3 * aw]
    u_ref[...] = proj[:, 3 * aw:3 * aw + POOL_WIDTH]
    k_ref[...] = k.astype(BF16)
    qT = q.T.astype(BF16)
    vT = v.T.astype(BF16)
    for c in range(tm // MOBA_BLOCK):
        sl = slice(c * MOBA_BLOCK, (c + 1) * MOBA_BLOCK)
        qT_ref[0, c] = qT[:, sl]
        vT_ref[0, c] = vT[:, sl]
        km_ref[0, c:c + 1, :] = jnp.mean(k[sl, :], axis=0, keepdims=True)


def _in_projection(x2, mod3, g1, w_qkvu, *, batch, seq):
    n, d = x2.shape
    tm = TM_PROJ
    tps = seq // tm
    nb = seq // MOBA_BLOCK
    cb = tm // MOBA_BLOCK
    wcols = 3 * ATTN_WIDTH + POOL_WIDTH
    assert w_qkvu.shape[1] % wcols == 0
    kern = functools.partial(_inproj_kernel, tm=tm)
    t_shape = jax.ShapeDtypeStruct((batch, nb, ATTN_WIDTH, MOBA_BLOCK), BF16)
    t_spec = pl.BlockSpec((1, cb, ATTN_WIDTH, MOBA_BLOCK), lambda i: (i // tps, i % tps, 0, 0))
    return pl.pallas_call(
        kern,
        grid=(n // tm,),
        in_specs=[pl.BlockSpec((tm, d), lambda i: (i, 0)),
                  pl.BlockSpec((1, 1, N_MOD * d), lambda i: (i // tps, 0, 0)),
                  pl.BlockSpec((1, d), lambda i: (0, 0)),
                  pl.BlockSpec((d, wcols), lambda i: (0, 0))],
        out_specs=[t_spec,
                   pl.BlockSpec((tm, ATTN_WIDTH), lambda i: (i, 0)),
                   t_spec,
                   pl.BlockSpec((tm, POOL_WIDTH), lambda i: (i, 0)),
                   pl.BlockSpec((1, cb, ATTN_WIDTH), lambda i: (i, 0, 0))],
        out_shape=[t_shape,
                   jax.ShapeDtypeStruct((n, ATTN_WIDTH), BF16),
                   t_shape,
                   jax.ShapeDtypeStruct((n, POOL_WIDTH), F32),
                   jax.ShapeDtypeStruct((n // tm, cb, ATTN_WIDTH), F32)],
        compiler_params=_cparams("arbitrary"),
        name="norm1_inproj",
    )(x2, mod3, g1, w_qkvu)


def _attn_kernel(qT_ref, k_ref, vT_ref, km_ref, bw_ref, o_ref, kaug_ref, vaug_ref, bias_ref, qnear_ref,
                 qfar_ref, sn_ref, sf_ref, *, nb):
    blk = MOBA_BLOCK
    hd = HEAD_DIM
    lane = lax.broadcasted_iota(jnp.int32, (blk, LANES), 1)

    ones = jnp.ones((ONES_ROWS, blk), BF16)

    def build_operands(j, carry):
        rows = pl.ds(pl.multiple_of(j * blk, blk), blk)
        kp = k_ref[0, rows, :]
        kaug_ref[0, rows, :] = jnp.where(lane < hd, kp, jnp.where(lane == hd + j, 1.0, 0.0).astype(BF16))
        kaug_ref[1, rows, :] = jnp.where(lane >= hd, kp, jnp.where(lane == j, 1.0, 0.0).astype(BF16))
        for hh in range(2):
            vaug_ref[j, hh] = jnp.concatenate([vT_ref[0, j, hh * hd:(hh + 1) * hd, :], ones], axis=0)
        return carry

    lax.fori_loop(0, nb, build_operands, 0)

    for hh in range(2):
        for t in range(2):
            wide = jnp.broadcast_to(bw_ref[hh, t:t + 1, :], (blk, 2 * blk))
            bias_ref[hh, t] = pltpu.roll(wide, 0, 1, stride=1, stride_axis=0)[:, blk:2 * blk]

    rowi = lax.broadcasted_iota(jnp.int32, (MAX_BLOCKS, blk), 0)
    km_lane = lax.broadcasted_iota(jnp.int32, (MAX_BLOCKS, LANES), 1)
    zpad = jnp.zeros((LANES - hd - MAX_BLOCKS, blk), BF16)
    km_split = []
    for hh in range(2):
        in_head = (km_lane >= hh * hd) & (km_lane < (hh + 1) * hd)
        km_h = jnp.where(in_head, km_ref[0], 0.0)
        km_hi = km_h.astype(BF16)
        km_split.append((km_hi, (km_h - km_hi.astype(F32)).astype(BF16)))

    def head_rows(hh):
        return slice(hh * hd, (hh + 1) * hd)

    grp = FAR_GROUP

    def retire_begin(state, mx):
        m, acc = state
        m_new = jnp.maximum(m, mx)
        return m_new, jnp.exp2(m - m_new) * acc

    def retire_slice(src_ref, hh, m_new, acc, g, j):
        p = jnp.exp2(src_ref[hh, g * blk:(g + 1) * blk, :] - m_new)
        return acc + _dot(vaug_ref[j, hh], p.astype(BF16))

    def fresh():
        return (jnp.full((1, blk), -jnp.inf, F32), jnp.zeros((hd + ONES_ROWS, blk), F32))

    def near_blocks(i):
        return tuple(jnp.where(i - (grp - 1) + g < 0, i + 1, i - (grp - 1) + g) for g in range(grp))

    def finish(i, states):
        for hh in range(2):
            _, acc = states[hh]
            o_ref[0, i, head_rows(hh), :] = (acc[0:hd] / acc[hd:hd + 1]).astype(BF16)

    def select(i):
        qp = qT_ref[0, i]
        for hh in range(2):
            km_hi, km_lo = km_split[hh]
            gate = _dot(km_hi, qp) + _dot(km_lo, qp)
            g = jnp.where(rowi < i, gate, -jnp.inf)
            sel = rowi == i
            for _ in range(MOBA_TOPK):
                m = jnp.max(g, axis=0, keepdims=True)
                hit = (g == m) & (m > -jnp.inf)
                idx = jnp.min(jnp.where(hit, rowi, MAX_BLOCKS), axis=0, keepdims=True)
                pick = rowi == idx
                sel = sel | pick
                g = jnp.where(pick, -jnp.inf, g)
            near = jnp.where(sel, 0.0, NEG_INF)
            far = jnp.where(rowi > i - grp, NEG_INF, near)
            qh = qp[head_rows(hh), :]
            if hh == 0:
                qnear_ref[hh] = jnp.concatenate([qh, near.astype(BF16), zpad], axis=0)
                qfar_ref[hh] = jnp.concatenate([qh, far.astype(BF16), zpad], axis=0)
            else:
                qnear_ref[hh] = jnp.concatenate([near.astype(BF16), zpad, qh], axis=0)
                qfar_ref[hh] = jnp.concatenate([far.astype(BF16), zpad, qh], axis=0)

    def park(dst_ref, q_ref, hh, g, j, bias=None):
        rows = pl.ds(pl.multiple_of(j * blk, blk), blk)
        s = _dot(kaug_ref[hh, rows, :], q_ref[hh])
        if bias is not None:
            s = s + bias
        dst_ref[hh, g * blk:(g + 1) * blk, :] = s
        return jnp.max(s, axis=0, keepdims=True)

    def park_near(i):
        idx = near_blocks(i)

        def quarter(g, hh):
            bias = None
            if g == grp - 1:
                bias = bias_ref[hh, 0]
            elif g == grp - 2:
                bias = bias_ref[hh, 1]
            return park(sn_ref, qnear_ref, hh, g, idx[g], bias)
        return quarter

    def park_far(c):
        return lambda g, hh: park(sf_ref, qfar_ref, hh, g, c * grp + g)

    def chunk_blocks(c):
        return tuple(c * grp + g for g in range(grp))

    def step(src_ref, states, maxes, vidx, quarter):
        begun = [retire_begin(states[hh], maxes[hh]) for hh in range(2)] if states is not None else None
        accs = [b[1] for b in begun] if begun else None
        new_max = [None, None]
        for g in range(grp):
            if begun:
                for hh in range(2):
                    accs[hh] = retire_slice(src_ref, hh, begun[hh][0], accs[hh], g, vidx[g])
            if quarter is not None:
                for hh in range(2):
                    cm = quarter(g, hh)
                    new_max[hh] = cm if new_max[hh] is None else jnp.maximum(new_max[hh], cm)
        new_states = tuple((begun[hh][0], accs[hh]) for hh in range(2)) if begun else None
        return new_states, tuple(new_max)

    def fresh_states():
        return (fresh(), fresh())

    def early_tile(i, near_max):
        nxt = jnp.minimum(i + 1, nb - 1)
        select(nxt)
        states, next_max = step(sn_ref, fresh_states(), near_max, near_blocks(i), park_near(nxt))
        finish(i, states)
        return next_max

    def late_tile(i, near_max):
        states, far_max = step(sn_ref, fresh_states(), near_max, near_blocks(i), park_far(0))

        def far_step(c, st):
            return step(sf_ref, st[0], st[1], chunk_blocks(c - 1), park_far(c))

        n_chunks = (i - (grp - 1) + grp - 1) // grp
        states, far_max = lax.fori_loop(1, n_chunks, far_step, (states, far_max))
        nxt = jnp.minimum(i + 1, nb - 1)
        select(nxt)
        states, next_max = step(sf_ref, states, far_max, chunk_blocks(n_chunks - 1), park_near(nxt))
        finish(i, states)
        return next_max

    n_early = min(grp, nb)
    select(0)
    _, near_max = step(None, None, None, None, park_near(0))
    near_max = lax.fori_loop(0, n_early, early_tile, near_max)
    lax.fori_loop(n_early, nb, late_tile, near_max)


def _rel_bucket_table(max_rel):
    n = np.arange(max_rel)
    max_exact = REL_BUCKETS // 2
    nf = np.maximum(n, 1).astype(np.float32)
    large = max_exact + (np.log(nf / np.float32(max_exact)) / np.float32(math.log(REL_MAX_DIST / max_exact))
                         * np.float32(REL_BUCKETS - max_exact)).astype(np.int32)
    large = np.minimum(large, REL_BUCKETS - 1)
    return np.where(n < max_exact, n, large)


def _bias_vectors(rel_bias):
    blk = MOBA_BLOCK
    table = _rel_bucket_table(2 * blk)
    r = np.arange(-blk, blk)
    bt = (rel_bias.T - rel_bias[REL_BUCKETS - 1][:, None]) * LOG2E
    own = jnp.where(jnp.asarray(r >= 0)[None], bt[:, table[np.maximum(r, 0)]], NEG_INF)
    prev = bt[:, table[np.minimum(r + blk, 2 * blk - 1)]]
    return jnp.stack([own, prev], axis=1).astype(F32)


def _moba_attention(qT, k3, vT, km_pad, bias_w, *, batch, seq):
    nb = seq // MOBA_BLOCK
    blk = MOBA_BLOCK
    assert nb % FAR_GROUP == 0 and nb <= MAX_BLOCKS
    kern = functools.partial(_attn_kernel, nb=nb)
    t_spec = pl.BlockSpec((1, nb, LANES, blk), lambda b, p: (b, 0, p, 0))
    return pl.pallas_call(
        kern,
        grid=(batch, N_HEADS // 2),
        in_specs=[t_spec,
                  pl.BlockSpec((1, seq, LANES), lambda b, p: (b, 0, p)),
                  t_spec,
                  pl.BlockSpec((1, MAX_BLOCKS, LANES), lambda b, p: (b, 0, p)),
                  pl.BlockSpec((2, 2, 2 * blk), lambda b, p: (p, 0, 0))],
        out_specs=t_spec,
        out_shape=jax.ShapeDtypeStruct((batch, nb, ATTN_WIDTH, blk), BF16),
        scratch_shapes=[pltpu.VMEM((2, seq, LANES), BF16),
                        pltpu.VMEM((nb, 2, HEAD_DIM + ONES_ROWS, blk), BF16),
                        pltpu.VMEM((2, 2, blk, blk), F32),
                        pltpu.VMEM((2, LANES, blk), BF16),
                        pltpu.VMEM((2, LANES, blk), BF16),
                        pltpu.VMEM((2, FAR_GROUP * blk, blk), F32),
                        pltpu.VMEM((2, FAR_GROUP * blk, blk), F32)],
        compiler_params=_cparams("arbitrary", "arbitrary"),
        name="moba_attention",
    )(qT, k3, vT, km_pad, bias_w)


def _post_kernel(x_ref, mod_ref, g1_ref, wz_ref, bz_ref, at_ref, u_ref, uh_ref, pw_ref, ps_ref,
                 wba_ref, wbp_ref, wo_ref, g2_ref, wrh_ref, wrl_ref, br_ref,
                 x1_ref, h2r_ref, route_ref, route_t_ref, counts_ref, xz_ref, carry_ref, *, tm, seq):
    d = D_MODEL
    i = pl.program_id(0)
    x = x_ref[...]
    mod = mod_ref[0]
    shift1, scale1, gate1 = mod[:, 0:d], mod[:, d:2 * d], mod[:, 2 * d:3 * d]
    shift2, scale2 = mod[:, 3 * d:4 * d], mod[:, 4 * d:5 * d]

    h = _rms_mod(x, g1_ref[...], scale1, shift1)
    zg = jax.nn.sigmoid(_dot(h.astype(BF16), wz_ref[...]) + bz_ref[...])

    ya = jnp.concatenate(
        [lax.dot_general(at_ref[0, c], wba_ref[...], (((0,), (0,)), ((), ())),
                         preferred_element_type=F32) for c in range(tm // MOBA_BLOCK)], axis=0)

    first = (i * tm) % seq == 0
    halo = jnp.where(first, 0.0, uh_ref[...])
    ubuf = jnp.concatenate([halo, u_ref[...]], axis=0)
    rows = tm + POOL_HALO
    pos = (lax.broadcasted_iota(jnp.int32, (tm, POOL_GROUP_WIDTH), 0) + i * tm) % seq
    outs = []
    for gi, w in enumerate(POOL_WINDOWS):
        e = ubuf[:, gi * POOL_GROUP_WIDTH:(gi + 1) * POOL_GROUP_WIDTH]
        s = e
        step = 1
        while step < w:
            s = s + pltpu.roll(s, step, 0)
            step *= 2
        cnt = jnp.minimum(pos + 1, w).astype(F32)
        dlt = s[POOL_HALO:rows] / cnt - e[POOL_HALO:rows]
        outs.append(_dot(dlt.astype(BF16), pw_ref[gi]))
    pooled = jnp.concatenate(outs, axis=-1) * ps_ref[...]
    yp = _dot(pooled.astype(BF16), wbp_ref[...])

    merged = zg[:, 0:d] * ya + zg[:, d:2 * d] * yp
    x1 = x + gate1 * _dot(merged.astype(BF16), wo_ref[...])
    x1_ref[...] = x1

    h2 = _rms_mod(x1, g2_ref[...], scale2, shift2)
    h2_hi = h2.astype(BF16)
    h2r_ref[...] = h2_hi.astype(F32)
    h2_lo = (h2 - h2_hi.astype(F32)).astype(BF16)
    z = _dot(h2_hi, wrh_ref[...]) + _dot(h2_lo, wrh_ref[...]) + _dot(h2_hi, wrl_ref[...]) + br_ref[...]

    lane = lax.broadcasted_iota(jnp.int32, (tm, LANES), 1)
    zgrp = jnp.where(lane < N_GROUPS, z, -jnp.inf)
    mg = jnp.max(zgrp, axis=-1, keepdims=True)
    pg_top = 1.0 / jnp.sum(jnp.exp(zgrp - mg), axis=-1, keepdims=True)
    g_idx = jnp.min(jnp.where(zgrp == mg, lane, LANES), axis=-1, keepdims=True)
    e_lane = lane - ROUTER_LANE0
    in_grp = (e_lane >= 0) & (e_lane < N_EXPERTS) & ((e_lane >> 3) == g_idx)
    ze = jnp.where(in_grp, z, -jnp.inf)
    m1 = jnp.max(ze, axis=-1, keepdims=True)
    i1 = jnp.min(jnp.where(ze == m1, lane, LANES), axis=-1, keepdims=True)
    ze2 = jnp.where(lane == i1, -jnp.inf, ze)
    m2 = jnp.max(ze2, axis=-1, keepdims=True)
    i2 = jnp.min(jnp.where(ze2 == m2, lane, LANES), axis=-1, keepdims=True)
    e2 = jnp.exp(m2 - m1)
    w1 = pg_top / (1.0 + e2)
    w2 = pg_top * e2 / (1.0 + e2)

    hot = jnp.where((lane == i1) | (lane == i2), 1.0, 0.0)
    tri = jnp.where(lax.broadcasted_iota(jnp.int32, (tm, tm), 1) <= lax.broadcasted_iota(jnp.int32, (tm, tm), 0),
                    1.0, 0.0).astype(BF16)
    prefix = _dot(tri, hot.astype(BF16))

    @pl.when(i == 0)
    def _():
        carry_ref[...] = jnp.zeros_like(carry_ref)

    base = carry_ref[...] + prefix - 1.0
    rank1 = jnp.sum(jnp.where(lane == i1, base, 0.0), axis=-1, keepdims=True)
    rank2 = jnp.sum(jnp.where(lane == i2, base, 0.0), axis=-1, keepdims=True)
    carry_ref[...] = carry_ref[...] + prefix[tm - 1:tm, :]
    counts_ref[...] = carry_ref[...]
    fields = ((i1 - ROUTER_LANE0).astype(F32), (i2 - ROUTER_LANE0).astype(F32), w1, w2, rank1, rank2)
    route = jnp.zeros((tm, LANES), F32)
    for col, val in enumerate(fields):
        route = jnp.where(lane == col, val, route)
    route_ref[...] = route
    route_t_ref[...] = route.T[0:ROUTE_FIELDS, :]
    xz_ref[...] = jnp.zeros_like(xz_ref)


def _post_attention(x2, mod3, g1, w_z, b_z, attnT, u, pool_w, pool_scale, w_ba, w_bp, w_out, g2,
                    wr_hi, wr_lo, b_r, *, batch, seq):
    n, d = x2.shape
    tm = TM_POST
    tps = seq // tm
    cb = tm // MOBA_BLOCK
    hb = tm // POOL_HALO
    sorted_rows = _expert_tiles(n) * TE_ROWS
    assert sorted_rows % (n // tm) == 0
    zrows = sorted_rows // (n // tm)
    kern = functools.partial(_post_kernel, tm=tm, seq=seq)
    const2 = lambda i: (0, 0)
    return pl.pallas_call(
        kern,
        grid=(n // tm,),
        in_specs=[pl.BlockSpec((tm, d), lambda i: (i, 0)),
                  pl.BlockSpec((1, 1, N_MOD * d), lambda i: (i // tps, 0, 0)),
                  pl.BlockSpec((1, d), const2),
                  pl.BlockSpec((d, 2 * d), lambda i: (0, 1)),
                  pl.BlockSpec((1, 2 * d), const2),
                  pl.BlockSpec((1, cb, ATTN_WIDTH, MOBA_BLOCK), lambda i: (i // tps, i % tps, 0, 0)),
                  pl.BlockSpec((tm, POOL_WIDTH), lambda i: (i, 0)),
                  pl.BlockSpec((POOL_HALO, POOL_WIDTH), lambda i: (jnp.maximum(i * hb - 1, 0), 0)),
                  pl.BlockSpec((len(POOL_WINDOWS), POOL_GROUP_WIDTH, POOL_GROUP_WIDTH), lambda i: (0, 0, 0)),
                  pl.BlockSpec((1, POOL_WIDTH), const2),
                  pl.BlockSpec((ATTN_WIDTH, d), const2),
                  pl.BlockSpec((POOL_WIDTH, d), const2),
                  pl.BlockSpec((d, d), const2),
                  pl.BlockSpec((1, d), const2),
                  pl.BlockSpec((d, LANES), const2),
                  pl.BlockSpec((d, LANES), const2),
                  pl.BlockSpec((1, LANES), const2)],
        out_specs=[pl.BlockSpec((tm, d), lambda i: (i, 0)),
                   pl.BlockSpec((tm, d), lambda i: (i, 0)),
                   pl.BlockSpec((tm, LANES), lambda i: (i, 0)),
                   pl.BlockSpec((ROUTE_FIELDS, tm), lambda i: (0, i)),
                   pl.BlockSpec((1, LANES), const2),
                   pl.BlockSpec((zrows, d), lambda i: (i, 0))],
        out_shape=[jax.ShapeDtypeStruct((n, d), F32),
                   jax.ShapeDtypeStruct((n, d), F32),
                   jax.ShapeDtypeStruct((n, LANES), F32),
                   jax.ShapeDtypeStruct((ROUTE_FIELDS, n), F32),
                   jax.ShapeDtypeStruct((1, LANES), F32),
                   jax.ShapeDtypeStruct((zrows * (n // tm), d), F32)],
        scratch_shapes=[pltpu.VMEM((1, LANES), F32)],
        compiler_params=_cparams("arbitrary"),
        name="merge_outproj_router",
    )(x2, mod3, g1, w_z, b_z, attnT, u, u, pool_w, pool_scale, w_ba, w_bp, w_out, g2, wr_hi, wr_lo, b_r)


def _routing_plan(route_t, counts, *, n):
    te = TE_ROWS
    n_tiles = _expert_tiles(n)
    cnt = counts[0, ROUTER_LANE0:ROUTER_LANE0 + N_EXPERTS].astype(jnp.int32)
    padded = ((cnt + te - 1) // te) * te
    ids = jnp.arange(N_EXPERTS, dtype=jnp.int32)
    ends = jnp.sum(jnp.where(ids[None, :] <= ids[:, None], padded[None, :], 0), axis=-1)
    starts = ends - padded
    pos = []
    for k in range(EXPERT_TOPK):
        expert = route_t[k].astype(jnp.int32)
        rank = route_t[4 + k].astype(jnp.int32)
        pos.append(jnp.sum(jnp.where(expert[:, None] == ids, starts, 0), axis=-1) + rank)
    tiles = jnp.arange(n_tiles, dtype=jnp.int32)
    tile_expert = jnp.sum((ends // te)[None, :] <= tiles[:, None], axis=-1)
    tile_expert = jnp.minimum(tile_expert, N_EXPERTS - 1).astype(jnp.int32)
    n_used = (ends[N_EXPERTS - 1:] // te).astype(jnp.int32)
    return pos, tile_expert, n_used


def _expert_tiles(n):
    return (EXPERT_TOPK * n) // TE_ROWS + N_EXPERTS


def _tile_pos(pos, tm):
    return jnp.concatenate([p.reshape(-1, 1, tm) for p in pos], axis=-1)


def _dispatch_kernel(pos_ref, h_ref, xz_ref, xs_ref, sem, *, tm):
    del xz_ref
    for r in range(tm):
        for k in range(EXPERT_TOPK):
            pltpu.make_async_copy(h_ref.at[pl.ds(r, 1)], xs_ref.at[pl.ds(pos_ref[0, 0, k * tm + r], 1)],
                                  sem).start(priority=(r + k) % 2)
    for k in range(EXPERT_TOPK):
        pltpu.make_async_copy(h_ref, h_ref, sem).wait()


def _dispatch(pos, h2r, xz, *, n):
    tm = TM_DISPATCH
    rows, d = xz.shape
    assert rows == _expert_tiles(n) * TE_ROWS
    pos3 = _tile_pos(pos, tm)
    return pl.pallas_call(
        functools.partial(_dispatch_kernel, tm=tm),
        grid=(n // tm,),
        in_specs=[pl.BlockSpec((1, 1, EXPERT_TOPK * tm), lambda i: (i, 0, 0), memory_space=pltpu.SMEM),
                  pl.BlockSpec((tm, d), lambda i: (i, 0)),
                  pl.BlockSpec(memory_space=pl.ANY)],
        out_specs=pl.BlockSpec(memory_space=pl.ANY),
        out_shape=jax.ShapeDtypeStruct((rows, d), F32),
        scratch_shapes=[pltpu.SemaphoreType.DMA(())],
        input_output_aliases={2: 0},
        compiler_params=_cparams("arbitrary"),
        name="moe_dispatch",
    )(pos3, h2r, xz)


def _experts_kernel(te_ref, nu_ref, xs_ref, wg_ref, wu_ref, wd_ref, y_ref, wg_b, wu_b, wd_b):
    t = pl.program_id(0)

    @pl.when(t < nu_ref[0])
    def _():
        e = te_ref[t]
        e_prev = te_ref[jnp.maximum(t - 1, 0)]

        @pl.when((t == 0) | (e != e_prev))
        def _():
            wg_b[...] = wg_ref[0].astype(BF16)
            wu_b[...] = wu_ref[0].astype(BF16)
            wd_b[...] = wd_ref[0].astype(BF16)

        x = xs_ref[...].astype(BF16)
        g = _dot(x, wg_b[...])
        u = _dot(x, wu_b[...])
        act = (g * jax.nn.sigmoid(g)) * u
        y_ref[...] = _dot(act.astype(BF16), wd_b[...])

    @pl.when(t >= nu_ref[0])
    def _():
        y_ref[...] = jnp.zeros_like(y_ref)


def _experts(tile_expert, n_used, xs, wg, wu, wd):
    rows, d = xs.shape
    te = TE_ROWS

    def used(t, nu):
        return jnp.minimum(t, nu[0] - 1)

    grid_spec = pltpu.PrefetchScalarGridSpec(
        num_scalar_prefetch=2,
        grid=(rows // te,),
        in_specs=[pl.BlockSpec((te, d), lambda t, te_r, nu: (used(t, nu), 0)),
                  pl.BlockSpec((1, d, D_EXPERT), lambda t, te_r, nu: (te_r[used(t, nu)], 0, 0)),
                  pl.BlockSpec((1, d, D_EXPERT), lambda t, te_r, nu: (te_r[used(t, nu)], 0, 0)),
                  pl.BlockSpec((1, D_EXPERT, d), lambda t, te_r, nu: (te_r[used(t, nu)], 0, 0))],
        out_specs=pl.BlockSpec((te, d), lambda t, te_r, nu: (t, 0)),
        scratch_shapes=[pltpu.VMEM((d, D_EXPERT), BF16), pltpu.VMEM((d, D_EXPERT), BF16),
                        pltpu.VMEM((D_EXPERT, d), BF16)])
    return pl.pallas_call(
        _experts_kernel,
        grid_spec=grid_spec,
        out_shape=jax.ShapeDtypeStruct((rows, d), F32),
        compiler_params=_cparams("arbitrary"),
        name="moe_experts",
    )(tile_expert, n_used, xs, wg, wu, wd)


def _final_kernel(pos_ref, x1_ref, route_ref, mod_ref, g_ref, y_ref, o_ref, ybuf, sem, *, tm, final_norm):
    d = D_MODEL

    for r in range(tm):
        for k in range(EXPERT_TOPK):
            pltpu.make_async_copy(y_ref.at[pl.ds(pos_ref[0, 0, k * tm + r], 1)], ybuf.at[k, pl.ds(r, 1)],
                                  sem).start(priority=(r + k) % 2)
    for k in range(EXPERT_TOPK):
        pltpu.make_async_copy(ybuf.at[k], ybuf.at[k], sem).wait()

    route = route_ref[...]
    lane = lax.broadcasted_iota(jnp.int32, route.shape, 1)
    y = jnp.zeros((tm, d), F32)
    for k in range(EXPERT_TOPK):
        wk = jnp.sum(jnp.where(lane == 2 + k, route, 0.0), axis=-1, keepdims=True)
        y = y + wk * ybuf[k]
    gate2 = mod_ref[0][:, 5 * d:6 * d]
    x = x1_ref[...] + gate2 * y
    if final_norm:
        x = (x * lax.rsqrt(jnp.mean(x * x, axis=-1, keepdims=True) + EPS)) * g_ref[...]
    o_ref[...] = x


def _final(pos, x1, route, mod3, gf, y, *, seq, final_norm):
    n, d = x1.shape
    tm = TM_FINAL
    tps = seq // tm
    pos3 = _tile_pos(pos, tm)
    return pl.pallas_call(
        functools.partial(_final_kernel, tm=tm, final_norm=final_norm),
        grid=(n // tm,),
        in_specs=[pl.BlockSpec((1, 1, EXPERT_TOPK * tm), lambda i: (i, 0, 0), memory_space=pltpu.SMEM),
                  pl.BlockSpec((tm, d), lambda i: (i, 0)),
                  pl.BlockSpec((tm, LANES), lambda i: (i, 0)),
                  pl.BlockSpec((1, 1, N_MOD * d), lambda i: (i // tps, 0, 0)),
                  pl.BlockSpec((1, d), lambda i: (0, 0)),
                  pl.BlockSpec(memory_space=pl.ANY)],
        out_specs=pl.BlockSpec((tm, d), lambda i: (i, 0)),
        out_shape=jax.ShapeDtypeStruct((n, d), F32),
        scratch_shapes=[pltpu.VMEM((EXPERT_TOPK, tm, d), F32), pltpu.SemaphoreType.DMA(())],
        compiler_params=_cparams("arbitrary"),
        name="moe_combine_final_norm",
    )(pos3, x1, route, mod3, gf, y)


def _split_bf16(w):
    hi = w.astype(BF16)
    return hi, (w - hi.astype(F32)).astype(BF16)


def kernel(x, c, w_ada, b_ada, norm1_g, w_in, b_gate, rel_bias, pool_w, pool_scale, w_branch_attn,
           w_branch_pool, w_out, norm2_g, w_router_group, b_router_group, w_router_expert,
           b_router_expert, w_expert_gate, w_expert_up, w_expert_down, norm_f_g):
    batch, seq, d = x.shape
    depth = w_ada.shape[0]
    n = batch * seq
    nb = seq // MOBA_BLOCK
    qkvu = 3 * ATTN_WIDTH + POOL_WIDTH
    bias_w = _bias_vectors(rel_bias)
    c_pad = jnp.pad(c, ((0, 8 - batch), (0, 0)))

    xc = x.reshape(n, d)
    for l in range(depth):
        mod3 = _modulation(c_pad, w_ada[l], b_ada[l][None])[:batch].reshape(batch, 1, N_MOD * d)
        w_in_b = w_in[l].astype(BF16)
        g1 = norm1_g[l][None]

        assert w_in_b.shape[1] == 2 * qkvu == qkvu + 2 * d
        qT, k2, vT, u, km = _in_projection(xc, mod3, g1, w_in_b, batch=batch, seq=seq)
        km_pad = jnp.pad(km.reshape(batch, nb, ATTN_WIDTH), ((0, 0), (0, MAX_BLOCKS - nb), (0, 0)))
        attnT = _moba_attention(qT, k2.reshape(batch, seq, ATTN_WIDTH), vT, km_pad, bias_w,
                                batch=batch, seq=seq)

        w_r = jnp.concatenate([w_router_group[l], w_router_expert[l]], axis=1)
        w_r = jnp.pad(w_r, ((0, 0), (0, LANES - w_r.shape[1])))
        b_r = jnp.concatenate([b_router_group[l], b_router_expert[l]])
        b_r = jnp.pad(b_r, (0, LANES - b_r.shape[0]))[None]
        wr_hi, wr_lo = _split_bf16(w_r)
        x1, h2r, route, route_t, counts, xz = _post_attention(
            xc, mod3, g1, w_in_b, b_gate[l][None], attnT, u, pool_w[l].astype(BF16),
            pool_scale[l][None], w_branch_attn[l].astype(BF16), w_branch_pool[l].astype(BF16),
            w_out[l].astype(BF16), norm2_g[l][None], wr_hi, wr_lo, b_r, batch=batch, seq=seq)

        pos, tile_expert, n_used = _routing_plan(route_t, counts, n=n)
        xs = _dispatch(pos, h2r, xz, n=n)
        y = _experts(tile_expert, n_used, xs, w_expert_gate[l], w_expert_up[l], w_expert_down[l])
        xc = _final(pos, x1, route, mod3, norm_f_g[None], y, seq=seq, final_norm=(l == depth - 1))
    return xc.reshape(batch, seq, d)
```

```python
import functools
import math

import numpy as np
import jax
import jax.numpy as jnp
from jax import lax
from jax.experimental import pallas as pl
from jax.experimental.pallas import tpu as pltpu

F32 = jnp.float32
BF16 = jnp.bfloat16

D_MODEL = 1024
N_HEADS = 8
HEAD_DIM = 64
ATTN_WIDTH = N_HEADS * HEAD_DIM
MOBA_BLOCK = 256
MOBA_TOPK = 3
MAX_BLOCKS = 32
FAR_GROUP = 4
ONES_ROWS = 16
LOG2E = math.log2(math.e)
POOL_WINDOWS = (2, 4, 8, 16)
POOL_WIDTH = 512
POOL_GROUP_WIDTH = 128
POOL_HALO = 16
REL_BUCKETS = 32
REL_MAX_DIST = 128
N_GROUPS = 4
EXPERTS_PER_GROUP = 8
N_EXPERTS = N_GROUPS * EXPERTS_PER_GROUP
D_EXPERT = 512
N_MOD = 6
EPS = 1e-6
NEG_INF = -1e30
LANES = 128
ROUTER_LANE0 = N_GROUPS
ROUTE_FIELDS = 8
VMEM_LIMIT = 56 * 1024 * 1024

TM_PROJ = 512
TM_POST = 512
TM_DISPATCH = 512
TE_ROWS = 512
EXPERT_TOPK = 2
TM_FINAL = 256


def _cparams(*sem):
    return pltpu.CompilerParams(dimension_semantics=sem, vmem_limit_bytes=VMEM_LIMIT)


def _dot(a, b):
    return jnp.dot(a, b, preferred_element_type=F32)


def _rms_mod(x, g, scale, shift):
    xn = x * lax.rsqrt(jnp.mean(x * x, axis=-1, keepdims=True) + EPS)
    return (xn * g) * (1.0 + scale) + shift


def _mod_kernel(c_ref, w_ref, b_ref, o_ref):
    c = c_ref[...]
    ca = c * jax.nn.sigmoid(c)
    o_ref[...] = _dot(ca.astype(BF16), w_ref[...].astype(BF16)) + b_ref[...]


def _modulation(c_pad, w_ada, b_ada):
    rows, d = c_pad.shape
    n_out = w_ada.shape[1]
    tn = 1024
    return pl.pallas_call(
        _mod_kernel,
        grid=(n_out // tn,),
        in_specs=[pl.BlockSpec((rows, d), lambda j: (0, 0)),
                  pl.BlockSpec((d, tn), lambda j: (0, j)),
                  pl.BlockSpec((1, tn), lambda j: (0, j))],
        out_specs=pl.BlockSpec((rows, tn), lambda j: (0, j)),
        out_shape=jax.ShapeDtypeStruct((rows, n_out), F32),
        compiler_params=_cparams("arbitrary"),
        name="adaln_mod",
    )(c_pad, w_ada, b_ada)


def _inproj_kernel(x_ref, mod_ref, g_ref, w_ref, qT_ref, k_ref, vT_ref, u_ref, km_ref, *, tm):
    d = D_MODEL
    mod = mod_ref[0]
    h = _rms_mod(x_ref[...], g_ref[...], mod[:, d:2 * d], mod[:, 0:d])
    proj = _dot(h.astype(BF16), w_ref[...])
    aw = ATTN_WIDTH
    q = proj[:, 0:aw] * (HEAD_DIM ** -0.5 * LOG2E)
    k = proj[:, aw:2 * aw]
    v = proj[:, 2 * aw:3 * aw]
    u_ref[...] = proj[:, 3 * aw:3 * aw + POOL_WIDTH]
    k_ref[...] = k.astype(BF16)
    qT = q.T.astype(BF16)
    vT = v.T.astype(BF16)
    for c in range(tm // MOBA_BLOCK):
        sl = slice(c * MOBA_BLOCK, (c + 1) * MOBA_BLOCK)
        qT_ref[0, c] = qT[:, sl]
        vT_ref[0, c] = vT[:, sl]
        km_ref[0, c:c + 1, :] = jnp.mean(k[sl, :], axis=0, keepdims=True)


def _in_projection(x2, mod3, g1, w_qkvu, *, batch, seq):
    n, d = x2.shape
    tm = TM_PROJ
    tps = seq // tm
    nb = seq // MOBA_BLOCK
    cb = tm // MOBA_BLOCK
    wcols = 3 * ATTN_WIDTH + POOL_WIDTH
    assert w_qkvu.shape[1] % wcols == 0
    kern = functools.partial(_inproj_kernel, tm=tm)
    t_shape = jax.ShapeDtypeStruct((batch, nb, ATTN_WIDTH, MOBA_BLOCK), BF16)
    t_spec = pl.BlockSpec((1, cb, ATTN_WIDTH, MOBA_BLOCK), lambda i: (i // tps, i % tps, 0, 0))
    return pl.pallas_call(
        kern,
        grid=(n // tm,),
        in_specs=[pl.BlockSpec((tm, d), lambda i: (i, 0)),
                  pl.BlockSpec((1, 1, N_MOD * d), lambda i: (i // tps, 0, 0)),
                  pl.BlockSpec((1, d), lambda i: (0, 0)),
                  pl.BlockSpec((d, wcols), lambda i: (0, 0))],
        out_specs=[t_spec,
                   pl.BlockSpec((tm, ATTN_WIDTH), lambda i: (i, 0)),
                   t_spec,
                   pl.BlockSpec((tm, POOL_WIDTH), lambda i: (i, 0)),
                   pl.BlockSpec((1, cb, ATTN_WIDTH), lambda i: (i, 0, 0))],
        out_shape=[t_shape,
                   jax.ShapeDtypeStruct((n, ATTN_WIDTH), BF16),
                   t_shape,
                   jax.ShapeDtypeStruct((n, POOL_WIDTH), F32),
                   jax.ShapeDtypeStruct((n // tm, cb, ATTN_WIDTH), F32)],
        compiler_params=_cparams("arbitrary"),
        name="norm1_inproj",
    )(x2, mod3, g1, w_qkvu)


def _attn_kernel(qT_ref, k_ref, vT_ref, km_ref, bw_ref, o_ref, kaug_ref, vaug_ref, bias_ref, qnear_ref,
                 qfar_ref, sn_ref, sf0_ref, sf1_ref, *, nb):
    blk = MOBA_BLOCK
    hd = HEAD_DIM
    lane = lax.broadcasted_iota(jnp.int32, (blk, LANES), 1)

    ones = jnp.ones((ONES_ROWS, blk), BF16)

    def build_operands(j, carry):
        rows = pl.ds(pl.multiple_of(j * blk, blk), blk)
        kp = k_ref[0, rows, :]
        kaug_ref[0, rows, :] = jnp.where(lane < hd, kp, jnp.where(lane == hd + j, 1.0, 0.0).astype(BF16))
        kaug_ref[1, rows, :] = jnp.where(lane >= hd, kp, jnp.where(lane == j, 1.0, 0.0).astype(BF16))
        for hh in range(2):
            vaug_ref[j, hh] = jnp.concatenate([vT_ref[0, j, hh * hd:(hh + 1) * hd, :], ones], axis=0)
        return carry

    lax.fori_loop(0, nb, build_operands, 0)

    for hh in range(2):
        for t in range(2):
            wide = jnp.broadcast_to(bw_ref[hh, t:t + 1, :], (blk, 2 * blk))
            bias_ref[hh, t] = pltpu.roll(wide, 0, 1, stride=1, stride_axis=0)[:, blk:2 * blk]

    rowi = lax.broadcasted_iota(jnp.int32, (MAX_BLOCKS, blk), 0)
    km_lane = lax.broadcasted_iota(jnp.int32, (MAX_BLOCKS, LANES), 1)
    zpad = jnp.zeros((LANES - hd - MAX_BLOCKS, blk), BF16)
    km_split = []
    for hh in range(2):
        in_head = (km_lane >= hh * hd) & (km_lane < (hh + 1) * hd)
        km_h = jnp.where(in_head, km_ref[0], 0.0)
        km_hi = km_h.astype(BF16)
        km_split.append((km_hi, (km_h - km_hi.astype(F32)).astype(BF16)))

    def head_rows(hh):
        return slice(hh * hd, (hh + 1) * hd)

    grp = FAR_GROUP

    def retire_begin(state, mx):
        m, acc = state
        m_new = jnp.maximum(m, mx)
        return m_new, jnp.exp2(m - m_new) * acc

    def retire_slice(src_ref, hh, m_new, acc, g, j):
        p = jnp.exp2(src_ref[hh, g * blk:(g + 1) * blk, :] - m_new)
        return acc + _dot(vaug_ref[j, hh], p.astype(BF16))

    def fresh():
        return (jnp.full((1, blk), -jnp.inf, F32), jnp.zeros((hd + ONES_ROWS, blk), F32))

    def near_blocks(i):
        return tuple(jnp.where(i - (grp - 1) + g < 0, i + 1, i - (grp - 1) + g) for g in range(grp))

    def finish(i, states):
        for hh in range(2):
            _, acc = states[hh]
            o_ref[0, i, head_rows(hh), :] = (acc[0:hd] / acc[hd:hd + 1]).astype(BF16)

    def select(i):
        qp = qT_ref[0, i]
        for hh in range(2):
            km_hi, km_lo = km_split[hh]
            gate = _dot(km_hi, qp) + _dot(km_lo, qp)
            g = jnp.where(rowi < i, gate, -jnp.inf)
            sel = rowi == i
            for _ in range(MOBA_TOPK):
                m = jnp.max(g, axis=0, keepdims=True)
                hit = (g == m) & (m > -jnp.inf)
                idx = jnp.min(jnp.where(hit, rowi, MAX_BLOCKS), axis=0, keepdims=True)
                pick = rowi == idx
                sel = sel | pick
                g = jnp.where(pick, -jnp.inf, g)
            near = jnp.where(sel, 0.0, NEG_INF)
            far = jnp.where(rowi > i - grp, NEG_INF, near)
            qh = qp[head_rows(hh), :]
            if hh == 0:
                qnear_ref[hh] = jnp.concatenate([qh, near.astype(BF16), zpad], axis=0)
                qfar_ref[hh] = jnp.concatenate([qh, far.astype(BF16), zpad], axis=0)
            else:
                qnear_ref[hh] = jnp.concatenate([near.astype(BF16), zpad, qh], axis=0)
                qfar_ref[hh] = jnp.concatenate([far.astype(BF16), zpad, qh], axis=0)

    def park(dst_ref, q_ref, hh, g, j, bias=None):
        rows = pl.ds(pl.multiple_of(j * blk, blk), blk)
        s = _dot(kaug_ref[hh, rows, :], q_ref[hh])
        if bias is not None:
            s = s + bias
        dst_ref[hh, g * blk:(g + 1) * blk, :] = s
        return jnp.max(s, axis=0, keepdims=True)

    def park_near(i):
        idx = near_blocks(i)

        def quarter(g, hh):
            bias = None
            if g == grp - 1:
                bias = bias_ref[hh, 0]
            elif g == grp - 2:
                bias = bias_ref[hh, 1]
            return park(sn_ref, qnear_ref, hh, g, idx[g], bias)
        return quarter

    def park_far(c, dst_ref):
        return lambda g, hh: park(dst_ref, qfar_ref, hh, g, c * grp + g)

    def chunk_blocks(c):
        return tuple(c * grp + g for g in range(grp))

    def step(src_ref, states, maxes, vidx, quarter, park_first):
        begun = [retire_begin(states[hh], maxes[hh]) for hh in range(2)] if states is not None else None
        accs = [b[1] for b in begun] if begun else None
        new_max = [None, None]

        def park_quarter(g):
            for hh in range(2):
                cm = quarter(g, hh)
                new_max[hh] = cm if new_max[hh] is None else jnp.maximum(new_max[hh], cm)

        for g in range(grp):
            if park_first:
                park_quarter(g)
            if begun:
                for hh in range(2):
                    accs[hh] = retire_slice(src_ref, hh, begun[hh][0], accs[hh], g, vidx[g])
            if not park_first and quarter is not None:
                park_quarter(g)
        new_states = tuple((begun[hh][0], accs[hh]) for hh in range(2)) if begun else None
        return new_states, tuple(new_max)

    def fresh_states():
        return (fresh(), fresh())

    def early_tile(i, near_max):
        nxt = jnp.minimum(i + 1, nb - 1)
        select(nxt)
        states, next_max = step(sn_ref, fresh_states(), near_max, near_blocks(i), park_near(nxt), False)
        finish(i, states)
        return next_max

    def late_tile(i, near_max):
        sf = (sf0_ref, sf1_ref)
        n_chunks = (i - (grp - 1) + grp - 1) // grp
        nxt = jnp.minimum(i + 1, nb - 1)
        states, far_max = step(sn_ref, fresh_states(), near_max, near_blocks(i), park_far(0, sf[0]), True)

        def pair(k, st):
            states, far_max = st
            states, far_max = step(sf[0], states, far_max, chunk_blocks(2 * k), park_far(2 * k + 1, sf[1]), True)
            return step(sf[1], states, far_max, chunk_blocks(2 * k + 1), park_far(2 * k + 2, sf[0]), True)

        states, far_max = lax.fori_loop(0, (n_chunks - 1) // 2, pair, (states, far_max))

        def last(src, states, far_max):
            select(nxt)
            states, next_max = step(src, states, far_max, chunk_blocks(n_chunks - 1), park_near(nxt), True)
            finish(i, states)
            return next_max

        def even_count(st):
            states, far_max = step(sf[0], st[0], st[1], chunk_blocks(n_chunks - 2),
                                   park_far(n_chunks - 1, sf[1]), True)
            return last(sf[1], states, far_max)

        def odd_count(st):
            return last(sf[0], st[0], st[1])

        return lax.cond(n_chunks % 2 == 0, even_count, odd_count, (states, far_max))

    n_early = min(grp, nb)
    select(0)
    _, near_max = step(None, None, None, None, park_near(0), True)
    near_max = lax.fori_loop(0, n_early, early_tile, near_max)
    lax.fori_loop(n_early, nb, late_tile, near_max)


def _rel_bucket_table(max_rel):
    n = np.arange(max_rel)
    max_exact = REL_BUCKETS // 2
    nf = np.maximum(n, 1).astype(np.float32)
    large = max_exact + (np.log(nf / np.float32(max_exact)) / np.float32(math.log(REL_MAX_DIST / max_exact))
                         * np.float32(REL_BUCKETS - max_exact)).astype(np.int32)
    large = np.minimum(large, REL_BUCKETS - 1)
    return np.where(n < max_exact, n, large)


def _bias_vectors(rel_bias):
    blk = MOBA_BLOCK
    table = _rel_bucket_table(2 * blk)
    r = np.arange(-blk, blk)
    bt = (rel_bias.T - rel_bias[REL_BUCKETS - 1][:, None]) * LOG2E
    own = jnp.where(jnp.asarray(r >= 0)[None], bt[:, table[np.maximum(r, 0)]], NEG_INF)
    prev = bt[:, table[np.minimum(r + blk, 2 * blk - 1)]]
    return jnp.stack([own, prev], axis=1).astype(F32)


def _moba_attention(qT, k3, vT, km_pad, bias_w, *, batch, seq):
    nb = seq // MOBA_BLOCK
    blk = MOBA_BLOCK
    assert nb % FAR_GROUP == 0 and nb <= MAX_BLOCKS
    kern = functools.partial(_attn_kernel, nb=nb)
    t_spec = pl.BlockSpec((1, nb, LANES, blk), lambda b, p: (b, 0, p, 0))
    return pl.pallas_call(
        kern,
        grid=(batch, N_HEADS // 2),
        in_specs=[t_spec,
                  pl.BlockSpec((1, seq, LANES), lambda b, p: (b, 0, p)),
                  t_spec,
                  pl.BlockSpec((1, MAX_BLOCKS, LANES), lambda b, p: (b, 0, p)),
                  pl.BlockSpec((2, 2, 2 * blk), lambda b, p: (p, 0, 0))],
        out_specs=t_spec,
        out_shape=jax.ShapeDtypeStruct((batch, nb, ATTN_WIDTH, blk), BF16),
        scratch_shapes=[pltpu.VMEM((2, seq, LANES), BF16),
                        pltpu.VMEM((nb, 2, HEAD_DIM + ONES_ROWS, blk), BF16),
                        pltpu.VMEM((2, 2, blk, blk), F32),
                        pltpu.VMEM((2, LANES, blk), BF16),
                        pltpu.VMEM((2, LANES, blk), BF16),
                        pltpu.VMEM((2, FAR_GROUP * blk, blk), F32),
                        pltpu.VMEM((2, FAR_GROUP * blk, blk), F32),
                        pltpu.VMEM((2, FAR_GROUP * blk, blk), F32)],
        compiler_params=_cparams("arbitrary", "arbitrary"),
        name="moba_attention",
    )(qT, k3, vT, km_pad, bias_w)


def _post_kernel(x_ref, mod_ref, g1_ref, wz_ref, bz_ref, at_ref, u_ref, uh_ref, pw_ref, ps_ref,
                 wba_ref, wbp_ref, wo_ref, g2_ref, wrh_ref, wrl_ref, br_ref,
                 x1_ref, h2r_ref, route_ref, route_t_ref, counts_ref, xz_ref, carry_ref, *, tm, seq):
    d = D_MODEL
    i = pl.program_id(0)
    x = x_ref[...]
    mod = mod_ref[0]
    shift1, scale1, gate1 = mod[:, 0:d], mod[:, d:2 * d], mod[:, 2 * d:3 * d]
    shift2, scale2 = mod[:, 3 * d:4 * d], mod[:, 4 * d:5 * d]

    h = _rms_mod(x, g1_ref[...], scale1, shift1)
    zg = jax.nn.sigmoid(_dot(h.astype(BF16), wz_ref[...]) + bz_ref[...])

    ya = jnp.concatenate(
        [lax.dot_general(at_ref[0, c], wba_ref[...], (((0,), (0,)), ((), ())),
                         preferred_element_type=F32) for c in range(tm // MOBA_BLOCK)], axis=0)

    first = (i * tm) % seq == 0
    halo = jnp.where(first, 0.0, uh_ref[...])
    ubuf = jnp.concatenate([halo, u_ref[...]], axis=0)
    rows = tm + POOL_HALO
    pos = (lax.broadcasted_iota(jnp.int32, (tm, POOL_GROUP_WIDTH), 0) + i * tm) % seq
    outs = []
    for gi, w in enumerate(POOL_WINDOWS):
        e = ubuf[:, gi * POOL_GROUP_WIDTH:(gi + 1) * POOL_GROUP_WIDTH]
        s = e
        step = 1
        while step < w:
            s = s + pltpu.roll(s, step, 0)
            step *= 2
        cnt = jnp.minimum(pos + 1, w).astype(F32)
        dlt = s[POOL_HALO:rows] / cnt - e[POOL_HALO:rows]
        outs.append(_dot(dlt.astype(BF16), pw_ref[gi]))
    pooled = jnp.concatenate(outs, axis=-1) * ps_ref[...]
    yp = _dot(pooled.astype(BF16), wbp_ref[...])

    merged = zg[:, 0:d] * ya + zg[:, d:2 * d] * yp
    x1 = x + gate1 * _dot(merged.astype(BF16), wo_ref[...])
    x1_ref[...] = x1

    h2 = _rms_mod(x1, g2_ref[...], scale2, shift2)
    h2_hi = h2.astype(BF16)
    h2r_ref[...] = h2_hi.astype(F32)
    h2_lo = (h2 - h2_hi.astype(F32)).astype(BF16)
    z = _dot(h2_hi, wrh_ref[...]) + _dot(h2_lo, wrh_ref[...]) + _dot(h2_hi, wrl_ref[...]) + br_ref[...]

    lane = lax.broadcasted_iota(jnp.int32, (tm, LANES), 1)
    zgrp = jnp.where(lane < N_GROUPS, z, -jnp.inf)
    mg = jnp.max(zgrp, axis=-1, keepdims=True)
    pg_top = 1.0 / jnp.sum(jnp.exp(zgrp - mg), axis=-1, keepdims=True)
    g_idx = jnp.min(jnp.where(zgrp == mg, lane, LANES), axis=-1, keepdims=True)
    e_lane = lane - ROUTER_LANE0
    in_grp = (e_lane >= 0) & (e_lane < N_EXPERTS) & ((e_lane >> 3) == g_idx)
    ze = jnp.where(in_grp, z, -jnp.inf)
    m1 = jnp.max(ze, axis=-1, keepdims=True)
    i1 = jnp.min(jnp.where(ze == m1, lane, LANES), axis=-1, keepdims=True)
    ze2 = jnp.where(lane == i1, -jnp.inf, ze)
    m2 = jnp.max(ze2, axis=-1, keepdims=True)
    i2 = jnp.min(jnp.where(ze2 == m2, lane, LANES), axis=-1, keepdims=True)
    e2 = jnp.exp(m2 - m1)
    w1 = pg_top / (1.0 + e2)
    w2 = pg_top * e2 / (1.0 + e2)

    hot = jnp.where((lane == i1) | (lane == i2), 1.0, 0.0)
    tri = jnp.where(lax.broadcasted_iota(jnp.int32, (tm, tm), 1) <= lax.broadcasted_iota(jnp.int32, (tm, tm), 0),
                    1.0, 0.0).astype(BF16)
    prefix = _dot(tri, hot.astype(BF16))

    @pl.when(i == 0)
    def _():
        carry_ref[...] = jnp.zeros_like(carry_ref)

    base = carry_ref[...] + prefix - 1.0
    rank1 = jnp.sum(jnp.where(lane == i1, base, 0.0), axis=-1, keepdims=True)
    rank2 = jnp.sum(jnp.where(lane == i2, base, 0.0), axis=-1, keepdims=True)
    carry_ref[...] = carry_ref[...] + prefix[tm - 1:tm, :]
    counts_ref[...] = carry_ref[...]
    fields = ((i1 - ROUTER_LANE0).astype(F32), (i2 - ROUTER_LANE0).astype(F32), w1, w2, rank1, rank2)
    route = jnp.zeros((tm, LANES), F32)
    for col, val in enumerate(fields):
        route = jnp.where(lane == col, val, route)
    route_ref[...] = route
    route_t_ref[...] = route.T[0:ROUTE_FIELDS, :]
    xz_ref[...] = jnp.zeros_like(xz_ref)


def _post_attention(x2, mod3, g1, w_z, b_z, attnT, u, pool_w, pool_scale, w_ba, w_bp, w_out, g2,
                    wr_hi, wr_lo, b_r, *, batch, seq):
    n, d = x2.shape
    tm = TM_POST
    tps = seq // tm
    cb = tm // MOBA_BLOCK
    hb = tm // POOL_HALO
    sorted_rows = _expert_tiles(n) * TE_ROWS
    assert sorted_rows % (n // tm) == 0
    zrows = sorted_rows // (n // tm)
    kern = functools.partial(_post_kernel, tm=tm, seq=seq)
    const2 = lambda i: (0, 0)
    return pl.pallas_call(
        kern,
        grid=(n // tm,),
        in_specs=[pl.BlockSpec((tm, d), lambda i: (i, 0)),
                  pl.BlockSpec((1, 1, N_MOD * d), lambda i: (i // tps, 0, 0)),
                  pl.BlockSpec((1, d), const2),
                  pl.BlockSpec((d, 2 * d), lambda i: (0, 1)),
                  pl.BlockSpec((1, 2 * d), const2),
                  pl.BlockSpec((1, cb, ATTN_WIDTH, MOBA_BLOCK), lambda i: (i // tps, i % tps, 0, 0)),
                  pl.BlockSpec((tm, POOL_WIDTH), lambda i: (i, 0)),
                  pl.BlockSpec((POOL_HALO, POOL_WIDTH), lambda i: (jnp.maximum(i * hb - 1, 0), 0)),
                  pl.BlockSpec((len(POOL_WINDOWS), POOL_GROUP_WIDTH, POOL_GROUP_WIDTH), lambda i: (0, 0, 0)),
                  pl.BlockSpec((1, POOL_WIDTH), const2),
                  pl.BlockSpec((ATTN_WIDTH, d), const2),
                  pl.BlockSpec((POOL_WIDTH, d), const2),
                  pl.BlockSpec((d, d), const2),
                  pl.BlockSpec((1, d), const2),
                  pl.BlockSpec((d, LANES), const2),
                  pl.BlockSpec((d, LANES), const2),
                  pl.BlockSpec((1, LANES), const2)],
        out_specs=[pl.BlockSpec((tm, d), lambda i: (i, 0)),
                   pl.BlockSpec((tm, d), lambda i: (i, 0)),
                   pl.BlockSpec((tm, LANES), lambda i: (i, 0)),
                   pl.BlockSpec((ROUTE_FIELDS, tm), lambda i: (0, i)),
                   pl.BlockSpec((1, LANES), const2),
                   pl.BlockSpec((zrows, d), lambda i: (i, 0))],
        out_shape=[jax.ShapeDtypeStruct((n, d), F32),
                   jax.ShapeDtypeStruct((n, d), F32),
                   jax.ShapeDtypeStruct((n, LANES), F32),
                   jax.ShapeDtypeStruct((ROUTE_FIELDS, n), F32),
                   jax.ShapeDtypeStruct((1, LANES), F32),
                   jax.ShapeDtypeStruct((zrows * (n // tm), d), F32)],
        scratch_shapes=[pltpu.VMEM((1, LANES), F32)],
        compiler_params=_cparams("arbitrary"),
        name="merge_outproj_router",
    )(x2, mod3, g1, w_z, b_z, attnT, u, u, pool_w, pool_scale, w_ba, w_bp, w_out, g2, wr_hi, wr_lo, b_r)


def _routing_plan(route_t, counts, *, n):
    te = TE_ROWS
    n_tiles = _expert_tiles(n)
    cnt = counts[0, ROUTER_LANE0:ROUTER_LANE0 + N_EXPERTS].astype(jnp.int32)
    padded = ((cnt + te - 1) // te) * te
    ids = jnp.arange(N_EXPERTS, dtype=jnp.int32)
    ends = jnp.sum(jnp.where(ids[None, :] <= ids[:, None], padded[None, :], 0), axis=-1)
    starts = ends - padded
    pos = []
    for k in range(EXPERT_TOPK):
        expert = route_t[k].astype(jnp.int32)
        rank = route_t[4 + k].astype(jnp.int32)
        pos.append(jnp.sum(jnp.where(expert[:, None] == ids, starts, 0), axis=-1) + rank)
    tiles = jnp.arange(n_tiles, dtype=jnp.int32)
    tile_expert = jnp.sum((ends // te)[None, :] <= tiles[:, None], axis=-1)
    tile_expert = jnp.minimum(tile_expert, N_EXPERTS - 1).astype(jnp.int32)
    n_used = (ends[N_EXPERTS - 1:] // te).astype(jnp.int32)
    return pos, tile_expert, n_used


def _expert_tiles(n):
    return (EXPERT_TOPK * n) // TE_ROWS + N_EXPERTS


def _tile_pos(pos, tm):
    return jnp.concatenate([p.reshape(-1, 1, tm) for p in pos], axis=-1)


def _dispatch_kernel(pos_ref, h_ref, xz_ref, xs_ref, sem, *, tm):
    del xz_ref
    for r in range(tm):
        for k in range(EXPERT_TOPK):
            pltpu.make_async_copy(h_ref.at[pl.ds(r, 1)], xs_ref.at[pl.ds(pos_ref[0, 0, k * tm + r], 1)],
                                  sem).start(priority=(r + k) % 2)
    for k in range(EXPERT_TOPK):
        pltpu.make_async_copy(h_ref, h_ref, sem).wait()


def _dispatch(pos, h2r, xz, *, n):
    tm = TM_DISPATCH
    rows, d = xz.shape
    assert rows == _expert_tiles(n) * TE_ROWS
    pos3 = _tile_pos(pos, tm)
    return pl.pallas_call(
        functools.partial(_dispatch_kernel, tm=tm),
        grid=(n // tm,),
        in_specs=[pl.BlockSpec((1, 1, EXPERT_TOPK * tm), lambda i: (i, 0, 0), memory_space=pltpu.SMEM),
                  pl.BlockSpec((tm, d), lambda i: (i, 0)),
                  pl.BlockSpec(memory_space=pl.ANY)],
        out_specs=pl.BlockSpec(memory_space=pl.ANY),
        out_shape=jax.ShapeDtypeStruct((rows, d), F32),
        scratch_shapes=[pltpu.SemaphoreType.DMA(())],
        input_output_aliases={2: 0},
        compiler_params=_cparams("arbitrary"),
        name="moe_dispatch",
    )(pos3, h2r, xz)


def _experts_kernel(te_ref, nu_ref, xs_ref, wg_ref, wu_ref, wd_ref, y_ref, wg_b, wu_b, wd_b):
    t = pl.program_id(0)

    @pl.when(t < nu_ref[0])
    def _():
        e = te_ref[t]
        e_prev = te_ref[jnp.maximum(t - 1, 0)]

        @pl.when((t == 0) | (e != e_prev))
        def _():
            wg_b[...] = wg_ref[0].astype(BF16)
            wu_b[...] = wu_ref[0].astype(BF16)
            wd_b[...] = wd_ref[0].astype(BF16)

        x = xs_ref[...].astype(BF16)
        g = _dot(x, wg_b[...])
        u = _dot(x, wu_b[...])
        act = (g * jax.nn.sigmoid(g)) * u
        y_ref[...] = _dot(act.astype(BF16), wd_b[...])

    @pl.when(t >= nu_ref[0])
    def _():
        y_ref[...] = jnp.zeros_like(y_ref)


def _experts(tile_expert, n_used, xs, wg, wu, wd):
    rows, d = xs.shape
    te = TE_ROWS

    def used(t, nu):
        return jnp.minimum(t, nu[0] - 1)

    grid_spec = pltpu.PrefetchScalarGridSpec(
        num_scalar_prefetch=2,
        grid=(rows // te,),
        in_specs=[pl.BlockSpec((te, d), lambda t, te_r, nu: (used(t, nu), 0)),
                  pl.BlockSpec((1, d, D_EXPERT), lambda t, te_r, nu: (te_r[used(t, nu)], 0, 0)),
                  pl.BlockSpec((1, d, D_EXPERT), lambda t, te_r, nu: (te_r[used(t, nu)], 0, 0)),
                  pl.BlockSpec((1, D_EXPERT, d), lambda t, te_r, nu: (te_r[used(t, nu)], 0, 0))],
        out_specs=pl.BlockSpec((te, d), lambda t, te_r, nu: (t, 0)),
        scratch_shapes=[pltpu.VMEM((d, D_EXPERT), BF16), pltpu.VMEM((d, D_EXPERT), BF16),
                        pltpu.VMEM((D_EXPERT, d), BF16)])
    return pl.pallas_call(
        _experts_kernel,
        grid_spec=grid_spec,
        out_shape=jax.ShapeDtypeStruct((rows, d), F32),
        compiler_params=_cparams("arbitrary"),
        name="moe_experts",
    )(tile_expert, n_used, xs, wg, wu, wd)


def _final_kernel(pos_ref, x1_ref, route_ref, mod_ref, g_ref, y_ref, o_ref, ybuf, sem, *, tm, final_norm):
    d = D_MODEL

    for r in range(tm):
        for k in range(EXPERT_TOPK):
            pltpu.make_async_copy(y_ref.at[pl.ds(pos_ref[0, 0, k * tm + r], 1)], ybuf.at[k, pl.ds(r, 1)],
                                  sem).start(priority=(r + k) % 2)
    for k in range(EXPERT_TOPK):
        pltpu.make_async_copy(ybuf.at[k], ybuf.at[k], sem).wait()

    route = route_ref[...]
    lane = lax.broadcasted_iota(jnp.int32, route.shape, 1)
    y = jnp.zeros((tm, d), F32)
    for k in range(EXPERT_TOPK):
        wk = jnp.sum(jnp.where(lane == 2 + k, route, 0.0), axis=-1, keepdims=True)
        y = y + wk * ybuf[k]
    gate2 = mod_ref[0][:, 5 * d:6 * d]
    x = x1_ref[...] + gate2 * y
    if final_norm:
        x = (x * lax.rsqrt(jnp.mean(x * x, axis=-1, keepdims=True) + EPS)) * g_ref[...]
    o_ref[...] = x


def _final(pos, x1, route, mod3, gf, y, *, seq, final_norm):
    n, d = x1.shape
    tm = TM_FINAL
    tps = seq // tm
    pos3 = _tile_pos(pos, tm)
    return pl.pallas_call(
        functools.partial(_final_kernel, tm=tm, final_norm=final_norm),
        grid=(n // tm,),
        in_specs=[pl.BlockSpec((1, 1, EXPERT_TOPK * tm), lambda i: (i, 0, 0), memory_space=pltpu.SMEM),
                  pl.BlockSpec((tm, d), lambda i: (i, 0)),
                  pl.BlockSpec((tm, LANES), lambda i: (i, 0)),
                  pl.BlockSpec((1, 1, N_MOD * d), lambda i: (i // tps, 0, 0)),
                  pl.BlockSpec((1, d), lambda i: (0, 0)),
                  pl.BlockSpec(memory_space=pl.ANY)],
        out_specs=pl.BlockSpec((tm, d), lambda i: (i, 0)),
        out_shape=jax.ShapeDtypeStruct((n, d), F32),
        scratch_shapes=[pltpu.VMEM((EXPERT_TOPK, tm, d), F32), pltpu.SemaphoreType.DMA(())],
        compiler_params=_cparams("arbitrary"),
        name="moe_combine_final_norm",
    )(pos3, x1, route, mod3, gf, y)


def _split_bf16(w):
    hi = w.astype(BF16)
    return hi, (w - hi.astype(F32)).astype(BF16)


def kernel(x, c, w_ada, b_ada, norm1_g, w_in, b_gate, rel_bias, pool_w, pool_scale, w_branch_attn,
           w_branch_pool, w_out, norm2_g, w_router_group, b_router_group, w_router_expert,
           b_router_expert, w_expert_gate, w_expert_up, w_expert_down, norm_f_g):
    batch, seq, d = x.shape
    depth = w_ada.shape[0]
    n = batch * seq
    nb = seq // MOBA_BLOCK
    qkvu = 3 * ATTN_WIDTH + POOL_WIDTH
    bias_w = _bias_vectors(rel_bias)
    c_pad = jnp.pad(c, ((0, 8 - batch), (0, 0)))

    xc = x.reshape(n, d)
    for l in range(depth):
        mod3 = _modulation(c_pad, w_ada[l], b_ada[l][None])[:batch].reshape(batch, 1, N_MOD * d)
        w_in_b = w_in[l].astype(BF16)
        g1 = norm1_g[l][None]

        assert w_in_b.shape[1] == 2 * qkvu == qkvu + 2 * d
        qT, k2, vT, u, km = _in_projection(xc, mod3, g1, w_in_b, batch=batch, seq=seq)
        km_pad = jnp.pad(km.reshape(batch, nb, ATTN_WIDTH), ((0, 0), (0, MAX_BLOCKS - nb), (0, 0)))
        attnT = _moba_attention(qT, k2.reshape(batch, seq, ATTN_WIDTH), vT, km_pad, bias_w,
                                batch=batch, seq=seq)

        w_r = jnp.concatenate([w_router_group[l], w_router_expert[l]], axis=1)
        w_r = jnp.pad(w_r, ((0, 0), (0, LANES - w_r.shape[1])))
        b_r = jnp.concatenate([b_router_group[l], b_router_expert[l]])
        b_r = jnp.pad(b_r, (0, LANES - b_r.shape[0]))[None]
        wr_hi, wr_lo = _split_bf16(w_r)
        x1, h2r, route, route_t, counts, xz = _post_attention(
            xc, mod3, g1, w_in_b, b_gate[l][None], attnT, u, pool_w[l].astype(BF16),
            pool_scale[l][None], w_branch_attn[l].astype(BF16), w_branch_pool[l].astype(BF16),
            w_out[l].astype(BF16), norm2_g[l][None], wr_hi, wr_lo, b_r, batch=batch, seq=seq)

        pos, tile_expert, n_used = _routing_plan(route_t, counts, n=n)
        xs = _dispatch(pos, h2r, xz, n=n)
        y = _experts(tile_expert, n_used, xs, w_expert_gate[l], w_expert_up[l], w_expert_down[l])
        xc = _final(pos, x1, route, mod3, norm_f_g[None], y, seq=seq, final_norm=(l == depth - 1))
    return xc.reshape(batch, seq, d)
```

```python
import functools
import math

import numpy as np
import jax
import jax.numpy as jnp
from jax import lax
from jax.experimental import pallas as pl
from jax.experimental.pallas import tpu as pltpu

F32 = jnp.float32
BF16 = jnp.bfloat16

D_MODEL = 1024
N_HEADS = 8
HEAD_DIM = 64
ATTN_WIDTH = N_HEADS * HEAD_DIM
MOBA_BLOCK = 256
MOBA_TOPK = 3
MAX_BLOCKS = 32
FAR_GROUP = 4
ONES_ROWS = 16
LOG2E = math.log2(math.e)
POOL_WINDOWS = (2, 4, 8, 16)
POOL_WIDTH = 512
POOL_GROUP_WIDTH = 128
POOL_HALO = 16
REL_BUCKETS = 32
REL_MAX_DIST = 128
N_GROUPS = 4
EXPERTS_PER_GROUP = 8
N_EXPERTS = N_GROUPS * EXPERTS_PER_GROUP
D_EXPERT = 512
N_MOD = 6
EPS = 1e-6
NEG_INF = -1e30
LANES = 128
ROUTER_LANE0 = N_GROUPS
ROUTE_FIELDS = 8
ROUTER_ROWS = 48
VMEM_LIMIT = 56 * 1024 * 1024

TM_PROJ = 512
TM_POST = 512
TM_DISPATCH = 512
TE_ROWS = 512
EXPERT_TOPK = 2
TM_FINAL = 256


def _cparams(*sem):
    return pltpu.CompilerParams(dimension_semantics=sem, vmem_limit_bytes=VMEM_LIMIT)


def _dot(a, b):
    return jnp.dot(a, b, preferred_element_type=F32)


def _rms_mod(x, g, scale, shift):
    xn = x * lax.rsqrt(jnp.mean(x * x, axis=-1, keepdims=True) + EPS)
    return (xn * g) * (1.0 + scale) + shift


def _mod_kernel(c_ref, w_ref, b_ref, o_ref):
    c = c_ref[...]
    ca = c * jax.nn.sigmoid(c)
    o_ref[...] = _dot(ca.astype(BF16), w_ref[...].astype(BF16)) + b_ref[...]


def _modulation(c_pad, w_ada, b_ada):
    rows, d = c_pad.shape
    n_out = w_ada.shape[1]
    tn = 1024
    return pl.pallas_call(
        _mod_kernel,
        grid=(n_out // tn,),
        in_specs=[pl.BlockSpec((rows, d), lambda j: (0, 0)),
                  pl.BlockSpec((d, tn), lambda j: (0, j)),
                  pl.BlockSpec((1, tn), lambda j: (0, j))],
        out_specs=pl.BlockSpec((rows, tn), lambda j: (0, j)),
        out_shape=jax.ShapeDtypeStruct((rows, n_out), F32),
        compiler_params=_cparams("arbitrary"),
        name="adaln_mod",
    )(c_pad, w_ada, b_ada)


def _inproj_kernel(x_ref, mod_ref, g_ref, w_ref, qT_ref, k_ref, vT_ref, u_ref, km_ref, *, tm):
    d = D_MODEL
    mod = mod_ref[0]
    h = _rms_mod(x_ref[...], g_ref[...], mod[:, d:2 * d], mod[:, 0:d])
    proj = _dot(h.astype(BF16), w_ref[...])
    aw = ATTN_WIDTH
    q = proj[:, 0:aw] * (HEAD_DIM ** -0.5 * LOG2E)
    k = proj[:, aw:2 * aw]
    v = proj[:, 2 * aw:3 * aw]
    u_ref[...] = proj[:, 3 * aw:3 * aw + POOL_WIDTH]
    k_ref[...] = k.astype(BF16)
    qT = q.T.astype(BF16)
    vT = v.T.astype(BF16)
    for c in range(tm // MOBA_BLOCK):
        sl = slice(c * MOBA_BLOCK, (c + 1) * MOBA_BLOCK)
        qT_ref[0, c] = qT[:, sl]
        vT_ref[0, c] = vT[:, sl]
        km_ref[0, c:c + 1, :] = jnp.mean(k[sl, :], axis=0, keepdims=True)


def _in_projection(x2, mod3, g1, w_qkvu, *, batch, seq):
    n, d = x2.shape
    tm = TM_PROJ
    tps = seq // tm
    nb = seq // MOBA_BLOCK
    cb = tm // MOBA_BLOCK
    wcols = 3 * ATTN_WIDTH + POOL_WIDTH
    assert w_qkvu.shape[1] % wcols == 0
    kern = functools.partial(_inproj_kernel, tm=tm)
    t_shape = jax.ShapeDtypeStruct((batch, nb, ATTN_WIDTH, MOBA_BLOCK), BF16)
    t_spec = pl.BlockSpec((1, cb, ATTN_WIDTH, MOBA_BLOCK), lambda i: (i // tps, i % tps, 0, 0))
    return pl.pallas_call(
        kern,
        grid=(n // tm,),
        in_specs=[pl.BlockSpec((tm, d), lambda i: (i, 0)),
                  pl.BlockSpec((1, 1, N_MOD * d), lambda i: (i // tps, 0, 0)),
                  pl.BlockSpec((1, d), lambda i: (0, 0)),
                  pl.BlockSpec((d, wcols), lambda i: (0, 0))],
        out_specs=[t_spec,
                   pl.BlockSpec((tm, ATTN_WIDTH), lambda i: (i, 0)),
                   t_spec,
                   pl.BlockSpec((tm, POOL_WIDTH), lambda i: (i, 0)),
                   pl.BlockSpec((1, cb, ATTN_WIDTH), lambda i: (i, 0, 0))],
        out_shape=[t_shape,
                   jax.ShapeDtypeStruct((n, ATTN_WIDTH), BF16),
                   t_shape,
                   jax.ShapeDtypeStruct((n, POOL_WIDTH), F32),
                   jax.ShapeDtypeStruct((n // tm, cb, ATTN_WIDTH), F32)],
        compiler_params=_cparams("arbitrary"),
        name="norm1_inproj",
    )(x2, mod3, g1, w_qkvu)


def _attn_kernel(qT_ref, k_ref, vT_ref, km_ref, bw_ref, o_ref, kaug_ref, vaug_ref, bias_ref, qnear_ref,
                 qfar_ref, sn_ref, sf0_ref, sf1_ref, *, nb):
    blk = MOBA_BLOCK
    hd = HEAD_DIM
    lane = lax.broadcasted_iota(jnp.int32, (blk, LANES), 1)

    ones = jnp.ones((ONES_ROWS, blk), BF16)

    def build_operands(j, carry):
        rows = pl.ds(pl.multiple_of(j * blk, blk), blk)
        kp = k_ref[0, rows, :]
        kaug_ref[0, rows, :] = jnp.where(lane < hd, kp, jnp.where(lane == hd + j, 1.0, 0.0).astype(BF16))
        kaug_ref[1, rows, :] = jnp.where(lane >= hd, kp, jnp.where(lane == j, 1.0, 0.0).astype(BF16))
        for hh in range(2):
            vaug_ref[j, hh] = jnp.concatenate([vT_ref[0, j, hh * hd:(hh + 1) * hd, :], ones], axis=0)
        return carry

    lax.fori_loop(0, nb, build_operands, 0)

    for hh in range(2):
        for t in range(2):
            wide = jnp.broadcast_to(bw_ref[hh, t:t + 1, :], (blk, 2 * blk))
            bias_ref[hh, t] = pltpu.roll(wide, 0, 1, stride=1, stride_axis=0)[:, blk:2 * blk]

    rowi = lax.broadcasted_iota(jnp.int32, (MAX_BLOCKS, blk), 0)
    km_lane = lax.broadcasted_iota(jnp.int32, (MAX_BLOCKS, LANES), 1)
    zpad = jnp.zeros((LANES - hd - MAX_BLOCKS, blk), BF16)
    km_split = []
    for hh in range(2):
        in_head = (km_lane >= hh * hd) & (km_lane < (hh + 1) * hd)
        km_h = jnp.where(in_head, km_ref[0], 0.0)
        km_hi = km_h.astype(BF16)
        km_split.append((km_hi, (km_h - km_hi.astype(F32)).astype(BF16)))

    def head_rows(hh):
        return slice(hh * hd, (hh + 1) * hd)

    grp = FAR_GROUP

    def retire_begin(state, mx):
        m, acc = state
        m_new = jnp.maximum(m, mx)
        return m_new, jnp.exp2(m - m_new) * acc

    def retire_slice(src_ref, hh, m_new, acc, g, j):
        p = jnp.exp2(src_ref[hh, g * blk:(g + 1) * blk, :] - m_new)
        return acc + _dot(vaug_ref[j, hh], p.astype(BF16))

    def fresh():
        return (jnp.full((1, blk), -jnp.inf, F32), jnp.zeros((hd + ONES_ROWS, blk), F32))

    def near_blocks(i):
        return tuple(jnp.where(i - (grp - 1) + g < 0, i + 1, i - (grp - 1) + g) for g in range(grp))

    def finish(i, states):
        for hh in range(2):
            _, acc = states[hh]
            o_ref[0, i, head_rows(hh), :] = (acc[0:hd] / acc[hd:hd + 1]).astype(BF16)

    def select(i):
        qp = qT_ref[0, i]
        for hh in range(2):
            km_hi, km_lo = km_split[hh]
            gate = _dot(km_hi, qp) + _dot(km_lo, qp)
            g = jnp.where(rowi < i, gate, -jnp.inf)
            sel = rowi == i
            for _ in range(MOBA_TOPK):
                m = jnp.max(g, axis=0, keepdims=True)
                hit = (g == m) & (m > -jnp.inf)
                idx = jnp.min(jnp.where(hit, rowi, MAX_BLOCKS), axis=0, keepdims=True)
                pick = rowi == idx
                sel = sel | pick
                g = jnp.where(pick, -jnp.inf, g)
            near = jnp.where(sel, 0.0, NEG_INF)
            far = jnp.where(rowi > i - grp, NEG_INF, near)
            qh = qp[head_rows(hh), :]
            if hh == 0:
                qnear_ref[hh] = jnp.concatenate([qh, near.astype(BF16), zpad], axis=0)
                qfar_ref[hh] = jnp.concatenate([qh, far.astype(BF16), zpad], axis=0)
            else:
                qnear_ref[hh] = jnp.concatenate([near.astype(BF16), zpad, qh], axis=0)
                qfar_ref[hh] = jnp.concatenate([far.astype(BF16), zpad, qh], axis=0)

    def park(dst_ref, q_ref, hh, g, j, bias=None):
        rows = pl.ds(pl.multiple_of(j * blk, blk), blk)
        s = _dot(kaug_ref[hh, rows, :], q_ref[hh])
        if bias is not None:
            s = s + bias
        dst_ref[hh, g * blk:(g + 1) * blk, :] = s
        return jnp.max(s, axis=0, keepdims=True)

    def park_near(i):
        idx = near_blocks(i)

        def quarter(g, hh):
            bias = None
            if g == grp - 1:
                bias = bias_ref[hh, 0]
            elif g == grp - 2:
                bias = bias_ref[hh, 1]
            return park(sn_ref, qnear_ref, hh, g, idx[g], bias)
        return quarter

    def park_far(c, dst_ref):
        return lambda g, hh: park(dst_ref, qfar_ref, hh, g, c * grp + g)

    def chunk_blocks(c):
        return tuple(c * grp + g for g in range(grp))

    def step(src_ref, states, maxes, vidx, quarter, park_first):
        begun = [retire_begin(states[hh], maxes[hh]) for hh in range(2)] if states is not None else None
        accs = [b[1] for b in begun] if begun else None
        new_max = [None, None]

        def park_quarter(g):
            for hh in range(2):
                cm = quarter(g, hh)
                new_max[hh] = cm if new_max[hh] is None else jnp.maximum(new_max[hh], cm)

        for g in range(grp):
            if park_first:
                park_quarter(g)
            if begun:
                for hh in range(2):
                    accs[hh] = retire_slice(src_ref, hh, begun[hh][0], accs[hh], g, vidx[g])
            if not park_first and quarter is not None:
                park_quarter(g)
        new_states = tuple((begun[hh][0], accs[hh]) for hh in range(2)) if begun else None
        return new_states, tuple(new_max)

    def fresh_states():
        return (fresh(), fresh())

    def early_tile(i, near_max):
        nxt = jnp.minimum(i + 1, nb - 1)
        select(nxt)
        states, next_max = step(sn_ref, fresh_states(), near_max, near_blocks(i), park_near(nxt), False)
        finish(i, states)
        return next_max

    def late_tile(i, near_max):
        sf = (sf0_ref, sf1_ref)
        n_chunks = (i - (grp - 1) + grp - 1) // grp
        nxt = jnp.minimum(i + 1, nb - 1)
        states, far_max = step(sn_ref, fresh_states(), near_max, near_blocks(i), park_far(0, sf[0]), True)

        def pair(k, st):
            states, far_max = st
            states, far_max = step(sf[0], states, far_max, chunk_blocks(2 * k), park_far(2 * k + 1, sf[1]), True)
            return step(sf[1], states, far_max, chunk_blocks(2 * k + 1), park_far(2 * k + 2, sf[0]), True)

        states, far_max = lax.fori_loop(0, (n_chunks - 1) // 2, pair, (states, far_max))

        def last(src, states, far_max):
            select(nxt)
            states, next_max = step(src, states, far_max, chunk_blocks(n_chunks - 1), park_near(nxt), True)
            finish(i, states)
            return next_max

        def even_count(st):
            states, far_max = step(sf[0], st[0], st[1], chunk_blocks(n_chunks - 2),
                                   park_far(n_chunks - 1, sf[1]), True)
            return last(sf[1], states, far_max)

        def odd_count(st):
            return last(sf[0], st[0], st[1])

        return lax.cond(n_chunks % 2 == 0, even_count, odd_count, (states, far_max))

    n_early = min(grp, nb)
    select(0)
    _, near_max = step(None, None, None, None, park_near(0), True)
    near_max = lax.fori_loop(0, n_early, early_tile, near_max)
    lax.fori_loop(n_early, nb, late_tile, near_max)


def _rel_bucket_table(max_rel):
    n = np.arange(max_rel)
    max_exact = REL_BUCKETS // 2
    nf = np.maximum(n, 1).astype(np.float32)
    large = max_exact + (np.log(nf / np.float32(max_exact)) / np.float32(math.log(REL_MAX_DIST / max_exact))
                         * np.float32(REL_BUCKETS - max_exact)).astype(np.int32)
    large = np.minimum(large, REL_BUCKETS - 1)
    return np.where(n < max_exact, n, large)


def _bias_vectors(rel_bias):
    blk = MOBA_BLOCK
    table = _rel_bucket_table(2 * blk)
    r = np.arange(-blk, blk)
    bt = (rel_bias.T - rel_bias[REL_BUCKETS - 1][:, None]) * LOG2E
    own = jnp.where(jnp.asarray(r >= 0)[None], bt[:, table[np.maximum(r, 0)]], NEG_INF)
    prev = bt[:, table[np.minimum(r + blk, 2 * blk - 1)]]
    return jnp.stack([own, prev], axis=1).astype(F32)


def _moba_attention(qT, k3, vT, km_pad, bias_w, *, batch, seq):
    nb = seq // MOBA_BLOCK
    blk = MOBA_BLOCK
    assert nb % FAR_GROUP == 0 and nb <= MAX_BLOCKS
    kern = functools.partial(_attn_kernel, nb=nb)
    t_spec = pl.BlockSpec((1, nb, LANES, blk), lambda b, p: (b, 0, p, 0))
    return pl.pallas_call(
        kern,
        grid=(batch, N_HEADS // 2),
        in_specs=[t_spec,
                  pl.BlockSpec((1, seq, LANES), lambda b, p: (b, 0, p)),
                  t_spec,
                  pl.BlockSpec((1, MAX_BLOCKS, LANES), lambda b, p: (b, 0, p)),
                  pl.BlockSpec((2, 2, 2 * blk), lambda b, p: (p, 0, 0))],
        out_specs=t_spec,
        out_shape=jax.ShapeDtypeStruct((batch, nb, ATTN_WIDTH, blk), BF16),
        scratch_shapes=[pltpu.VMEM((2, seq, LANES), BF16),
                        pltpu.VMEM((nb, 2, HEAD_DIM + ONES_ROWS, blk), BF16),
                        pltpu.VMEM((2, 2, blk, blk), F32),
                        pltpu.VMEM((2, LANES, blk), BF16),
                        pltpu.VMEM((2, LANES, blk), BF16),
                        pltpu.VMEM((2, FAR_GROUP * blk, blk), F32),
                        pltpu.VMEM((2, FAR_GROUP * blk, blk), F32),
                        pltpu.VMEM((2, FAR_GROUP * blk, blk), F32)],
        compiler_params=_cparams("arbitrary", "arbitrary"),
        name="moba_attention",
    )(qT, k3, vT, km_pad, bias_w)


def _post_kernel(x_ref, mod_ref, g1_ref, wz_ref, bz_ref, at_ref, u_ref, uh_ref, pw_ref, ps_ref,
                 wba_ref, wbp_ref, wo_ref, g2_ref, wrh_ref, wrl_ref, br_ref,
                 x1_ref, h2r_ref, route_ref, route_t_ref, counts_ref, xz_ref, carry_ref, *, tm, seq):
    d = D_MODEL
    i = pl.program_id(0)
    xz_ref[...] = jnp.zeros_like(xz_ref)
    x = x_ref[...]
    mod = mod_ref[0]
    shift1, scale1, gate1 = mod[:, 0:d], mod[:, d:2 * d], mod[:, 2 * d:3 * d]
    shift2, scale2 = mod[:, 3 * d:4 * d], mod[:, 4 * d:5 * d]

    h = _rms_mod(x, g1_ref[...], scale1, shift1)
    zg = 0.5 * jnp.tanh(0.5 * (_dot(h.astype(BF16), wz_ref[...]) + bz_ref[...])) + 0.5

    ya = jnp.concatenate(
        [lax.dot_general(at_ref[0, c], wba_ref[...], (((0,), (0,)), ((), ())),
                         preferred_element_type=F32) for c in range(tm // MOBA_BLOCK)], axis=0)

    first = (i * tm) % seq == 0
    halo = jnp.where(first, 0.0, uh_ref[...])
    ubuf = jnp.concatenate([halo, u_ref[...]], axis=0)
    rows = tm + POOL_HALO
    pos = (lax.broadcasted_iota(jnp.int32, (tm, POOL_GROUP_WIDTH), 0) + i * tm) % seq
    outs = []
    for gi, w in enumerate(POOL_WINDOWS):
        e = ubuf[:, gi * POOL_GROUP_WIDTH:(gi + 1) * POOL_GROUP_WIDTH]
        s = e
        step = 1
        while step < w:
            s = s + pltpu.roll(s, step, 0)
            step *= 2
        cnt = jnp.minimum(pos + 1, w).astype(F32)
        dlt = s[POOL_HALO:rows] / cnt - e[POOL_HALO:rows]
        outs.append(_dot(dlt.astype(BF16), pw_ref[gi]))
    pooled = jnp.concatenate(outs, axis=-1) * ps_ref[...]
    yp = _dot(pooled.astype(BF16), wbp_ref[...])

    merged = zg[:, 0:d] * ya + zg[:, d:2 * d] * yp
    x1 = x + gate1 * _dot(merged.astype(BF16), wo_ref[...])
    x1_ref[...] = x1

    h2 = _rms_mod(x1, g2_ref[...], scale2, shift2)
    h2_hi = h2.astype(BF16)
    h2r_ref[...] = h2_hi.astype(F32)
    h2_lo = (h2 - h2_hi.astype(F32)).astype(BF16)
    z = _dot(h2_hi, wrh_ref[...]) + _dot(h2_lo, wrh_ref[...]) + _dot(h2_hi, wrl_ref[...]) + br_ref[...]

    rr = ROUTER_ROWS
    zt = z.T[0:rr, :]
    row = lax.broadcasted_iota(jnp.int32, (rr, tm), 0)
    zgrp = jnp.where(row < N_GROUPS, zt, -jnp.inf)
    mg = jnp.max(zgrp, axis=0, keepdims=True)
    pg_top = 1.0 / jnp.sum(jnp.exp(zgrp - mg), axis=0, keepdims=True)
    g_idx = jnp.min(jnp.where(zgrp == mg, row, rr), axis=0, keepdims=True)
    e_row = row - ROUTER_LANE0
    in_grp = (e_row >= 0) & (e_row < N_EXPERTS) & ((e_row >> 3) == g_idx)
    ze = jnp.where(in_grp, zt, -jnp.inf)
    m1 = jnp.max(ze, axis=0, keepdims=True)
    i1 = jnp.min(jnp.where(ze == m1, row, rr), axis=0, keepdims=True)
    ze2 = jnp.where(row == i1, -jnp.inf, ze)
    m2 = jnp.max(ze2, axis=0, keepdims=True)
    i2 = jnp.min(jnp.where(ze2 == m2, row, rr), axis=0, keepdims=True)
    e2 = jnp.exp(m2 - m1)
    w1 = pg_top / (1.0 + e2)
    w2 = pg_top * e2 / (1.0 + e2)

    hot = jnp.where((row == i1) | (row == i2), 1.0, 0.0)
    tri = jnp.where(lax.broadcasted_iota(jnp.int32, (tm, tm), 0) <= lax.broadcasted_iota(jnp.int32, (tm, tm), 1),
                    1.0, 0.0).astype(BF16)
    prefix = _dot(hot.astype(BF16), tri)

    @pl.when(i == 0)
    def _():
        carry_ref[...] = jnp.zeros_like(carry_ref)

    carry = carry_ref[...]
    base = carry[:, 0:1] + prefix - 1.0
    rank1 = jnp.sum(jnp.where(row == i1, base, 0.0), axis=0, keepdims=True)
    rank2 = jnp.sum(jnp.where(row == i2, base, 0.0), axis=0, keepdims=True)
    carry_ref[...] = carry + prefix[:, tm - 1:tm]
    counts_ref[...] = carry_ref[...]
    fields = ((i1 - ROUTER_LANE0).astype(F32), (i2 - ROUTER_LANE0).astype(F32), w1, w2, rank1, rank2)
    frow = lax.broadcasted_iota(jnp.int32, (ROUTE_FIELDS, tm), 0)
    route_t = jnp.zeros((ROUTE_FIELDS, tm), F32)
    for k, val in enumerate(fields):
        route_t = jnp.where(frow == k, val, route_t)
    route_t_ref[...] = route_t
    route_ref[...] = jnp.concatenate([route_t, jnp.zeros((LANES - ROUTE_FIELDS, tm), F32)], axis=0).T


def _post_attention(x2, mod3, g1, w_z, b_z, attnT, u, pool_w, pool_scale, w_ba, w_bp, w_out, g2,
                    wr_hi, wr_lo, b_r, *, batch, seq):
    n, d = x2.shape
    tm = TM_POST
    tps = seq // tm
    cb = tm // MOBA_BLOCK
    hb = tm // POOL_HALO
    sorted_rows = _expert_tiles(n) * TE_ROWS
    assert sorted_rows % (n // tm) == 0
    zrows = sorted_rows // (n // tm)
    kern = functools.partial(_post_kernel, tm=tm, seq=seq)
    const2 = lambda i: (0, 0)
    return pl.pallas_call(
        kern,
        grid=(n // tm,),
        in_specs=[pl.BlockSpec((tm, d), lambda i: (i, 0)),
                  pl.BlockSpec((1, 1, N_MOD * d), lambda i: (i // tps, 0, 0)),
                  pl.BlockSpec((1, d), const2),
                  pl.BlockSpec((d, 2 * d), lambda i: (0, 1)),
                  pl.BlockSpec((1, 2 * d), const2),
                  pl.BlockSpec((1, cb, ATTN_WIDTH, MOBA_BLOCK), lambda i: (i // tps, i % tps, 0, 0)),
                  pl.BlockSpec((tm, POOL_WIDTH), lambda i: (i, 0)),
                  pl.BlockSpec((POOL_HALO, POOL_WIDTH), lambda i: (jnp.maximum(i * hb - 1, 0), 0)),
                  pl.BlockSpec((len(POOL_WINDOWS), POOL_GROUP_WIDTH, POOL_GROUP_WIDTH), lambda i: (0, 0, 0)),
                  pl.BlockSpec((1, POOL_WIDTH), const2),
                  pl.BlockSpec((ATTN_WIDTH, d), const2),
                  pl.BlockSpec((POOL_WIDTH, d), const2),
                  pl.BlockSpec((d, d), const2),
                  pl.BlockSpec((1, d), const2),
                  pl.BlockSpec((d, LANES), const2),
                  pl.BlockSpec((d, LANES), const2),
                  pl.BlockSpec((1, LANES), const2)],
        out_specs=[pl.BlockSpec((tm, d), lambda i: (i, 0)),
                   pl.BlockSpec((tm, d), lambda i: (i, 0)),
                   pl.BlockSpec((tm, LANES), lambda i: (i, 0)),
                   pl.BlockSpec((ROUTE_FIELDS, tm), lambda i: (0, i)),
                   pl.BlockSpec((ROUTER_ROWS, LANES), const2),
                   pl.BlockSpec((zrows, d), lambda i: (i, 0))],
        out_shape=[jax.ShapeDtypeStruct((n, d), F32),
                   jax.ShapeDtypeStruct((n, d), F32),
                   jax.ShapeDtypeStruct((n, LANES), F32),
                   jax.ShapeDtypeStruct((ROUTE_FIELDS, n), F32),
                   jax.ShapeDtypeStruct((ROUTER_ROWS, LANES), F32),
                   jax.ShapeDtypeStruct((zrows * (n // tm), d), F32)],
        scratch_shapes=[pltpu.VMEM((ROUTER_ROWS, LANES), F32)],
        compiler_params=_cparams("arbitrary"),
        name="merge_outproj_router",
    )(x2, mod3, g1, w_z, b_z, attnT, u, u, pool_w, pool_scale, w_ba, w_bp, w_out, g2, wr_hi, wr_lo, b_r)


def _routing_plan(route_t, counts, *, n):
    te = TE_ROWS
    n_tiles = _expert_tiles(n)
    cnt = counts[ROUTER_LANE0:ROUTER_LANE0 + N_EXPERTS, 0].astype(jnp.int32)
    padded = ((cnt + te - 1) // te) * te
    ids = jnp.arange(N_EXPERTS, dtype=jnp.int32)
    ends = jnp.sum(jnp.where(ids[None, :] <= ids[:, None], padded[None, :], 0), axis=-1)
    starts = ends - padded
    pos = []
    for k in range(EXPERT_TOPK):
        expert = route_t[k].astype(jnp.int32)
        rank = route_t[4 + k].astype(jnp.int32)
        pos.append(jnp.sum(jnp.where(expert[:, None] == ids, starts, 0), axis=-1) + rank)
    tiles = jnp.arange(n_tiles, dtype=jnp.int32)
    tile_expert = jnp.sum((ends // te)[None, :] <= tiles[:, None], axis=-1)
    tile_expert = jnp.minimum(tile_expert, N_EXPERTS - 1).astype(jnp.int32)
    n_used = (ends[N_EXPERTS - 1:] // te).astype(jnp.int32)
    return pos, tile_expert, n_used


def _expert_tiles(n):
    return (EXPERT_TOPK * n) // TE_ROWS + N_EXPERTS


def _tile_pos(pos, tm):
    return jnp.concatenate([p.reshape(-1, 1, tm) for p in pos], axis=-1)


def _dispatch_kernel(pos_ref, h_ref, xz_ref, xs_ref, sem, *, tm):
    del xz_ref
    for r in range(tm):
        for k in range(EXPERT_TOPK):
            pltpu.make_async_copy(h_ref.at[pl.ds(r, 1)], xs_ref.at[pl.ds(pos_ref[0, 0, k * tm + r], 1)],
                                  sem).start(priority=(r + k) % 2)
    for k in range(EXPERT_TOPK):
        pltpu.make_async_copy(h_ref, h_ref, sem).wait()


def _dispatch(pos, h2r, xz, *, n):
    tm = TM_DISPATCH
    rows, d = xz.shape
    assert rows == _expert_tiles(n) * TE_ROWS
    pos3 = _tile_pos(pos, tm)
    return pl.pallas_call(
        functools.partial(_dispatch_kernel, tm=tm),
        grid=(n // tm,),
        in_specs=[pl.BlockSpec((1, 1, EXPERT_TOPK * tm), lambda i: (i, 0, 0), memory_space=pltpu.SMEM),
                  pl.BlockSpec((tm, d), lambda i: (i, 0)),
                  pl.BlockSpec(memory_space=pl.ANY)],
        out_specs=pl.BlockSpec(memory_space=pl.ANY),
        out_shape=jax.ShapeDtypeStruct((rows, d), F32),
        scratch_shapes=[pltpu.SemaphoreType.DMA(())],
        input_output_aliases={2: 0},
        compiler_params=_cparams("arbitrary"),
        name="moe_dispatch",
    )(pos3, h2r, xz)


def _experts_kernel(te_ref, nu_ref, xs_ref, wg_ref, wu_ref, wd_ref, y_ref, wg_b, wu_b, wd_b):
    t = pl.program_id(0)

    @pl.when(t < nu_ref[0])
    def _():
        e = te_ref[t]
        e_prev = te_ref[jnp.maximum(t - 1, 0)]

        @pl.when((t == 0) | (e != e_prev))
        def _():
            wg_b[...] = wg_ref[0].astype(BF16)
            wu_b[...] = wu_ref[0].astype(BF16)
            wd_b[...] = wd_ref[0].astype(BF16)

        x = xs_ref[...].astype(BF16)
        g = _dot(x, wg_b[...])
        u = _dot(x, wu_b[...])
        act = (g * jax.nn.sigmoid(g)) * u
        y_ref[...] = _dot(act.astype(BF16), wd_b[...])

    @pl.when(t >= nu_ref[0])
    def _():
        y_ref[...] = jnp.zeros_like(y_ref)


def _experts(tile_expert, n_used, xs, wg, wu, wd):
    rows, d = xs.shape
    te = TE_ROWS

    def used(t, nu):
        return jnp.minimum(t, nu[0] - 1)

    grid_spec = pltpu.PrefetchScalarGridSpec(
        num_scalar_prefetch=2,
        grid=(rows // te,),
        in_specs=[pl.BlockSpec((te, d), lambda t, te_r, nu: (used(t, nu), 0)),
                  pl.BlockSpec((1, d, D_EXPERT), lambda t, te_r, nu: (te_r[used(t, nu)], 0, 0)),
                  pl.BlockSpec((1, d, D_EXPERT), lambda t, te_r, nu: (te_r[used(t, nu)], 0, 0)),
                  pl.BlockSpec((1, D_EXPERT, d), lambda t, te_r, nu: (te_r[used(t, nu)], 0, 0))],
        out_specs=pl.BlockSpec((te, d), lambda t, te_r, nu: (t, 0)),
        scratch_shapes=[pltpu.VMEM((d, D_EXPERT), BF16), pltpu.VMEM((d, D_EXPERT), BF16),
                        pltpu.VMEM((D_EXPERT, d), BF16)])
    return pl.pallas_call(
        _experts_kernel,
        grid_spec=grid_spec,
        out_shape=jax.ShapeDtypeStruct((rows, d), F32),
        compiler_params=_cparams("arbitrary"),
        name="moe_experts",
    )(tile_expert, n_used, xs, wg, wu, wd)


def _final_kernel(pos_ref, pos_next_ref, x1_ref, route_ref, mod_ref, g_ref, y_ref, o_ref, ybuf, sems, *,
                  tm, final_norm):
    d = D_MODEL
    i = pl.program_id(0)
    slot = i % 2

    def gather(p_ref, s):
        for r in range(tm):
            for k in range(EXPERT_TOPK):
                pltpu.make_async_copy(y_ref.at[pl.ds(p_ref[0, 0, k * tm + r], 1)],
                                      ybuf.at[s, k, pl.ds(r, 1)], sems.at[s]).start(priority=(r + k) % 2)

    @pl.when(i == 0)
    def _():
        gather(pos_ref, 0)

    @pl.when(i + 1 < pl.num_programs(0))
    def _():
        gather(pos_next_ref, 1 - slot)

    for k in range(EXPERT_TOPK):
        pltpu.make_async_copy(ybuf.at[slot, k], ybuf.at[slot, k], sems.at[slot]).wait()

    route = route_ref[...]
    lane = lax.broadcasted_iota(jnp.int32, route.shape, 1)
    y = jnp.zeros((tm, d), F32)
    for k in range(EXPERT_TOPK):
        wk = jnp.sum(jnp.where(lane == 2 + k, route, 0.0), axis=-1, keepdims=True)
        y = y + wk * ybuf[slot, k]
    gate2 = mod_ref[0][:, 5 * d:6 * d]
    x = x1_ref[...] + gate2 * y
    if final_norm:
        x = (x * lax.rsqrt(jnp.mean(x * x, axis=-1, keepdims=True) + EPS)) * g_ref[...]
    o_ref[...] = x


def _final(pos, x1, route, mod3, gf, y, *, seq, final_norm):
    n, d = x1.shape
    tm = TM_FINAL
    tps = seq // tm
    pos3 = _tile_pos(pos, tm)
    last = n // tm - 1
    return pl.pallas_call(
        functools.partial(_final_kernel, tm=tm, final_norm=final_norm),
        grid=(n // tm,),
        in_specs=[pl.BlockSpec((1, 1, EXPERT_TOPK * tm), lambda i: (i, 0, 0), memory_space=pltpu.SMEM),
                  pl.BlockSpec((1, 1, EXPERT_TOPK * tm), lambda i: (jnp.minimum(i + 1, last), 0, 0),
                               memory_space=pltpu.SMEM),
                  pl.BlockSpec((tm, d), lambda i: (i, 0)),
                  pl.BlockSpec((tm, LANES), lambda i: (i, 0)),
                  pl.BlockSpec((1, 1, N_MOD * d), lambda i: (i // tps, 0, 0)),
                  pl.BlockSpec((1, d), lambda i: (0, 0)),
                  pl.BlockSpec(memory_space=pl.ANY)],
        out_specs=pl.BlockSpec((tm, d), lambda i: (i, 0)),
        out_shape=jax.ShapeDtypeStruct((n, d), F32),
        scratch_shapes=[pltpu.VMEM((2, EXPERT_TOPK, tm, d), F32), pltpu.SemaphoreType.DMA((2,))],
        compiler_params=_cparams("arbitrary"),
        name="moe_combine_final_norm",
    )(pos3, pos3, x1, route, mod3, gf, y)


def _split_bf16(w):
    hi = w.astype(BF16)
    return hi, (w - hi.astype(F32)).astype(BF16)


def kernel(x, c, w_ada, b_ada, norm1_g, w_in, b_gate, rel_bias, pool_w, pool_scale, w_branch_attn,
           w_branch_pool, w_out, norm2_g, w_router_group, b_router_group, w_router_expert,
           b_router_expert, w_expert_gate, w_expert_up, w_expert_down, norm_f_g):
    batch, seq, d = x.shape
    depth = w_ada.shape[0]
    n = batch * seq
    nb = seq // MOBA_BLOCK
    qkvu = 3 * ATTN_WIDTH + POOL_WIDTH
    bias_w = _bias_vectors(rel_bias)
    c_pad = jnp.pad(c, ((0, 8 - batch), (0, 0)))

    xc = x.reshape(n, d)
    for l in range(depth):
        mod3 = _modulation(c_pad, w_ada[l], b_ada[l][None])[:batch].reshape(batch, 1, N_MOD * d)
        w_in_b = w_in[l].astype(BF16)
        g1 = norm1_g[l][None]

        assert w_in_b.shape[1] == 2 * qkvu == qkvu + 2 * d
        qT, k2, vT, u, km = _in_projection(xc, mod3, g1, w_in_b, batch=batch, seq=seq)
        km_pad = jnp.pad(km.reshape(batch, nb, ATTN_WIDTH), ((0, 0), (0, MAX_BLOCKS - nb), (0, 0)))
        attnT = _moba_attention(qT, k2.reshape(batch, seq, ATTN_WIDTH), vT, km_pad, bias_w,
                                batch=batch, seq=seq)

        w_r = jnp.concatenate([w_router_group[l], w_router_expert[l]], axis=1)
        w_r = jnp.pad(w_r, ((0, 0), (0, LANES - w_r.shape[1])))
        b_r = jnp.concatenate([b_router_group[l], b_router_expert[l]])
        b_r = jnp.pad(b_r, (0, LANES - b_r.shape[0]))[None]
        wr_hi, wr_lo = _split_bf16(w_r)
        x1, h2r, route, route_t, counts, xz = _post_attention(
            xc, mod3, g1, w_in_b, b_gate[l][None], attnT, u, pool_w[l].astype(BF16),
            pool_scale[l][None], w_branch_attn[l].astype(BF16), w_branch_pool[l].astype(BF16),
            w_out[l].astype(BF16), norm2_g[l][None], wr_hi, wr_lo, b_r, batch=batch, seq=seq)

        pos, tile_expert, n_used = _routing_plan(route_t, counts, n=n)
        xs = _dispatch(pos, h2r, xz, n=n)
        y = _experts(tile_expert, n_used, xs, w_expert_gate[l], w_expert_up[l], w_expert_down[l])
        xc = _final(pos, x1, route, mod3, norm_f_g[None], y, seq=seq, final_norm=(l == depth - 1))
    return xc.reshape(batch, seq, d)
```

```python
import functools
import math

import numpy as np
import jax
import jax.numpy as jnp
from jax import lax
from jax.experimental import pallas as pl
from jax.experimental.pallas import tpu as pltpu

F32 = jnp.float32
BF16 = jnp.bfloat16

D_MODEL = 1024
N_HEADS = 8
HEAD_DIM = 64
ATTN_WIDTH = N_HEADS * HEAD_DIM
MOBA_BLOCK = 256
MOBA_TOPK = 3
MAX_BLOCKS = 32
FAR_GROUP = 4
HEADS_PER_STEP = 4
ONES_ROWS = 16
LOG2E = math.log2(math.e)
POOL_WINDOWS = (2, 4, 8, 16)
POOL_WIDTH = 512
POOL_GROUP_WIDTH = 128
POOL_HALO = 16
REL_BUCKETS = 32
REL_MAX_DIST = 128
N_GROUPS = 4
EXPERTS_PER_GROUP = 8
N_EXPERTS = N_GROUPS * EXPERTS_PER_GROUP
D_EXPERT = 512
N_MOD = 6
EPS = 1e-6
NEG_INF = -1e30
LANES = 128
ROUTER_LANE0 = N_GROUPS
ROUTE_FIELDS = 8
ROUTER_ROWS = 48
VMEM_LIMIT = 56 * 1024 * 1024

TM_PROJ = 512
TM_POST = 512
TM_DISPATCH = 512
TE_ROWS = 512
EXPERT_TOPK = 2
TM_FINAL = 256


def _cparams(*sem):
    return pltpu.CompilerParams(dimension_semantics=sem, vmem_limit_bytes=VMEM_LIMIT)


def _dot(a, b):
    return jnp.dot(a, b, preferred_element_type=F32)


def _rms_mod(x, g, scale, shift):
    xn = x * lax.rsqrt(jnp.mean(x * x, axis=-1, keepdims=True) + EPS)
    return (xn * g) * (1.0 + scale) + shift


def _mod_kernel(c_ref, w_ref, b_ref, o_ref):
    c = c_ref[...]
    ca = c * jax.nn.sigmoid(c)
    o_ref[...] = _dot(ca.astype(BF16), w_ref[...].astype(BF16)) + b_ref[...]


def _modulation(c_pad, w_ada, b_ada):
    rows, d = c_pad.shape
    n_out = w_ada.shape[1]
    tn = 1024
    return pl.pallas_call(
        _mod_kernel,
        grid=(n_out // tn,),
        in_specs=[pl.BlockSpec((rows, d), lambda j: (0, 0)),
                  pl.BlockSpec((d, tn), lambda j: (0, j)),
                  pl.BlockSpec((1, tn), lambda j: (0, j))],
        out_specs=pl.BlockSpec((rows, tn), lambda j: (0, j)),
        out_shape=jax.ShapeDtypeStruct((rows, n_out), F32),
        compiler_params=_cparams("arbitrary"),
        name="adaln_mod",
    )(c_pad, w_ada, b_ada)


def _inproj_kernel(x_ref, mod_ref, g_ref, w_ref, qT_ref, k_ref, vT_ref, u_ref, km_ref, *, tm):
    d = D_MODEL
    mod = mod_ref[0]
    h = _rms_mod(x_ref[...], g_ref[...], mod[:, d:2 * d], mod[:, 0:d])
    proj = _dot(h.astype(BF16), w_ref[...])
    aw = ATTN_WIDTH
    q = proj[:, 0:aw] * (HEAD_DIM ** -0.5 * LOG2E)
    k = proj[:, aw:2 * aw]
    v = proj[:, 2 * aw:3 * aw]
    u_ref[...] = proj[:, 3 * aw:3 * aw + POOL_WIDTH]
    k_ref[...] = k.astype(BF16)
    qT = q.T.astype(BF16)
    vT = v.T.astype(BF16)
    for c in range(tm // MOBA_BLOCK):
        sl = slice(c * MOBA_BLOCK, (c + 1) * MOBA_BLOCK)
        qT_ref[0, c] = qT[:, sl]
        vT_ref[0, c] = vT[:, sl]
        km_ref[0, c:c + 1, :] = jnp.mean(k[sl, :], axis=0, keepdims=True)


def _in_projection(x2, mod3, g1, w_qkvu, *, batch, seq):
    n, d = x2.shape
    tm = TM_PROJ
    tps = seq // tm
    nb = seq // MOBA_BLOCK
    cb = tm // MOBA_BLOCK
    wcols = 3 * ATTN_WIDTH + POOL_WIDTH
    assert w_qkvu.shape[1] % wcols == 0
    kern = functools.partial(_inproj_kernel, tm=tm)
    t_shape = jax.ShapeDtypeStruct((batch, nb, ATTN_WIDTH, MOBA_BLOCK), BF16)
    t_spec = pl.BlockSpec((1, cb, ATTN_WIDTH, MOBA_BLOCK), lambda i: (i // tps, i % tps, 0, 0))
    return pl.pallas_call(
        kern,
        grid=(n // tm,),
        in_specs=[pl.BlockSpec((tm, d), lambda i: (i, 0)),
                  pl.BlockSpec((1, 1, N_MOD * d), lambda i: (i // tps, 0, 0)),
                  pl.BlockSpec((1, d), lambda i: (0, 0)),
                  pl.BlockSpec((d, wcols), lambda i: (0, 0))],
        out_specs=[t_spec,
                   pl.BlockSpec((tm, ATTN_WIDTH), lambda i: (i, 0)),
                   t_spec,
                   pl.BlockSpec((tm, POOL_WIDTH), lambda i: (i, 0)),
                   pl.BlockSpec((1, cb, ATTN_WIDTH), lambda i: (i, 0, 0))],
        out_shape=[t_shape,
                   jax.ShapeDtypeStruct((n, ATTN_WIDTH), BF16),
                   t_shape,
                   jax.ShapeDtypeStruct((n, POOL_WIDTH), F32),
                   jax.ShapeDtypeStruct((n // tm, cb, ATTN_WIDTH), F32)],
        compiler_params=_cparams("arbitrary"),
        name="norm1_inproj",
    )(x2, mod3, g1, w_qkvu)


def _attn_kernel(qT_ref, k_ref, vT_ref, km_ref, bw_ref, o_ref, kaug_ref, vaug_ref, bias_ref, qnear_ref,
                 qfar_ref, sn_ref, sf0_ref, sf1_ref, *, nb):
    blk = MOBA_BLOCK
    hd = HEAD_DIM
    nh = HEADS_PER_STEP
    heads = range(nh)
    lane = lax.broadcasted_iota(jnp.int32, (blk, LANES), 1)

    def pair_lanes(hh):
        return slice((hh // 2) * LANES, (hh // 2 + 1) * LANES)

    def head_rows(hh):
        return slice(hh * hd, (hh + 1) * hd)

    ones = jnp.ones((ONES_ROWS, blk), BF16)

    def build_operands(j, carry):
        rows = pl.ds(pl.multiple_of(j * blk, blk), blk)
        for hh in heads:
            kp = k_ref[0, rows, pair_lanes(hh)]
            if hh % 2 == 0:
                kaug_ref[hh, rows, :] = jnp.where(lane < hd, kp, jnp.where(lane == hd + j, 1.0, 0.0).astype(BF16))
            else:
                kaug_ref[hh, rows, :] = jnp.where(lane >= hd, kp, jnp.where(lane == j, 1.0, 0.0).astype(BF16))
            vaug_ref[j, hh] = jnp.concatenate([vT_ref[0, j, head_rows(hh), :], ones], axis=0)
        return carry

    lax.fori_loop(0, nb, build_operands, 0)

    for hh in heads:
        for t in range(2):
            wide = jnp.broadcast_to(bw_ref[hh, t:t + 1, :], (blk, 2 * blk))
            bias_ref[hh, t] = pltpu.roll(wide, 0, 1, stride=1, stride_axis=0)[:, blk:2 * blk]

    rowi = lax.broadcasted_iota(jnp.int32, (MAX_BLOCKS, blk), 0)
    km_lane = lax.broadcasted_iota(jnp.int32, (MAX_BLOCKS, LANES), 1)
    zpad = jnp.zeros((LANES - hd - MAX_BLOCKS, blk), BF16)
    km_split = []
    for hh in heads:
        sub = hh % 2
        in_head = (km_lane >= sub * hd) & (km_lane < (sub + 1) * hd)
        km_h = jnp.where(in_head, km_ref[0, :, pair_lanes(hh)], 0.0)
        km_hi = km_h.astype(BF16)
        km_split.append((km_hi, (km_h - km_hi.astype(F32)).astype(BF16)))

    grp = FAR_GROUP

    def retire_begin(state, mx):
        m, acc = state
        m_new = jnp.maximum(m, mx)
        return m_new, jnp.exp2(m - m_new) * acc

    def retire_slice(src_ref, hh, m_new, acc, g, j):
        p = jnp.exp2(src_ref[hh, g * blk:(g + 1) * blk, :] - m_new)
        return acc + _dot(vaug_ref[j, hh], p.astype(BF16))

    def fresh():
        return (jnp.full((1, blk), -jnp.inf, F32), jnp.zeros((hd + ONES_ROWS, blk), F32))

    def near_blocks(i):
        return tuple(jnp.where(i - (grp - 1) + g < 0, i + 1, i - (grp - 1) + g) for g in range(grp))

    def finish(i, states):
        for hh in heads:
            _, acc = states[hh]
            o_ref[0, i, head_rows(hh), :] = (acc[0:hd] / acc[hd:hd + 1]).astype(BF16)

    def select(i):
        for hh in heads:
            qp = qT_ref[0, i, pair_lanes(hh), :]
            km_hi, km_lo = km_split[hh]
            gate = _dot(km_hi, qp) + _dot(km_lo, qp)
            g = jnp.where(rowi < i, gate, -jnp.inf)
            sel = rowi == i
            for _ in range(MOBA_TOPK):
                m = jnp.max(g, axis=0, keepdims=True)
                hit = (g == m) & (m > -jnp.inf)
                idx = jnp.min(jnp.where(hit, rowi, MAX_BLOCKS), axis=0, keepdims=True)
                pick = rowi == idx
                sel = sel | pick
                g = jnp.where(pick, -jnp.inf, g)
            near = jnp.where(sel, 0.0, NEG_INF)
            far = jnp.where(rowi > i - grp, NEG_INF, near)
            qh = qp[head_rows(hh % 2), :]
            if hh % 2 == 0:
                qnear_ref[hh] = jnp.concatenate([qh, near.astype(BF16), zpad], axis=0)
                qfar_ref[hh] = jnp.concatenate([qh, far.astype(BF16), zpad], axis=0)
            else:
                qnear_ref[hh] = jnp.concatenate([near.astype(BF16), zpad, qh], axis=0)
                qfar_ref[hh] = jnp.concatenate([far.astype(BF16), zpad, qh], axis=0)

    def park(dst_ref, q_ref, hh, g, j, bias=None):
        rows = pl.ds(pl.multiple_of(j * blk, blk), blk)
        s = _dot(kaug_ref[hh, rows, :], q_ref[hh])
        if bias is not None:
            s = s + bias
        dst_ref[hh, g * blk:(g + 1) * blk, :] = s
        return jnp.max(s, axis=0, keepdims=True)

    def park_near(i):
        idx = near_blocks(i)

        def quarter(g, hh):
            bias = None
            if g == grp - 1:
                bias = bias_ref[hh, 0]
            elif g == grp - 2:
                bias = bias_ref[hh, 1]
            return park(sn_ref, qnear_ref, hh, g, idx[g], bias)
        return quarter

    def park_far(c, dst_ref):
        return lambda g, hh: park(dst_ref, qfar_ref, hh, g, c * grp + g)

    def chunk_blocks(c):
        return tuple(c * grp + g for g in range(grp))

    def step(src_ref, states, maxes, vidx, quarter, park_first):
        begun = [retire_begin(states[hh], maxes[hh]) for hh in heads] if states is not None else None
        accs = [b[1] for b in begun] if begun else None
        new_max = [None] * nh

        def park_quarter(g):
            for hh in heads:
                cm = quarter(g, hh)
                new_max[hh] = cm if new_max[hh] is None else jnp.maximum(new_max[hh], cm)

        for g in range(grp):
            if park_first:
                park_quarter(g)
            if begun:
                for hh in heads:
                    accs[hh] = retire_slice(src_ref, hh, begun[hh][0], accs[hh], g, vidx[g])
            if not park_first and quarter is not None:
                park_quarter(g)
        new_states = tuple((begun[hh][0], accs[hh]) for hh in heads) if begun else None
        return new_states, tuple(new_max)

    def fresh_states():
        return tuple(fresh() for _ in heads)

    def early_tile(i, near_max):
        nxt = jnp.minimum(i + 1, nb - 1)
        select(nxt)
        states, next_max = step(sn_ref, fresh_states(), near_max, near_blocks(i), park_near(nxt), False)
        finish(i, states)
        return next_max

    def late_tile(i, near_max):
        sf = (sf0_ref, sf1_ref)
        n_chunks = (i - (grp - 1) + grp - 1) // grp
        nxt = jnp.minimum(i + 1, nb - 1)
        states, far_max = step(sn_ref, fresh_states(), near_max, near_blocks(i), park_far(0, sf[0]), True)

        def pair(k, st):
            states, far_max = st
            states, far_max = step(sf[0], states, far_max, chunk_blocks(2 * k), park_far(2 * k + 1, sf[1]), True)
            return step(sf[1], states, far_max, chunk_blocks(2 * k + 1), park_far(2 * k + 2, sf[0]), True)

        states, far_max = lax.fori_loop(0, (n_chunks - 1) // 2, pair, (states, far_max))

        def last(src, states, far_max):
            select(nxt)
            states, next_max = step(src, states, far_max, chunk_blocks(n_chunks - 1), park_near(nxt), True)
            finish(i, states)
            return next_max

        def even_count(st):
            states, far_max = step(sf[0], st[0], st[1], chunk_blocks(n_chunks - 2),
                                   park_far(n_chunks - 1, sf[1]), True)
            return last(sf[1], states, far_max)

        def odd_count(st):
            return last(sf[0], st[0], st[1])

        return lax.cond(n_chunks % 2 == 0, even_count, odd_count, (states, far_max))

    n_early = min(grp, nb)
    select(0)
    _, near_max = step(None, None, None, None, park_near(0), True)
    near_max = lax.fori_loop(0, n_early, early_tile, near_max)
    lax.fori_loop(n_early, nb, late_tile, near_max)


def _rel_bucket_table(max_rel):
    n = np.arange(max_rel)
    max_exact = REL_BUCKETS // 2
    nf = np.maximum(n, 1).astype(np.float32)
    large = max_exact + (np.log(nf / np.float32(max_exact)) / np.float32(math.log(REL_MAX_DIST / max_exact))
                         * np.float32(REL_BUCKETS - max_exact)).astype(np.int32)
    large = np.minimum(large, REL_BUCKETS - 1)
    return np.where(n < max_exact, n, large)


def _bias_vectors(rel_bias):
    blk = MOBA_BLOCK
    table = _rel_bucket_table(2 * blk)
    r = np.arange(-blk, blk)
    bt = (rel_bias.T - rel_bias[REL_BUCKETS - 1][:, None]) * LOG2E
    own = jnp.where(jnp.asarray(r >= 0)[None], bt[:, table[np.maximum(r, 0)]], NEG_INF)
    prev = bt[:, table[np.minimum(r + blk, 2 * blk - 1)]]
    return jnp.stack([own, prev], axis=1).astype(F32)


def _moba_attention(qT, k3, vT, km_pad, bias_w, *, batch, seq):
    nb = seq // MOBA_BLOCK
    blk = MOBA_BLOCK
    assert nb % FAR_GROUP == 0 and nb <= MAX_BLOCKS
    kern = functools.partial(_attn_kernel, nb=nb)
    nh = HEADS_PER_STEP
    width = nh * HEAD_DIM
    assert nh % 2 == 0 and N_HEADS % nh == 0

    def big(shape, index_map):
        return pl.BlockSpec(shape, index_map, pipeline_mode=pl.Buffered(1))

    return pl.pallas_call(
        kern,
        grid=(batch, N_HEADS // nh),
        in_specs=[big((1, nb, width, blk), lambda b, p: (b, 0, p, 0)),
                  big((1, seq, width), lambda b, p: (b, 0, p)),
                  big((1, nb, width, blk), lambda b, p: (b, 0, p, 0)),
                  pl.BlockSpec((1, MAX_BLOCKS, width), lambda b, p: (b, 0, p)),
                  pl.BlockSpec((nh, 2, 2 * blk), lambda b, p: (p, 0, 0))],
        out_specs=pl.BlockSpec((1, nb, width, blk), lambda b, p: (b, 0, p, 0)),
        out_shape=jax.ShapeDtypeStruct((batch, nb, ATTN_WIDTH, blk), BF16),
        scratch_shapes=[pltpu.VMEM((nh, seq, LANES), BF16),
                        pltpu.VMEM((nb, nh, HEAD_DIM + ONES_ROWS, blk), BF16),
                        pltpu.VMEM((nh, 2, blk, blk), F32),
                        pltpu.VMEM((nh, LANES, blk), BF16),
                        pltpu.VMEM((nh, LANES, blk), BF16),
                        pltpu.VMEM((nh, FAR_GROUP * blk, blk), F32),
                        pltpu.VMEM((nh, FAR_GROUP * blk, blk), F32),
                        pltpu.VMEM((nh, FAR_GROUP * blk, blk), F32)],
        compiler_params=_cparams("arbitrary", "arbitrary"),
        name="moba_attention",
    )(qT, k3, vT, km_pad, bias_w)


def _post_kernel(x_ref, mod_ref, g1_ref, wz_ref, bz_ref, at_ref, u_ref, uh_ref, pw_ref, ps_ref,
                 wba_ref, wbp_ref, wo_ref, g2_ref, wrh_ref, wrl_ref, br_ref,
                 x1_ref, h2r_ref, route_ref, route_t_ref, counts_ref, xz_ref, carry_ref, *, tm, seq):
    d = D_MODEL
    i = pl.program_id(0)
    x = x_ref[...]
    mod = mod_ref[0]
    shift1, scale1, gate1 = mod[:, 0:d], mod[:, d:2 * d], mod[:, 2 * d:3 * d]
    shift2, scale2 = mod[:, 3 * d:4 * d], mod[:, 4 * d:5 * d]

    h = _rms_mod(x, g1_ref[...], scale1, shift1)
    zg = 0.5 * jnp.tanh(0.5 * (_dot(h.astype(BF16), wz_ref[...]) + bz_ref[...])) + 0.5

    ya = jnp.concatenate(
        [lax.dot_general(at_ref[0, c], wba_ref[...], (((0,), (0,)), ((), ())),
                         preferred_element_type=F32) for c in range(tm // MOBA_BLOCK)], axis=0)

    first = (i * tm) % seq == 0
    halo = jnp.where(first, 0.0, uh_ref[...])
    ubuf = jnp.concatenate([halo, u_ref[...]], axis=0)
    rows = tm + POOL_HALO
    pos = (lax.broadcasted_iota(jnp.int32, (tm, POOL_GROUP_WIDTH), 0) + i * tm) % seq
    outs = []
    for gi, w in enumerate(POOL_WINDOWS):
        e = ubuf[:, gi * POOL_GROUP_WIDTH:(gi + 1) * POOL_GROUP_WIDTH]
        s = e
        step = 1
        while step < w:
            s = s + pltpu.roll(s, step, 0)
            step *= 2
        cnt = jnp.minimum(pos + 1, w).astype(F32)
        dlt = s[POOL_HALO:rows] / cnt - e[POOL_HALO:rows]
        outs.append(_dot(dlt.astype(BF16), pw_ref[gi]))
    pooled = jnp.concatenate(outs, axis=-1) * ps_ref[...]
    yp = _dot(pooled.astype(BF16), wbp_ref[...])

    merged = zg[:, 0:d] * ya + zg[:, d:2 * d] * yp
    x1 = x + gate1 * _dot(merged.astype(BF16), wo_ref[...])
    x1_ref[...] = x1

    h2 = _rms_mod(x1, g2_ref[...], scale2, shift2)
    h2_hi = h2.astype(BF16)
    h2r_ref[...] = h2_hi.astype(F32)
    h2_lo = (h2 - h2_hi.astype(F32)).astype(BF16)
    z = _dot(h2_hi, wrh_ref[...]) + _dot(h2_lo, wrh_ref[...]) + _dot(h2_hi, wrl_ref[...]) + br_ref[...]

    rr = ROUTER_ROWS
    zt = z.T[0:rr, :]
    row = lax.broadcasted_iota(jnp.int32, (rr, tm), 0)
    zgrp = jnp.where(row < N_GROUPS, zt, -jnp.inf)
    mg = jnp.max(zgrp, axis=0, keepdims=True)
    pg_top = 1.0 / jnp.sum(jnp.exp(zgrp - mg), axis=0, keepdims=True)
    g_idx = jnp.min(jnp.where(zgrp == mg, row, rr), axis=0, keepdims=True)
    e_row = row - ROUTER_LANE0
    in_grp = (e_row >= 0) & (e_row < N_EXPERTS) & ((e_row >> 3) == g_idx)
    ze = jnp.where(in_grp, zt, -jnp.inf)
    m1 = jnp.max(ze, axis=0, keepdims=True)
    i1 = jnp.min(jnp.where(ze == m1, row, rr), axis=0, keepdims=True)
    ze2 = jnp.where(row == i1, -jnp.inf, ze)
    m2 = jnp.max(ze2, axis=0, keepdims=True)
    i2 = jnp.min(jnp.where(ze2 == m2, row, rr), axis=0, keepdims=True)
    e2 = jnp.exp(m2 - m1)
    w1 = pg_top / (1.0 + e2)
    w2 = pg_top * e2 / (1.0 + e2)

    hot = jnp.where((row == i1) | (row == i2), 1.0, 0.0)
    tri = jnp.where(lax.broadcasted_iota(jnp.int32, (tm, tm), 0) <= lax.broadcasted_iota(jnp.int32, (tm, tm), 1),
                    1.0, 0.0).astype(BF16)
    prefix = _dot(hot.astype(BF16), tri)

    @pl.when(i == 0)
    def _():
        carry_ref[...] = jnp.zeros_like(carry_ref)

    carry = carry_ref[...]
    base = carry[:, 0:1] + prefix - 1.0
    rank1 = jnp.sum(jnp.where(row == i1, base, 0.0), axis=0, keepdims=True)
    rank2 = jnp.sum(jnp.where(row == i2, base, 0.0), axis=0, keepdims=True)
    carry_ref[...] = carry + prefix[:, tm - 1:tm]
    counts_ref[...] = carry_ref[...]
    fields = ((i1 - ROUTER_LANE0).astype(F32), (i2 - ROUTER_LANE0).astype(F32), w1, w2, rank1, rank2)
    frow = lax.broadcasted_iota(jnp.int32, (ROUTE_FIELDS, tm), 0)
    route_t = jnp.zeros((ROUTE_FIELDS, tm), F32)
    for k, val in enumerate(fields):
        route_t = jnp.where(frow == k, val, route_t)
    route_t_ref[...] = route_t
    route_ref[...] = jnp.concatenate([route_t, jnp.zeros((LANES - ROUTE_FIELDS, tm), F32)], axis=0).T
    xz_ref[...] = jnp.zeros_like(xz_ref)


def _post_attention(x2, mod3, g1, w_z, b_z, attnT, u, pool_w, pool_scale, w_ba, w_bp, w_out, g2,
                    wr_hi, wr_lo, b_r, *, batch, seq):
    n, d = x2.shape
    tm = TM_POST
    tps = seq // tm
    cb = tm // MOBA_BLOCK
    hb = tm // POOL_HALO
    sorted_rows = _expert_tiles(n) * TE_ROWS
    assert sorted_rows % (n // tm) == 0
    zrows = sorted_rows // (n // tm)
    kern = functools.partial(_post_kernel, tm=tm, seq=seq)
    const2 = lambda i: (0, 0)
    return pl.pallas_call(
        kern,
        grid=(n // tm,),
        in_specs=[pl.BlockSpec((tm, d), lambda i: (i, 0)),
                  pl.BlockSpec((1, 1, N_MOD * d), lambda i: (i // tps, 0, 0)),
                  pl.BlockSpec((1, d), const2),
                  pl.BlockSpec((d, 2 * d), lambda i: (0, 1)),
                  pl.BlockSpec((1, 2 * d), const2),
                  pl.BlockSpec((1, cb, ATTN_WIDTH, MOBA_BLOCK), lambda i: (i // tps, i % tps, 0, 0)),
                  pl.BlockSpec((tm, POOL_WIDTH), lambda i: (i, 0)),
                  pl.BlockSpec((POOL_HALO, POOL_WIDTH), lambda i: (jnp.maximum(i * hb - 1, 0), 0)),
                  pl.BlockSpec((len(POOL_WINDOWS), POOL_GROUP_WIDTH, POOL_GROUP_WIDTH), lambda i: (0, 0, 0)),
                  pl.BlockSpec((1, POOL_WIDTH), const2),
                  pl.BlockSpec((ATTN_WIDTH, d), const2),
                  pl.BlockSpec((POOL_WIDTH, d), const2),
                  pl.BlockSpec((d, d), const2),
                  pl.BlockSpec((1, d), const2),
                  pl.BlockSpec((d, LANES), const2),
                  pl.BlockSpec((d, LANES), const2),
                  pl.BlockSpec((1, LANES), const2)],
        out_specs=[pl.BlockSpec((tm, d), lambda i: (i, 0)),
                   pl.BlockSpec((tm, d), lambda i: (i, 0)),
                   pl.BlockSpec((tm, LANES), lambda i: (i, 0)),
                   pl.BlockSpec((ROUTE_FIELDS, tm), lambda i: (0, i)),
                   pl.BlockSpec((ROUTER_ROWS, LANES), const2),
                   pl.BlockSpec((zrows, d), lambda i: (i, 0))],
        out_shape=[jax.ShapeDtypeStruct((n, d), F32),
                   jax.ShapeDtypeStruct((n, d), F32),
                   jax.ShapeDtypeStruct((n, LANES), F32),
                   jax.ShapeDtypeStruct((ROUTE_FIELDS, n), F32),
                   jax.ShapeDtypeStruct((ROUTER_ROWS, LANES), F32),
                   jax.ShapeDtypeStruct((zrows * (n // tm), d), F32)],
        scratch_shapes=[pltpu.VMEM((ROUTER_ROWS, LANES), F32)],
        compiler_params=_cparams("arbitrary"),
        name="merge_outproj_router",
    )(x2, mod3, g1, w_z, b_z, attnT, u, u, pool_w, pool_scale, w_ba, w_bp, w_out, g2, wr_hi, wr_lo, b_r)


def _routing_plan(route_t, counts, *, n):
    te = TE_ROWS
    n_tiles = _expert_tiles(n)
    cnt = counts[ROUTER_LANE0:ROUTER_LANE0 + N_EXPERTS, 0].astype(jnp.int32)
    padded = ((cnt + te - 1) // te) * te
    ids = jnp.arange(N_EXPERTS, dtype=jnp.int32)
    ends = jnp.sum(jnp.where(ids[None, :] <= ids[:, None], padded[None, :], 0), axis=-1)
    starts = ends - padded
    pos = []
    for k in range(EXPERT_TOPK):
        expert = route_t[k].astype(jnp.int32)
        rank = route_t[4 + k].astype(jnp.int32)
        pos.append(jnp.sum(jnp.where(expert[:, None] == ids, starts, 0), axis=-1) + rank)
    tiles = jnp.arange(n_tiles, dtype=jnp.int32)
    tile_expert = jnp.sum((ends // te)[None, :] <= tiles[:, None], axis=-1)
    tile_expert = jnp.minimum(tile_expert, N_EXPERTS - 1).astype(jnp.int32)
    n_used = (ends[N_EXPERTS - 1:] // te).astype(jnp.int32)
    return pos, tile_expert, n_used


def _expert_tiles(n):
    return (EXPERT_TOPK * n) // TE_ROWS + N_EXPERTS


def _tile_pos(pos, tm):
    return jnp.concatenate([p.reshape(-1, 1, tm) for p in pos], axis=-1)


def _dispatch_kernel(pos_ref, h_ref, xz_ref, xs_ref, sem, *, tm):
    del xz_ref
    for r in range(tm):
        for k in range(EXPERT_TOPK):
            pltpu.make_async_copy(h_ref.at[pl.ds(r, 1)], xs_ref.at[pl.ds(pos_ref[0, 0, k * tm + r], 1)],
                                  sem).start(priority=(r + k) % 2)
    for k in range(EXPERT_TOPK):
        pltpu.make_async_copy(h_ref, h_ref, sem).wait()


def _dispatch(pos, h2r, xz, *, n):
    tm = TM_DISPATCH
    rows, d = xz.shape
    assert rows == _expert_tiles(n) * TE_ROWS
    pos3 = _tile_pos(pos, tm)
    return pl.pallas_call(
        functools.partial(_dispatch_kernel, tm=tm),
        grid=(n // tm,),
        in_specs=[pl.BlockSpec((1, 1, EXPERT_TOPK * tm), lambda i: (i, 0, 0), memory_space=pltpu.SMEM),
                  pl.BlockSpec((tm, d), lambda i: (i, 0)),
                  pl.BlockSpec(memory_space=pl.ANY)],
        out_specs=pl.BlockSpec(memory_space=pl.ANY),
        out_shape=jax.ShapeDtypeStruct((rows, d), F32),
        scratch_shapes=[pltpu.SemaphoreType.DMA(())],
        input_output_aliases={2: 0},
        compiler_params=_cparams("arbitrary"),
        name="moe_dispatch",
    )(pos3, h2r, xz)


def _experts_kernel(te_ref, nu_ref, xs_ref, wg_ref, wu_ref, wd_ref, y_ref, wg_b, wu_b, wd_b):
    t = pl.program_id(0)

    @pl.when(t < nu_ref[0])
    def _():
        e = te_ref[t]
        e_prev = te_ref[jnp.maximum(t - 1, 0)]

        @pl.when((t == 0) | (e != e_prev))
        def _():
            wg_b[...] = wg_ref[0].astype(BF16)
            wu_b[...] = wu_ref[0].astype(BF16)
            wd_b[...] = wd_ref[0].astype(BF16)

        x = xs_ref[...].astype(BF16)
        g = _dot(x, wg_b[...])
        u = _dot(x, wu_b[...])
        act = (g * jax.nn.sigmoid(g)) * u
        y_ref[...] = _dot(act.astype(BF16), wd_b[...])

    @pl.when(t >= nu_ref[0])
    def _():
        y_ref[...] = jnp.zeros_like(y_ref)


def _experts(tile_expert, n_used, xs, wg, wu, wd):
    rows, d = xs.shape
    te = TE_ROWS

    def used(t, nu):
        return jnp.minimum(t, nu[0] - 1)

    grid_spec = pltpu.PrefetchScalarGridSpec(
        num_scalar_prefetch=2,
        grid=(rows // te,),
        in_specs=[pl.BlockSpec((te, d), lambda t, te_r, nu: (used(t, nu), 0)),
                  pl.BlockSpec((1, d, D_EXPERT), lambda t, te_r, nu: (te_r[used(t, nu)], 0, 0)),
                  pl.BlockSpec((1, d, D_EXPERT), lambda t, te_r, nu: (te_r[used(t, nu)], 0, 0)),
                  pl.BlockSpec((1, D_EXPERT, d), lambda t, te_r, nu: (te_r[used(t, nu)], 0, 0))],
        out_specs=pl.BlockSpec((te, d), lambda t, te_r, nu: (t, 0)),
        scratch_shapes=[pltpu.VMEM((d, D_EXPERT), BF16), pltpu.VMEM((d, D_EXPERT), BF16),
                        pltpu.VMEM((D_EXPERT, d), BF16)])
    return pl.pallas_call(
        _experts_kernel,
        grid_spec=grid_spec,
        out_shape=jax.ShapeDtypeStruct((rows, d), F32),
        compiler_params=_cparams("arbitrary"),
        name="moe_experts",
    )(tile_expert, n_used, xs, wg, wu, wd)


def _final_kernel(pos_ref, pos_next_ref, x1_ref, route_ref, mod_ref, g_ref, y_ref, o_ref, ybuf, sems, *,
                  tm, final_norm):
    d = D_MODEL
    i = pl.program_id(0)
    slot = i % 2

    def gather(p_ref, s):
        for r in range(tm):
            for k in range(EXPERT_TOPK):
                pltpu.make_async_copy(y_ref.at[pl.ds(p_ref[0, 0, k * tm + r], 1)],
                                      ybuf.at[s, k, pl.ds(r, 1)], sems.at[s]).start(priority=(r + k) % 2)

    @pl.when(i == 0)
    def _():
        gather(pos_ref, 0)

    @pl.when(i + 1 < pl.num_programs(0))
    def _():
        gather(pos_next_ref, 1 - slot)

    for k in range(EXPERT_TOPK):
        pltpu.make_async_copy(ybuf.at[slot, k], ybuf.at[slot, k], sems.at[slot]).wait()

    route = route_ref[...]
    lane = lax.broadcasted_iota(jnp.int32, route.shape, 1)
    y = jnp.zeros((tm, d), F32)
    for k in range(EXPERT_TOPK):
        wk = jnp.sum(jnp.where(lane == 2 + k, route, 0.0), axis=-1, keepdims=True)
        y = y + wk * ybuf[slot, k]
    gate2 = mod_ref[0][:, 5 * d:6 * d]
    x = x1_ref[...] + gate2 * y
    if final_norm:
        x = (x * lax.rsqrt(jnp.mean(x * x, axis=-1, keepdims=True) + EPS)) * g_ref[...]
    o_ref[...] = x


def _final(pos, x1, route, mod3, gf, y, *, seq, final_norm):
    n, d = x1.shape
    tm = TM_FINAL
    tps = seq // tm
    pos3 = _tile_pos(pos, tm)
    last = n // tm - 1
    return pl.pallas_call(
        functools.partial(_final_kernel, tm=tm, final_norm=final_norm),
        grid=(n // tm,),
        in_specs=[pl.BlockSpec((1, 1, EXPERT_TOPK * tm), lambda i: (i, 0, 0), memory_space=pltpu.SMEM),
                  pl.BlockSpec((1, 1, EXPERT_TOPK * tm), lambda i: (jnp.minimum(i + 1, last), 0, 0),
                               memory_space=pltpu.SMEM),
                  pl.BlockSpec((tm, d), lambda i: (i, 0)),
                  pl.BlockSpec((tm, LANES), lambda i: (i, 0)),
                  pl.BlockSpec((1, 1, N_MOD * d), lambda i: (i // tps, 0, 0)),
                  pl.BlockSpec((1, d), lambda i: (0, 0)),
                  pl.BlockSpec(memory_space=pl.ANY)],
        out_specs=pl.BlockSpec((tm, d), lambda i: (i, 0)),
        out_shape=jax.ShapeDtypeStruct((n, d), F32),
        scratch_shapes=[pltpu.VMEM((2, EXPERT_TOPK, tm, d), F32), pltpu.SemaphoreType.DMA((2,))],
        compiler_params=_cparams("arbitrary"),
        name="moe_combine_final_norm",
    )(pos3, pos3, x1, route, mod3, gf, y)


def _split_bf16(w):
    hi = w.astype(BF16)
    return hi, (w - hi.astype(F32)).astype(BF16)


def kernel(x, c, w_ada, b_ada, norm1_g, w_in, b_gate, rel_bias, pool_w, pool_scale, w_branch_attn,
           w_branch_pool, w_out, norm2_g, w_router_group, b_router_group, w_router_expert,
           b_router_expert, w_expert_gate, w_expert_up, w_expert_down, norm_f_g):
    batch, seq, d = x.shape
    depth = w_ada.shape[0]
    n = batch * seq
    nb = seq // MOBA_BLOCK
    qkvu = 3 * ATTN_WIDTH + POOL_WIDTH
    bias_w = _bias_vectors(rel_bias)
    c_pad = jnp.pad(c, ((0, 8 - batch), (0, 0)))

    xc = x.reshape(n, d)
    for l in range(depth):
        mod3 = _modulation(c_pad, w_ada[l], b_ada[l][None])[:batch].reshape(batch, 1, N_MOD * d)
        w_in_b = w_in[l].astype(BF16)
        g1 = norm1_g[l][None]

        assert w_in_b.shape[1] == 2 * qkvu == qkvu + 2 * d
        qT, k2, vT, u, km = _in_projection(xc, mod3, g1, w_in_b, batch=batch, seq=seq)
        km_pad = jnp.pad(km.reshape(batch, nb, ATTN_WIDTH), ((0, 0), (0, MAX_BLOCKS - nb), (0, 0)))
        attnT = _moba_attention(qT, k2.reshape(batch, seq, ATTN_WIDTH), vT, km_pad, bias_w,
                                batch=batch, seq=seq)

        w_r = jnp.concatenate([w_router_group[l], w_router_expert[l]], axis=1)
        w_r = jnp.pad(w_r, ((0, 0), (0, LANES - w_r.shape[1])))
        b_r = jnp.concatenate([b_router_group[l], b_router_expert[l]])
        b_r = jnp.pad(b_r, (0, LANES - b_r.shape[0]))[None]
        wr_hi, wr_lo = _split_bf16(w_r)
        x1, h2r, route, route_t, counts, xz = _post_attention(
            xc, mod3, g1, w_in_b, b_gate[l][None], attnT, u, pool_w[l].astype(BF16),
            pool_scale[l][None], w_branch_attn[l].astype(BF16), w_branch_pool[l].astype(BF16),
            w_out[l].astype(BF16), norm2_g[l][None], wr_hi, wr_lo, b_r, batch=batch, seq=seq)

        pos, tile_expert, n_used = _routing_plan(route_t, counts, n=n)
        xs = _dispatch(pos, h2r, xz, n=n)
        y = _experts(tile_expert, n_used, xs, w_expert_gate[l], w_expert_up[l], w_expert_down[l])
        xc = _final(pos, x1, route, mod3, norm_f_g[None], y, seq=seq, final_norm=(l == depth - 1))
    return xc.reshape(batch, seq, d)
```

```python
import functools
import math

import numpy as np
import jax
import jax.numpy as jnp
from jax import lax
from jax.experimental import pallas as pl
from jax.experimental.pallas import tpu as pltpu

F32 = jnp.float32
BF16 = jnp.bfloat16

D_MODEL = 1024
N_HEADS = 8
HEAD_DIM = 64
ATTN_WIDTH = N_HEADS * HEAD_DIM
MOBA_BLOCK = 256
MOBA_TOPK = 3
MAX_BLOCKS = 32
FAR_GROUP = 4
HEADS_PER_STEP = 4
ONES_ROWS = 16
LOG2E = math.log2(math.e)
POOL_WINDOWS = (2, 4, 8, 16)
POOL_WIDTH = 512
POOL_GROUP_WIDTH = 128
POOL_HALO = 16
REL_BUCKETS = 32
REL_MAX_DIST = 128
N_GROUPS = 4
EXPERTS_PER_GROUP = 8
N_EXPERTS = N_GROUPS * EXPERTS_PER_GROUP
D_EXPERT = 512
N_MOD = 6
EPS = 1e-6
NEG_INF = -1e30
LANES = 128
ROUTER_LANE0 = N_GROUPS
ROUTE_FIELDS = 8
ROUTER_ROWS = 48
VMEM_LIMIT = 56 * 1024 * 1024

TM_PROJ = 512
TM_POST = 512
TM_DISPATCH = 1024
TE_ROWS = 512
EXPERT_TOPK = 2
TM_FINAL = 256


def _cparams(*sem):
    return pltpu.CompilerParams(dimension_semantics=sem, vmem_limit_bytes=VMEM_LIMIT)


def _dot(a, b):
    return jnp.dot(a, b, preferred_element_type=F32)


def _rms_mod(x, g, scale, shift):
    xn = x * lax.rsqrt(jnp.mean(x * x, axis=-1, keepdims=True) + EPS)
    return (xn * g) * (1.0 + scale) + shift


def _mod_kernel(c_ref, w_ref, b_ref, o_ref):
    c = c_ref[...]
    ca = c * jax.nn.sigmoid(c)
    o_ref[...] = _dot(ca.astype(BF16), w_ref[...].astype(BF16)) + b_ref[...]


def _modulation(c_pad, w_ada, b_ada):
    rows, d = c_pad.shape
    n_out = w_ada.shape[1]
    tn = 1024
    return pl.pallas_call(
        _mod_kernel,
        grid=(n_out // tn,),
        in_specs=[pl.BlockSpec((rows, d), lambda j: (0, 0)),
                  pl.BlockSpec((d, tn), lambda j: (0, j)),
                  pl.BlockSpec((1, tn), lambda j: (0, j))],
        out_specs=pl.BlockSpec((rows, tn), lambda j: (0, j)),
        out_shape=jax.ShapeDtypeStruct((rows, n_out), F32),
        compiler_params=_cparams("arbitrary"),
        name="adaln_mod",
    )(c_pad, w_ada, b_ada)


def _inproj_kernel(x_ref, mod_ref, g_ref, w_ref, qT_ref, k_ref, vT_ref, u_ref, km_ref, h_ref, *, tm):
    d = D_MODEL
    mod = mod_ref[0]
    h = _rms_mod(x_ref[...], g_ref[...], mod[:, d:2 * d], mod[:, 0:d]).astype(BF16)
    h_ref[...] = h
    proj = _dot(h, w_ref[...])
    aw = ATTN_WIDTH
    q = proj[:, 0:aw] * (HEAD_DIM ** -0.5 * LOG2E)
    k = proj[:, aw:2 * aw]
    v = proj[:, 2 * aw:3 * aw]
    u_ref[...] = proj[:, 3 * aw:3 * aw + POOL_WIDTH]
    k_ref[...] = k.astype(BF16)
    qT = q.T.astype(BF16)
    vT = v.T.astype(BF16)
    for c in range(tm // MOBA_BLOCK):
        sl = slice(c * MOBA_BLOCK, (c + 1) * MOBA_BLOCK)
        qT_ref[0, c] = qT[:, sl]
        vT_ref[0, c] = vT[:, sl]
        km_ref[0, c:c + 1, :] = jnp.mean(k[sl, :], axis=0, keepdims=True)


def _in_projection(x2, mod3, g1, w_qkvu, *, batch, seq):
    n, d = x2.shape
    tm = TM_PROJ
    tps = seq // tm
    nb = seq // MOBA_BLOCK
    cb = tm // MOBA_BLOCK
    wcols = 3 * ATTN_WIDTH + POOL_WIDTH
    assert w_qkvu.shape[1] % wcols == 0
    kern = functools.partial(_inproj_kernel, tm=tm)
    t_shape = jax.ShapeDtypeStruct((batch, nb, ATTN_WIDTH, MOBA_BLOCK), BF16)
    t_spec = pl.BlockSpec((1, cb, ATTN_WIDTH, MOBA_BLOCK), lambda i: (i // tps, i % tps, 0, 0))
    return pl.pallas_call(
        kern,
        grid=(n // tm,),
        in_specs=[pl.BlockSpec((tm, d), lambda i: (i, 0)),
                  pl.BlockSpec((1, 1, N_MOD * d), lambda i: (i // tps, 0, 0)),
                  pl.BlockSpec((1, d), lambda i: (0, 0)),
                  pl.BlockSpec((d, wcols), lambda i: (0, 0))],
        out_specs=[t_spec,
                   pl.BlockSpec((tm, ATTN_WIDTH), lambda i: (i, 0)),
                   t_spec,
                   pl.BlockSpec((tm, POOL_WIDTH), lambda i: (i, 0)),
                   pl.BlockSpec((1, cb, ATTN_WIDTH), lambda i: (i, 0, 0)),
                   pl.BlockSpec((tm, d), lambda i: (i, 0))],
        out_shape=[t_shape,
                   jax.ShapeDtypeStruct((n, ATTN_WIDTH), BF16),
                   t_shape,
                   jax.ShapeDtypeStruct((n, POOL_WIDTH), F32),
                   jax.ShapeDtypeStruct((n // tm, cb, ATTN_WIDTH), F32),
                   jax.ShapeDtypeStruct((n, d), BF16)],
        compiler_params=_cparams("arbitrary"),
        name="norm1_inproj",
    )(x2, mod3, g1, w_qkvu)


def _attn_kernel(qT_ref, k_ref, vT_ref, km_ref, bw_ref, o_ref, kaug_ref, vaug_ref, bias_ref, qnear_ref,
                 qfar_ref, sn_ref, sf0_ref, sf1_ref, *, nb):
    blk = MOBA_BLOCK
    hd = HEAD_DIM
    nh = HEADS_PER_STEP
    heads = range(nh)
    lane = lax.broadcasted_iota(jnp.int32, (blk, LANES), 1)

    def pair_lanes(hh):
        return slice((hh // 2) * LANES, (hh // 2 + 1) * LANES)

    def head_rows(hh):
        return slice(hh * hd, (hh + 1) * hd)

    ones = jnp.ones((ONES_ROWS, blk), BF16)

    def build_operands(j, carry):
        rows = pl.ds(pl.multiple_of(j * blk, blk), blk)
        for hh in heads:
            kp = k_ref[0, rows, pair_lanes(hh)]
            if hh % 2 == 0:
                kaug_ref[hh, rows, :] = jnp.where(lane < hd, kp, jnp.where(lane == hd + j, 1.0, 0.0).astype(BF16))
            else:
                kaug_ref[hh, rows, :] = jnp.where(lane >= hd, kp, jnp.where(lane == j, 1.0, 0.0).astype(BF16))
            vaug_ref[j, hh] = jnp.concatenate([vT_ref[0, j, head_rows(hh), :], ones], axis=0)
        return carry

    lax.fori_loop(0, nb, build_operands, 0)

    for hh in heads:
        for t in range(2):
            wide = jnp.broadcast_to(bw_ref[hh, t:t + 1, :], (blk, 2 * blk))
            bias_ref[hh, t] = pltpu.roll(wide, 0, 1, stride=1, stride_axis=0)[:, blk:2 * blk]

    rowi = lax.broadcasted_iota(jnp.int32, (MAX_BLOCKS, blk), 0)
    km_lane = lax.broadcasted_iota(jnp.int32, (MAX_BLOCKS, LANES), 1)
    zpad = jnp.zeros((LANES - hd - MAX_BLOCKS, blk), BF16)
    km_split = []
    for hh in heads:
        sub = hh % 2
        in_head = (km_lane >= sub * hd) & (km_lane < (sub + 1) * hd)
        km_h = jnp.where(in_head, km_ref[0, :, pair_lanes(hh)], 0.0)
        km_hi = km_h.astype(BF16)
        km_split.append((km_hi, (km_h - km_hi.astype(F32)).astype(BF16)))

    grp = FAR_GROUP

    def retire_begin(state, mx):
        m, acc = state
        m_new = jnp.maximum(m, mx)
        return m_new, jnp.exp2(m - m_new) * acc

    def retire_slice(src_ref, hh, m_new, acc, g, j):
        p = jnp.exp2(src_ref[hh, g * blk:(g + 1) * blk, :] - m_new)
        return acc + _dot(vaug_ref[j, hh], p.astype(BF16))

    def fresh():
        return (jnp.full((1, blk), -jnp.inf, F32), jnp.zeros((hd + ONES_ROWS, blk), F32))

    def near_blocks(i):
        return tuple(jnp.where(i - (grp - 1) + g < 0, i + 1, i - (grp - 1) + g) for g in range(grp))

    def finish(i, states):
        for hh in heads:
            _, acc = states[hh]
            o_ref[0, i, head_rows(hh), :] = (acc[0:hd] / acc[hd:hd + 1]).astype(BF16)

    def select(i):
        for hh in heads:
            qp = qT_ref[0, i, pair_lanes(hh), :]
            km_hi, km_lo = km_split[hh]
            gate = _dot(km_hi, qp) + _dot(km_lo, qp)
            g = jnp.where(rowi < i, gate, -jnp.inf)
            sel = rowi == i
            for _ in range(MOBA_TOPK):
                m = jnp.max(g, axis=0, keepdims=True)
                hit = (g == m) & (m > -jnp.inf)
                idx = jnp.min(jnp.where(hit, rowi, MAX_BLOCKS), axis=0, keepdims=True)
                pick = rowi == idx
                sel = sel | pick
                g = jnp.where(pick, -jnp.inf, g)
            near = jnp.where(sel, 0.0, NEG_INF)
            far = jnp.where(rowi > i - grp, NEG_INF, near)
            qh = qp[head_rows(hh % 2), :]
            if hh % 2 == 0:
                qnear_ref[hh] = jnp.concatenate([qh, near.astype(BF16), zpad], axis=0)
                qfar_ref[hh] = jnp.concatenate([qh, far.astype(BF16), zpad], axis=0)
            else:
                qnear_ref[hh] = jnp.concatenate([near.astype(BF16), zpad, qh], axis=0)
                qfar_ref[hh] = jnp.concatenate([far.astype(BF16), zpad, qh], axis=0)

    def park(dst_ref, q_ref, hh, g, j, bias=None):
        rows = pl.ds(pl.multiple_of(j * blk, blk), blk)
        s = _dot(kaug_ref[hh, rows, :], q_ref[hh])
        if bias is not None:
            s = s + bias
        dst_ref[hh, g * blk:(g + 1) * blk, :] = s
        return jnp.max(s, axis=0, keepdims=True)

    def park_near(i):
        idx = near_blocks(i)

        def quarter(g, hh):
            bias = None
            if g == grp - 1:
                bias = bias_ref[hh, 0]
            elif g == grp - 2:
                bias = bias_ref[hh, 1]
            return park(sn_ref, qnear_ref, hh, g, idx[g], bias)
        return quarter

    def park_far(c, dst_ref):
        return lambda g, hh: park(dst_ref, qfar_ref, hh, g, c * grp + g)

    def chunk_blocks(c):
        return tuple(c * grp + g for g in range(grp))

    def step(src_ref, states, maxes, vidx, quarter, park_first):
        begun = [retire_begin(states[hh], maxes[hh]) for hh in heads] if states is not None else None
        accs = [b[1] for b in begun] if begun else None
        new_max = [None] * nh

        def park_quarter(g):
            for hh in heads:
                cm = quarter(g, hh)
                new_max[hh] = cm if new_max[hh] is None else jnp.maximum(new_max[hh], cm)

        for g in range(grp):
            if park_first:
                park_quarter(g)
            if begun:
                for hh in heads:
                    accs[hh] = retire_slice(src_ref, hh, begun[hh][0], accs[hh], g, vidx[g])
            if not park_first and quarter is not None:
                park_quarter(g)
        new_states = tuple((begun[hh][0], accs[hh]) for hh in heads) if begun else None
        return new_states, tuple(new_max)

    def fresh_states():
        return tuple(fresh() for _ in heads)

    def early_tile(i, near_max):
        nxt = jnp.minimum(i + 1, nb - 1)
        select(nxt)
        states, next_max = step(sn_ref, fresh_states(), near_max, near_blocks(i), park_near(nxt), False)
        finish(i, states)
        return next_max

    def late_tile(i, near_max):
        sf = (sf0_ref, sf1_ref)
        n_chunks = (i - (grp - 1) + grp - 1) // grp
        nxt = jnp.minimum(i + 1, nb - 1)
        states, far_max = step(sn_ref, fresh_states(), near_max, near_blocks(i), park_far(0, sf[0]), True)

        def pair(k, st):
            states, far_max = st
            states, far_max = step(sf[0], states, far_max, chunk_blocks(2 * k), park_far(2 * k + 1, sf[1]), True)
            return step(sf[1], states, far_max, chunk_blocks(2 * k + 1), park_far(2 * k + 2, sf[0]), True)

        states, far_max = lax.fori_loop(0, (n_chunks - 1) // 2, pair, (states, far_max))

        def last(src, states, far_max):
            select(nxt)
            states, next_max = step(src, states, far_max, chunk_blocks(n_chunks - 1), park_near(nxt), True)
            finish(i, states)
            return next_max

        def even_count(st):
            states, far_max = step(sf[0], st[0], st[1], chunk_blocks(n_chunks - 2),
                                   park_far(n_chunks - 1, sf[1]), True)
            return last(sf[1], states, far_max)

        def odd_count(st):
            return last(sf[0], st[0], st[1])

        return lax.cond(n_chunks % 2 == 0, even_count, odd_count, (states, far_max))

    n_early = min(grp, nb)
    select(0)
    _, near_max = step(None, None, None, None, park_near(0), True)
    near_max = lax.fori_loop(0, n_early, early_tile, near_max)
    lax.fori_loop(n_early, nb, late_tile, near_max)


def _rel_bucket_table(max_rel):
    n = np.arange(max_rel)
    max_exact = REL_BUCKETS // 2
    nf = np.maximum(n, 1).astype(np.float32)
    large = max_exact + (np.log(nf / np.float32(max_exact)) / np.float32(math.log(REL_MAX_DIST / max_exact))
                         * np.float32(REL_BUCKETS - max_exact)).astype(np.int32)
    large = np.minimum(large, REL_BUCKETS - 1)
    return np.where(n < max_exact, n, large)


def _bias_vectors(rel_bias):
    blk = MOBA_BLOCK
    table = _rel_bucket_table(2 * blk)
    r = np.arange(-blk, blk)
    bt = (rel_bias.T - rel_bias[REL_BUCKETS - 1][:, None]) * LOG2E
    own = jnp.where(jnp.asarray(r >= 0)[None], bt[:, table[np.maximum(r, 0)]], NEG_INF)
    prev = bt[:, table[np.minimum(r + blk, 2 * blk - 1)]]
    return jnp.stack([own, prev], axis=1).astype(F32)


def _moba_attention(qT, k3, vT, km_pad, bias_w, *, batch, seq):
    nb = seq // MOBA_BLOCK
    blk = MOBA_BLOCK
    assert nb % FAR_GROUP == 0 and nb <= MAX_BLOCKS
    kern = functools.partial(_attn_kernel, nb=nb)
    nh = HEADS_PER_STEP
    width = nh * HEAD_DIM
    assert nh % 2 == 0 and N_HEADS % nh == 0

    def big(shape, index_map):
        return pl.BlockSpec(shape, index_map, pipeline_mode=pl.Buffered(1))

    return pl.pallas_call(
        kern,
        grid=(batch, N_HEADS // nh),
        in_specs=[big((1, nb, width, blk), lambda b, p: (b, 0, p, 0)),
                  big((1, seq, width), lambda b, p: (b, 0, p)),
                  big((1, nb, width, blk), lambda b, p: (b, 0, p, 0)),
                  pl.BlockSpec((1, MAX_BLOCKS, width), lambda b, p: (b, 0, p)),
                  pl.BlockSpec((nh, 2, 2 * blk), lambda b, p: (p, 0, 0))],
        out_specs=pl.BlockSpec((1, nb, width, blk), lambda b, p: (b, 0, p, 0)),
        out_shape=jax.ShapeDtypeStruct((batch, nb, ATTN_WIDTH, blk), BF16),
        scratch_shapes=[pltpu.VMEM((nh, seq, LANES), BF16),
                        pltpu.VMEM((nb, nh, HEAD_DIM + ONES_ROWS, blk), BF16),
                        pltpu.VMEM((nh, 2, blk, blk), F32),
                        pltpu.VMEM((nh, LANES, blk), BF16),
                        pltpu.VMEM((nh, LANES, blk), BF16),
                        pltpu.VMEM((nh, FAR_GROUP * blk, blk), F32),
                        pltpu.VMEM((nh, FAR_GROUP * blk, blk), F32),
                        pltpu.VMEM((nh, FAR_GROUP * blk, blk), F32)],
        compiler_params=_cparams("arbitrary", "arbitrary"),
        name="moba_attention",
    )(qT, k3, vT, km_pad, bias_w)


def _post_kernel(x_ref, mod_ref, h_ref, wz_ref, bz_ref, at_ref, u_ref, uh_ref, pw_ref, ps_ref,
                 wba_ref, wbp_ref, wo_ref, g2_ref, wrh_ref, wrl_ref, br_ref,
                 x1_ref, h2r_ref, route_ref, route_t_ref, counts_ref, xz_ref, carry_ref, *, tm, seq):
    d = D_MODEL
    i = pl.program_id(0)
    x = x_ref[...]
    mod = mod_ref[0]
    gate1 = mod[:, 2 * d:3 * d]
    shift2, scale2 = mod[:, 3 * d:4 * d], mod[:, 4 * d:5 * d]

    zg = 0.5 * jnp.tanh(0.5 * (_dot(h_ref[...], wz_ref[...]) + bz_ref[...])) + 0.5

    ya = jnp.concatenate(
        [lax.dot_general(at_ref[0, c], wba_ref[...], (((0,), (0,)), ((), ())),
                         preferred_element_type=F32) for c in range(tm // MOBA_BLOCK)], axis=0)

    first = (i * tm) % seq == 0
    halo = jnp.where(first, 0.0, uh_ref[...])
    ubuf = jnp.concatenate([halo, u_ref[...]], axis=0)
    rows = tm + POOL_HALO
    pos = (lax.broadcasted_iota(jnp.int32, (tm, POOL_GROUP_WIDTH), 0) + i * tm) % seq
    outs = []
    for gi, w in enumerate(POOL_WINDOWS):
        e = ubuf[:, gi * POOL_GROUP_WIDTH:(gi + 1) * POOL_GROUP_WIDTH]
        s = e
        step = 1
        while step < w:
            s = s + pltpu.roll(s, step, 0)
            step *= 2
        cnt = jnp.minimum(pos + 1, w).astype(F32)
        dlt = s[POOL_HALO:rows] / cnt - e[POOL_HALO:rows]
        outs.append(_dot(dlt.astype(BF16), pw_ref[gi]))
    pooled = jnp.concatenate(outs, axis=-1) * ps_ref[...]
    yp = _dot(pooled.astype(BF16), wbp_ref[...])

    merged = zg[:, 0:d] * ya + zg[:, d:2 * d] * yp
    x1 = x + gate1 * _dot(merged.astype(BF16), wo_ref[...])
    x1_ref[...] = x1

    h2 = _rms_mod(x1, g2_ref[...], scale2, shift2)
    h2_hi = h2.astype(BF16)
    h2r_ref[...] = h2_hi.astype(F32)
    h2_lo = (h2 - h2_hi.astype(F32)).astype(BF16)
    z = _dot(h2_hi, wrh_ref[...]) + _dot(h2_lo, wrh_ref[...]) + _dot(h2_hi, wrl_ref[...]) + br_ref[...]

    rr = ROUTER_ROWS
    zt = z.T[0:rr, :]
    row = lax.broadcasted_iota(jnp.int32, (rr, tm), 0)
    zgrp = jnp.where(row < N_GROUPS, zt, -jnp.inf)
    mg = jnp.max(zgrp, axis=0, keepdims=True)
    pg_top = 1.0 / jnp.sum(jnp.exp(zgrp - mg), axis=0, keepdims=True)
    g_idx = jnp.min(jnp.where(zgrp == mg, row, rr), axis=0, keepdims=True)
    e_row = row - ROUTER_LANE0
    in_grp = (e_row >= 0) & (e_row < N_EXPERTS) & ((e_row >> 3) == g_idx)
    ze = jnp.where(in_grp, zt, -jnp.inf)
    m1 = jnp.max(ze, axis=0, keepdims=True)
    i1 = jnp.min(jnp.where(ze == m1, row, rr), axis=0, keepdims=True)
    ze2 = jnp.where(row == i1, -jnp.inf, ze)
    m2 = jnp.max(ze2, axis=0, keepdims=True)
    i2 = jnp.min(jnp.where(ze2 == m2, row, rr), axis=0, keepdims=True)
    e2 = jnp.exp(m2 - m1)
    w1 = pg_top / (1.0 + e2)
    w2 = pg_top * e2 / (1.0 + e2)

    hot = jnp.where((row == i1) | (row == i2), 1.0, 0.0)
    tri = jnp.where(lax.broadcasted_iota(jnp.int32, (tm, tm), 0) <= lax.broadcasted_iota(jnp.int32, (tm, tm), 1),
                    1.0, 0.0).astype(BF16)
    prefix = _dot(hot.astype(BF16), tri)

    @pl.when(i == 0)
    def _():
        carry_ref[...] = jnp.zeros_like(carry_ref)

    carry = carry_ref[...]
    base = carry[:, 0:1] + prefix - 1.0
    rank1 = jnp.sum(jnp.where(row == i1, base, 0.0), axis=0, keepdims=True)
    rank2 = jnp.sum(jnp.where(row == i2, base, 0.0), axis=0, keepdims=True)
    carry_ref[...] = carry + prefix[:, tm - 1:tm]
    counts_ref[...] = carry_ref[...]
    fields = ((i1 - ROUTER_LANE0).astype(F32), (i2 - ROUTER_LANE0).astype(F32), w1, w2, rank1, rank2)
    frow = lax.broadcasted_iota(jnp.int32, (ROUTE_FIELDS, tm), 0)
    route_t = jnp.zeros((ROUTE_FIELDS, tm), F32)
    for k, val in enumerate(fields):
        route_t = jnp.where(frow == k, val, route_t)
    route_t_ref[...] = route_t
    route_ref[...] = jnp.concatenate([route_t, jnp.zeros((LANES - ROUTE_FIELDS, tm), F32)], axis=0).T
    xz_ref[...] = jnp.zeros_like(xz_ref)


def _post_attention(x2, mod3, h1, w_z, b_z, attnT, u, pool_w, pool_scale, w_ba, w_bp, w_out, g2,
                    wr_hi, wr_lo, b_r, *, batch, seq):
    n, d = x2.shape
    tm = TM_POST
    tps = seq // tm
    cb = tm // MOBA_BLOCK
    hb = tm // POOL_HALO
    sorted_rows = _expert_tiles(n) * TE_ROWS
    assert sorted_rows % (n // tm) == 0
    zrows = sorted_rows // (n // tm)
    kern = functools.partial(_post_kernel, tm=tm, seq=seq)
    const2 = lambda i: (0, 0)
    return pl.pallas_call(
        kern,
        grid=(n // tm,),
        in_specs=[pl.BlockSpec((tm, d), lambda i: (i, 0)),
                  pl.BlockSpec((1, 1, N_MOD * d), lambda i: (i // tps, 0, 0)),
                  pl.BlockSpec((tm, d), lambda i: (i, 0)),
                  pl.BlockSpec((d, 2 * d), lambda i: (0, 1)),
                  pl.BlockSpec((1, 2 * d), const2),
                  pl.BlockSpec((1, cb, ATTN_WIDTH, MOBA_BLOCK), lambda i: (i // tps, i % tps, 0, 0)),
                  pl.BlockSpec((tm, POOL_WIDTH), lambda i: (i, 0)),
                  pl.BlockSpec((POOL_HALO, POOL_WIDTH), lambda i: (jnp.maximum(i * hb - 1, 0), 0)),
                  pl.BlockSpec((len(POOL_WINDOWS), POOL_GROUP_WIDTH, POOL_GROUP_WIDTH), lambda i: (0, 0, 0)),
                  pl.BlockSpec((1, POOL_WIDTH), const2),
                  pl.BlockSpec((ATTN_WIDTH, d), const2),
                  pl.BlockSpec((POOL_WIDTH, d), const2),
                  pl.BlockSpec((d, d), const2),
                  pl.BlockSpec((1, d), const2),
                  pl.BlockSpec((d, LANES), const2),
                  pl.BlockSpec((d, LANES), const2),
                  pl.BlockSpec((1, LANES), const2)],
        out_specs=[pl.BlockSpec((tm, d), lambda i: (i, 0)),
                   pl.BlockSpec((tm, d), lambda i: (i, 0)),
                   pl.BlockSpec((tm, LANES), lambda i: (i, 0)),
                   pl.BlockSpec((ROUTE_FIELDS, tm), lambda i: (0, i)),
                   pl.BlockSpec((ROUTER_ROWS, LANES), const2),
                   pl.BlockSpec((zrows, d), lambda i: (i, 0))],
        out_shape=[jax.ShapeDtypeStruct((n, d), F32),
                   jax.ShapeDtypeStruct((n, d), F32),
                   jax.ShapeDtypeStruct((n, LANES), F32),
                   jax.ShapeDtypeStruct((ROUTE_FIELDS, n), F32),
                   jax.ShapeDtypeStruct((ROUTER_ROWS, LANES), F32),
                   jax.ShapeDtypeStruct((zrows * (n // tm), d), F32)],
        scratch_shapes=[pltpu.VMEM((ROUTER_ROWS, LANES), F32)],
        compiler_params=_cparams("arbitrary"),
        name="merge_outproj_router",
    )(x2, mod3, h1, w_z, b_z, attnT, u, u, pool_w, pool_scale, w_ba, w_bp, w_out, g2, wr_hi, wr_lo, b_r)


def _routing_plan(route_t, counts, *, n):
    te = TE_ROWS
    n_tiles = _expert_tiles(n)
    cnt = counts[ROUTER_LANE0:ROUTER_LANE0 + N_EXPERTS, 0].astype(jnp.int32)
    padded = ((cnt + te - 1) // te) * te
    ids = jnp.arange(N_EXPERTS, dtype=jnp.int32)
    ends = jnp.sum(jnp.where(ids[None, :] <= ids[:, None], padded[None, :], 0), axis=-1)
    starts = ends - padded
    pos = []
    for k in range(EXPERT_TOPK):
        expert = route_t[k].astype(jnp.int32)
        rank = route_t[4 + k].astype(jnp.int32)
        pos.append(jnp.sum(jnp.where(expert[:, None] == ids, starts, 0), axis=-1) + rank)
    tiles = jnp.arange(n_tiles, dtype=jnp.int32)
    tile_expert = jnp.sum((ends // te)[None, :] <= tiles[:, None], axis=-1)
    tile_expert = jnp.minimum(tile_expert, N_EXPERTS - 1).astype(jnp.int32)
    n_used = (ends[N_EXPERTS - 1:] // te).astype(jnp.int32)
    return pos, tile_expert, n_used


def _expert_tiles(n):
    return (EXPERT_TOPK * n) // TE_ROWS + N_EXPERTS


def _tile_pos(pos, tm):
    return jnp.concatenate([p.reshape(-1, 1, tm) for p in pos], axis=-1)


def _dispatch_kernel(pos_ref, h_ref, xz_ref, xs_ref, sem, *, tm):
    del xz_ref
    for r in range(tm):
        for k in range(EXPERT_TOPK):
            pltpu.make_async_copy(h_ref.at[pl.ds(r, 1)], xs_ref.at[pl.ds(pos_ref[0, 0, k * tm + r], 1)],
                                  sem).start(priority=(r + k) % 2)
    for k in range(EXPERT_TOPK):
        pltpu.make_async_copy(h_ref, h_ref, sem).wait()


def _dispatch(pos, h2r, xz, *, n):
    tm = TM_DISPATCH
    rows, d = xz.shape
    assert rows == _expert_tiles(n) * TE_ROWS
    pos3 = _tile_pos(pos, tm)
    return pl.pallas_call(
        functools.partial(_dispatch_kernel, tm=tm),
        grid=(n // tm,),
        in_specs=[pl.BlockSpec((1, 1, EXPERT_TOPK * tm), lambda i: (i, 0, 0), memory_space=pltpu.SMEM),
                  pl.BlockSpec((tm, d), lambda i: (i, 0)),
                  pl.BlockSpec(memory_space=pl.ANY)],
        out_specs=pl.BlockSpec(memory_space=pl.ANY),
        out_shape=jax.ShapeDtypeStruct((rows, d), F32),
        scratch_shapes=[pltpu.SemaphoreType.DMA(())],
        input_output_aliases={2: 0},
        compiler_params=_cparams("arbitrary"),
        name="moe_dispatch",
    )(pos3, h2r, xz)


def _experts_kernel(te_ref, nu_ref, xs_ref, wg_ref, wu_ref, wd_ref, y_ref, wg_b, wu_b, wd_b):
    t = pl.program_id(0)

    @pl.when(t < nu_ref[0])
    def _():
        e = te_ref[t]
        e_prev = te_ref[jnp.maximum(t - 1, 0)]

        @pl.when((t == 0) | (e != e_prev))
        def _():
            wg_b[...] = wg_ref[0].astype(BF16)
            wu_b[...] = wu_ref[0].astype(BF16)
            wd_b[...] = wd_ref[0].astype(BF16)

        x = xs_ref[...].astype(BF16)
        g = _dot(x, wg_b[...])
        u = _dot(x, wu_b[...])
        act = (g * jax.nn.sigmoid(g)) * u
        y_ref[...] = _dot(act.astype(BF16), wd_b[...])

    @pl.when(t >= nu_ref[0])
    def _():
        y_ref[...] = jnp.zeros_like(y_ref)


def _experts(tile_expert, n_used, xs, wg, wu, wd):
    rows, d = xs.shape
    te = TE_ROWS

    def used(t, nu):
        return jnp.minimum(t, nu[0] - 1)

    grid_spec = pltpu.PrefetchScalarGridSpec(
        num_scalar_prefetch=2,
        grid=(rows // te,),
        in_specs=[pl.BlockSpec((te, d), lambda t, te_r, nu: (used(t, nu), 0)),
                  pl.BlockSpec((1, d, D_EXPERT), lambda t, te_r, nu: (te_r[used(t, nu)], 0, 0)),
                  pl.BlockSpec((1, d, D_EXPERT), lambda t, te_r, nu: (te_r[used(t, nu)], 0, 0)),
                  pl.BlockSpec((1, D_EXPERT, d), lambda t, te_r, nu: (te_r[used(t, nu)], 0, 0))],
        out_specs=pl.BlockSpec((te, d), lambda t, te_r, nu: (t, 0)),
        scratch_shapes=[pltpu.VMEM((d, D_EXPERT), BF16), pltpu.VMEM((d, D_EXPERT), BF16),
                        pltpu.VMEM((D_EXPERT, d), BF16)])
    return pl.pallas_call(
        _experts_kernel,
        grid_spec=grid_spec,
        out_shape=jax.ShapeDtypeStruct((rows, d), F32),
        compiler_params=_cparams("arbitrary"),
        name="moe_experts",
    )(tile_expert, n_used, xs, wg, wu, wd)


def _final_kernel(pos_ref, pos_next_ref, x1_ref, route_ref, mod_ref, g_ref, y_ref, o_ref, ybuf, sems, *,
                  tm, final_norm):
    d = D_MODEL
    i = pl.program_id(0)
    slot = i % 2

    def gather(p_ref, s):
        for r in range(tm):
            for k in range(EXPERT_TOPK):
                pltpu.make_async_copy(y_ref.at[pl.ds(p_ref[0, 0, k * tm + r], 1)],
                                      ybuf.at[s, k, pl.ds(r, 1)], sems.at[s]).start(priority=(r + k) % 2)

    @pl.when(i == 0)
    def _():
        gather(pos_ref, 0)

    @pl.when(i + 1 < pl.num_programs(0))
    def _():
        gather(pos_next_ref, 1 - slot)

    for k in range(EXPERT_TOPK):
        pltpu.make_async_copy(ybuf.at[slot, k], ybuf.at[slot, k], sems.at[slot]).wait()

    route = route_ref[...]
    lane = lax.broadcasted_iota(jnp.int32, route.shape, 1)
    y = jnp.zeros((tm, d), F32)
    for k in range(EXPERT_TOPK):
        wk = jnp.sum(jnp.where(lane == 2 + k, route, 0.0), axis=-1, keepdims=True)
        y = y + wk * ybuf[slot, k]
    gate2 = mod_ref[0][:, 5 * d:6 * d]
    x = x1_ref[...] + gate2 * y
    if final_norm:
        x = (x * lax.rsqrt(jnp.mean(x * x, axis=-1, keepdims=True) + EPS)) * g_ref[...]
    o_ref[...] = x


def _final(pos, x1, route, mod3, gf, y, *, seq, final_norm):
    n, d = x1.shape
    tm = TM_FINAL
    tps = seq // tm
    pos3 = _tile_pos(pos, tm)
    last = n // tm - 1
    return pl.pallas_call(
        functools.partial(_final_kernel, tm=tm, final_norm=final_norm),
        grid=(n // tm,),
        in_specs=[pl.BlockSpec((1, 1, EXPERT_TOPK * tm), lambda i: (i, 0, 0), memory_space=pltpu.SMEM),
                  pl.BlockSpec((1, 1, EXPERT_TOPK * tm), lambda i: (jnp.minimum(i + 1, last), 0, 0),
                               memory_space=pltpu.SMEM),
                  pl.BlockSpec((tm, d), lambda i: (i, 0)),
                  pl.BlockSpec((tm, LANES), lambda i: (i, 0)),
                  pl.BlockSpec((1, 1, N_MOD * d), lambda i: (i // tps, 0, 0)),
                  pl.BlockSpec((1, d), lambda i: (0, 0)),
                  pl.BlockSpec(memory_space=pl.ANY)],
        out_specs=pl.BlockSpec((tm, d), lambda i: (i, 0)),
        out_shape=jax.ShapeDtypeStruct((n, d), F32),
        scratch_shapes=[pltpu.VMEM((2, EXPERT_TOPK, tm, d), F32), pltpu.SemaphoreType.DMA((2,))],
        compiler_params=_cparams("arbitrary"),
        name="moe_combine_final_norm",
    )(pos3, pos3, x1, route, mod3, gf, y)


def _split_bf16(w):
    hi = w.astype(BF16)
    return hi, (w - hi.astype(F32)).astype(BF16)


def kernel(x, c, w_ada, b_ada, norm1_g, w_in, b_gate, rel_bias, pool_w, pool_scale, w_branch_attn,
           w_branch_pool, w_out, norm2_g, w_router_group, b_router_group, w_router_expert,
           b_router_expert, w_expert_gate, w_expert_up, w_expert_down, norm_f_g):
    batch, seq, d = x.shape
    depth = w_ada.shape[0]
    n = batch * seq
    nb = seq // MOBA_BLOCK
    qkvu = 3 * ATTN_WIDTH + POOL_WIDTH
    bias_w = _bias_vectors(rel_bias)
    c_pad = jnp.pad(c, ((0, 8 - batch), (0, 0)))

    xc = x.reshape(n, d)
    for l in range(depth):
        mod3 = _modulation(c_pad, w_ada[l], b_ada[l][None])[:batch].reshape(batch, 1, N_MOD * d)
        w_in_b = w_in[l].astype(BF16)
        g1 = norm1_g[l][None]

        assert w_in_b.shape[1] == 2 * qkvu == qkvu + 2 * d
        qT, k2, vT, u, km, h1 = _in_projection(xc, mod3, g1, w_in_b, batch=batch, seq=seq)
        km_pad = jnp.pad(km.reshape(batch, nb, ATTN_WIDTH), ((0, 0), (0, MAX_BLOCKS - nb), (0, 0)))
        attnT = _moba_attention(qT, k2.reshape(batch, seq, ATTN_WIDTH), vT, km_pad, bias_w,
                                batch=batch, seq=seq)

        w_r = jnp.concatenate([w_router_group[l], w_router_expert[l]], axis=1)
        w_r = jnp.pad(w_r, ((0, 0), (0, LANES - w_r.shape[1])))
        b_r = jnp.concatenate([b_router_group[l], b_router_expert[l]])
        b_r = jnp.pad(b_r, (0, LANES - b_r.shape[0]))[None]
        wr_hi, wr_lo = _split_bf16(w_r)
        x1, h2r, route, route_t, counts, xz = _post_attention(
            xc, mod3, h1, w_in_b, b_gate[l][None], attnT, u, pool_w[l].astype(BF16),
            pool_scale[l][None], w_branch_attn[l].astype(BF16), w_branch_pool[l].astype(BF16),
            w_out[l].astype(BF16), norm2_g[l][None], wr_hi, wr_lo, b_r, batch=batch, seq=seq)

        pos, tile_expert, n_used = _routing_plan(route_t, counts, n=n)
        xs = _dispatch(pos, h2r, xz, n=n)
        y = _experts(tile_expert, n_used, xs, w_expert_gate[l], w_expert_up[l], w_expert_down[l])
        xc = _final(pos, x1, route, mod3, norm_f_g[None], y, seq=seq, final_norm=(l == depth - 1))
    return xc.reshape(batch, seq, d)
```

```python
import functools
import math

import numpy as np
import jax
import jax.numpy as jnp
from jax import lax
from jax.experimental import pallas as pl
from jax.experimental.pallas import tpu as pltpu

F32 = jnp.float32
BF16 = jnp.bfloat16

D_MODEL = 1024
N_HEADS = 8
HEAD_DIM = 64
ATTN_WIDTH = N_HEADS * HEAD_DIM
MOBA_BLOCK = 256
MOBA_TOPK = 3
MAX_BLOCKS = 32
FAR_GROUP = 4
HEADS_PER_STEP = 4
ONES_ROWS = 16
LOG2E = math.log2(math.e)
POOL_WINDOWS = (2, 4, 8, 16)
POOL_WIDTH = 512
POOL_GROUP_WIDTH = 128
POOL_HALO = 16
REL_BUCKETS = 32
REL_MAX_DIST = 128
N_GROUPS = 4
EXPERTS_PER_GROUP = 8
N_EXPERTS = N_GROUPS * EXPERTS_PER_GROUP
D_EXPERT = 512
N_MOD = 6
EPS = 1e-6
NEG_INF = -1e30
LANES = 128
ROUTER_LANE0 = N_GROUPS
ROUTE_FIELDS = 8
ROUTER_ROWS = 48
VMEM_LIMIT = 56 * 1024 * 1024

TM_PROJ = 512
TM_POST = 512
ZERO_COPIES = 4
POST_PART = 256
TM_DISPATCH = 1024
TE_ROWS = 512
EXPERT_TOPK = 2
TM_FINAL = 256


def _cparams(*sem):
    return pltpu.CompilerParams(dimension_semantics=sem, vmem_limit_bytes=VMEM_LIMIT)


def _dot(a, b):
    return jnp.dot(a, b, preferred_element_type=F32)


def _rms_mod(x, g, scale, shift):
    xn = x * lax.rsqrt(jnp.mean(x * x, axis=-1, keepdims=True) + EPS)
    return (xn * g) * (1.0 + scale) + shift


def _mod_kernel(c_ref, w_ref, b_ref, o_ref):
    c = c_ref[...]
    ca = c * jax.nn.sigmoid(c)
    o_ref[...] = _dot(ca.astype(BF16), w_ref[...].astype(BF16)) + b_ref[...]


def _modulation(c_pad, w_ada, b_ada):
    rows, d = c_pad.shape
    n_out = w_ada.shape[1]
    tn = 1024
    return pl.pallas_call(
        _mod_kernel,
        grid=(n_out // tn,),
        in_specs=[pl.BlockSpec((rows, d), lambda j: (0, 0)),
                  pl.BlockSpec((d, tn), lambda j: (0, j)),
                  pl.BlockSpec((1, tn), lambda j: (0, j))],
        out_specs=pl.BlockSpec((rows, tn), lambda j: (0, j)),
        out_shape=jax.ShapeDtypeStruct((rows, n_out), F32),
        compiler_params=_cparams("arbitrary"),
        name="adaln_mod",
    )(c_pad, w_ada, b_ada)


def _inproj_kernel(x_ref, mod_ref, g_ref, w_ref, qT_ref, k_ref, vT_ref, u_ref, km_ref, h_ref, *, tm):
    d = D_MODEL
    mod = mod_ref[0]
    h = _rms_mod(x_ref[...], g_ref[...], mod[:, d:2 * d], mod[:, 0:d]).astype(BF16)
    h_ref[...] = h
    proj = _dot(h, w_ref[...])
    aw = ATTN_WIDTH
    q = proj[:, 0:aw] * (HEAD_DIM ** -0.5 * LOG2E)
    k = proj[:, aw:2 * aw]
    v = proj[:, 2 * aw:3 * aw]
    u_ref[...] = proj[:, 3 * aw:3 * aw + POOL_WIDTH]
    k_ref[...] = k.astype(BF16)
    qT = q.T.astype(BF16)
    vT = v.T.astype(BF16)
    for c in range(tm // MOBA_BLOCK):
        sl = slice(c * MOBA_BLOCK, (c + 1) * MOBA_BLOCK)
        qT_ref[0, c] = qT[:, sl]
        vT_ref[0, c] = vT[:, sl]
        km_ref[0, c:c + 1, :] = jnp.mean(k[sl, :], axis=0, keepdims=True)


def _in_projection(x2, mod3, g1, w_qkvu, *, batch, seq):
    n, d = x2.shape
    tm = TM_PROJ
    tps = seq // tm
    nb = seq // MOBA_BLOCK
    cb = tm // MOBA_BLOCK
    wcols = 3 * ATTN_WIDTH + POOL_WIDTH
    assert w_qkvu.shape[1] % wcols == 0
    kern = functools.partial(_inproj_kernel, tm=tm)
    t_shape = jax.ShapeDtypeStruct((batch, nb, ATTN_WIDTH, MOBA_BLOCK), BF16)
    t_spec = pl.BlockSpec((1, cb, ATTN_WIDTH, MOBA_BLOCK), lambda i: (i // tps, i % tps, 0, 0))
    return pl.pallas_call(
        kern,
        grid=(n // tm,),
        in_specs=[pl.BlockSpec((tm, d), lambda i: (i, 0)),
                  pl.BlockSpec((1, 1, N_MOD * d), lambda i: (i // tps, 0, 0)),
                  pl.BlockSpec((1, d), lambda i: (0, 0)),
                  pl.BlockSpec((d, wcols), lambda i: (0, 0))],
        out_specs=[t_spec,
                   pl.BlockSpec((tm, ATTN_WIDTH), lambda i: (i, 0)),
                   t_spec,
                   pl.BlockSpec((tm, POOL_WIDTH), lambda i: (i, 0)),
                   pl.BlockSpec((1, cb, ATTN_WIDTH), lambda i: (i, 0, 0)),
                   pl.BlockSpec((tm, d), lambda i: (i, 0))],
        out_shape=[t_shape,
                   jax.ShapeDtypeStruct((n, ATTN_WIDTH), BF16),
                   t_shape,
                   jax.ShapeDtypeStruct((n, POOL_WIDTH), F32),
                   jax.ShapeDtypeStruct((n // tm, cb, ATTN_WIDTH), F32),
                   jax.ShapeDtypeStruct((n, d), BF16)],
        compiler_params=_cparams("arbitrary"),
        name="norm1_inproj",
    )(x2, mod3, g1, w_qkvu)


def _attn_kernel(qT_ref, k_ref, vT_ref, km_ref, bw_ref, o_ref, kaug_ref, vaug_ref, bias_ref, qnear_ref,
                 qfar_ref, sn_ref, sf0_ref, sf1_ref, *, nb):
    blk = MOBA_BLOCK
    hd = HEAD_DIM
    nh = HEADS_PER_STEP
    heads = range(nh)
    lane = lax.broadcasted_iota(jnp.int32, (blk, LANES), 1)

    def pair_lanes(hh):
        return slice((hh // 2) * LANES, (hh // 2 + 1) * LANES)

    def head_rows(hh):
        return slice(hh * hd, (hh + 1) * hd)

    ones = jnp.ones((ONES_ROWS, blk), BF16)

    def build_operands(j, carry):
        rows = pl.ds(pl.multiple_of(j * blk, blk), blk)
        for hh in heads:
            kp = k_ref[0, rows, pair_lanes(hh)]
            if hh % 2 == 0:
                kaug_ref[hh, rows, :] = jnp.where(lane < hd, kp, jnp.where(lane == hd + j, 1.0, 0.0).astype(BF16))
            else:
                kaug_ref[hh, rows, :] = jnp.where(lane >= hd, kp, jnp.where(lane == j, 1.0, 0.0).astype(BF16))
            vaug_ref[j, hh] = jnp.concatenate([vT_ref[0, j, head_rows(hh), :], ones], axis=0)
        return carry

    lax.fori_loop(0, nb, build_operands, 0)

    for hh in heads:
        for t in range(2):
            wide = jnp.broadcast_to(bw_ref[hh, t:t + 1, :], (blk, 2 * blk))
            bias_ref[hh, t] = pltpu.roll(wide, 0, 1, stride=1, stride_axis=0)[:, blk:2 * blk]

    rowi = lax.broadcasted_iota(jnp.int32, (MAX_BLOCKS, blk), 0)
    km_lane = lax.broadcasted_iota(jnp.int32, (MAX_BLOCKS, LANES), 1)
    zpad = jnp.zeros((LANES - hd - MAX_BLOCKS, blk), BF16)
    km_split = []
    for hh in heads:
        sub = hh % 2
        in_head = (km_lane >= sub * hd) & (km_lane < (sub + 1) * hd)
        km_h = jnp.where(in_head, km_ref[0, :, pair_lanes(hh)], 0.0)
        km_hi = km_h.astype(BF16)
        km_split.append((km_hi, (km_h - km_hi.astype(F32)).astype(BF16)))

    grp = FAR_GROUP

    def retire_begin(state, mx):
        m, acc = state
        m_new = jnp.maximum(m, mx)
        return m_new, jnp.exp2(m - m_new) * acc

    def retire_slice(src_ref, hh, m_new, acc, g, j):
        p = jnp.exp2(src_ref[hh, g * blk:(g + 1) * blk, :] - m_new)
        return acc + _dot(vaug_ref[j, hh], p.astype(BF16))

    def fresh():
        return (jnp.full((1, blk), -jnp.inf, F32), jnp.zeros((hd + ONES_ROWS, blk), F32))

    def near_blocks(i):
        return tuple(jnp.where(i - (grp - 1) + g < 0, i + 1, i - (grp - 1) + g) for g in range(grp))

    def finish(i, states):
        for hh in heads:
            _, acc = states[hh]
            o_ref[0, i, head_rows(hh), :] = (acc[0:hd] / acc[hd:hd + 1]).astype(BF16)

    def select(i):
        for hh in heads:
            qp = qT_ref[0, i, pair_lanes(hh), :]
            km_hi, km_lo = km_split[hh]
            gate = _dot(km_hi, qp) + _dot(km_lo, qp)
            g = jnp.where(rowi < i, gate, -jnp.inf)
            sel = rowi == i
            for _ in range(MOBA_TOPK):
                m = jnp.max(g, axis=0, keepdims=True)
                hit = (g == m) & (m > -jnp.inf)
                idx = jnp.min(jnp.where(hit, rowi, MAX_BLOCKS), axis=0, keepdims=True)
                pick = rowi == idx
                sel = sel | pick
                g = jnp.where(pick, -jnp.inf, g)
            near = jnp.where(sel, 0.0, NEG_INF)
            far = jnp.where(rowi > i - grp, NEG_INF, near)
            qh = qp[head_rows(hh % 2), :]
            if hh % 2 == 0:
                qnear_ref[hh] = jnp.concatenate([qh, near.astype(BF16), zpad], axis=0)
                qfar_ref[hh] = jnp.concatenate([qh, far.astype(BF16), zpad], axis=0)
            else:
                qnear_ref[hh] = jnp.concatenate([near.astype(BF16), zpad, qh], axis=0)
                qfar_ref[hh] = jnp.concatenate([far.astype(BF16), zpad, qh], axis=0)

    def park(dst_ref, q_ref, hh, g, j, bias=None):
        rows = pl.ds(pl.multiple_of(j * blk, blk), blk)
        s = _dot(kaug_ref[hh, rows, :], q_ref[hh])
        if bias is not None:
            s = s + bias
        dst_ref[hh, g * blk:(g + 1) * blk, :] = s
        return jnp.max(s, axis=0, keepdims=True)

    def park_near(i):
        idx = near_blocks(i)

        def quarter(g, hh):
            bias = None
            if g == grp - 1:
                bias = bias_ref[hh, 0]
            elif g == grp - 2:
                bias = bias_ref[hh, 1]
            return park(sn_ref, qnear_ref, hh, g, idx[g], bias)
        return quarter

    def park_far(c, dst_ref):
        return lambda g, hh: park(dst_ref, qfar_ref, hh, g, c * grp + g)

    def chunk_blocks(c):
        return tuple(c * grp + g for g in range(grp))

    def step(src_ref, states, maxes, vidx, quarter, park_first):
        begun = [retire_begin(states[hh], maxes[hh]) for hh in heads] if states is not None else None
        accs = [b[1] for b in begun] if begun else None
        new_max = [None] * nh

        def park_quarter(g):
            for hh in heads:
                cm = quarter(g, hh)
                new_max[hh] = cm if new_max[hh] is None else jnp.maximum(new_max[hh], cm)

        for g in range(grp):
            if park_first:
                park_quarter(g)
            if begun:
                for hh in heads:
                    accs[hh] = retire_slice(src_ref, hh, begun[hh][0], accs[hh], g, vidx[g])
            if not park_first and quarter is not None:
                park_quarter(g)
        new_states = tuple((begun[hh][0], accs[hh]) for hh in heads) if begun else None
        return new_states, tuple(new_max)

    def fresh_states():
        return tuple(fresh() for _ in heads)

    def early_tile(i, near_max):
        nxt = jnp.minimum(i + 1, nb - 1)
        select(nxt)
        states, next_max = step(sn_ref, fresh_states(), near_max, near_blocks(i), park_near(nxt), False)
        finish(i, states)
        return next_max

    def late_tile(i, near_max):
        sf = (sf0_ref, sf1_ref)
        n_chunks = (i - (grp - 1) + grp - 1) // grp
        nxt = jnp.minimum(i + 1, nb - 1)
        states, far_max = step(sn_ref, fresh_states(), near_max, near_blocks(i), park_far(0, sf[0]), True)

        def pair(k, st):
            states, far_max = st
            states, far_max = step(sf[0], states, far_max, chunk_blocks(2 * k), park_far(2 * k + 1, sf[1]), True)
            return step(sf[1], states, far_max, chunk_blocks(2 * k + 1), park_far(2 * k + 2, sf[0]), True)

        states, far_max = lax.fori_loop(0, (n_chunks - 1) // 2, pair, (states, far_max))

        def last(src, states, far_max):
            select(nxt)
            states, next_max = step(src, states, far_max, chunk_blocks(n_chunks - 1), park_near(nxt), True)
            finish(i, states)
            return next_max

        def even_count(st):
            states, far_max = step(sf[0], st[0], st[1], chunk_blocks(n_chunks - 2),
                                   park_far(n_chunks - 1, sf[1]), True)
            return last(sf[1], states, far_max)

        def odd_count(st):
            return last(sf[0], st[0], st[1])

        return lax.cond(n_chunks % 2 == 0, even_count, odd_count, (states, far_max))

    n_early = min(grp, nb)
    select(0)
    _, near_max = step(None, None, None, None, park_near(0), True)
    near_max = lax.fori_loop(0, n_early, early_tile, near_max)
    lax.fori_loop(n_early, nb, late_tile, near_max)


def _rel_bucket_table(max_rel):
    n = np.arange(max_rel)
    max_exact = REL_BUCKETS // 2
    nf = np.maximum(n, 1).astype(np.float32)
    large = max_exact + (np.log(nf / np.float32(max_exact)) / np.float32(math.log(REL_MAX_DIST / max_exact))
                         * np.float32(REL_BUCKETS - max_exact)).astype(np.int32)
    large = np.minimum(large, REL_BUCKETS - 1)
    return np.where(n < max_exact, n, large)


def _bias_vectors(rel_bias):
    blk = MOBA_BLOCK
    table = _rel_bucket_table(2 * blk)
    r = np.arange(-blk, blk)
    bt = (rel_bias.T - rel_bias[REL_BUCKETS - 1][:, None]) * LOG2E
    own = jnp.where(jnp.asarray(r >= 0)[None], bt[:, table[np.maximum(r, 0)]], NEG_INF)
    prev = bt[:, table[np.minimum(r + blk, 2 * blk - 1)]]
    return jnp.stack([own, prev], axis=1).astype(F32)


def _moba_attention(qT, k3, vT, km_pad, bias_w, *, batch, seq):
    nb = seq // MOBA_BLOCK
    blk = MOBA_BLOCK
    assert nb % FAR_GROUP == 0 and nb <= MAX_BLOCKS
    kern = functools.partial(_attn_kernel, nb=nb)
    nh = HEADS_PER_STEP
    width = nh * HEAD_DIM
    assert nh % 2 == 0 and N_HEADS % nh == 0

    def big(shape, index_map):
        return pl.BlockSpec(shape, index_map, pipeline_mode=pl.Buffered(1))

    return pl.pallas_call(
        kern,
        grid=(batch, N_HEADS // nh),
        in_specs=[big((1, nb, width, blk), lambda b, p: (b, 0, p, 0)),
                  big((1, seq, width), lambda b, p: (b, 0, p)),
                  big((1, nb, width, blk), lambda b, p: (b, 0, p, 0)),
                  pl.BlockSpec((1, MAX_BLOCKS, width), lambda b, p: (b, 0, p)),
                  pl.BlockSpec((nh, 2, 2 * blk), lambda b, p: (p, 0, 0))],
        out_specs=pl.BlockSpec((1, nb, width, blk), lambda b, p: (b, 0, p, 0)),
        out_shape=jax.ShapeDtypeStruct((batch, nb, ATTN_WIDTH, blk), BF16),
        scratch_shapes=[pltpu.VMEM((nh, seq, LANES), BF16),
                        pltpu.VMEM((nb, nh, HEAD_DIM + ONES_ROWS, blk), BF16),
                        pltpu.VMEM((nh, 2, blk, blk), F32),
                        pltpu.VMEM((nh, LANES, blk), BF16),
                        pltpu.VMEM((nh, LANES, blk), BF16),
                        pltpu.VMEM((nh, FAR_GROUP * blk, blk), F32),
                        pltpu.VMEM((nh, FAR_GROUP * blk, blk), F32),
                        pltpu.VMEM((nh, FAR_GROUP * blk, blk), F32)],
        compiler_params=_cparams("arbitrary", "arbitrary"),
        name="moba_attention",
    )(qT, k3, vT, km_pad, bias_w)


def _post_kernel(x_ref, mod_ref, h_ref, wz_ref, bz_ref, at_ref, u_ref, uh_ref, pw_ref, ps_ref,
                 wba_ref, wbp_ref, wo_ref, g2_ref, wrh_ref, wrl_ref, br_ref,
                 x1_ref, h2r_ref, route_ref, route_t_ref, counts_ref, xz_ref, carry_ref, zero_ref, zsem, *,
                 tm, seq, zrows):
    d = D_MODEL
    i = pl.program_id(0)

    @pl.when(i == 0)
    def _():
        zero_ref[...] = jnp.zeros_like(zero_ref)

    zpiece = zero_ref.shape[0]
    zero_copies = [pltpu.make_async_copy(zero_ref, xz_ref.at[pl.ds(i * zrows + q * zpiece, zpiece)], zsem)
                   for q in range(zrows // zpiece)]
    for cp in zero_copies:
        cp.start()

    x = x_ref[...]
    mod = mod_ref[0]
    gate1 = mod[:, 2 * d:3 * d]
    shift2, scale2 = mod[:, 3 * d:4 * d], mod[:, 4 * d:5 * d]

    part = POST_PART
    parts = range(tm // part)
    first = (i * tm) % seq == 0
    halo = jnp.where(first, 0.0, uh_ref[...])
    ubuf = jnp.concatenate([halo, u_ref[...]], axis=0)

    def rows_of(c):
        return slice(c * part, (c + 1) * part)

    def gates(t):
        return 0.5 * jnp.tanh(0.5 * t) + 0.5

    def branches(c):
        blk_i, off = (c * part) // MOBA_BLOCK, (c * part) % MOBA_BLOCK
        ya = lax.dot_general(at_ref[0, blk_i, :, off:off + part], wba_ref[...], (((0,), (0,)), ((), ())),
                             preferred_element_type=F32)
        nrow = part + POOL_HALO
        ub = ubuf[c * part:c * part + nrow, :]
        pos = (lax.broadcasted_iota(jnp.int32, (part, POOL_GROUP_WIDTH), 0) + i * tm + c * part) % seq
        outs = []
        for gi, w in enumerate(POOL_WINDOWS):
            e = ub[:, gi * POOL_GROUP_WIDTH:(gi + 1) * POOL_GROUP_WIDTH]
            s = e
            step = 1
            while step < w:
                s = s + pltpu.roll(s, step, 0)
                step *= 2
            cnt = jnp.minimum(pos + 1, w).astype(F32)
            dlt = s[POOL_HALO:nrow] / cnt - e[POOL_HALO:nrow]
            outs.append(_dot(dlt.astype(BF16), pw_ref[gi]))
        pooled = jnp.concatenate(outs, axis=-1) * ps_ref[...]
        return ya, _dot(pooled.astype(BF16), wbp_ref[...])

    def residual(c, zg, ya, yp):
        merged = zg[:, 0:d] * ya + zg[:, d:2 * d] * yp
        x1 = x[rows_of(c), :] + gate1 * _dot(merged.astype(BF16), wo_ref[...])
        x1_ref[rows_of(c), :] = x1
        return x1

    def norm2(c, x1):
        h2 = _rms_mod(x1, g2_ref[...], scale2, shift2)
        h2_hi = h2.astype(BF16)
        h2r_ref[rows_of(c), :] = h2_hi.astype(F32)
        return h2_hi, (h2 - h2_hi.astype(F32)).astype(BF16)

    t = _dot(h_ref[...], wz_ref[...]) + bz_ref[...]
    zg, ybr, x1p, h2p = {}, {}, {}, {}
    for c in range(len(parts) + 2):
        if c in parts:
            zg[c] = gates(t[rows_of(c), :])
            ybr[c] = branches(c)
        if c - 1 in parts:
            x1p[c - 1] = residual(c - 1, zg[c - 1], *ybr[c - 1])
        if c - 2 in parts:
            h2p[c - 2] = norm2(c - 2, x1p[c - 2])
    h2_hi = jnp.concatenate([h2p[c][0] for c in parts], axis=0)
    h2_lo = jnp.concatenate([h2p[c][1] for c in parts], axis=0)
    z = _dot(h2_hi, wrh_ref[...]) + _dot(h2_lo, wrh_ref[...]) + _dot(h2_hi, wrl_ref[...]) + br_ref[...]

    rr = ROUTER_ROWS
    zt = z.T[0:rr, :]
    row = lax.broadcasted_iota(jnp.int32, (rr, tm), 0)
    zgrp = jnp.where(row < N_GROUPS, zt, -jnp.inf)
    mg = jnp.max(zgrp, axis=0, keepdims=True)
    pg_top = 1.0 / jnp.sum(jnp.exp(zgrp - mg), axis=0, keepdims=True)
    g_idx = jnp.min(jnp.where(zgrp == mg, row, rr), axis=0, keepdims=True)
    e_row = row - ROUTER_LANE0
    in_grp = (e_row >= 0) & (e_row < N_EXPERTS) & ((e_row >> 3) == g_idx)
    ze = jnp.where(in_grp, zt, -jnp.inf)
    m1 = jnp.max(ze, axis=0, keepdims=True)
    i1 = jnp.min(jnp.where(ze == m1, row, rr), axis=0, keepdims=True)
    ze2 = jnp.where(row == i1, -jnp.inf, ze)
    m2 = jnp.max(ze2, axis=0, keepdims=True)
    i2 = jnp.min(jnp.where(ze2 == m2, row, rr), axis=0, keepdims=True)
    e2 = jnp.exp(m2 - m1)
    w1 = pg_top / (1.0 + e2)
    w2 = pg_top * e2 / (1.0 + e2)

    hot = jnp.where((row == i1) | (row == i2), 1.0, 0.0)
    tri = jnp.where(lax.broadcasted_iota(jnp.int32, (tm, tm), 0) <= lax.broadcasted_iota(jnp.int32, (tm, tm), 1),
                    1.0, 0.0).astype(BF16)
    prefix = _dot(hot.astype(BF16), tri)

    @pl.when(i == 0)
    def _():
        carry_ref[...] = jnp.zeros_like(carry_ref)

    carry = carry_ref[...]
    base = carry[:, 0:1] + prefix - 1.0
    rank1 = jnp.sum(jnp.where(row == i1, base, 0.0), axis=0, keepdims=True)
    rank2 = jnp.sum(jnp.where(row == i2, base, 0.0), axis=0, keepdims=True)
    carry_ref[...] = carry + prefix[:, tm - 1:tm]
    counts_ref[...] = carry_ref[...]
    fields = ((i1 - ROUTER_LANE0).astype(F32), (i2 - ROUTER_LANE0).astype(F32), w1, w2, rank1, rank2)
    frow = lax.broadcasted_iota(jnp.int32, (ROUTE_FIELDS, tm), 0)
    route_t = jnp.zeros((ROUTE_FIELDS, tm), F32)
    for k, val in enumerate(fields):
        route_t = jnp.where(frow == k, val, route_t)
    route_t_ref[...] = route_t
    route_ref[...] = jnp.concatenate([route_t, jnp.zeros((LANES - ROUTE_FIELDS, tm), F32)], axis=0).T
    for cp in zero_copies:
        cp.wait()


def _post_attention(x2, mod3, h1, w_z, b_z, attnT, u, pool_w, pool_scale, w_ba, w_bp, w_out, g2,
                    wr_hi, wr_lo, b_r, *, batch, seq):
    n, d = x2.shape
    tm = TM_POST
    tps = seq // tm
    cb = tm // MOBA_BLOCK
    hb = tm // POOL_HALO
    sorted_rows = _expert_tiles(n) * TE_ROWS
    assert sorted_rows % (n // tm) == 0
    zrows = sorted_rows // (n // tm)
    assert zrows % (8 * ZERO_COPIES) == 0
    kern = functools.partial(_post_kernel, tm=tm, seq=seq, zrows=zrows)
    const2 = lambda i: (0, 0)
    return pl.pallas_call(
        kern,
        grid=(n // tm,),
        in_specs=[pl.BlockSpec((tm, d), lambda i: (i, 0)),
                  pl.BlockSpec((1, 1, N_MOD * d), lambda i: (i // tps, 0, 0)),
                  pl.BlockSpec((tm, d), lambda i: (i, 0)),
                  pl.BlockSpec((d, 2 * d), lambda i: (0, 1)),
                  pl.BlockSpec((1, 2 * d), const2),
                  pl.BlockSpec((1, cb, ATTN_WIDTH, MOBA_BLOCK), lambda i: (i // tps, i % tps, 0, 0)),
                  pl.BlockSpec((tm, POOL_WIDTH), lambda i: (i, 0)),
                  pl.BlockSpec((POOL_HALO, POOL_WIDTH), lambda i: (jnp.maximum(i * hb - 1, 0), 0)),
                  pl.BlockSpec((len(POOL_WINDOWS), POOL_GROUP_WIDTH, POOL_GROUP_WIDTH), lambda i: (0, 0, 0)),
                  pl.BlockSpec((1, POOL_WIDTH), const2),
                  pl.BlockSpec((ATTN_WIDTH, d), const2),
                  pl.BlockSpec((POOL_WIDTH, d), const2),
                  pl.BlockSpec((d, d), const2),
                  pl.BlockSpec((1, d), const2),
                  pl.BlockSpec((d, LANES), const2),
                  pl.BlockSpec((d, LANES), const2),
                  pl.BlockSpec((1, LANES), const2)],
        out_specs=[pl.BlockSpec((tm, d), lambda i: (i, 0)),
                   pl.BlockSpec((tm, d), lambda i: (i, 0)),
                   pl.BlockSpec((tm, LANES), lambda i: (i, 0)),
                   pl.BlockSpec((ROUTE_FIELDS, tm), lambda i: (0, i)),
                   pl.BlockSpec((ROUTER_ROWS, LANES), const2),
                   pl.BlockSpec(memory_space=pl.ANY)],
        out_shape=[jax.ShapeDtypeStruct((n, d), F32),
                   jax.ShapeDtypeStruct((n, d), F32),
                   jax.ShapeDtypeStruct((n, LANES), F32),
                   jax.ShapeDtypeStruct((ROUTE_FIELDS, n), F32),
                   jax.ShapeDtypeStruct((ROUTER_ROWS, LANES), F32),
                   jax.ShapeDtypeStruct((zrows * (n // tm), d), F32)],
        scratch_shapes=[pltpu.VMEM((ROUTER_ROWS, LANES), F32),
                        pltpu.VMEM((zrows // ZERO_COPIES, d), F32), pltpu.SemaphoreType.DMA(())],
        compiler_params=_cparams("arbitrary"),
        name="merge_outproj_router",
    )(x2, mod3, h1, w_z, b_z, attnT, u, u, pool_w, pool_scale, w_ba, w_bp, w_out, g2, wr_hi, wr_lo, b_r)


def _routing_plan(route_t, counts, *, n):
    te = TE_ROWS
    n_tiles = _expert_tiles(n)
    cnt = counts[ROUTER_LANE0:ROUTER_LANE0 + N_EXPERTS, 0].astype(jnp.int32)
    padded = ((cnt + te - 1) // te) * te
    ids = jnp.arange(N_EXPERTS, dtype=jnp.int32)
    ends = jnp.sum(jnp.where(ids[None, :] <= ids[:, None], padded[None, :], 0), axis=-1)
    starts = ends - padded
    pos = []
    for k in range(EXPERT_TOPK):
        expert = route_t[k].astype(jnp.int32)
        rank = route_t[4 + k].astype(jnp.int32)
        pos.append(jnp.sum(jnp.where(expert[:, None] == ids, starts, 0), axis=-1) + rank)
    tiles = jnp.arange(n_tiles, dtype=jnp.int32)
    tile_expert = jnp.sum((ends // te)[None, :] <= tiles[:, None], axis=-1)
    tile_expert = jnp.minimum(tile_expert, N_EXPERTS - 1).astype(jnp.int32)
    n_used = (ends[N_EXPERTS - 1:] // te).astype(jnp.int32)
    return pos, tile_expert, n_used


def _expert_tiles(n):
    return (EXPERT_TOPK * n) // TE_ROWS + N_EXPERTS


def _tile_pos(pos, tm):
    return jnp.concatenate([p.reshape(-1, 1, tm) for p in pos], axis=-1)


def _dispatch_kernel(pos_ref, h_ref, xz_ref, xs_ref, sem, *, tm):
    del xz_ref
    for r in range(tm):
        for k in range(EXPERT_TOPK):
            pltpu.make_async_copy(h_ref.at[pl.ds(r, 1)], xs_ref.at[pl.ds(pos_ref[0, 0, k * tm + r], 1)],
                                  sem).start(priority=(r + k) % 2)
    for k in range(EXPERT_TOPK):
        pltpu.make_async_copy(h_ref, h_ref, sem).wait()


def _dispatch(pos, h2r, xz, *, n):
    tm = TM_DISPATCH
    rows, d = xz.shape
    assert rows == _expert_tiles(n) * TE_ROWS
    pos3 = _tile_pos(pos, tm)
    return pl.pallas_call(
        functools.partial(_dispatch_kernel, tm=tm),
        grid=(n // tm,),
        in_specs=[pl.BlockSpec((1, 1, EXPERT_TOPK * tm), lambda i: (i, 0, 0), memory_space=pltpu.SMEM),
                  pl.BlockSpec((tm, d), lambda i: (i, 0)),
                  pl.BlockSpec(memory_space=pl.ANY)],
        out_specs=pl.BlockSpec(memory_space=pl.ANY),
        out_shape=jax.ShapeDtypeStruct((rows, d), F32),
        scratch_shapes=[pltpu.SemaphoreType.DMA(())],
        input_output_aliases={2: 0},
        compiler_params=_cparams("arbitrary"),
        name="moe_dispatch",
    )(pos3, h2r, xz)


def _experts_kernel(te_ref, nu_ref, xs_ref, wg_ref, wu_ref, wd_ref, y_ref, wg_b, wu_b, wd_b):
    t = pl.program_id(0)

    @pl.when(t < nu_ref[0])
    def _():
        e = te_ref[t]
        e_prev = te_ref[jnp.maximum(t - 1, 0)]

        @pl.when((t == 0) | (e != e_prev))
        def _():
            wg_b[...] = wg_ref[0].astype(BF16)
            wu_b[...] = wu_ref[0].astype(BF16)
            wd_b[...] = wd_ref[0].astype(BF16)

        x = xs_ref[...].astype(BF16)
        g = _dot(x, wg_b[...])
        u = _dot(x, wu_b[...])
        act = (g * jax.nn.sigmoid(g)) * u
        y_ref[...] = _dot(act.astype(BF16), wd_b[...])

    @pl.when(t >= nu_ref[0])
    def _():
        y_ref[...] = jnp.zeros_like(y_ref)


def _experts(tile_expert, n_used, xs, wg, wu, wd):
    rows, d = xs.shape
    te = TE_ROWS

    def used(t, nu):
        return jnp.minimum(t, nu[0] - 1)

    grid_spec = pltpu.PrefetchScalarGridSpec(
        num_scalar_prefetch=2,
        grid=(rows // te,),
        in_specs=[pl.BlockSpec((te, d), lambda t, te_r, nu: (used(t, nu), 0)),
                  pl.BlockSpec((1, d, D_EXPERT), lambda t, te_r, nu: (te_r[used(t, nu)], 0, 0)),
                  pl.BlockSpec((1, d, D_EXPERT), lambda t, te_r, nu: (te_r[used(t, nu)], 0, 0)),
                  pl.BlockSpec((1, D_EXPERT, d), lambda t, te_r, nu: (te_r[used(t, nu)], 0, 0))],
        out_specs=pl.BlockSpec((te, d), lambda t, te_r, nu: (t, 0)),
        scratch_shapes=[pltpu.VMEM((d, D_EXPERT), BF16), pltpu.VMEM((d, D_EXPERT), BF16),
                        pltpu.VMEM((D_EXPERT, d), BF16)])
    return pl.pallas_call(
        _experts_kernel,
        grid_spec=grid_spec,
        out_shape=jax.ShapeDtypeStruct((rows, d), F32),
        compiler_params=_cparams("arbitrary"),
        name="moe_experts",
    )(tile_expert, n_used, xs, wg, wu, wd)


def _final_kernel(pos_ref, pos_next_ref, x1_ref, route_ref, mod_ref, g_ref, y_ref, o_ref, ybuf, sems, *,
                  tm, final_norm):
    d = D_MODEL
    i = pl.program_id(0)
    slot = i % 2

    def gather(p_ref, s):
        for r in range(tm):
            for k in range(EXPERT_TOPK):
                pltpu.make_async_copy(y_ref.at[pl.ds(p_ref[0, 0, k * tm + r], 1)],
                                      ybuf.at[s, k, pl.ds(r, 1)], sems.at[s]).start(priority=(r + k) % 2)

    @pl.when(i == 0)
    def _():
        gather(pos_ref, 0)

    @pl.when(i + 1 < pl.num_programs(0))
    def _():
        gather(pos_next_ref, 1 - slot)

    for k in range(EXPERT_TOPK):
        pltpu.make_async_copy(ybuf.at[slot, k], ybuf.at[slot, k], sems.at[slot]).wait()

    route = route_ref[...]
    lane = lax.broadcasted_iota(jnp.int32, route.shape, 1)
    y = jnp.zeros((tm, d), F32)
    for k in range(EXPERT_TOPK):
        wk = jnp.sum(jnp.where(lane == 2 + k, route, 0.0), axis=-1, keepdims=True)
        y = y + wk * ybuf[slot, k]
    gate2 = mod_ref[0][:, 5 * d:6 * d]
    x = x1_ref[...] + gate2 * y
    if final_norm:
        x = (x * lax.rsqrt(jnp.mean(x * x, axis=-1, keepdims=True) + EPS)) * g_ref[...]
    o_ref[...] = x


def _final(pos, x1, route, mod3, gf, y, *, seq, final_norm):
    n, d = x1.shape
    tm = TM_FINAL
    tps = seq // tm
    pos3 = _tile_pos(pos, tm)
    last = n // tm - 1
    return pl.pallas_call(
        functools.partial(_final_kernel, tm=tm, final_norm=final_norm),
        grid=(n // tm,),
        in_specs=[pl.BlockSpec((1, 1, EXPERT_TOPK * tm), lambda i: (i, 0, 0), memory_space=pltpu.SMEM),
                  pl.BlockSpec((1, 1, EXPERT_TOPK * tm), lambda i: (jnp.minimum(i + 1, last), 0, 0),
                               memory_space=pltpu.SMEM),
                  pl.BlockSpec((tm, d), lambda i: (i, 0)),
                  pl.BlockSpec((tm, LANES), lambda i: (i, 0)),
                  pl.BlockSpec((1, 1, N_MOD * d), lambda i: (i // tps, 0, 0)),
                  pl.BlockSpec((1, d), lambda i: (0, 0)),
                  pl.BlockSpec(memory_space=pl.ANY)],
        out_specs=pl.BlockSpec((tm, d), lambda i: (i, 0)),
        out_shape=jax.ShapeDtypeStruct((n, d), F32),
        scratch_shapes=[pltpu.VMEM((2, EXPERT_TOPK, tm, d), F32), pltpu.SemaphoreType.DMA((2,))],
        compiler_params=_cparams("arbitrary"),
        name="moe_combine_final_norm",
    )(pos3, pos3, x1, route, mod3, gf, y)


def _split_bf16(w):
    hi = w.astype(BF16)
    return hi, (w - hi.astype(F32)).astype(BF16)


def kernel(x, c, w_ada, b_ada, norm1_g, w_in, b_gate, rel_bias, pool_w, pool_scale, w_branch_attn,
           w_branch_pool, w_out, norm2_g, w_router_group, b_router_group, w_router_expert,
           b_router_expert, w_expert_gate, w_expert_up, w_expert_down, norm_f_g):
    batch, seq, d = x.shape
    depth = w_ada.shape[0]
    n = batch * seq
    nb = seq // MOBA_BLOCK
    qkvu = 3 * ATTN_WIDTH + POOL_WIDTH
    bias_w = _bias_vectors(rel_bias)
    c_pad = jnp.pad(c, ((0, 8 - batch), (0, 0)))

    xc = x.reshape(n, d)
    for l in range(depth):
        mod3 = _modulation(c_pad, w_ada[l], b_ada[l][None])[:batch].reshape(batch, 1, N_MOD * d)
        w_in_b = w_in[l].astype(BF16)
        g1 = norm1_g[l][None]

        assert w_in_b.shape[1] == 2 * qkvu == qkvu + 2 * d
        qT, k2, vT, u, km, h1 = _in_projection(xc, mod3, g1, w_in_b, batch=batch, seq=seq)
        km_pad = jnp.pad(km.reshape(batch, nb, ATTN_WIDTH), ((0, 0), (0, MAX_BLOCKS - nb), (0, 0)))
        attnT = _moba_attention(qT, k2.reshape(batch, seq, ATTN_WIDTH), vT, km_pad, bias_w,
                                batch=batch, seq=seq)

        w_r = jnp.concatenate([w_router_group[l], w_router_expert[l]], axis=1)
        w_r = jnp.pad(w_r, ((0, 0), (0, LANES - w_r.shape[1])))
        b_r = jnp.concatenate([b_router_group[l], b_router_expert[l]])
        b_r = jnp.pad(b_r, (0, LANES - b_r.shape[0]))[None]
        wr_hi, wr_lo = _split_bf16(w_r)
        x1, h2r, route, route_t, counts, xz = _post_attention(
            xc, mod3, h1, w_in_b, b_gate[l][None], attnT, u, pool_w[l].astype(BF16),
            pool_scale[l][None], w_branch_attn[l].astype(BF16), w_branch_pool[l].astype(BF16),
            w_out[l].astype(BF16), norm2_g[l][None], wr_hi, wr_lo, b_r, batch=batch, seq=seq)

        pos, tile_expert, n_used = _routing_plan(route_t, counts, n=n)
        xs = _dispatch(pos, h2r, xz, n=n)
        y = _experts(tile_expert, n_used, xs, w_expert_gate[l], w_expert_up[l], w_expert_down[l])
        xc = _final(pos, x1, route, mod3, norm_f_g[None], y, seq=seq, final_norm=(l == depth - 1))
    return xc.reshape(batch, seq, d)
```

```python
import functools
import math

import numpy as np
import jax
import jax.numpy as jnp
from jax import lax
from jax.experimental import pallas as pl
from jax.experimental.pallas import tpu as pltpu

F32 = jnp.float32
BF16 = jnp.bfloat16

D_MODEL = 1024
N_HEADS = 8
HEAD_DIM = 64
ATTN_WIDTH = N_HEADS * HEAD_DIM
MOBA_BLOCK = 256
MOBA_TOPK = 3
MAX_BLOCKS = 32
FAR_GROUP = 4
HEADS_PER_STEP = 4
ONES_ROWS = 16
LOG2E = math.log2(math.e)
POOL_WINDOWS = (2, 4, 8, 16)
POOL_WIDTH = 512
POOL_GROUP_WIDTH = 128
POOL_HALO = 16
REL_BUCKETS = 32
REL_MAX_DIST = 128
N_GROUPS = 4
EXPERTS_PER_GROUP = 8
N_EXPERTS = N_GROUPS * EXPERTS_PER_GROUP
D_EXPERT = 512
N_MOD = 6
EPS = 1e-6
NEG_INF = -1e30
LANES = 128
ROUTER_LANE0 = N_GROUPS
ROUTE_FIELDS = 8
ROUTER_ROWS = 48
VMEM_LIMIT = 56 * 1024 * 1024

TM_PROJ = 512
TM_POST = 512
POST_PART = 256
TM_DISPATCH = 1024
TE_ROWS = 512
EXPERT_TOPK = 2
TM_FINAL = 256


def _cparams(*sem):
    return pltpu.CompilerParams(dimension_semantics=sem, vmem_limit_bytes=VMEM_LIMIT)


def _dot(a, b):
    return jnp.dot(a, b, preferred_element_type=F32)


def _rms_mod(x, g, scale, shift):
    xn = x * lax.rsqrt(jnp.mean(x * x, axis=-1, keepdims=True) + EPS)
    return (xn * g) * (1.0 + scale) + shift


def _mod_kernel(c_ref, w_ref, b_ref, o_ref):
    c = c_ref[...]
    ca = c * jax.nn.sigmoid(c)
    o_ref[...] = _dot(ca.astype(BF16), w_ref[...].astype(BF16)) + b_ref[...]


def _modulation(c_pad, w_ada, b_ada):
    rows, d = c_pad.shape
    n_out = w_ada.shape[1]
    tn = 1024
    return pl.pallas_call(
        _mod_kernel,
        grid=(n_out // tn,),
        in_specs=[pl.BlockSpec((rows, d), lambda j: (0, 0)),
                  pl.BlockSpec((d, tn), lambda j: (0, j)),
                  pl.BlockSpec((1, tn), lambda j: (0, j))],
        out_specs=pl.BlockSpec((rows, tn), lambda j: (0, j)),
        out_shape=jax.ShapeDtypeStruct((rows, n_out), F32),
        compiler_params=_cparams("arbitrary"),
        name="adaln_mod",
    )(c_pad, w_ada, b_ada)


def _inproj_kernel(x_ref, mod_ref, g_ref, w_ref, qT_ref, k_ref, vT_ref, u_ref, km_ref, h_ref, *, tm):
    d = D_MODEL
    mod = mod_ref[0]
    h = _rms_mod(x_ref[...], g_ref[...], mod[:, d:2 * d], mod[:, 0:d]).astype(BF16)
    h_ref[...] = h
    proj = _dot(h, w_ref[...])
    aw = ATTN_WIDTH
    q = proj[:, 0:aw] * (HEAD_DIM ** -0.5 * LOG2E)
    k = proj[:, aw:2 * aw]
    v = proj[:, 2 * aw:3 * aw]
    u_ref[...] = proj[:, 3 * aw:3 * aw + POOL_WIDTH]
    k_ref[...] = k.astype(BF16)
    qT = q.T.astype(BF16)
    vT = v.T.astype(BF16)
    for c in range(tm // MOBA_BLOCK):
        sl = slice(c * MOBA_BLOCK, (c + 1) * MOBA_BLOCK)
        qT_ref[0, c] = qT[:, sl]
        vT_ref[0, c] = vT[:, sl]
        km_ref[0, c:c + 1, :] = jnp.mean(k[sl, :], axis=0, keepdims=True)


def _in_projection(x2, mod3, g1, w_qkvu, *, batch, seq):
    n, d = x2.shape
    tm = TM_PROJ
    tps = seq // tm
    nb = seq // MOBA_BLOCK
    cb = tm // MOBA_BLOCK
    wcols = 3 * ATTN_WIDTH + POOL_WIDTH
    assert w_qkvu.shape[1] % wcols == 0
    kern = functools.partial(_inproj_kernel, tm=tm)
    t_shape = jax.ShapeDtypeStruct((batch, nb, ATTN_WIDTH, MOBA_BLOCK), BF16)
    t_spec = pl.BlockSpec((1, cb, ATTN_WIDTH, MOBA_BLOCK), lambda i: (i // tps, i % tps, 0, 0))
    return pl.pallas_call(
        kern,
        grid=(n // tm,),
        in_specs=[pl.BlockSpec((tm, d), lambda i: (i, 0)),
                  pl.BlockSpec((1, 1, N_MOD * d), lambda i: (i // tps, 0, 0)),
                  pl.BlockSpec((1, d), lambda i: (0, 0)),
                  pl.BlockSpec((d, wcols), lambda i: (0, 0))],
        out_specs=[t_spec,
                   pl.BlockSpec((tm, ATTN_WIDTH), lambda i: (i, 0)),
                   t_spec,
                   pl.BlockSpec((tm, POOL_WIDTH), lambda i: (i, 0)),
                   pl.BlockSpec((1, cb, ATTN_WIDTH), lambda i: (i, 0, 0)),
                   pl.BlockSpec((tm, d), lambda i: (i, 0))],
        out_shape=[t_shape,
                   jax.ShapeDtypeStruct((n, ATTN_WIDTH), BF16),
                   t_shape,
                   jax.ShapeDtypeStruct((n, POOL_WIDTH), F32),
                   jax.ShapeDtypeStruct((n // tm, cb, ATTN_WIDTH), F32),
                   jax.ShapeDtypeStruct((n, d), BF16)],
        compiler_params=_cparams("arbitrary"),
        name="norm1_inproj",
    )(x2, mod3, g1, w_qkvu)


def _attn_kernel(qT_ref, k_ref, vT_ref, km_ref, bw_ref, o_ref, kaug_ref, vaug_ref, bias_ref, qnear_ref,
                 qfar_ref, sn_ref, sf0_ref, sf1_ref, *, nb):
    blk = MOBA_BLOCK
    hd = HEAD_DIM
    nh = HEADS_PER_STEP
    heads = range(nh)
    lane = lax.broadcasted_iota(jnp.int32, (blk, LANES), 1)

    def pair_lanes(hh):
        return slice((hh // 2) * LANES, (hh // 2 + 1) * LANES)

    def head_rows(hh):
        return slice(hh * hd, (hh + 1) * hd)

    ones = jnp.ones((ONES_ROWS, blk), BF16)

    def build_operands(j, carry):
        rows = pl.ds(pl.multiple_of(j * blk, blk), blk)
        for hh in heads:
            kp = k_ref[0, rows, pair_lanes(hh)]
            if hh % 2 == 0:
                kaug_ref[hh, rows, :] = jnp.where(lane < hd, kp, jnp.where(lane == hd + j, 1.0, 0.0).astype(BF16))
            else:
                kaug_ref[hh, rows, :] = jnp.where(lane >= hd, kp, jnp.where(lane == j, 1.0, 0.0).astype(BF16))
            vaug_ref[j, hh] = jnp.concatenate([vT_ref[0, j, head_rows(hh), :], ones], axis=0)
        return carry

    lax.fori_loop(0, nb, build_operands, 0)

    for hh in heads:
        for t in range(2):
            wide = jnp.broadcast_to(bw_ref[hh, t:t + 1, :], (blk, 2 * blk))
            bias_ref[hh, t] = pltpu.roll(wide, 0, 1, stride=1, stride_axis=0)[:, blk:2 * blk]

    rowi = lax.broadcasted_iota(jnp.int32, (MAX_BLOCKS, blk), 0)
    km_lane = lax.broadcasted_iota(jnp.int32, (MAX_BLOCKS, LANES), 1)
    zpad = jnp.zeros((LANES - hd - MAX_BLOCKS, blk), BF16)
    km_split = []
    for hh in heads:
        sub = hh % 2
        in_head = (km_lane >= sub * hd) & (km_lane < (sub + 1) * hd)
        km_h = jnp.where(in_head, km_ref[0, :, pair_lanes(hh)], 0.0)
        km_hi = km_h.astype(BF16)
        km_split.append((km_hi, (km_h - km_hi.astype(F32)).astype(BF16)))

    grp = FAR_GROUP

    def retire_begin(state, mx):
        m, acc = state
        m_new = jnp.maximum(m, mx)
        return m_new, jnp.exp2(m - m_new) * acc

    def retire_slice(src_ref, hh, m_new, acc, g, j):
        p = jnp.exp2(src_ref[hh, g * blk:(g + 1) * blk, :] - m_new)
        return acc + _dot(vaug_ref[j, hh], p.astype(BF16))

    def fresh():
        return (jnp.full((1, blk), -jnp.inf, F32), jnp.zeros((hd + ONES_ROWS, blk), F32))

    def near_blocks(i):
        return tuple(jnp.where(i - (grp - 1) + g < 0, i + 1, i - (grp - 1) + g) for g in range(grp))

    def finish(i, states):
        for hh in heads:
            _, acc = states[hh]
            o_ref[0, i, head_rows(hh), :] = (acc[0:hd] / acc[hd:hd + 1]).astype(BF16)

    def select(i):
        for hh in heads:
            qp = qT_ref[0, i, pair_lanes(hh), :]
            km_hi, km_lo = km_split[hh]
            gate = _dot(km_hi, qp) + _dot(km_lo, qp)
            g = jnp.where(rowi < i, gate, -jnp.inf)
            sel = rowi == i
            for _ in range(MOBA_TOPK):
                m = jnp.max(g, axis=0, keepdims=True)
                hit = (g == m) & (m > -jnp.inf)
                idx = jnp.min(jnp.where(hit, rowi, MAX_BLOCKS), axis=0, keepdims=True)
                pick = rowi == idx
                sel = sel | pick
                g = jnp.where(pick, -jnp.inf, g)
            near = jnp.where(sel, 0.0, NEG_INF)
            far = jnp.where(rowi > i - grp, NEG_INF, near)
            qh = qp[head_rows(hh % 2), :]
            if hh % 2 == 0:
                qnear_ref[hh] = jnp.concatenate([qh, near.astype(BF16), zpad], axis=0)
                qfar_ref[hh] = jnp.concatenate([qh, far.astype(BF16), zpad], axis=0)
            else:
                qnear_ref[hh] = jnp.concatenate([near.astype(BF16), zpad, qh], axis=0)
                qfar_ref[hh] = jnp.concatenate([far.astype(BF16), zpad, qh], axis=0)

    def park(dst_ref, q_ref, hh, g, j, bias=None):
        rows = pl.ds(pl.multiple_of(j * blk, blk), blk)
        s = _dot(kaug_ref[hh, rows, :], q_ref[hh])
        if bias is not None:
            s = s + bias
        dst_ref[hh, g * blk:(g + 1) * blk, :] = s
        return jnp.max(s, axis=0, keepdims=True)

    def park_near(i):
        idx = near_blocks(i)

        def quarter(g, hh):
            bias = None
            if g == grp - 1:
                bias = bias_ref[hh, 0]
            elif g == grp - 2:
                bias = bias_ref[hh, 1]
            return park(sn_ref, qnear_ref, hh, g, idx[g], bias)
        return quarter

    def park_far(c, dst_ref):
        return lambda g, hh: park(dst_ref, qfar_ref, hh, g, c * grp + g)

    def chunk_blocks(c):
        return tuple(c * grp + g for g in range(grp))

    def step(src_ref, states, maxes, vidx, quarter, park_first):
        begun = [retire_begin(states[hh], maxes[hh]) for hh in heads] if states is not None else None
        accs = [b[1] for b in begun] if begun else None
        new_max = [None] * nh

        def park_quarter(g):
            for hh in heads:
                cm = quarter(g, hh)
                new_max[hh] = cm if new_max[hh] is None else jnp.maximum(new_max[hh], cm)

        for g in range(grp):
            if park_first:
                park_quarter(g)
            if begun:
                for hh in heads:
                    accs[hh] = retire_slice(src_ref, hh, begun[hh][0], accs[hh], g, vidx[g])
            if not park_first and quarter is not None:
                park_quarter(g)
        new_states = tuple((begun[hh][0], accs[hh]) for hh in heads) if begun else None
        return new_states, tuple(new_max)

    def fresh_states():
        return tuple(fresh() for _ in heads)

    def early_tile(i, near_max):
        nxt = jnp.minimum(i + 1, nb - 1)
        select(nxt)
        states, next_max = step(sn_ref, fresh_states(), near_max, near_blocks(i), park_near(nxt), False)
        finish(i, states)
        return next_max

    def late_tile(i, near_max):
        sf = (sf0_ref, sf1_ref)
        n_chunks = (i - (grp - 1) + grp - 1) // grp
        nxt = jnp.minimum(i + 1, nb - 1)
        states, far_max = step(sn_ref, fresh_states(), near_max, near_blocks(i), park_far(0, sf[0]), True)

        def pair(k, st):
            states, far_max = st
            states, far_max = step(sf[0], states, far_max, chunk_blocks(2 * k), park_far(2 * k + 1, sf[1]), True)
            return step(sf[1], states, far_max, chunk_blocks(2 * k + 1), park_far(2 * k + 2, sf[0]), True)

        states, far_max = lax.fori_loop(0, (n_chunks - 1) // 2, pair, (states, far_max))

        def last(src, states, far_max):
            select(nxt)
            states, next_max = step(src, states, far_max, chunk_blocks(n_chunks - 1), park_near(nxt), True)
            finish(i, states)
            return next_max

        def even_count(st):
            states, far_max = step(sf[0], st[0], st[1], chunk_blocks(n_chunks - 2),
                                   park_far(n_chunks - 1, sf[1]), True)
            return last(sf[1], states, far_max)

        def odd_count(st):
            return last(sf[0], st[0], st[1])

        return lax.cond(n_chunks % 2 == 0, even_count, odd_count, (states, far_max))

    n_early = min(grp, nb)
    select(0)
    _, near_max = step(None, None, None, None, park_near(0), True)
    near_max = lax.fori_loop(0, n_early, early_tile, near_max)
    lax.fori_loop(n_early, nb, late_tile, near_max)


def _rel_bucket_table(max_rel):
    n = np.arange(max_rel)
    max_exact = REL_BUCKETS // 2
    nf = np.maximum(n, 1).astype(np.float32)
    large = max_exact + (np.log(nf / np.float32(max_exact)) / np.float32(math.log(REL_MAX_DIST / max_exact))
                         * np.float32(REL_BUCKETS - max_exact)).astype(np.int32)
    large = np.minimum(large, REL_BUCKETS - 1)
    return np.where(n < max_exact, n, large)


def _bias_vectors(rel_bias):
    blk = MOBA_BLOCK
    table = _rel_bucket_table(2 * blk)
    r = np.arange(-blk, blk)
    bt = (rel_bias.T - rel_bias[REL_BUCKETS - 1][:, None]) * LOG2E
    own = jnp.where(jnp.asarray(r >= 0)[None], bt[:, table[np.maximum(r, 0)]], NEG_INF)
    prev = bt[:, table[np.minimum(r + blk, 2 * blk - 1)]]
    return jnp.stack([own, prev], axis=1).astype(F32)


def _moba_attention(qT, k3, vT, km_pad, bias_w, *, batch, seq):
    nb = seq // MOBA_BLOCK
    blk = MOBA_BLOCK
    assert nb % FAR_GROUP == 0 and nb <= MAX_BLOCKS
    kern = functools.partial(_attn_kernel, nb=nb)
    nh = HEADS_PER_STEP
    width = nh * HEAD_DIM
    assert nh % 2 == 0 and N_HEADS % nh == 0

    def big(shape, index_map):
        return pl.BlockSpec(shape, index_map, pipeline_mode=pl.Buffered(1))

    return pl.pallas_call(
        kern,
        grid=(batch, N_HEADS // nh),
        in_specs=[big((1, nb, width, blk), lambda b, p: (b, 0, p, 0)),
                  big((1, seq, width), lambda b, p: (b, 0, p)),
                  big((1, nb, width, blk), lambda b, p: (b, 0, p, 0)),
                  pl.BlockSpec((1, MAX_BLOCKS, width), lambda b, p: (b, 0, p)),
                  pl.BlockSpec((nh, 2, 2 * blk), lambda b, p: (p, 0, 0))],
        out_specs=pl.BlockSpec((1, nb, width, blk), lambda b, p: (b, 0, p, 0)),
        out_shape=jax.ShapeDtypeStruct((batch, nb, ATTN_WIDTH, blk), BF16),
        scratch_shapes=[pltpu.VMEM((nh, seq, LANES), BF16),
                        pltpu.VMEM((nb, nh, HEAD_DIM + ONES_ROWS, blk), BF16),
                        pltpu.VMEM((nh, 2, blk, blk), F32),
                        pltpu.VMEM((nh, LANES, blk), BF16),
                        pltpu.VMEM((nh, LANES, blk), BF16),
                        pltpu.VMEM((nh, FAR_GROUP * blk, blk), F32),
                        pltpu.VMEM((nh, FAR_GROUP * blk, blk), F32),
                        pltpu.VMEM((nh, FAR_GROUP * blk, blk), F32)],
        compiler_params=_cparams("arbitrary", "arbitrary"),
        name="moba_attention",
    )(qT, k3, vT, km_pad, bias_w)


def _post_kernel(x_ref, mod_ref, h_ref, wz_ref, bz_ref, at_ref, u_ref, uh_ref, pw_ref, ps_ref,
                 wba_ref, wbp_ref, wo_ref, g2_ref, wrh_ref, wrl_ref, br_ref,
                 x1_ref, h2r_ref, route_ref, route_t_ref, counts_ref, xz_ref, carry_ref, *, tm, seq):
    d = D_MODEL
    i = pl.program_id(0)
    x = x_ref[...]
    mod = mod_ref[0]
    gate1 = mod[:, 2 * d:3 * d]
    shift2, scale2 = mod[:, 3 * d:4 * d], mod[:, 4 * d:5 * d]

    part = POST_PART
    parts = range(tm // part)
    first = (i * tm) % seq == 0
    halo = jnp.where(first, 0.0, uh_ref[...])
    ubuf = jnp.concatenate([halo, u_ref[...]], axis=0)

    def rows_of(c):
        return slice(c * part, (c + 1) * part)

    def gates(t):
        return 0.5 * jnp.tanh(0.5 * t) + 0.5

    def branches(c):
        blk_i, off = (c * part) // MOBA_BLOCK, (c * part) % MOBA_BLOCK
        ya = lax.dot_general(at_ref[0, blk_i, :, off:off + part], wba_ref[...], (((0,), (0,)), ((), ())),
                             preferred_element_type=F32)
        nrow = part + POOL_HALO
        ub = ubuf[c * part:c * part + nrow, :]
        pos = (lax.broadcasted_iota(jnp.int32, (part, POOL_GROUP_WIDTH), 0) + i * tm + c * part) % seq
        outs = []
        for gi, w in enumerate(POOL_WINDOWS):
            e = ub[:, gi * POOL_GROUP_WIDTH:(gi + 1) * POOL_GROUP_WIDTH]
            s = e
            step = 1
            while step < w:
                s = s + pltpu.roll(s, step, 0)
                step *= 2
            cnt = jnp.minimum(pos + 1, w).astype(F32)
            dlt = s[POOL_HALO:nrow] / cnt - e[POOL_HALO:nrow]
            outs.append(_dot(dlt.astype(BF16), pw_ref[gi]))
        pooled = jnp.concatenate(outs, axis=-1) * ps_ref[...]
        return ya, _dot(pooled.astype(BF16), wbp_ref[...])

    def residual(c, zg, ya, yp):
        merged = zg[:, 0:d] * ya + zg[:, d:2 * d] * yp
        x1 = x[rows_of(c), :] + gate1 * _dot(merged.astype(BF16), wo_ref[...])
        x1_ref[rows_of(c), :] = x1
        return x1

    def norm2(c, x1):
        h2 = _rms_mod(x1, g2_ref[...], scale2, shift2)
        h2_hi = h2.astype(BF16)
        h2r_ref[rows_of(c), :] = h2_hi.astype(F32)
        return h2_hi, (h2 - h2_hi.astype(F32)).astype(BF16)

    t = _dot(h_ref[...], wz_ref[...]) + bz_ref[...]
    zg, ybr, x1p, h2p = {}, {}, {}, {}
    for c in range(len(parts) + 2):
        if c in parts:
            zg[c] = gates(t[rows_of(c), :])
            ybr[c] = branches(c)
        if c - 1 in parts:
            x1p[c - 1] = residual(c - 1, zg[c - 1], *ybr[c - 1])
        if c - 2 in parts:
            h2p[c - 2] = norm2(c - 2, x1p[c - 2])
    h2_hi = jnp.concatenate([h2p[c][0] for c in parts], axis=0)
    h2_lo = jnp.concatenate([h2p[c][1] for c in parts], axis=0)
    z = _dot(h2_hi, wrh_ref[...]) + _dot(h2_lo, wrh_ref[...]) + _dot(h2_hi, wrl_ref[...]) + br_ref[...]

    rr = ROUTER_ROWS
    zt = z.T[0:rr, :]
    row = lax.broadcasted_iota(jnp.int32, (rr, tm), 0)
    zgrp = jnp.where(row < N_GROUPS, zt, -jnp.inf)
    mg = jnp.max(zgrp, axis=0, keepdims=True)
    pg_top = 1.0 / jnp.sum(jnp.exp(zgrp - mg), axis=0, keepdims=True)
    g_idx = jnp.min(jnp.where(zgrp == mg, row, rr), axis=0, keepdims=True)
    e_row = row - ROUTER_LANE0
    in_grp = (e_row >= 0) & (e_row < N_EXPERTS) & ((e_row >> 3) == g_idx)
    ze = jnp.where(in_grp, zt, -jnp.inf)
    m1 = jnp.max(ze, axis=0, keepdims=True)
    i1 = jnp.min(jnp.where(ze == m1, row, rr), axis=0, keepdims=True)
    ze2 = jnp.where(row == i1, -jnp.inf, ze)
    m2 = jnp.max(ze2, axis=0, keepdims=True)
    i2 = jnp.min(jnp.where(ze2 == m2, row, rr), axis=0, keepdims=True)
    e2 = jnp.exp(m2 - m1)
    w1 = pg_top / (1.0 + e2)
    w2 = pg_top * e2 / (1.0 + e2)

    hot = jnp.where((row == i1) | (row == i2), 1.0, 0.0)
    tri = jnp.where(lax.broadcasted_iota(jnp.int32, (tm, tm), 0) <= lax.broadcasted_iota(jnp.int32, (tm, tm), 1),
                    1.0, 0.0).astype(BF16)
    prefix = _dot(hot.astype(BF16), tri)

    @pl.when(i == 0)
    def _():
        carry_ref[...] = jnp.zeros_like(carry_ref)

    carry = carry_ref[...]
    base = carry[:, 0:1] + prefix - 1.0
    rank1 = jnp.sum(jnp.where(row == i1, base, 0.0), axis=0, keepdims=True)
    rank2 = jnp.sum(jnp.where(row == i2, base, 0.0), axis=0, keepdims=True)
    carry_ref[...] = carry + prefix[:, tm - 1:tm]
    counts_ref[...] = carry_ref[...]
    fields = ((i1 - ROUTER_LANE0).astype(F32), (i2 - ROUTER_LANE0).astype(F32), w1, w2, rank1, rank2)
    frow = lax.broadcasted_iota(jnp.int32, (ROUTE_FIELDS, tm), 0)
    route_t = jnp.zeros((ROUTE_FIELDS, tm), F32)
    for k, val in enumerate(fields):
        route_t = jnp.where(frow == k, val, route_t)
    route_t_ref[...] = route_t
    route_ref[...] = jnp.concatenate([route_t, jnp.zeros((LANES - ROUTE_FIELDS, tm), F32)], axis=0).T
    xz_ref[...] = jnp.zeros_like(xz_ref)


def _post_attention(x2, mod3, h1, w_z, b_z, attnT, u, pool_w, pool_scale, w_ba, w_bp, w_out, g2,
                    wr_hi, wr_lo, b_r, *, batch, seq):
    n, d = x2.shape
    tm = TM_POST
    tps = seq // tm
    cb = tm // MOBA_BLOCK
    hb = tm // POOL_HALO
    sorted_rows = _expert_tiles(n) * TE_ROWS
    assert sorted_rows % (n // tm) == 0
    zrows = sorted_rows // (n // tm)
    kern = functools.partial(_post_kernel, tm=tm, seq=seq)
    const2 = lambda i: (0, 0)
    return pl.pallas_call(
        kern,
        grid=(n // tm,),
        in_specs=[pl.BlockSpec((tm, d), lambda i: (i, 0)),
                  pl.BlockSpec((1, 1, N_MOD * d), lambda i: (i // tps, 0, 0)),
                  pl.BlockSpec((tm, d), lambda i: (i, 0)),
                  pl.BlockSpec((d, 2 * d), lambda i: (0, 1)),
                  pl.BlockSpec((1, 2 * d), const2),
                  pl.BlockSpec((1, cb, ATTN_WIDTH, MOBA_BLOCK), lambda i: (i // tps, i % tps, 0, 0)),
                  pl.BlockSpec((tm, POOL_WIDTH), lambda i: (i, 0)),
                  pl.BlockSpec((POOL_HALO, POOL_WIDTH), lambda i: (jnp.maximum(i * hb - 1, 0), 0)),
                  pl.BlockSpec((len(POOL_WINDOWS), POOL_GROUP_WIDTH, POOL_GROUP_WIDTH), lambda i: (0, 0, 0)),
                  pl.BlockSpec((1, POOL_WIDTH), const2),
                  pl.BlockSpec((ATTN_WIDTH, d), const2),
                  pl.BlockSpec((POOL_WIDTH, d), const2),
                  pl.BlockSpec((d, d), const2),
                  pl.BlockSpec((1, d), const2),
                  pl.BlockSpec((d, LANES), const2),
                  pl.BlockSpec((d, LANES), const2),
                  pl.BlockSpec((1, LANES), const2)],
        out_specs=[pl.BlockSpec((tm, d), lambda i: (i, 0)),
                   pl.BlockSpec((tm, d), lambda i: (i, 0)),
                   pl.BlockSpec((tm, LANES), lambda i: (i, 0)),
                   pl.BlockSpec((ROUTE_FIELDS, tm), lambda i: (0, i)),
                   pl.BlockSpec((ROUTER_ROWS, LANES), const2),
                   pl.BlockSpec((zrows, d), lambda i: (i, 0))],
        out_shape=[jax.ShapeDtypeStruct((n, d), F32),
                   jax.ShapeDtypeStruct((n, d), F32),
                   jax.ShapeDtypeStruct((n, LANES), F32),
                   jax.ShapeDtypeStruct((ROUTE_FIELDS, n), F32),
                   jax.ShapeDtypeStruct((ROUTER_ROWS, LANES), F32),
                   jax.ShapeDtypeStruct((zrows * (n // tm), d), F32)],
        scratch_shapes=[pltpu.VMEM((ROUTER_ROWS, LANES), F32)],
        compiler_params=_cparams("arbitrary"),
        name="merge_outproj_router",
    )(x2, mod3, h1, w_z, b_z, attnT, u, u, pool_w, pool_scale, w_ba, w_bp, w_out, g2, wr_hi, wr_lo, b_r)


def _routing_plan(route_t, counts, *, n):
    te = TE_ROWS
    n_tiles = _expert_tiles(n)
    cnt = counts[ROUTER_LANE0:ROUTER_LANE0 + N_EXPERTS, 0].astype(jnp.int32)
    padded = ((cnt + te - 1) // te) * te
    ids = jnp.arange(N_EXPERTS, dtype=jnp.int32)
    ends = jnp.sum(jnp.where(ids[None, :] <= ids[:, None], padded[None, :], 0), axis=-1)
    starts = ends - padded
    pos = []
    for k in range(EXPERT_TOPK):
        expert = route_t[k].astype(jnp.int32)
        rank = route_t[4 + k].astype(jnp.int32)
        pos.append(jnp.sum(jnp.where(expert[:, None] == ids, starts, 0), axis=-1) + rank)
    tiles = jnp.arange(n_tiles, dtype=jnp.int32)
    tile_expert = jnp.sum((ends // te)[None, :] <= tiles[:, None], axis=-1)
    tile_expert = jnp.minimum(tile_expert, N_EXPERTS - 1).astype(jnp.int32)
    n_used = (ends[N_EXPERTS - 1:] // te).astype(jnp.int32)
    return pos, tile_expert, n_used


def _expert_tiles(n):
    return (EXPERT_TOPK * n) // TE_ROWS + N_EXPERTS


def _tile_pos(pos, tm):
    return jnp.concatenate([p.reshape(-1, 1, tm) for p in pos], axis=-1)


def _dispatch_kernel(pos_ref, h_ref, xz_ref, xs_ref, sem, *, tm):
    del xz_ref
    for r in range(tm):
        for k in range(EXPERT_TOPK):
            pltpu.make_async_copy(h_ref.at[pl.ds(r, 1)], xs_ref.at[pl.ds(pos_ref[0, 0, k * tm + r], 1)],
                                  sem).start(priority=(r + k) % 2)
    for k in range(EXPERT_TOPK):
        pltpu.make_async_copy(h_ref, h_ref, sem).wait()


def _dispatch(pos, h2r, xz, *, n):
    tm = TM_DISPATCH
    rows, d = xz.shape
    assert rows == _expert_tiles(n) * TE_ROWS
    pos3 = _tile_pos(pos, tm)
    return pl.pallas_call(
        functools.partial(_dispatch_kernel, tm=tm),
        grid=(n // tm,),
        in_specs=[pl.BlockSpec((1, 1, EXPERT_TOPK * tm), lambda i: (i, 0, 0), memory_space=pltpu.SMEM),
                  pl.BlockSpec((tm, d), lambda i: (i, 0)),
                  pl.BlockSpec(memory_space=pl.ANY)],
        out_specs=pl.BlockSpec(memory_space=pl.ANY),
        out_shape=jax.ShapeDtypeStruct((rows, d), F32),
        scratch_shapes=[pltpu.SemaphoreType.DMA(())],
        input_output_aliases={2: 0},
        compiler_params=_cparams("arbitrary"),
        name="moe_dispatch",
    )(pos3, h2r, xz)


def _experts_kernel(te_ref, nu_ref, xs_ref, wg_ref, wu_ref, wd_ref, y_ref, wg_b, wu_b, wd_b):
    t = pl.program_id(0)

    @pl.when(t < nu_ref[0])
    def _():
        e = te_ref[t]
        e_prev = te_ref[jnp.maximum(t - 1, 0)]

        @pl.when((t == 0) | (e != e_prev))
        def _():
            wg_b[...] = wg_ref[0].astype(BF16)
            wu_b[...] = wu_ref[0].astype(BF16)
            wd_b[...] = wd_ref[0].astype(BF16)

        x = xs_ref[...].astype(BF16)
        g = _dot(x, wg_b[...])
        u = _dot(x, wu_b[...])
        act = (g * jax.nn.sigmoid(g)) * u
        y_ref[...] = _dot(act.astype(BF16), wd_b[...])

    @pl.when(t >= nu_ref[0])
    def _():
        y_ref[...] = jnp.zeros_like(y_ref)


def _experts(tile_expert, n_used, xs, wg, wu, wd):
    rows, d = xs.shape
    te = TE_ROWS

    def used(t, nu):
        return jnp.minimum(t, nu[0] - 1)

    grid_spec = pltpu.PrefetchScalarGridSpec(
        num_scalar_prefetch=2,
        grid=(rows // te,),
        in_specs=[pl.BlockSpec((te, d), lambda t, te_r, nu: (used(t, nu), 0)),
                  pl.BlockSpec((1, d, D_EXPERT), lambda t, te_r, nu: (te_r[used(t, nu)], 0, 0)),
                  pl.BlockSpec((1, d, D_EXPERT), lambda t, te_r, nu: (te_r[used(t, nu)], 0, 0)),
                  pl.BlockSpec((1, D_EXPERT, d), lambda t, te_r, nu: (te_r[used(t, nu)], 0, 0))],
        out_specs=pl.BlockSpec((te, d), lambda t, te_r, nu: (t, 0)),
        scratch_shapes=[pltpu.VMEM((d, D_EXPERT), BF16), pltpu.VMEM((d, D_EXPERT), BF16),
                        pltpu.VMEM((D_EXPERT, d), BF16)])
    return pl.pallas_call(
        _experts_kernel,
        grid_spec=grid_spec,
        out_shape=jax.ShapeDtypeStruct((rows, d), F32),
        compiler_params=_cparams("arbitrary"),
        name="moe_experts",
    )(tile_expert, n_used, xs, wg, wu, wd)


def _final_kernel(pos_ref, pos_next_ref, x1_ref, route_ref, mod_ref, g_ref, y_ref, o_ref, ybuf, sems, *,
                  tm, final_norm):
    d = D_MODEL
    i = pl.program_id(0)
    slot = i % 2

    def gather(p_ref, s):
        for r in range(tm):
            for k in range(EXPERT_TOPK):
                pltpu.make_async_copy(y_ref.at[pl.ds(p_ref[0, 0, k * tm + r], 1)],
                                      ybuf.at[s, k, pl.ds(r, 1)], sems.at[s]).start(priority=(r + k) % 2)

    @pl.when(i == 0)
    def _():
        gather(pos_ref, 0)

    @pl.when(i + 1 < pl.num_programs(0))
    def _():
        gather(pos_next_ref, 1 - slot)

    for k in range(EXPERT_TOPK):
        pltpu.make_async_copy(ybuf.at[slot, k], ybuf.at[slot, k], sems.at[slot]).wait()

    route = route_ref[...]
    lane = lax.broadcasted_iota(jnp.int32, route.shape, 1)
    y = jnp.zeros((tm, d), F32)
    for k in range(EXPERT_TOPK):
        wk = jnp.sum(jnp.where(lane == 2 + k, route, 0.0), axis=-1, keepdims=True)
        y = y + wk * ybuf[slot, k]
    gate2 = mod_ref[0][:, 5 * d:6 * d]
    x = x1_ref[...] + gate2 * y
    if final_norm:
        x = (x * lax.rsqrt(jnp.mean(x * x, axis=-1, keepdims=True) + EPS)) * g_ref[...]
    o_ref[...] = x


def _final(pos, x1, route, mod3, gf, y, *, seq, final_norm):
    n, d = x1.shape
    tm = TM_FINAL
    tps = seq // tm
    pos3 = _tile_pos(pos, tm)
    last = n // tm - 1
    return pl.pallas_call(
        functools.partial(_final_kernel, tm=tm, final_norm=final_norm),
        grid=(n // tm,),
        in_specs=[pl.BlockSpec((1, 1, EXPERT_TOPK * tm), lambda i: (i, 0, 0), memory_space=pltpu.SMEM),
                  pl.BlockSpec((1, 1, EXPERT_TOPK * tm), lambda i: (jnp.minimum(i + 1, last), 0, 0),
                               memory_space=pltpu.SMEM),
                  pl.BlockSpec((tm, d), lambda i: (i, 0)),
                  pl.BlockSpec((tm, LANES), lambda i: (i, 0)),
                  pl.BlockSpec((1, 1, N_MOD * d), lambda i: (i // tps, 0, 0)),
                  pl.BlockSpec((1, d), lambda i: (0, 0)),
                  pl.BlockSpec(memory_space=pl.ANY)],
        out_specs=pl.BlockSpec((tm, d), lambda i: (i, 0)),
        out_shape=jax.ShapeDtypeStruct((n, d), F32),
        scratch_shapes=[pltpu.VMEM((2, EXPERT_TOPK, tm, d), F32), pltpu.SemaphoreType.DMA((2,))],
        compiler_params=_cparams("arbitrary"),
        name="moe_combine_final_norm",
    )(pos3, pos3, x1, route, mod3, gf, y)


def _split_bf16(w):
    hi = w.astype(BF16)
    return hi, (w - hi.astype(F32)).astype(BF16)


def kernel(x, c, w_ada, b_ada, norm1_g, w_in, b_gate, rel_bias, pool_w, pool_scale, w_branch_attn,
           w_branch_pool, w_out, norm2_g, w_router_group, b_router_group, w_router_expert,
           b_router_expert, w_expert_gate, w_expert_up, w_expert_down, norm_f_g):
    batch, seq, d = x.shape
    depth = w_ada.shape[0]
    n = batch * seq
    nb = seq // MOBA_BLOCK
    qkvu = 3 * ATTN_WIDTH + POOL_WIDTH
    bias_w = _bias_vectors(rel_bias)
    c_pad = jnp.pad(c, ((0, 8 - batch), (0, 0)))

    xc = x.reshape(n, d)
    for l in range(depth):
        mod3 = _modulation(c_pad, w_ada[l], b_ada[l][None])[:batch].reshape(batch, 1, N_MOD * d)
        w_in_b = w_in[l].astype(BF16)
        g1 = norm1_g[l][None]

        assert w_in_b.shape[1] == 2 * qkvu == qkvu + 2 * d
        qT, k2, vT, u, km, h1 = _in_projection(xc, mod3, g1, w_in_b, batch=batch, seq=seq)
        km_pad = jnp.pad(km.reshape(batch, nb, ATTN_WIDTH), ((0, 0), (0, MAX_BLOCKS - nb), (0, 0)))
        attnT = _moba_attention(qT, k2.reshape(batch, seq, ATTN_WIDTH), vT, km_pad, bias_w,
                                batch=batch, seq=seq)

        w_r = jnp.concatenate([w_router_group[l], w_router_expert[l]], axis=1)
        w_r = jnp.pad(w_r, ((0, 0), (0, LANES - w_r.shape[1])))
        b_r = jnp.concatenate([b_router_group[l], b_router_expert[l]])
        b_r = jnp.pad(b_r, (0, LANES - b_r.shape[0]))[None]
        wr_hi, wr_lo = _split_bf16(w_r)
        x1, h2r, route, route_t, counts, xz = _post_attention(
            xc, mod3, h1, w_in_b, b_gate[l][None], attnT, u, pool_w[l].astype(BF16),
            pool_scale[l][None], w_branch_attn[l].astype(BF16), w_branch_pool[l].astype(BF16),
            w_out[l].astype(BF16), norm2_g[l][None], wr_hi, wr_lo, b_r, batch=batch, seq=seq)

        pos, tile_expert, n_used = _routing_plan(route_t, counts, n=n)
        xs = _dispatch(pos, h2r, xz, n=n)
        y = _experts(tile_expert, n_used, xs, w_expert_gate[l], w_expert_up[l], w_expert_down[l])
        xc = _final(pos, x1, route, mod3, norm_f_g[None], y, seq=seq, final_norm=(l == depth - 1))
    return xc.reshape(batch, seq, d)
```

```python
import functools
import math

import numpy as np
import jax
import jax.numpy as jnp
from jax import lax
from jax.experimental import pallas as pl
from jax.experimental.pallas import tpu as pltpu

F32 = jnp.float32
BF16 = jnp.bfloat16

D_MODEL = 1024
N_HEADS = 8
HEAD_DIM = 64
ATTN_WIDTH = N_HEADS * HEAD_DIM
MOBA_BLOCK = 256
MOBA_TOPK = 3
MAX_BLOCKS = 32
FAR_GROUP = 4
HEADS_PER_STEP = 4
ONES_ROWS = 16
LOG2E = math.log2(math.e)
POOL_WINDOWS = (2, 4, 8, 16)
POOL_WIDTH = 512
POOL_GROUP_WIDTH = 128
POOL_HALO = 16
REL_BUCKETS = 32
REL_MAX_DIST = 128
N_GROUPS = 4
EXPERTS_PER_GROUP = 8
N_EXPERTS = N_GROUPS * EXPERTS_PER_GROUP
D_EXPERT = 512
N_MOD = 6
EPS = 1e-6
NEG_INF = -1e30
LANES = 128
ROUTER_LANE0 = N_GROUPS
ROUTE_FIELDS = 8
ROUTER_ROWS = 48
VMEM_LIMIT = 56 * 1024 * 1024

TM_PROJ = 1024
TM_POST = 512
TM_DISPATCH = 2048
TE_ROWS = 512
EXPERT_TOPK = 2
TM_FINAL = 256


def _cparams(*sem):
    return pltpu.CompilerParams(dimension_semantics=sem, vmem_limit_bytes=VMEM_LIMIT)


def _dot(a, b):
    return jnp.dot(a, b, preferred_element_type=F32)


def _rms_mod(x, g, scale, shift):
    xn = x * lax.rsqrt(jnp.mean(x * x, axis=-1, keepdims=True) + EPS)
    return (xn * g) * (1.0 + scale) + shift


def _mod_kernel(c_ref, w_ref, b_ref, o_ref):
    c = c_ref[...]
    ca = c * jax.nn.sigmoid(c)
    o_ref[...] = _dot(ca.astype(BF16), w_ref[...].astype(BF16)) + b_ref[...]


def _modulation(c_pad, w_ada, b_ada):
    rows, d = c_pad.shape
    n_out = w_ada.shape[1]
    tn = 1024
    return pl.pallas_call(
        _mod_kernel,
        grid=(n_out // tn,),
        in_specs=[pl.BlockSpec((rows, d), lambda j: (0, 0)),
                  pl.BlockSpec((d, tn), lambda j: (0, j)),
                  pl.BlockSpec((1, tn), lambda j: (0, j))],
        out_specs=pl.BlockSpec((rows, tn), lambda j: (0, j)),
        out_shape=jax.ShapeDtypeStruct((rows, n_out), F32),
        compiler_params=_cparams("arbitrary"),
        name="adaln_mod",
    )(c_pad, w_ada, b_ada)


def _inproj_kernel(x_ref, mod_ref, g_ref, w_ref, qT_ref, k_ref, vT_ref, u_ref, km_ref, h_ref, *, tm):
    d = D_MODEL
    mod = mod_ref[0]
    h = _rms_mod(x_ref[...], g_ref[...], mod[:, d:2 * d], mod[:, 0:d]).astype(BF16)
    h_ref[...] = h
    proj = _dot(h, w_ref[...])
    aw = ATTN_WIDTH
    q = proj[:, 0:aw] * (HEAD_DIM ** -0.5 * LOG2E)
    k = proj[:, aw:2 * aw]
    v = proj[:, 2 * aw:3 * aw]
    u_ref[...] = proj[:, 3 * aw:3 * aw + POOL_WIDTH]
    k_ref[...] = k.astype(BF16)
    qT = q.T.astype(BF16)
    vT = v.T.astype(BF16)
    for c in range(tm // MOBA_BLOCK):
        sl = slice(c * MOBA_BLOCK, (c + 1) * MOBA_BLOCK)
        qT_ref[0, c] = qT[:, sl]
        vT_ref[0, c] = vT[:, sl]
        km_ref[0, c:c + 1, :] = jnp.mean(k[sl, :], axis=0, keepdims=True)


def _in_projection(x2, mod3, g1, w_qkvu, *, batch, seq):
    n, d = x2.shape
    tm = TM_PROJ
    tps = seq // tm
    nb = seq // MOBA_BLOCK
    cb = tm // MOBA_BLOCK
    wcols = 3 * ATTN_WIDTH + POOL_WIDTH
    assert w_qkvu.shape[1] % wcols == 0
    kern = functools.partial(_inproj_kernel, tm=tm)
    t_shape = jax.ShapeDtypeStruct((batch, nb, ATTN_WIDTH, MOBA_BLOCK), BF16)
    t_spec = pl.BlockSpec((1, cb, ATTN_WIDTH, MOBA_BLOCK), lambda i: (i // tps, i % tps, 0, 0))
    return pl.pallas_call(
        kern,
        grid=(n // tm,),
        in_specs=[pl.BlockSpec((tm, d), lambda i: (i, 0)),
                  pl.BlockSpec((1, 1, N_MOD * d), lambda i: (i // tps, 0, 0)),
                  pl.BlockSpec((1, d), lambda i: (0, 0)),
                  pl.BlockSpec((d, wcols), lambda i: (0, 0))],
        out_specs=[t_spec,
                   pl.BlockSpec((tm, ATTN_WIDTH), lambda i: (i, 0)),
                   t_spec,
                   pl.BlockSpec((tm, POOL_WIDTH), lambda i: (i, 0)),
                   pl.BlockSpec((1, cb, ATTN_WIDTH), lambda i: (i, 0, 0)),
                   pl.BlockSpec((tm, d), lambda i: (i, 0))],
        out_shape=[t_shape,
                   jax.ShapeDtypeStruct((n, ATTN_WIDTH), BF16),
                   t_shape,
                   jax.ShapeDtypeStruct((n, POOL_WIDTH), F32),
                   jax.ShapeDtypeStruct((n // tm, cb, ATTN_WIDTH), F32),
                   jax.ShapeDtypeStruct((n, d), BF16)],
        compiler_params=_cparams("arbitrary"),
        name="norm1_inproj",
    )(x2, mod3, g1, w_qkvu)


def _attn_kernel(qT_ref, k_ref, vT_ref, km_ref, bw_ref, o_ref, kaug_ref, vaug_ref, bias_ref, qnear_ref,
                 qfar_ref, sn_ref, sf0_ref, sf1_ref, *, nb):
    blk = MOBA_BLOCK
    hd = HEAD_DIM
    nh = HEADS_PER_STEP
    heads = range(nh)
    lane = lax.broadcasted_iota(jnp.int32, (blk, LANES), 1)

    def pair_lanes(hh):
        return slice((hh // 2) * LANES, (hh // 2 + 1) * LANES)

    def head_rows(hh):
        return slice(hh * hd, (hh + 1) * hd)

    ones = jnp.ones((ONES_ROWS, blk), BF16)

    def build_operands(j, carry):
        rows = pl.ds(pl.multiple_of(j * blk, blk), blk)
        for hh in heads:
            kp = k_ref[0, rows, pair_lanes(hh)]
            if hh % 2 == 0:
                kaug_ref[hh, rows, :] = jnp.where(lane < hd, kp, jnp.where(lane == hd + j, 1.0, 0.0).astype(BF16))
            else:
                kaug_ref[hh, rows, :] = jnp.where(lane >= hd, kp, jnp.where(lane == j, 1.0, 0.0).astype(BF16))
            vaug_ref[j, hh] = jnp.concatenate([vT_ref[0, j, head_rows(hh), :], ones], axis=0)
        return carry

    lax.fori_loop(0, nb, build_operands, 0)

    for hh in heads:
        for t in range(2):
            wide = jnp.broadcast_to(bw_ref[hh, t:t + 1, :], (blk, 2 * blk))
            bias_ref[hh, t] = pltpu.roll(wide, 0, 1, stride=1, stride_axis=0)[:, blk:2 * blk]

    rowi = lax.broadcasted_iota(jnp.int32, (MAX_BLOCKS, blk), 0)
    km_lane = lax.broadcasted_iota(jnp.int32, (MAX_BLOCKS, LANES), 1)
    zpad = jnp.zeros((LANES - hd - MAX_BLOCKS, blk), BF16)
    km_split = []
    for hh in heads:
        sub = hh % 2
        in_head = (km_lane >= sub * hd) & (km_lane < (sub + 1) * hd)
        km_h = jnp.where(in_head, km_ref[0, :, pair_lanes(hh)], 0.0)
        km_hi = km_h.astype(BF16)
        km_split.append((km_hi, (km_h - km_hi.astype(F32)).astype(BF16)))

    grp = FAR_GROUP

    def retire_begin(state, mx):
        m, acc = state
        m_new = jnp.maximum(m, mx)
        return m_new, jnp.exp2(m - m_new) * acc

    def retire_slice(src_ref, hh, m_new, acc, g, j):
        p = jnp.exp2(src_ref[hh, g * blk:(g + 1) * blk, :] - m_new)
        return acc + _dot(vaug_ref[j, hh], p.astype(BF16))

    def fresh():
        return (jnp.full((1, blk), -jnp.inf, F32), jnp.zeros((hd + ONES_ROWS, blk), F32))

    def near_blocks(i):
        return tuple(jnp.where(i - (grp - 1) + g < 0, i + 1, i - (grp - 1) + g) for g in range(grp))

    def finish(i, states):
        for hh in heads:
            _, acc = states[hh]
            o_ref[0, i, head_rows(hh), :] = (acc[0:hd] / acc[hd:hd + 1]).astype(BF16)

    def select(i):
        for hh in heads:
            qp = qT_ref[0, i, pair_lanes(hh), :]
            km_hi, km_lo = km_split[hh]
            gate = _dot(km_hi, qp) + _dot(km_lo, qp)
            g = jnp.where(rowi < i, gate, -jnp.inf)
            sel = rowi == i
            for _ in range(MOBA_TOPK):
                m = jnp.max(g, axis=0, keepdims=True)
                hit = (g == m) & (m > -jnp.inf)
                idx = jnp.min(jnp.where(hit, rowi, MAX_BLOCKS), axis=0, keepdims=True)
                pick = rowi == idx
                sel = sel | pick
                g = jnp.where(pick, -jnp.inf, g)
            near = jnp.where(sel, 0.0, NEG_INF)
            far = jnp.where(rowi > i - grp, NEG_INF, near)
            qh = qp[head_rows(hh % 2), :]
            if hh % 2 == 0:
                qnear_ref[hh] = jnp.concatenate([qh, near.astype(BF16), zpad], axis=0)
                qfar_ref[hh] = jnp.concatenate([qh, far.astype(BF16), zpad], axis=0)
            else:
                qnear_ref[hh] = jnp.concatenate([near.astype(BF16), zpad, qh], axis=0)
                qfar_ref[hh] = jnp.concatenate([far.astype(BF16), zpad, qh], axis=0)

    def park(dst_ref, q_ref, hh, g, j, bias=None):
        rows = pl.ds(pl.multiple_of(j * blk, blk), blk)
        s = _dot(kaug_ref[hh, rows, :], q_ref[hh])
        if bias is not None:
            s = s + bias
        dst_ref[hh, g * blk:(g + 1) * blk, :] = s
        return jnp.max(s, axis=0, keepdims=True)

    def park_near(i):
        idx = near_blocks(i)

        def quarter(g, hh):
            bias = None
            if g == grp - 1:
                bias = bias_ref[hh, 0]
            elif g == grp - 2:
                bias = bias_ref[hh, 1]
            return park(sn_ref, qnear_ref, hh, g, idx[g], bias)
        return quarter

    def park_far(c, dst_ref):
        return lambda g, hh: park(dst_ref, qfar_ref, hh, g, c * grp + g)

    def chunk_blocks(c):
        return tuple(c * grp + g for g in range(grp))

    def step(src_ref, states, maxes, vidx, quarter, park_first):
        begun = [retire_begin(states[hh], maxes[hh]) for hh in heads] if states is not None else None
        accs = [b[1] for b in begun] if begun else None
        new_max = [None] * nh

        def park_quarter(g):
            for hh in heads:
                cm = quarter(g, hh)
                new_max[hh] = cm if new_max[hh] is None else jnp.maximum(new_max[hh], cm)

        for g in range(grp):
            if park_first:
                park_quarter(g)
            if begun:
                for hh in heads:
                    accs[hh] = retire_slice(src_ref, hh, begun[hh][0], accs[hh], g, vidx[g])
            if not park_first and quarter is not None:
                park_quarter(g)
        new_states = tuple((begun[hh][0], accs[hh]) for hh in heads) if begun else None
        return new_states, tuple(new_max)

    def fresh_states():
        return tuple(fresh() for _ in heads)

    def early_tile(i, near_max):
        nxt = jnp.minimum(i + 1, nb - 1)
        select(nxt)
        states, next_max = step(sn_ref, fresh_states(), near_max, near_blocks(i), park_near(nxt), False)
        finish(i, states)
        return next_max

    def late_tile(i, near_max):
        sf = (sf0_ref, sf1_ref)
        n_chunks = (i - (grp - 1) + grp - 1) // grp
        nxt = jnp.minimum(i + 1, nb - 1)
        states, far_max = step(sn_ref, fresh_states(), near_max, near_blocks(i), park_far(0, sf[0]), True)

        def pair(k, st):
            states, far_max = st
            states, far_max = step(sf[0], states, far_max, chunk_blocks(2 * k), park_far(2 * k + 1, sf[1]), True)
            return step(sf[1], states, far_max, chunk_blocks(2 * k + 1), park_far(2 * k + 2, sf[0]), True)

        states, far_max = lax.fori_loop(0, (n_chunks - 1) // 2, pair, (states, far_max))

        def last(src, states, far_max):
            select(nxt)
            states, next_max = step(src, states, far_max, chunk_blocks(n_chunks - 1), park_near(nxt), True)
            finish(i, states)
            return next_max

        def even_count(st):
            states, far_max = step(sf[0], st[0], st[1], chunk_blocks(n_chunks - 2),
                                   park_far(n_chunks - 1, sf[1]), True)
            return last(sf[1], states, far_max)

        def odd_count(st):
            return last(sf[0], st[0], st[1])

        return lax.cond(n_chunks % 2 == 0, even_count, odd_count, (states, far_max))

    n_early = min(grp, nb)
    select(0)
    _, near_max = step(None, None, None, None, park_near(0), True)
    near_max = lax.fori_loop(0, n_early, early_tile, near_max)
    lax.fori_loop(n_early, nb, late_tile, near_max)


def _rel_bucket_table(max_rel):
    n = np.arange(max_rel)
    max_exact = REL_BUCKETS // 2
    nf = np.maximum(n, 1).astype(np.float32)
    large = max_exact + (np.log(nf / np.float32(max_exact)) / np.float32(math.log(REL_MAX_DIST / max_exact))
                         * np.float32(REL_BUCKETS - max_exact)).astype(np.int32)
    large = np.minimum(large, REL_BUCKETS - 1)
    return np.where(n < max_exact, n, large)


def _bias_vectors(rel_bias):
    blk = MOBA_BLOCK
    table = _rel_bucket_table(2 * blk)
    r = np.arange(-blk, blk)
    bt = (rel_bias.T - rel_bias[REL_BUCKETS - 1][:, None]) * LOG2E
    own = jnp.where(jnp.asarray(r >= 0)[None], bt[:, table[np.maximum(r, 0)]], NEG_INF)
    prev = bt[:, table[np.minimum(r + blk, 2 * blk - 1)]]
    return jnp.stack([own, prev], axis=1).astype(F32)


def _moba_attention(qT, k3, vT, km_pad, bias_w, *, batch, seq):
    nb = seq // MOBA_BLOCK
    blk = MOBA_BLOCK
    assert nb % FAR_GROUP == 0 and nb <= MAX_BLOCKS
    kern = functools.partial(_attn_kernel, nb=nb)
    nh = HEADS_PER_STEP
    width = nh * HEAD_DIM
    assert nh % 2 == 0 and N_HEADS % nh == 0

    def big(shape, index_map):
        return pl.BlockSpec(shape, index_map, pipeline_mode=pl.Buffered(1))

    return pl.pallas_call(
        kern,
        grid=(batch, N_HEADS // nh),
        in_specs=[big((1, nb, width, blk), lambda b, p: (b, 0, p, 0)),
                  big((1, seq, width), lambda b, p: (b, 0, p)),
                  big((1, nb, width, blk), lambda b, p: (b, 0, p, 0)),
                  pl.BlockSpec((1, MAX_BLOCKS, width), lambda b, p: (b, 0, p)),
                  pl.BlockSpec((nh, 2, 2 * blk), lambda b, p: (p, 0, 0))],
        out_specs=pl.BlockSpec((1, nb, width, blk), lambda b, p: (b, 0, p, 0)),
        out_shape=jax.ShapeDtypeStruct((batch, nb, ATTN_WIDTH, blk), BF16),
        scratch_shapes=[pltpu.VMEM((nh, seq, LANES), BF16),
                        pltpu.VMEM((nb, nh, HEAD_DIM + ONES_ROWS, blk), BF16),
                        pltpu.VMEM((nh, 2, blk, blk), F32),
                        pltpu.VMEM((nh, LANES, blk), BF16),
                        pltpu.VMEM((nh, LANES, blk), BF16),
                        pltpu.VMEM((nh, FAR_GROUP * blk, blk), F32),
                        pltpu.VMEM((nh, FAR_GROUP * blk, blk), F32),
                        pltpu.VMEM((nh, FAR_GROUP * blk, blk), F32)],
        compiler_params=_cparams("arbitrary", "arbitrary"),
        name="moba_attention",
    )(qT, k3, vT, km_pad, bias_w)


def _post_kernel(x_ref, mod_ref, h_ref, wz_ref, bz_ref, at_ref, u_ref, uh_ref, pw_ref, ps_ref,
                 wba_ref, wbp_ref, wo_ref, g2_ref, wrh_ref, wrl_ref, br_ref,
                 x1_ref, h2r_ref, route_ref, route_t_ref, counts_ref, xz_ref, carry_ref, *, tm, seq):
    d = D_MODEL
    i = pl.program_id(0)
    x = x_ref[...]
    mod = mod_ref[0]
    gate1 = mod[:, 2 * d:3 * d]
    shift2, scale2 = mod[:, 3 * d:4 * d], mod[:, 4 * d:5 * d]

    zg = 0.5 * jnp.tanh(0.5 * (_dot(h_ref[...], wz_ref[...]) + bz_ref[...])) + 0.5

    ya = jnp.concatenate(
        [lax.dot_general(at_ref[0, c], wba_ref[...], (((0,), (0,)), ((), ())),
                         preferred_element_type=F32) for c in range(tm // MOBA_BLOCK)], axis=0)

    first = (i * tm) % seq == 0
    halo = jnp.where(first, 0.0, uh_ref[...])
    ubuf = jnp.concatenate([halo, u_ref[...]], axis=0)
    rows = tm + POOL_HALO
    pos = (lax.broadcasted_iota(jnp.int32, (tm, POOL_GROUP_WIDTH), 0) + i * tm) % seq
    outs = []
    for gi, w in enumerate(POOL_WINDOWS):
        e = ubuf[:, gi * POOL_GROUP_WIDTH:(gi + 1) * POOL_GROUP_WIDTH]
        s = e
        step = 1
        while step < w:
            s = s + pltpu.roll(s, step, 0)
            step *= 2
        cnt = jnp.minimum(pos + 1, w).astype(F32)
        dlt = s[POOL_HALO:rows] / cnt - e[POOL_HALO:rows]
        outs.append(_dot(dlt.astype(BF16), pw_ref[gi]))
    pooled = jnp.concatenate(outs, axis=-1) * ps_ref[...]
    yp = _dot(pooled.astype(BF16), wbp_ref[...])

    merged = zg[:, 0:d] * ya + zg[:, d:2 * d] * yp
    x1 = x + gate1 * _dot(merged.astype(BF16), wo_ref[...])
    x1_ref[...] = x1

    h2 = _rms_mod(x1, g2_ref[...], scale2, shift2)
    h2_hi = h2.astype(BF16)
    h2r_ref[...] = h2_hi.astype(F32)
    h2_lo = (h2 - h2_hi.astype(F32)).astype(BF16)
    z = _dot(h2_hi, wrh_ref[...]) + _dot(h2_lo, wrh_ref[...]) + _dot(h2_hi, wrl_ref[...]) + br_ref[...]

    rr = ROUTER_ROWS
    zt = z.T[0:rr, :]
    row = lax.broadcasted_iota(jnp.int32, (rr, tm), 0)
    zgrp = jnp.where(row < N_GROUPS, zt, -jnp.inf)
    mg = jnp.max(zgrp, axis=0, keepdims=True)
    pg_top = 1.0 / jnp.sum(jnp.exp(zgrp - mg), axis=0, keepdims=True)
    g_idx = jnp.min(jnp.where(zgrp == mg, row, rr), axis=0, keepdims=True)
    e_row = row - ROUTER_LANE0
    in_grp = (e_row >= 0) & (e_row < N_EXPERTS) & ((e_row >> 3) == g_idx)
    ze = jnp.where(in_grp, zt, -jnp.inf)
    m1 = jnp.max(ze, axis=0, keepdims=True)
    i1 = jnp.min(jnp.where(ze == m1, row, rr), axis=0, keepdims=True)
    ze2 = jnp.where(row == i1, -jnp.inf, ze)
    m2 = jnp.max(ze2, axis=0, keepdims=True)
    i2 = jnp.min(jnp.where(ze2 == m2, row, rr), axis=0, keepdims=True)
    e2 = jnp.exp(m2 - m1)
    w1 = pg_top / (1.0 + e2)
    w2 = pg_top * e2 / (1.0 + e2)

    hot = jnp.where((row == i1) | (row == i2), 1.0, 0.0)
    tri = jnp.where(lax.broadcasted_iota(jnp.int32, (tm, tm), 0) <= lax.broadcasted_iota(jnp.int32, (tm, tm), 1),
                    1.0, 0.0).astype(BF16)
    prefix = _dot(hot.astype(BF16), tri)

    @pl.when(i == 0)
    def _():
        carry_ref[...] = jnp.zeros_like(carry_ref)

    carry = carry_ref[...]
    base = carry[:, 0:1] + prefix - 1.0
    rank1 = jnp.sum(jnp.where(row == i1, base, 0.0), axis=0, keepdims=True)
    rank2 = jnp.sum(jnp.where(row == i2, base, 0.0), axis=0, keepdims=True)
    carry_ref[...] = carry + prefix[:, tm - 1:tm]
    counts_ref[...] = carry_ref[...]
    fields = ((i1 - ROUTER_LANE0).astype(F32), (i2 - ROUTER_LANE0).astype(F32), w1, w2, rank1, rank2)
    frow = lax.broadcasted_iota(jnp.int32, (ROUTE_FIELDS, tm), 0)
    route_t = jnp.zeros((ROUTE_FIELDS, tm), F32)
    for k, val in enumerate(fields):
        route_t = jnp.where(frow == k, val, route_t)
    route_t_ref[...] = route_t
    route_ref[...] = jnp.concatenate([route_t, jnp.zeros((LANES - ROUTE_FIELDS, tm), F32)], axis=0).T
    xz_ref[...] = jnp.zeros_like(xz_ref)


def _post_attention(x2, mod3, h1, w_z, b_z, attnT, u, pool_w, pool_scale, w_ba, w_bp, w_out, g2,
                    wr_hi, wr_lo, b_r, *, batch, seq):
    n, d = x2.shape
    tm = TM_POST
    tps = seq // tm
    cb = tm // MOBA_BLOCK
    hb = tm // POOL_HALO
    sorted_rows = _expert_tiles(n) * TE_ROWS
    assert sorted_rows % (n // tm) == 0
    zrows = sorted_rows // (n // tm)
    kern = functools.partial(_post_kernel, tm=tm, seq=seq)
    const2 = lambda i: (0, 0)
    return pl.pallas_call(
        kern,
        grid=(n // tm,),
        in_specs=[pl.BlockSpec((tm, d), lambda i: (i, 0)),
                  pl.BlockSpec((1, 1, N_MOD * d), lambda i: (i // tps, 0, 0)),
                  pl.BlockSpec((tm, d), lambda i: (i, 0)),
                  pl.BlockSpec((d, 2 * d), lambda i: (0, 1)),
                  pl.BlockSpec((1, 2 * d), const2),
                  pl.BlockSpec((1, cb, ATTN_WIDTH, MOBA_BLOCK), lambda i: (i // tps, i % tps, 0, 0)),
                  pl.BlockSpec((tm, POOL_WIDTH), lambda i: (i, 0)),
                  pl.BlockSpec((POOL_HALO, POOL_WIDTH), lambda i: (jnp.maximum(i * hb - 1, 0), 0)),
                  pl.BlockSpec((len(POOL_WINDOWS), POOL_GROUP_WIDTH, POOL_GROUP_WIDTH), lambda i: (0, 0, 0)),
                  pl.BlockSpec((1, POOL_WIDTH), const2),
                  pl.BlockSpec((ATTN_WIDTH, d), const2),
                  pl.BlockSpec((POOL_WIDTH, d), const2),
                  pl.BlockSpec((d, d), const2),
                  pl.BlockSpec((1, d), const2),
                  pl.BlockSpec((d, LANES), const2),
                  pl.BlockSpec((d, LANES), const2),
                  pl.BlockSpec((1, LANES), const2)],
        out_specs=[pl.BlockSpec((tm, d), lambda i: (i, 0)),
                   pl.BlockSpec((tm, d), lambda i: (i, 0)),
                   pl.BlockSpec((tm, LANES), lambda i: (i, 0)),
                   pl.BlockSpec((ROUTE_FIELDS, tm), lambda i: (0, i)),
                   pl.BlockSpec((ROUTER_ROWS, LANES), const2),
                   pl.BlockSpec((zrows, d), lambda i: (i, 0))],
        out_shape=[jax.ShapeDtypeStruct((n, d), F32),
                   jax.ShapeDtypeStruct((n, d), F32),
                   jax.ShapeDtypeStruct((n, LANES), F32),
                   jax.ShapeDtypeStruct((ROUTE_FIELDS, n), F32),
                   jax.ShapeDtypeStruct((ROUTER_ROWS, LANES), F32),
                   jax.ShapeDtypeStruct((zrows * (n // tm), d), F32)],
        scratch_shapes=[pltpu.VMEM((ROUTER_ROWS, LANES), F32)],
        compiler_params=_cparams("arbitrary"),
        name="merge_outproj_router",
    )(x2, mod3, h1, w_z, b_z, attnT, u, u, pool_w, pool_scale, w_ba, w_bp, w_out, g2, wr_hi, wr_lo, b_r)


def _routing_plan(route_t, counts, *, n):
    te = TE_ROWS
    n_tiles = _expert_tiles(n)
    cnt = counts[ROUTER_LANE0:ROUTER_LANE0 + N_EXPERTS, 0].astype(jnp.int32)
    padded = ((cnt + te - 1) // te) * te
    ids = jnp.arange(N_EXPERTS, dtype=jnp.int32)
    ends = jnp.sum(jnp.where(ids[None, :] <= ids[:, None], padded[None, :], 0), axis=-1)
    starts = ends - padded
    pos = []
    for k in range(EXPERT_TOPK):
        expert = route_t[k].astype(jnp.int32)
        rank = route_t[4 + k].astype(jnp.int32)
        pos.append(jnp.sum(jnp.where(expert[:, None] == ids, starts, 0), axis=-1) + rank)
    tiles = jnp.arange(n_tiles, dtype=jnp.int32)
    tile_expert = jnp.sum((ends // te)[None, :] <= tiles[:, None], axis=-1)
    tile_expert = jnp.minimum(tile_expert, N_EXPERTS - 1).astype(jnp.int32)
    n_used = (ends[N_EXPERTS - 1:] // te).astype(jnp.int32)
    return pos, tile_expert, n_used


def _expert_tiles(n):
    return (EXPERT_TOPK * n) // TE_ROWS + N_EXPERTS


def _tile_pos(pos, tm):
    return jnp.concatenate([p.reshape(-1, 1, tm) for p in pos], axis=-1)


def _dispatch_kernel(pos_ref, h_ref, xz_ref, xs_ref, sem, *, tm):
    del xz_ref
    for r in range(tm):
        for k in range(EXPERT_TOPK):
            pltpu.make_async_copy(h_ref.at[pl.ds(r, 1)], xs_ref.at[pl.ds(pos_ref[0, 0, k * tm + r], 1)],
                                  sem).start(priority=(r + k) % 2)
    for k in range(EXPERT_TOPK):
        pltpu.make_async_copy(h_ref, h_ref, sem).wait()


def _dispatch(pos, h2r, xz, *, n):
    tm = TM_DISPATCH
    rows, d = xz.shape
    assert rows == _expert_tiles(n) * TE_ROWS
    pos3 = _tile_pos(pos, tm)
    return pl.pallas_call(
        functools.partial(_dispatch_kernel, tm=tm),
        grid=(n // tm,),
        in_specs=[pl.BlockSpec((1, 1, EXPERT_TOPK * tm), lambda i: (i, 0, 0), memory_space=pltpu.SMEM),
                  pl.BlockSpec((tm, d), lambda i: (i, 0)),
                  pl.BlockSpec(memory_space=pl.ANY)],
        out_specs=pl.BlockSpec(memory_space=pl.ANY),
        out_shape=jax.ShapeDtypeStruct((rows, d), F32),
        scratch_shapes=[pltpu.SemaphoreType.DMA(())],
        input_output_aliases={2: 0},
        compiler_params=_cparams("arbitrary"),
        name="moe_dispatch",
    )(pos3, h2r, xz)


def _experts_kernel(te_ref, nu_ref, xs_ref, wg_ref, wu_ref, wd_ref, y_ref, wg_b, wu_b, wd_b):
    t = pl.program_id(0)

    @pl.when(t < nu_ref[0])
    def _():
        e = te_ref[t]
        e_prev = te_ref[jnp.maximum(t - 1, 0)]

        @pl.when((t == 0) | (e != e_prev))
        def _():
            wg_b[...] = wg_ref[0].astype(BF16)
            wu_b[...] = wu_ref[0].astype(BF16)
            wd_b[...] = wd_ref[0].astype(BF16)

        x = xs_ref[...].astype(BF16)
        g = _dot(x, wg_b[...])
        u = _dot(x, wu_b[...])
        act = (g * jax.nn.sigmoid(g)) * u
        y_ref[...] = _dot(act.astype(BF16), wd_b[...])

    @pl.when(t >= nu_ref[0])
    def _():
        y_ref[...] = jnp.zeros_like(y_ref)


def _experts(tile_expert, n_used, xs, wg, wu, wd):
    rows, d = xs.shape
    te = TE_ROWS

    def used(t, nu):
        return jnp.minimum(t, nu[0] - 1)

    grid_spec = pltpu.PrefetchScalarGridSpec(
        num_scalar_prefetch=2,
        grid=(rows // te,),
        in_specs=[pl.BlockSpec((te, d), lambda t, te_r, nu: (used(t, nu), 0)),
                  pl.BlockSpec((1, d, D_EXPERT), lambda t, te_r, nu: (te_r[used(t, nu)], 0, 0)),
                  pl.BlockSpec((1, d, D_EXPERT), lambda t, te_r, nu: (te_r[used(t, nu)], 0, 0)),
                  pl.BlockSpec((1, D_EXPERT, d), lambda t, te_r, nu: (te_r[used(t, nu)], 0, 0))],
        out_specs=pl.BlockSpec((te, d), lambda t, te_r, nu: (t, 0)),
        scratch_shapes=[pltpu.VMEM((d, D_EXPERT), BF16), pltpu.VMEM((d, D_EXPERT), BF16),
                        pltpu.VMEM((D_EXPERT, d), BF16)])
    return pl.pallas_call(
        _experts_kernel,
        grid_spec=grid_spec,
        out_shape=jax.ShapeDtypeStruct((rows, d), F32),
        compiler_params=_cparams("arbitrary"),
        name="moe_experts",
    )(tile_expert, n_used, xs, wg, wu, wd)


def _final_kernel(pos_ref, pos_next_ref, x1_ref, route_ref, mod_ref, g_ref, y_ref, o_ref, ybuf, sems, *,
                  tm, final_norm):
    d = D_MODEL
    i = pl.program_id(0)
    slot = i % 2

    def gather(p_ref, s):
        for r in range(tm):
            for k in range(EXPERT_TOPK):
                pltpu.make_async_copy(y_ref.at[pl.ds(p_ref[0, 0, k * tm + r], 1)],
                                      ybuf.at[s, k, pl.ds(r, 1)], sems.at[s]).start(priority=(r + k) % 2)

    @pl.when(i == 0)
    def _():
        gather(pos_ref, 0)

    @pl.when(i + 1 < pl.num_programs(0))
    def _():
        gather(pos_next_ref, 1 - slot)

    for k in range(EXPERT_TOPK):
        pltpu.make_async_copy(ybuf.at[slot, k], ybuf.at[slot, k], sems.at[slot]).wait()

    route = route_ref[...]
    lane = lax.broadcasted_iota(jnp.int32, route.shape, 1)
    y = jnp.zeros((tm, d), F32)
    for k in range(EXPERT_TOPK):
        wk = jnp.sum(jnp.where(lane == 2 + k, route, 0.0), axis=-1, keepdims=True)
        y = y + wk * ybuf[slot, k]
    gate2 = mod_ref[0][:, 5 * d:6 * d]
    x = x1_ref[...] + gate2 * y
    if final_norm:
        x = (x * lax.rsqrt(jnp.mean(x * x, axis=-1, keepdims=True) + EPS)) * g_ref[...]
    o_ref[...] = x


def _final(pos, x1, route, mod3, gf, y, *, seq, final_norm):
    n, d = x1.shape
    tm = TM_FINAL
    tps = seq // tm
    pos3 = _tile_pos(pos, tm)
    last = n // tm - 1
    return pl.pallas_call(
        functools.partial(_final_kernel, tm=tm, final_norm=final_norm),
        grid=(n // tm,),
        in_specs=[pl.BlockSpec((1, 1, EXPERT_TOPK * tm), lambda i: (i, 0, 0), memory_space=pltpu.SMEM),
                  pl.BlockSpec((1, 1, EXPERT_TOPK * tm), lambda i: (jnp.minimum(i + 1, last), 0, 0),
                               memory_space=pltpu.SMEM),
                  pl.BlockSpec((tm, d), lambda i: (i, 0)),
                  pl.BlockSpec((tm, LANES), lambda i: (i, 0)),
                  pl.BlockSpec((1, 1, N_MOD * d), lambda i: (i // tps, 0, 0)),
                  pl.BlockSpec((1, d), lambda i: (0, 0)),
                  pl.BlockSpec(memory_space=pl.ANY)],
        out_specs=pl.BlockSpec((tm, d), lambda i: (i, 0)),
        out_shape=jax.ShapeDtypeStruct((n, d), F32),
        scratch_shapes=[pltpu.VMEM((2, EXPERT_TOPK, tm, d), F32), pltpu.SemaphoreType.DMA((2,))],
        compiler_params=_cparams("arbitrary"),
        name="moe_combine_final_norm",
    )(pos3, pos3, x1, route, mod3, gf, y)


def _split_bf16(w):
    hi = w.astype(BF16)
    return hi, (w - hi.astype(F32)).astype(BF16)


def kernel(x, c, w_ada, b_ada, norm1_g, w_in, b_gate, rel_bias, pool_w, pool_scale, w_branch_attn,
           w_branch_pool, w_out, norm2_g, w_router_group, b_router_group, w_router_expert,
           b_router_expert, w_expert_gate, w_expert_up, w_expert_down, norm_f_g):
    batch, seq, d = x.shape
    depth = w_ada.shape[0]
    n = batch * seq
    nb = seq // MOBA_BLOCK
    qkvu = 3 * ATTN_WIDTH + POOL_WIDTH
    bias_w = _bias_vectors(rel_bias)
    c_pad = jnp.pad(c, ((0, 8 - batch), (0, 0)))

    xc = x.reshape(n, d)
    for l in range(depth):
        mod3 = _modulation(c_pad, w_ada[l], b_ada[l][None])[:batch].reshape(batch, 1, N_MOD * d)
        w_in_b = w_in[l].astype(BF16)
        g1 = norm1_g[l][None]

        assert w_in_b.shape[1] == 2 * qkvu == qkvu + 2 * d
        qT, k2, vT, u, km, h1 = _in_projection(xc, mod3, g1, w_in_b, batch=batch, seq=seq)
        km_pad = jnp.pad(km.reshape(batch, nb, ATTN_WIDTH), ((0, 0), (0, MAX_BLOCKS - nb), (0, 0)))
        attnT = _moba_attention(qT, k2.reshape(batch, seq, ATTN_WIDTH), vT, km_pad, bias_w,
                                batch=batch, seq=seq)

        w_r = jnp.concatenate([w_router_group[l], w_router_expert[l]], axis=1)
        w_r = jnp.pad(w_r, ((0, 0), (0, LANES - w_r.shape[1])))
        b_r = jnp.concatenate([b_router_group[l], b_router_expert[l]])
        b_r = jnp.pad(b_r, (0, LANES - b_r.shape[0]))[None]
        wr_hi, wr_lo = _split_bf16(w_r)
        x1, h2r, route, route_t, counts, xz = _post_attention(
            xc, mod3, h1, w_in_b, b_gate[l][None], attnT, u, pool_w[l].astype(BF16),
            pool_scale[l][None], w_branch_attn[l].astype(BF16), w_branch_pool[l].astype(BF16),
            w_out[l].astype(BF16), norm2_g[l][None], wr_hi, wr_lo, b_r, batch=batch, seq=seq)

        pos, tile_expert, n_used = _routing_plan(route_t, counts, n=n)
        xs = _dispatch(pos, h2r, xz, n=n)
        y = _experts(tile_expert, n_used, xs, w_expert_gate[l], w_expert_up[l], w_expert_down[l])
        xc = _final(pos, x1, route, mod3, norm_f_g[None], y, seq=seq, final_norm=(l == depth - 1))
    return xc.reshape(batch, seq, d)
```

```python
import functools
import math

import numpy as np
import jax
import jax.numpy as jnp
from jax import lax
from jax.experimental import pallas as pl
from jax.experimental.pallas import tpu as pltpu

F32 = jnp.float32
BF16 = jnp.bfloat16

D_MODEL = 1024
N_HEADS = 8
HEAD_DIM = 64
ATTN_WIDTH = N_HEADS * HEAD_DIM
MOBA_BLOCK = 256
MOBA_TOPK = 3
MAX_BLOCKS = 32
FAR_GROUP = 4
HEADS_PER_STEP = 4
ONES_ROWS = 16
LOG2E = math.log2(math.e)
POOL_WINDOWS = (2, 4, 8, 16)
POOL_WIDTH = 512
POOL_GROUP_WIDTH = 128
POOL_HALO = 16
REL_BUCKETS = 32
REL_MAX_DIST = 128
N_GROUPS = 4
EXPERTS_PER_GROUP = 8
N_EXPERTS = N_GROUPS * EXPERTS_PER_GROUP
D_EXPERT = 512
N_MOD = 6
EPS = 1e-6
NEG_INF = -1e30
LANES = 128
ROUTER_LANE0 = N_GROUPS
ROUTE_FIELDS = 8
ROUTER_ROWS = 48
VMEM_LIMIT = 56 * 1024 * 1024

TM_PROJ = 1024
TM_POST = 512
TM_DISPATCH = 2048
TE_ROWS = 512
EXPERT_TOPK = 2
TM_FINAL = 256


def _cparams(*sem):
    return pltpu.CompilerParams(dimension_semantics=sem, vmem_limit_bytes=VMEM_LIMIT)


def _dot(a, b):
    return jnp.dot(a, b, preferred_element_type=F32)


def _rms_mod(x, g, scale, shift):
    xn = x * lax.rsqrt(jnp.mean(x * x, axis=-1, keepdims=True) + EPS)
    return (xn * g) * (1.0 + scale) + shift


def _mod_kernel(c_ref, w_ref, b_ref, o_ref):
    c = c_ref[...]
    ca = c * jax.nn.sigmoid(c)
    o_ref[...] = _dot(ca.astype(BF16), w_ref[...].astype(BF16)) + b_ref[...]


def _modulation(c_pad, w_ada, b_ada):
    rows, d = c_pad.shape
    n_out = w_ada.shape[1]
    tn = 1024
    return pl.pallas_call(
        _mod_kernel,
        grid=(n_out // tn,),
        in_specs=[pl.BlockSpec((rows, d), lambda j: (0, 0)),
                  pl.BlockSpec((d, tn), lambda j: (0, j)),
                  pl.BlockSpec((1, tn), lambda j: (0, j))],
        out_specs=pl.BlockSpec((rows, tn), lambda j: (0, j)),
        out_shape=jax.ShapeDtypeStruct((rows, n_out), F32),
        compiler_params=_cparams("arbitrary"),
        name="adaln_mod",
    )(c_pad, w_ada, b_ada)


def _inproj_kernel(x_ref, mod_ref, g_ref, w_ref, qT_ref, k_ref, vT_ref, u_ref, km_ref, h_ref, *, tm):
    d = D_MODEL
    mod = mod_ref[0]
    h = _rms_mod(x_ref[...], g_ref[...], mod[:, d:2 * d], mod[:, 0:d]).astype(BF16)
    h_ref[...] = h
    proj = _dot(h, w_ref[...])
    aw = ATTN_WIDTH
    q = proj[:, 0:aw] * (HEAD_DIM ** -0.5 * LOG2E)
    k = proj[:, aw:2 * aw]
    v = proj[:, 2 * aw:3 * aw]
    u_ref[...] = proj[:, 3 * aw:3 * aw + POOL_WIDTH]
    k_ref[...] = k.astype(BF16)
    qT = q.T.astype(BF16)
    vT = v.T.astype(BF16)
    for c in range(tm // MOBA_BLOCK):
        sl = slice(c * MOBA_BLOCK, (c + 1) * MOBA_BLOCK)
        qT_ref[0, c] = qT[:, sl]
        vT_ref[0, c] = vT[:, sl]
        km_ref[0, c:c + 1, :] = jnp.mean(k[sl, :], axis=0, keepdims=True)


def _in_projection(x2, mod3, g1, w_qkvu, *, batch, seq):
    n, d = x2.shape
    tm = TM_PROJ
    tps = seq // tm
    nb = seq // MOBA_BLOCK
    cb = tm // MOBA_BLOCK
    wcols = 3 * ATTN_WIDTH + POOL_WIDTH
    assert w_qkvu.shape[1] % wcols == 0
    kern = functools.partial(_inproj_kernel, tm=tm)
    t_shape = jax.ShapeDtypeStruct((batch, nb, ATTN_WIDTH, MOBA_BLOCK), BF16)
    t_spec = pl.BlockSpec((1, cb, ATTN_WIDTH, MOBA_BLOCK), lambda i: (i // tps, i % tps, 0, 0))
    return pl.pallas_call(
        kern,
        grid=(n // tm,),
        in_specs=[pl.BlockSpec((tm, d), lambda i: (i, 0)),
                  pl.BlockSpec((1, 1, N_MOD * d), lambda i: (i // tps, 0, 0)),
                  pl.BlockSpec((1, d), lambda i: (0, 0)),
                  pl.BlockSpec((d, wcols), lambda i: (0, 0))],
        out_specs=[t_spec,
                   pl.BlockSpec((tm, ATTN_WIDTH), lambda i: (i, 0)),
                   t_spec,
                   pl.BlockSpec((tm, POOL_WIDTH), lambda i: (i, 0)),
                   pl.BlockSpec((1, cb, ATTN_WIDTH), lambda i: (i, 0, 0)),
                   pl.BlockSpec((tm, d), lambda i: (i, 0))],
        out_shape=[t_shape,
                   jax.ShapeDtypeStruct((n, ATTN_WIDTH), BF16),
                   t_shape,
                   jax.ShapeDtypeStruct((n, POOL_WIDTH), F32),
                   jax.ShapeDtypeStruct((n // tm, cb, ATTN_WIDTH), F32),
                   jax.ShapeDtypeStruct((n, d), BF16)],
        compiler_params=_cparams("arbitrary"),
        name="norm1_inproj",
    )(x2, mod3, g1, w_qkvu)


def _attn_kernel(qT_ref, k_ref, vT_ref, km_ref, bw_ref, o_ref, kaug_ref, vaug_ref, bias_ref, qnear_ref,
                 qfar_ref, sn_ref, sf0_ref, sf1_ref, *, nb):
    blk = MOBA_BLOCK
    hd = HEAD_DIM
    nh = HEADS_PER_STEP
    heads = range(nh)
    lane = lax.broadcasted_iota(jnp.int32, (blk, LANES), 1)

    def pair_lanes(hh):
        return slice((hh // 2) * LANES, (hh // 2 + 1) * LANES)

    def head_rows(hh):
        return slice(hh * hd, (hh + 1) * hd)

    ones = jnp.ones((ONES_ROWS, blk), BF16)

    def build_operands(j, carry):
        rows = pl.ds(pl.multiple_of(j * blk, blk), blk)
        for hh in heads:
            kp = k_ref[0, rows, pair_lanes(hh)]
            if hh % 2 == 0:
                kaug_ref[hh, rows, :] = jnp.where(lane < hd, kp, jnp.where(lane == hd + j, 1.0, 0.0).astype(BF16))
            else:
                kaug_ref[hh, rows, :] = jnp.where(lane >= hd, kp, jnp.where(lane == j, 1.0, 0.0).astype(BF16))
            vaug_ref[j, hh] = jnp.concatenate([vT_ref[0, j, head_rows(hh), :], ones], axis=0)
        return carry

    lax.fori_loop(0, nb, build_operands, 0)

    for hh in heads:
        for t in range(2):
            wide = jnp.broadcast_to(bw_ref[hh, t:t + 1, :], (blk, 2 * blk))
            bias_ref[hh, t] = pltpu.roll(wide, 0, 1, stride=1, stride_axis=0)[:, blk:2 * blk]

    rowi = lax.broadcasted_iota(jnp.int32, (MAX_BLOCKS, blk), 0)
    km_lane = lax.broadcasted_iota(jnp.int32, (MAX_BLOCKS, LANES), 1)
    zpad = jnp.zeros((LANES - hd - MAX_BLOCKS, blk), BF16)
    km_split = []
    for hh in heads:
        sub = hh % 2
        in_head = (km_lane >= sub * hd) & (km_lane < (sub + 1) * hd)
        km_h = jnp.where(in_head, km_ref[0, :, pair_lanes(hh)], 0.0)
        km_hi = km_h.astype(BF16)
        km_split.append((km_hi, (km_h - km_hi.astype(F32)).astype(BF16)))

    grp = FAR_GROUP

    def retire_begin(state, mx):
        m, acc = state
        m_new = jnp.maximum(m, mx)
        return m_new, jnp.exp2(m - m_new) * acc

    def retire_slice(src_ref, hh, m_new, acc, g, j):
        p = jnp.exp2(src_ref[hh, g * blk:(g + 1) * blk, :] - m_new)
        return acc + _dot(vaug_ref[j, hh], p.astype(BF16))

    def fresh():
        return (jnp.full((1, blk), -jnp.inf, F32), jnp.zeros((hd + ONES_ROWS, blk), F32))

    def near_blocks(i):
        return tuple(jnp.where(i - (grp - 1) + g < 0, i + 1, i - (grp - 1) + g) for g in range(grp))

    def finish(i, states):
        for hh in heads:
            _, acc = states[hh]
            o_ref[0, i, head_rows(hh), :] = (acc[0:hd] / acc[hd:hd + 1]).astype(BF16)

    def select(i):
        for hh in heads:
            qp = qT_ref[0, i, pair_lanes(hh), :]
            km_hi, km_lo = km_split[hh]
            gate = _dot(km_hi, qp) + _dot(km_lo, qp)
            g = jnp.where(rowi < i, gate, -jnp.inf)
            sel = rowi == i
            for _ in range(MOBA_TOPK):
                m = jnp.max(g, axis=0, keepdims=True)
                hit = (g == m) & (m > -jnp.inf)
                idx = jnp.min(jnp.where(hit, rowi, MAX_BLOCKS), axis=0, keepdims=True)
                pick = rowi == idx
                sel = sel | pick
                g = jnp.where(pick, -jnp.inf, g)
            near = jnp.where(sel, 0.0, NEG_INF)
            far = jnp.where(rowi > i - grp, NEG_INF, near)
            qh = qp[head_rows(hh % 2), :]
            if hh % 2 == 0:
                qnear_ref[hh] = jnp.concatenate([qh, near.astype(BF16), zpad], axis=0)
                qfar_ref[hh] = jnp.concatenate([qh, far.astype(BF16), zpad], axis=0)
            else:
                qnear_ref[hh] = jnp.concatenate([near.astype(BF16), zpad, qh], axis=0)
                qfar_ref[hh] = jnp.concatenate([far.astype(BF16), zpad, qh], axis=0)

    def park(dst_ref, q_ref, hh, g, j, bias=None):
        rows = pl.ds(pl.multiple_of(j * blk, blk), blk)
        s = _dot(kaug_ref[hh, rows, :], q_ref[hh])
        if bias is not None:
            s = s + bias
        dst_ref[hh, g * blk:(g + 1) * blk, :] = s
        return jnp.max(s, axis=0, keepdims=True)

    def park_near(i):
        idx = near_blocks(i)

        def quarter(g, hh):
            bias = None
            if g == grp - 1:
                bias = bias_ref[hh, 0]
            elif g == grp - 2:
                bias = bias_ref[hh, 1]
            return park(sn_ref, qnear_ref, hh, g, idx[g], bias)
        return quarter

    def park_far(c, dst_ref):
        return lambda g, hh: park(dst_ref, qfar_ref, hh, g, c * grp + g)

    def chunk_blocks(c):
        return tuple(c * grp + g for g in range(grp))

    def step(src_ref, states, maxes, vidx, quarter, park_first):
        begun = [retire_begin(states[hh], maxes[hh]) for hh in heads] if states is not None else None
        accs = [b[1] for b in begun] if begun else None
        new_max = [None] * nh

        def park_quarter(g):
            for hh in heads:
                cm = quarter(g, hh)
                new_max[hh] = cm if new_max[hh] is None else jnp.maximum(new_max[hh], cm)

        for g in range(grp):
            if park_first:
                park_quarter(g)
            if begun:
                for hh in heads:
                    accs[hh] = retire_slice(src_ref, hh, begun[hh][0], accs[hh], g, vidx[g])
            if not park_first and quarter is not None:
                park_quarter(g)
        new_states = tuple((begun[hh][0], accs[hh]) for hh in heads) if begun else None
        return new_states, tuple(new_max)

    def fresh_states():
        return tuple(fresh() for _ in heads)

    def early_tile(i, near_max):
        nxt = jnp.minimum(i + 1, nb - 1)
        select(nxt)
        states, next_max = step(sn_ref, fresh_states(), near_max, near_blocks(i), park_near(nxt), False)
        finish(i, states)
        return next_max

    def late_tile(i, near_max):
        sf = (sf0_ref, sf1_ref)
        n_chunks = (i - (grp - 1) + grp - 1) // grp
        nxt = jnp.minimum(i + 1, nb - 1)
        states, far_max = step(sn_ref, fresh_states(), near_max, near_blocks(i), park_far(0, sf[0]), True)

        def pair(k, st):
            states, far_max = st
            states, far_max = step(sf[0], states, far_max, chunk_blocks(2 * k), park_far(2 * k + 1, sf[1]), True)
            return step(sf[1], states, far_max, chunk_blocks(2 * k + 1), park_far(2 * k + 2, sf[0]), True)

        states, far_max = lax.fori_loop(0, (n_chunks - 1) // 2, pair, (states, far_max))

        def last(src, states, far_max):
            select(nxt)
            states, next_max = step(src, states, far_max, chunk_blocks(n_chunks - 1), park_near(nxt), True)
            finish(i, states)
            return next_max

        def even_count(st):
            states, far_max = step(sf[0], st[0], st[1], chunk_blocks(n_chunks - 2),
                                   park_far(n_chunks - 1, sf[1]), True)
            return last(sf[1], states, far_max)

        def odd_count(st):
            return last(sf[0], st[0], st[1])

        return lax.cond(n_chunks % 2 == 0, even_count, odd_count, (states, far_max))

    n_early = min(grp, nb)
    select(0)
    _, near_max = step(None, None, None, None, park_near(0), True)
    near_max = lax.fori_loop(0, n_early, early_tile, near_max)
    lax.fori_loop(n_early, nb, late_tile, near_max)


def _rel_bucket_table(max_rel):
    n = np.arange(max_rel)
    max_exact = REL_BUCKETS // 2
    nf = np.maximum(n, 1).astype(np.float32)
    large = max_exact + (np.log(nf / np.float32(max_exact)) / np.float32(math.log(REL_MAX_DIST / max_exact))
                         * np.float32(REL_BUCKETS - max_exact)).astype(np.int32)
    large = np.minimum(large, REL_BUCKETS - 1)
    return np.where(n < max_exact, n, large)


def _bias_vectors(rel_bias):
    blk = MOBA_BLOCK
    table = _rel_bucket_table(2 * blk)
    r = np.arange(-blk, blk)
    bt = (rel_bias.T - rel_bias[REL_BUCKETS - 1][:, None]) * LOG2E
    own = jnp.where(jnp.asarray(r >= 0)[None], bt[:, table[np.maximum(r, 0)]], NEG_INF)
    prev = bt[:, table[np.minimum(r + blk, 2 * blk - 1)]]
    return jnp.stack([own, prev], axis=1).astype(F32)


def _moba_attention(qT, k3, vT, km_pad, bias_w, *, batch, seq):
    nb = seq // MOBA_BLOCK
    blk = MOBA_BLOCK
    assert nb % FAR_GROUP == 0 and nb <= MAX_BLOCKS
    kern = functools.partial(_attn_kernel, nb=nb)
    nh = HEADS_PER_STEP
    width = nh * HEAD_DIM
    assert nh % 2 == 0 and N_HEADS % nh == 0

    def big(shape, index_map):
        return pl.BlockSpec(shape, index_map, pipeline_mode=pl.Buffered(1))

    return pl.pallas_call(
        kern,
        grid=(batch, N_HEADS // nh),
        in_specs=[big((1, nb, width, blk), lambda b, p: (b, 0, p, 0)),
                  big((1, seq, width), lambda b, p: (b, 0, p)),
                  big((1, nb, width, blk), lambda b, p: (b, 0, p, 0)),
                  pl.BlockSpec((1, MAX_BLOCKS, width), lambda b, p: (b, 0, p)),
                  pl.BlockSpec((nh, 2, 2 * blk), lambda b, p: (p, 0, 0))],
        out_specs=pl.BlockSpec((1, nb, width, blk), lambda b, p: (b, 0, p, 0)),
        out_shape=jax.ShapeDtypeStruct((batch, nb, ATTN_WIDTH, blk), BF16),
        scratch_shapes=[pltpu.VMEM((nh, seq, LANES), BF16),
                        pltpu.VMEM((nb, nh, HEAD_DIM + ONES_ROWS, blk), BF16),
                        pltpu.VMEM((nh, 2, blk, blk), F32),
                        pltpu.VMEM((nh, LANES, blk), BF16),
                        pltpu.VMEM((nh, LANES, blk), BF16),
                        pltpu.VMEM((nh, FAR_GROUP * blk, blk), F32),
                        pltpu.VMEM((nh, FAR_GROUP * blk, blk), F32),
                        pltpu.VMEM((nh, FAR_GROUP * blk, blk), F32)],
        compiler_params=_cparams("arbitrary", "arbitrary"),
        name="moba_attention",
    )(qT, k3, vT, km_pad, bias_w)


def _post_kernel(x_ref, mod_ref, h_ref, wz_ref, bz_ref, at_ref, u_ref, uh_ref, pw_ref, ps_ref,
                 wba_ref, wbp_ref, wo_ref, g2_ref, wrh_ref, wrl_ref, br_ref,
                 x1_ref, h2r_ref, route_ref, route_t_ref, counts_ref, xz_ref, carry_ref, *, tm, seq):
    d = D_MODEL
    i = pl.program_id(0)
    x = x_ref[...]
    mod = mod_ref[0]
    gate1 = mod[:, 2 * d:3 * d]
    shift2, scale2 = mod[:, 3 * d:4 * d], mod[:, 4 * d:5 * d]

    zg = 0.5 * jnp.tanh(0.5 * (_dot(h_ref[...], wz_ref[...]) + bz_ref[...])) + 0.5

    ya = jnp.concatenate(
        [lax.dot_general(at_ref[0, c], wba_ref[...], (((0,), (0,)), ((), ())),
                         preferred_element_type=F32) for c in range(tm // MOBA_BLOCK)], axis=0)

    first = (i * tm) % seq == 0
    halo = jnp.where(first, 0.0, uh_ref[...])
    ubuf = jnp.concatenate([halo, u_ref[...]], axis=0)
    rows = tm + POOL_HALO
    pos = (lax.broadcasted_iota(jnp.int32, (tm, POOL_GROUP_WIDTH), 0) + i * tm) % seq
    outs = []
    for gi, w in enumerate(POOL_WINDOWS):
        e = ubuf[:, gi * POOL_GROUP_WIDTH:(gi + 1) * POOL_GROUP_WIDTH]
        s = e
        step = 1
        while step < w:
            s = s + pltpu.roll(s, step, 0)
            step *= 2
        cnt = jnp.minimum(pos + 1, w).astype(F32)
        dlt = s[POOL_HALO:rows] / cnt - e[POOL_HALO:rows]
        outs.append(_dot(dlt.astype(BF16), pw_ref[gi]))
    pooled = jnp.concatenate(outs, axis=-1) * ps_ref[...]
    yp = _dot(pooled.astype(BF16), wbp_ref[...])

    merged = zg[:, 0:d] * ya + zg[:, d:2 * d] * yp
    x1 = x + gate1 * _dot(merged.astype(BF16), wo_ref[...])
    x1_ref[...] = x1

    h2 = _rms_mod(x1, g2_ref[...], scale2, shift2)
    h2_hi = h2.astype(BF16)
    h2r_ref[...] = h2_hi.astype(F32)
    h2_lo = (h2 - h2_hi.astype(F32)).astype(BF16)
    z = _dot(h2_hi, wrh_ref[...]) + _dot(h2_lo, wrh_ref[...]) + _dot(h2_hi, wrl_ref[...]) + br_ref[...]

    rr = ROUTER_ROWS
    zt = z.T[0:rr, :]
    row = lax.broadcasted_iota(jnp.int32, (rr, tm), 0)
    zgrp = jnp.where(row < N_GROUPS, zt, -jnp.inf)
    mg = jnp.max(zgrp, axis=0, keepdims=True)
    pg_top = 1.0 / jnp.sum(jnp.exp(zgrp - mg), axis=0, keepdims=True)
    g_idx = jnp.min(jnp.where(zgrp == mg, row, rr), axis=0, keepdims=True)
    e_row = row - ROUTER_LANE0
    in_grp = (e_row >= 0) & (e_row < N_EXPERTS) & ((e_row >> 3) == g_idx)
    ze = jnp.where(in_grp, zt, -jnp.inf)
    m1 = jnp.max(ze, axis=0, keepdims=True)
    i1 = jnp.min(jnp.where(ze == m1, row, rr), axis=0, keepdims=True)
    ze2 = jnp.where(row == i1, -jnp.inf, ze)
    m2 = jnp.max(ze2, axis=0, keepdims=True)
    i2 = jnp.min(jnp.where(ze2 == m2, row, rr), axis=0, keepdims=True)
    e2 = jnp.exp(m2 - m1)
    w1 = pg_top / (1.0 + e2)
    w2 = pg_top * e2 / (1.0 + e2)

    hot = jnp.where((row == i1) | (row == i2), 1.0, 0.0)
    tri = jnp.where(lax.broadcasted_iota(jnp.int32, (tm, tm), 0) <= lax.broadcasted_iota(jnp.int32, (tm, tm), 1),
                    1.0, 0.0).astype(BF16)
    prefix = _dot(hot.astype(BF16), tri)

    @pl.when(i == 0)
    def _():
        carry_ref[...] = jnp.zeros_like(carry_ref)

    carry = carry_ref[...]
    base = carry[:, 0:1] + prefix - 1.0
    rank1 = jnp.sum(jnp.where(row == i1, base, 0.0), axis=0, keepdims=True)
    rank2 = jnp.sum(jnp.where(row == i2, base, 0.0), axis=0, keepdims=True)
    carry_ref[...] = carry + prefix[:, tm - 1:tm]
    counts_ref[...] = carry_ref[...]
    fields = ((i1 - ROUTER_LANE0).astype(F32), (i2 - ROUTER_LANE0).astype(F32), w1, w2, rank1, rank2)
    frow = lax.broadcasted_iota(jnp.int32, (ROUTE_FIELDS, tm), 0)
    route_t = jnp.zeros((ROUTE_FIELDS, tm), F32)
    for k, val in enumerate(fields):
        route_t = jnp.where(frow == k, val, route_t)
    route_t_ref[...] = route_t
    route_ref[...] = jnp.concatenate([route_t, jnp.zeros((LANES - ROUTE_FIELDS, tm), F32)], axis=0).T
    xz_ref[...] = jnp.zeros_like(xz_ref)


def _post_attention(x2, mod3, h1, w_z, b_z, attnT, u, pool_w, pool_scale, w_ba, w_bp, w_out, g2,
                    wr_hi, wr_lo, b_r, *, batch, seq):
    n, d = x2.shape
    tm = TM_POST
    tps = seq // tm
    cb = tm // MOBA_BLOCK
    hb = tm // POOL_HALO
    sorted_rows = _expert_tiles(n) * TE_ROWS
    assert sorted_rows % (n // tm) == 0
    zrows = sorted_rows // (n // tm)
    kern = functools.partial(_post_kernel, tm=tm, seq=seq)
    const2 = lambda i: (0, 0)
    return pl.pallas_call(
        kern,
        grid=(n // tm,),
        in_specs=[pl.BlockSpec((tm, d), lambda i: (i, 0)),
                  pl.BlockSpec((1, 1, N_MOD * d), lambda i: (i // tps, 0, 0)),
                  pl.BlockSpec((tm, d), lambda i: (i, 0)),
                  pl.BlockSpec((d, 2 * d), lambda i: (0, 1)),
                  pl.BlockSpec((1, 2 * d), const2),
                  pl.BlockSpec((1, cb, ATTN_WIDTH, MOBA_BLOCK), lambda i: (i // tps, i % tps, 0, 0)),
                  pl.BlockSpec((tm, POOL_WIDTH), lambda i: (i, 0)),
                  pl.BlockSpec((POOL_HALO, POOL_WIDTH), lambda i: (jnp.maximum(i * hb - 1, 0), 0)),
                  pl.BlockSpec((len(POOL_WINDOWS), POOL_GROUP_WIDTH, POOL_GROUP_WIDTH), lambda i: (0, 0, 0)),
                  pl.BlockSpec((1, POOL_WIDTH), const2),
                  pl.BlockSpec((ATTN_WIDTH, d), const2),
                  pl.BlockSpec((POOL_WIDTH, d), const2),
                  pl.BlockSpec((d, d), const2),
                  pl.BlockSpec((1, d), const2),
                  pl.BlockSpec((d, LANES), const2),
                  pl.BlockSpec((d, LANES), const2),
                  pl.BlockSpec((1, LANES), const2)],
        out_specs=[pl.BlockSpec((tm, d), lambda i: (i, 0)),
                   pl.BlockSpec((tm, d), lambda i: (i, 0)),
                   pl.BlockSpec((tm, LANES), lambda i: (i, 0)),
                   pl.BlockSpec((ROUTE_FIELDS, tm), lambda i: (0, i)),
                   pl.BlockSpec((ROUTER_ROWS, LANES), const2),
                   pl.BlockSpec((zrows, d), lambda i: (i, 0))],
        out_shape=[jax.ShapeDtypeStruct((n, d), F32),
                   jax.ShapeDtypeStruct((n, d), F32),
                   jax.ShapeDtypeStruct((n, LANES), F32),
                   jax.ShapeDtypeStruct((ROUTE_FIELDS, n), F32),
                   jax.ShapeDtypeStruct((ROUTER_ROWS, LANES), F32),
                   jax.ShapeDtypeStruct((zrows * (n // tm), d), F32)],
        scratch_shapes=[pltpu.VMEM((ROUTER_ROWS, LANES), F32)],
        compiler_params=_cparams("arbitrary"),
        name="merge_outproj_router",
    )(x2, mod3, h1, w_z, b_z, attnT, u, u, pool_w, pool_scale, w_ba, w_bp, w_out, g2, wr_hi, wr_lo, b_r)


def _routing_plan(route_t, counts, *, n):
    te = TE_ROWS
    n_tiles = _expert_tiles(n)
    cnt = counts[ROUTER_LANE0:ROUTER_LANE0 + N_EXPERTS, 0].astype(jnp.int32)
    padded = ((cnt + te - 1) // te) * te
    ids = jnp.arange(N_EXPERTS, dtype=jnp.int32)
    ends = jnp.sum(jnp.where(ids[None, :] <= ids[:, None], padded[None, :], 0), axis=-1)
    starts = ends - padded
    pos = []
    for k in range(EXPERT_TOPK):
        expert = route_t[k].astype(jnp.int32)
        rank = route_t[4 + k].astype(jnp.int32)
        pos.append(jnp.sum(jnp.where(expert[:, None] == ids, starts, 0), axis=-1) + rank)
    tiles = jnp.arange(n_tiles, dtype=jnp.int32)
    tile_expert = jnp.sum((ends // te)[None, :] <= tiles[:, None], axis=-1)
    tile_expert = jnp.minimum(tile_expert, N_EXPERTS - 1).astype(jnp.int32)
    n_used = (ends[N_EXPERTS - 1:] // te).astype(jnp.int32)
    later = (ids[None, :] > ids[:, None]) & (cnt[None, :] > 0)
    next_expert = jnp.min(jnp.where(later, ids[None, :], N_EXPERTS), axis=-1)
    next_expert = jnp.where(next_expert == N_EXPERTS, -1, next_expert).astype(jnp.int32)
    return pos, tile_expert, n_used, next_expert


def _expert_tiles(n):
    return (EXPERT_TOPK * n) // TE_ROWS + N_EXPERTS


def _tile_pos(pos, tm):
    return jnp.concatenate([p.reshape(-1, 1, tm) for p in pos], axis=-1)


def _dispatch_kernel(pos_ref, h_ref, xz_ref, xs_ref, sem, *, tm):
    del xz_ref
    for r in range(tm):
        for k in range(EXPERT_TOPK):
            pltpu.make_async_copy(h_ref.at[pl.ds(r, 1)], xs_ref.at[pl.ds(pos_ref[0, 0, k * tm + r], 1)],
                                  sem).start(priority=(r + k) % 2)
    for k in range(EXPERT_TOPK):
        pltpu.make_async_copy(h_ref, h_ref, sem).wait()


def _dispatch(pos, h2r, xz, *, n):
    tm = TM_DISPATCH
    rows, d = xz.shape
    assert rows == _expert_tiles(n) * TE_ROWS
    pos3 = _tile_pos(pos, tm)
    return pl.pallas_call(
        functools.partial(_dispatch_kernel, tm=tm),
        grid=(n // tm,),
        in_specs=[pl.BlockSpec((1, 1, EXPERT_TOPK * tm), lambda i: (i, 0, 0), memory_space=pltpu.SMEM),
                  pl.BlockSpec((tm, d), lambda i: (i, 0)),
                  pl.BlockSpec(memory_space=pl.ANY)],
        out_specs=pl.BlockSpec(memory_space=pl.ANY),
        out_shape=jax.ShapeDtypeStruct((rows, d), F32),
        scratch_shapes=[pltpu.SemaphoreType.DMA(())],
        input_output_aliases={2: 0},
        compiler_params=_cparams("arbitrary"),
        name="moe_dispatch",
    )(pos3, h2r, xz)


def _experts_kernel(te_ref, nu_ref, nx_ref, xs_ref, wg_ref, wu_ref, wd_ref, y_ref,
                    wg_b, wu_b, wd_b, wg_f, wu_f, wd_f, sems, turn_ref):
    t = pl.program_id(0)

    def fetch(e, slot):
        return [pltpu.make_async_copy(src.at[e], dst.at[slot], sems.at[slot])
                for src, dst in ((wg_ref, wg_f), (wu_ref, wu_f), (wd_ref, wd_f))]

    @pl.when(t < nu_ref[0])
    def _():
        e = te_ref[t]
        e_prev = te_ref[jnp.maximum(t - 1, 0)]

        @pl.when(t == 0)
        def _():
            turn_ref[0] = 0
            for cp in fetch(e, 0):
                cp.start()

        @pl.when((t == 0) | (e != e_prev))
        def _():
            slot = turn_ref[0] % 2
            for cp in fetch(e, slot):
                cp.wait()
            wg_b[...] = wg_f[slot].astype(BF16)
            wu_b[...] = wu_f[slot].astype(BF16)
            wd_b[...] = wd_f[slot].astype(BF16)
            e_next = nx_ref[e]

            @pl.when(e_next >= 0)
            def _():
                for cp in fetch(e_next, 1 - slot):
                    cp.start()

            turn_ref[0] = turn_ref[0] + 1

        x = xs_ref[...].astype(BF16)
        g = _dot(x, wg_b[...])
        u = _dot(x, wu_b[...])
        act = (g * jax.nn.sigmoid(g)) * u
        y_ref[...] = _dot(act.astype(BF16), wd_b[...])

    @pl.when(t >= nu_ref[0])
    def _():
        y_ref[...] = jnp.zeros_like(y_ref)


def _experts(tile_expert, n_used, next_expert, xs, wg, wu, wd):
    rows, d = xs.shape
    te = TE_ROWS
    hbm = pl.BlockSpec(memory_space=pl.ANY)
    grid_spec = pltpu.PrefetchScalarGridSpec(
        num_scalar_prefetch=3,
        grid=(rows // te,),
        in_specs=[pl.BlockSpec((te, d), lambda t, te_r, nu, nx: (jnp.minimum(t, nu[0] - 1), 0)),
                  hbm, hbm, hbm],
        out_specs=pl.BlockSpec((te, d), lambda t, te_r, nu, nx: (t, 0)),
        scratch_shapes=[pltpu.VMEM((d, D_EXPERT), BF16), pltpu.VMEM((d, D_EXPERT), BF16),
                        pltpu.VMEM((D_EXPERT, d), BF16),
                        pltpu.VMEM((2, d, D_EXPERT), F32), pltpu.VMEM((2, d, D_EXPERT), F32),
                        pltpu.VMEM((2, D_EXPERT, d), F32),
                        pltpu.SemaphoreType.DMA((2,)), pltpu.SMEM((1,), jnp.int32)])
    return pl.pallas_call(
        _experts_kernel,
        grid_spec=grid_spec,
        out_shape=jax.ShapeDtypeStruct((rows, d), F32),
        compiler_params=_cparams("arbitrary"),
        name="moe_experts",
    )(tile_expert, n_used, next_expert, xs, wg, wu, wd)


def _final_kernel(pos_ref, pos_next_ref, x1_ref, route_ref, mod_ref, g_ref, y_ref, o_ref, ybuf, sems, *,
                  tm, final_norm):
    d = D_MODEL
    i = pl.program_id(0)
    slot = i % 2

    def gather(p_ref, s):
        for r in range(tm):
            for k in range(EXPERT_TOPK):
                pltpu.make_async_copy(y_ref.at[pl.ds(p_ref[0, 0, k * tm + r], 1)],
                                      ybuf.at[s, k, pl.ds(r, 1)], sems.at[s]).start(priority=(r + k) % 2)

    @pl.when(i == 0)
    def _():
        gather(pos_ref, 0)

    @pl.when(i + 1 < pl.num_programs(0))
    def _():
        gather(pos_next_ref, 1 - slot)

    for k in range(EXPERT_TOPK):
        pltpu.make_async_copy(ybuf.at[slot, k], ybuf.at[slot, k], sems.at[slot]).wait()

    route = route_ref[...]
    lane = lax.broadcasted_iota(jnp.int32, route.shape, 1)
    y = jnp.zeros((tm, d), F32)
    for k in range(EXPERT_TOPK):
        wk = jnp.sum(jnp.where(lane == 2 + k, route, 0.0), axis=-1, keepdims=True)
        y = y + wk * ybuf[slot, k]
    gate2 = mod_ref[0][:, 5 * d:6 * d]
    x = x1_ref[...] + gate2 * y
    if final_norm:
        x = (x * lax.rsqrt(jnp.mean(x * x, axis=-1, keepdims=True) + EPS)) * g_ref[...]
    o_ref[...] = x


def _final(pos, x1, route, mod3, gf, y, *, seq, final_norm):
    n, d = x1.shape
    tm = TM_FINAL
    tps = seq // tm
    pos3 = _tile_pos(pos, tm)
    last = n // tm - 1
    return pl.pallas_call(
        functools.partial(_final_kernel, tm=tm, final_norm=final_norm),
        grid=(n // tm,),
        in_specs=[pl.BlockSpec((1, 1, EXPERT_TOPK * tm), lambda i: (i, 0, 0), memory_space=pltpu.SMEM),
                  pl.BlockSpec((1, 1, EXPERT_TOPK * tm), lambda i: (jnp.minimum(i + 1, last), 0, 0),
                               memory_space=pltpu.SMEM),
                  pl.BlockSpec((tm, d), lambda i: (i, 0)),
                  pl.BlockSpec((tm, LANES), lambda i: (i, 0)),
                  pl.BlockSpec((1, 1, N_MOD * d), lambda i: (i // tps, 0, 0)),
                  pl.BlockSpec((1, d), lambda i: (0, 0)),
                  pl.BlockSpec(memory_space=pl.ANY)],
        out_specs=pl.BlockSpec((tm, d), lambda i: (i, 0)),
        out_shape=jax.ShapeDtypeStruct((n, d), F32),
        scratch_shapes=[pltpu.VMEM((2, EXPERT_TOPK, tm, d), F32), pltpu.SemaphoreType.DMA((2,))],
        compiler_params=_cparams("arbitrary"),
        name="moe_combine_final_norm",
    )(pos3, pos3, x1, route, mod3, gf, y)


def _split_bf16(w):
    hi = w.astype(BF16)
    return hi, (w - hi.astype(F32)).astype(BF16)


def kernel(x, c, w_ada, b_ada, norm1_g, w_in, b_gate, rel_bias, pool_w, pool_scale, w_branch_attn,
           w_branch_pool, w_out, norm2_g, w_router_group, b_router_group, w_router_expert,
           b_router_expert, w_expert_gate, w_expert_up, w_expert_down, norm_f_g):
    batch, seq, d = x.shape
    depth = w_ada.shape[0]
    n = batch * seq
    nb = seq // MOBA_BLOCK
    qkvu = 3 * ATTN_WIDTH + POOL_WIDTH
    bias_w = _bias_vectors(rel_bias)
    c_pad = jnp.pad(c, ((0, 8 - batch), (0, 0)))

    xc = x.reshape(n, d)
    for l in range(depth):
        mod3 = _modulation(c_pad, w_ada[l], b_ada[l][None])[:batch].reshape(batch, 1, N_MOD * d)
        w_in_b = w_in[l].astype(BF16)
        g1 = norm1_g[l][None]

        assert w_in_b.shape[1] == 2 * qkvu == qkvu + 2 * d
        qT, k2, vT, u, km, h1 = _in_projection(xc, mod3, g1, w_in_b, batch=batch, seq=seq)
        km_pad = jnp.pad(km.reshape(batch, nb, ATTN_WIDTH), ((0, 0), (0, MAX_BLOCKS - nb), (0, 0)))
        attnT = _moba_attention(qT, k2.reshape(batch, seq, ATTN_WIDTH), vT, km_pad, bias_w,
                                batch=batch, seq=seq)

        w_r = jnp.concatenate([w_router_group[l], w_router_expert[l]], axis=1)
        w_r = jnp.pad(w_r, ((0, 0), (0, LANES - w_r.shape[1])))
        b_r = jnp.concatenate([b_router_group[l], b_router_expert[l]])
        b_r = jnp.pad(b_r, (0, LANES - b_r.shape[0]))[None]
        wr_hi, wr_lo = _split_bf16(w_r)
        x1, h2r, route, route_t, counts, xz = _post_attention(
            xc, mod3, h1, w_in_b, b_gate[l][None], attnT, u, pool_w[l].astype(BF16),
            pool_scale[l][None], w_branch_attn[l].astype(BF16), w_branch_pool[l].astype(BF16),
            w_out[l].astype(BF16), norm2_g[l][None], wr_hi, wr_lo, b_r, batch=batch, seq=seq)

        pos, tile_expert, n_used, next_expert = _routing_plan(route_t, counts, n=n)
        xs = _dispatch(pos, h2r, xz, n=n)
        y = _experts(tile_expert, n_used, next_expert, xs, w_expert_gate[l], w_expert_up[l],
                     w_expert_down[l])
        xc = _final(pos, x1, route, mod3, norm_f_g[None], y, seq=seq, final_norm=(l == depth - 1))
    return xc.reshape(batch, seq, d)
```

```python
import functools
import math

import numpy as np
import jax
import jax.numpy as jnp
from jax import lax
from jax.experimental import pallas as pl
from jax.experimental.pallas import tpu as pltpu

F32 = jnp.float32
BF16 = jnp.bfloat16

D_MODEL = 1024
N_HEADS = 8
HEAD_DIM = 64
ATTN_WIDTH = N_HEADS * HEAD_DIM
MOBA_BLOCK = 256
MOBA_TOPK = 3
MAX_BLOCKS = 32
FAR_GROUP = 4
HEADS_PER_STEP = 4
ONES_ROWS = 16
LOG2E = math.log2(math.e)
POOL_WINDOWS = (2, 4, 8, 16)
POOL_WIDTH = 512
POOL_GROUP_WIDTH = 128
POOL_HALO = 16
REL_BUCKETS = 32
REL_MAX_DIST = 128
N_GROUPS = 4
EXPERTS_PER_GROUP = 8
N_EXPERTS = N_GROUPS * EXPERTS_PER_GROUP
D_EXPERT = 512
N_MOD = 6
EPS = 1e-6
NEG_INF = -1e30
LANES = 128
ROUTER_LANE0 = N_GROUPS
ROUTE_FIELDS = 8
ROUTER_ROWS = 48
VMEM_LIMIT = 56 * 1024 * 1024

TM_PROJ = 1024
TM_POST = 512
TM_DISPATCH = 2048
TE_ROWS = 512
EXPERT_TOPK = 2
TM_FINAL = 256


def _cparams(*sem):
    return pltpu.CompilerParams(dimension_semantics=sem, vmem_limit_bytes=VMEM_LIMIT)


def _dot(a, b):
    return jnp.dot(a, b, preferred_element_type=F32)


def _rms_mod(x, g, scale, shift):
    xn = x * lax.rsqrt(jnp.mean(x * x, axis=-1, keepdims=True) + EPS)
    return (xn * g) * (1.0 + scale) + shift


def _mod_kernel(c_ref, w_ref, b_ref, o_ref):
    c = c_ref[...]
    ca = c * jax.nn.sigmoid(c)
    o_ref[...] = _dot(ca.astype(BF16), w_ref[...].astype(BF16)) + b_ref[...]


def _modulation(c_pad, w_ada, b_ada):
    rows, d = c_pad.shape
    n_out = w_ada.shape[1]
    tn = 1024
    return pl.pallas_call(
        _mod_kernel,
        grid=(n_out // tn,),
        in_specs=[pl.BlockSpec((rows, d), lambda j: (0, 0)),
                  pl.BlockSpec((d, tn), lambda j: (0, j)),
                  pl.BlockSpec((1, tn), lambda j: (0, j))],
        out_specs=pl.BlockSpec((rows, tn), lambda j: (0, j)),
        out_shape=jax.ShapeDtypeStruct((rows, n_out), F32),
        compiler_params=_cparams("arbitrary"),
        name="adaln_mod",
    )(c_pad, w_ada, b_ada)


def _inproj_kernel(x_ref, mod_ref, g_ref, w_ref, qT_ref, k_ref, vT_ref, u_ref, km_ref, h_ref, *, tm):
    d = D_MODEL
    mod = mod_ref[0]
    h = _rms_mod(x_ref[...], g_ref[...], mod[:, d:2 * d], mod[:, 0:d]).astype(BF16)
    h_ref[...] = h
    proj = _dot(h, w_ref[...])
    aw = ATTN_WIDTH
    q = proj[:, 0:aw] * (HEAD_DIM ** -0.5 * LOG2E)
    k = proj[:, aw:2 * aw]
    v = proj[:, 2 * aw:3 * aw]
    u_ref[...] = proj[:, 3 * aw:3 * aw + POOL_WIDTH]
    k_ref[...] = k.astype(BF16)
    qT = q.T.astype(BF16)
    vT = v.T.astype(BF16)
    for c in range(tm // MOBA_BLOCK):
        sl = slice(c * MOBA_BLOCK, (c + 1) * MOBA_BLOCK)
        qT_ref[0, c] = qT[:, sl]
        vT_ref[0, c] = vT[:, sl]
        km_ref[0, c:c + 1, :] = jnp.mean(k[sl, :], axis=0, keepdims=True)


def _in_projection(x2, mod3, g1, w_qkvu, *, batch, seq):
    n, d = x2.shape
    tm = TM_PROJ
    tps = seq // tm
    nb = seq // MOBA_BLOCK
    cb = tm // MOBA_BLOCK
    wcols = 3 * ATTN_WIDTH + POOL_WIDTH
    assert w_qkvu.shape[1] % wcols == 0
    kern = functools.partial(_inproj_kernel, tm=tm)
    t_shape = jax.ShapeDtypeStruct((batch, nb, ATTN_WIDTH, MOBA_BLOCK), BF16)
    t_spec = pl.BlockSpec((1, cb, ATTN_WIDTH, MOBA_BLOCK), lambda i: (i // tps, i % tps, 0, 0))
    return pl.pallas_call(
        kern,
        grid=(n // tm,),
        in_specs=[pl.BlockSpec((tm, d), lambda i: (i, 0)),
                  pl.BlockSpec((1, 1, N_MOD * d), lambda i: (i // tps, 0, 0)),
                  pl.BlockSpec((1, d), lambda i: (0, 0)),
                  pl.BlockSpec((d, wcols), lambda i: (0, 0))],
        out_specs=[t_spec,
                   pl.BlockSpec((tm, ATTN_WIDTH), lambda i: (i, 0)),
                   t_spec,
                   pl.BlockSpec((tm, POOL_WIDTH), lambda i: (i, 0)),
                   pl.BlockSpec((1, cb, ATTN_WIDTH), lambda i: (i, 0, 0)),
                   pl.BlockSpec((tm, d), lambda i: (i, 0))],
        out_shape=[t_shape,
                   jax.ShapeDtypeStruct((n, ATTN_WIDTH), BF16),
                   t_shape,
                   jax.ShapeDtypeStruct((n, POOL_WIDTH), F32),
                   jax.ShapeDtypeStruct((n // tm, cb, ATTN_WIDTH), F32),
                   jax.ShapeDtypeStruct((n, d), BF16)],
        compiler_params=_cparams("arbitrary"),
        name="norm1_inproj",
    )(x2, mod3, g1, w_qkvu)


def _attn_kernel(qT_ref, k_hbm, v_hbm, km_ref, bw_ref, o_ref, kaug_ref, vaug_ref, bias_ref, qnear_ref,
                 qfar_ref, sn_ref, sf0_ref, sf1_ref, k_ref, vT_ref, kv_sems, *, nb):
    blk = MOBA_BLOCK
    hd = HEAD_DIM
    nh = HEADS_PER_STEP
    heads = range(nh)
    lane = lax.broadcasted_iota(jnp.int32, (blk, LANES), 1)

    def pair_lanes(hh):
        return slice((hh // 2) * LANES, (hh // 2 + 1) * LANES)

    def head_rows(hh):
        return slice(hh * hd, (hh + 1) * hd)

    ones = jnp.ones((ONES_ROWS, blk), BF16)

    width = nh * hd
    n_inner = pl.num_programs(1)
    step_id = pl.program_id(0) * n_inner + pl.program_id(1)

    def kv_copies(sid):
        bb, col = sid // n_inner, pl.multiple_of((sid % n_inner) * width, width)
        return (pltpu.make_async_copy(k_hbm.at[bb, :, pl.ds(col, width)], k_ref, kv_sems.at[0]),
                pltpu.make_async_copy(v_hbm.at[bb, :, pl.ds(col, width), :], vT_ref, kv_sems.at[1]))

    @pl.when(step_id == 0)
    def _():
        for cp in kv_copies(step_id):
            cp.start()

    for cp in kv_copies(step_id):
        cp.wait()

    def build_operands(j, carry):
        rows = pl.ds(pl.multiple_of(j * blk, blk), blk)
        for hh in heads:
            kp = k_ref[rows, pair_lanes(hh)]
            if hh % 2 == 0:
                kaug_ref[hh, rows, :] = jnp.where(lane < hd, kp, jnp.where(lane == hd + j, 1.0, 0.0).astype(BF16))
            else:
                kaug_ref[hh, rows, :] = jnp.where(lane >= hd, kp, jnp.where(lane == j, 1.0, 0.0).astype(BF16))
            vaug_ref[j, hh] = jnp.concatenate([vT_ref[j, head_rows(hh), :], ones], axis=0)
        return carry

    lax.fori_loop(0, nb, build_operands, 0)

    @pl.when(step_id + 1 < pl.num_programs(0) * n_inner)
    def _():
        for cp in kv_copies(step_id + 1):
            cp.start()

    for hh in heads:
        for t in range(2):
            wide = jnp.broadcast_to(bw_ref[hh, t:t + 1, :], (blk, 2 * blk))
            bias_ref[hh, t] = pltpu.roll(wide, 0, 1, stride=1, stride_axis=0)[:, blk:2 * blk]

    rowi = lax.broadcasted_iota(jnp.int32, (MAX_BLOCKS, blk), 0)
    km_lane = lax.broadcasted_iota(jnp.int32, (MAX_BLOCKS, LANES), 1)
    zpad = jnp.zeros((LANES - hd - MAX_BLOCKS, blk), BF16)
    km_split = []
    for hh in heads:
        sub = hh % 2
        in_head = (km_lane >= sub * hd) & (km_lane < (sub + 1) * hd)
        km_h = jnp.where(in_head, km_ref[0, :, pair_lanes(hh)], 0.0)
        km_hi = km_h.astype(BF16)
        km_split.append((km_hi, (km_h - km_hi.astype(F32)).astype(BF16)))

    grp = FAR_GROUP

    def retire_begin(state, mx):
        m, acc = state
        m_new = jnp.maximum(m, mx)
        return m_new, jnp.exp2(m - m_new) * acc

    def retire_slice(src_ref, hh, m_new, acc, g, j):
        p = jnp.exp2(src_ref[hh, g * blk:(g + 1) * blk, :] - m_new)
        return acc + _dot(vaug_ref[j, hh], p.astype(BF16))

    def fresh():
        return (jnp.full((1, blk), -jnp.inf, F32), jnp.zeros((hd + ONES_ROWS, blk), F32))

    def near_blocks(i):
        return tuple(jnp.where(i - (grp - 1) + g < 0, i + 1, i - (grp - 1) + g) for g in range(grp))

    def finish(i, states):
        for hh in heads:
            _, acc = states[hh]
            o_ref[0, i, head_rows(hh), :] = (acc[0:hd] / acc[hd:hd + 1]).astype(BF16)

    def select(i):
        for hh in heads:
            qp = qT_ref[0, i, pair_lanes(hh), :]
            km_hi, km_lo = km_split[hh]
            gate = _dot(km_hi, qp) + _dot(km_lo, qp)
            g = jnp.where(rowi < i, gate, -jnp.inf)
            sel = rowi == i
            for _ in range(MOBA_TOPK):
                m = jnp.max(g, axis=0, keepdims=True)
                hit = (g == m) & (m > -jnp.inf)
                idx = jnp.min(jnp.where(hit, rowi, MAX_BLOCKS), axis=0, keepdims=True)
                pick = rowi == idx
                sel = sel | pick
                g = jnp.where(pick, -jnp.inf, g)
            near = jnp.where(sel, 0.0, NEG_INF)
            far = jnp.where(rowi > i - grp, NEG_INF, near)
            qh = qp[head_rows(hh % 2), :]
            if hh % 2 == 0:
                qnear_ref[hh] = jnp.concatenate([qh, near.astype(BF16), zpad], axis=0)
                qfar_ref[hh] = jnp.concatenate([qh, far.astype(BF16), zpad], axis=0)
            else:
                qnear_ref[hh] = jnp.concatenate([near.astype(BF16), zpad, qh], axis=0)
                qfar_ref[hh] = jnp.concatenate([far.astype(BF16), zpad, qh], axis=0)

    def park(dst_ref, q_ref, hh, g, j, bias=None):
        rows = pl.ds(pl.multiple_of(j * blk, blk), blk)
        s = _dot(kaug_ref[hh, rows, :], q_ref[hh])
        if bias is not None:
            s = s + bias
        dst_ref[hh, g * blk:(g + 1) * blk, :] = s
        return jnp.max(s, axis=0, keepdims=True)

    def park_near(i):
        idx = near_blocks(i)

        def quarter(g, hh):
            bias = None
            if g == grp - 1:
                bias = bias_ref[hh, 0]
            elif g == grp - 2:
                bias = bias_ref[hh, 1]
            return park(sn_ref, qnear_ref, hh, g, idx[g], bias)
        return quarter

    def park_far(c, dst_ref):
        return lambda g, hh: park(dst_ref, qfar_ref, hh, g, c * grp + g)

    def chunk_blocks(c):
        return tuple(c * grp + g for g in range(grp))

    def step(src_ref, states, maxes, vidx, quarter, park_first):
        begun = [retire_begin(states[hh], maxes[hh]) for hh in heads] if states is not None else None
        accs = [b[1] for b in begun] if begun else None
        new_max = [None] * nh

        def park_quarter(g):
            for hh in heads:
                cm = quarter(g, hh)
                new_max[hh] = cm if new_max[hh] is None else jnp.maximum(new_max[hh], cm)

        for g in range(grp):
            if park_first:
                park_quarter(g)
            if begun:
                for hh in heads:
                    accs[hh] = retire_slice(src_ref, hh, begun[hh][0], accs[hh], g, vidx[g])
            if not park_first and quarter is not None:
                park_quarter(g)
        new_states = tuple((begun[hh][0], accs[hh]) for hh in heads) if begun else None
        return new_states, tuple(new_max)

    def fresh_states():
        return tuple(fresh() for _ in heads)

    def early_tile(i, near_max):
        nxt = jnp.minimum(i + 1, nb - 1)
        select(nxt)
        states, next_max = step(sn_ref, fresh_states(), near_max, near_blocks(i), park_near(nxt), False)
        finish(i, states)
        return next_max

    def late_tile(i, near_max):
        sf = (sf0_ref, sf1_ref)
        n_chunks = (i - (grp - 1) + grp - 1) // grp
        nxt = jnp.minimum(i + 1, nb - 1)
        states, far_max = step(sn_ref, fresh_states(), near_max, near_blocks(i), park_far(0, sf[0]), True)

        def pair(k, st):
            states, far_max = st
            states, far_max = step(sf[0], states, far_max, chunk_blocks(2 * k), park_far(2 * k + 1, sf[1]), True)
            return step(sf[1], states, far_max, chunk_blocks(2 * k + 1), park_far(2 * k + 2, sf[0]), True)

        states, far_max = lax.fori_loop(0, (n_chunks - 1) // 2, pair, (states, far_max))

        def last(src, states, far_max):
            select(nxt)
            states, next_max = step(src, states, far_max, chunk_blocks(n_chunks - 1), park_near(nxt), True)
            finish(i, states)
            return next_max

        def even_count(st):
            states, far_max = step(sf[0], st[0], st[1], chunk_blocks(n_chunks - 2),
                                   park_far(n_chunks - 1, sf[1]), True)
            return last(sf[1], states, far_max)

        def odd_count(st):
            return last(sf[0], st[0], st[1])

        return lax.cond(n_chunks % 2 == 0, even_count, odd_count, (states, far_max))

    n_early = min(grp, nb)
    select(0)
    _, near_max = step(None, None, None, None, park_near(0), True)
    near_max = lax.fori_loop(0, n_early, early_tile, near_max)
    lax.fori_loop(n_early, nb, late_tile, near_max)


def _rel_bucket_table(max_rel):
    n = np.arange(max_rel)
    max_exact = REL_BUCKETS // 2
    nf = np.maximum(n, 1).astype(np.float32)
    large = max_exact + (np.log(nf / np.float32(max_exact)) / np.float32(math.log(REL_MAX_DIST / max_exact))
                         * np.float32(REL_BUCKETS - max_exact)).astype(np.int32)
    large = np.minimum(large, REL_BUCKETS - 1)
    return np.where(n < max_exact, n, large)


def _bias_vectors(rel_bias):
    blk = MOBA_BLOCK
    table = _rel_bucket_table(2 * blk)
    r = np.arange(-blk, blk)
    bt = (rel_bias.T - rel_bias[REL_BUCKETS - 1][:, None]) * LOG2E
    own = jnp.where(jnp.asarray(r >= 0)[None], bt[:, table[np.maximum(r, 0)]], NEG_INF)
    prev = bt[:, table[np.minimum(r + blk, 2 * blk - 1)]]
    return jnp.stack([own, prev], axis=1).astype(F32)


def _moba_attention(qT, k3, vT, km_pad, bias_w, *, batch, seq):
    nb = seq // MOBA_BLOCK
    blk = MOBA_BLOCK
    assert nb % FAR_GROUP == 0 and nb <= MAX_BLOCKS
    kern = functools.partial(_attn_kernel, nb=nb)
    nh = HEADS_PER_STEP
    width = nh * HEAD_DIM
    assert nh % 2 == 0 and N_HEADS % nh == 0

    hbm = pl.BlockSpec(memory_space=pl.ANY)
    return pl.pallas_call(
        kern,
        grid=(batch, N_HEADS // nh),
        in_specs=[pl.BlockSpec((1, nb, width, blk), lambda b, p: (b, 0, p, 0)),
                  hbm, hbm,
                  pl.BlockSpec((1, MAX_BLOCKS, width), lambda b, p: (b, 0, p)),
                  pl.BlockSpec((nh, 2, 2 * blk), lambda b, p: (p, 0, 0))],
        out_specs=pl.BlockSpec((1, nb, width, blk), lambda b, p: (b, 0, p, 0)),
        out_shape=jax.ShapeDtypeStruct((batch, nb, ATTN_WIDTH, blk), BF16),
        scratch_shapes=[pltpu.VMEM((nh, seq, LANES), BF16),
                        pltpu.VMEM((nb, nh, HEAD_DIM + ONES_ROWS, blk), BF16),
                        pltpu.VMEM((nh, 2, blk, blk), F32),
                        pltpu.VMEM((nh, LANES, blk), BF16),
                        pltpu.VMEM((nh, LANES, blk), BF16),
                        pltpu.VMEM((nh, FAR_GROUP * blk, blk), F32),
                        pltpu.VMEM((nh, FAR_GROUP * blk, blk), F32),
                        pltpu.VMEM((nh, FAR_GROUP * blk, blk), F32),
                        pltpu.VMEM((seq, width), BF16),
                        pltpu.VMEM((nb, width, blk), BF16),
                        pltpu.SemaphoreType.DMA((2,))],
        compiler_params=_cparams("arbitrary", "arbitrary"),
        name="moba_attention",
    )(qT, k3, vT, km_pad, bias_w)


def _post_kernel(x_ref, mod_ref, h_ref, wz_ref, bz_ref, at_ref, u_ref, uh_ref, pw_ref, ps_ref,
                 wba_ref, wbp_ref, wo_ref, g2_ref, wrh_ref, wrl_ref, br_ref,
                 x1_ref, h2r_ref, route_ref, route_t_ref, counts_ref, xz_ref, carry_ref, *, tm, seq):
    d = D_MODEL
    i = pl.program_id(0)
    x = x_ref[...]
    mod = mod_ref[0]
    gate1 = mod[:, 2 * d:3 * d]
    shift2, scale2 = mod[:, 3 * d:4 * d], mod[:, 4 * d:5 * d]

    zg = 0.5 * jnp.tanh(0.5 * (_dot(h_ref[...], wz_ref[...]) + bz_ref[...])) + 0.5

    ya = jnp.concatenate(
        [lax.dot_general(at_ref[0, c], wba_ref[...], (((0,), (0,)), ((), ())),
                         preferred_element_type=F32) for c in range(tm // MOBA_BLOCK)], axis=0)

    first = (i * tm) % seq == 0
    halo = jnp.where(first, 0.0, uh_ref[...])
    ubuf = jnp.concatenate([halo, u_ref[...]], axis=0)
    rows = tm + POOL_HALO
    pos = (lax.broadcasted_iota(jnp.int32, (tm, POOL_GROUP_WIDTH), 0) + i * tm) % seq
    outs = []
    for gi, w in enumerate(POOL_WINDOWS):
        e = ubuf[:, gi * POOL_GROUP_WIDTH:(gi + 1) * POOL_GROUP_WIDTH]
        s = e
        step = 1
        while step < w:
            s = s + pltpu.roll(s, step, 0)
            step *= 2
        cnt = jnp.minimum(pos + 1, w).astype(F32)
        dlt = s[POOL_HALO:rows] / cnt - e[POOL_HALO:rows]
        outs.append(_dot(dlt.astype(BF16), pw_ref[gi]))
    pooled = jnp.concatenate(outs, axis=-1) * ps_ref[...]
    yp = _dot(pooled.astype(BF16), wbp_ref[...])

    merged = zg[:, 0:d] * ya + zg[:, d:2 * d] * yp
    x1 = x + gate1 * _dot(merged.astype(BF16), wo_ref[...])
    x1_ref[...] = x1

    h2 = _rms_mod(x1, g2_ref[...], scale2, shift2)
    h2_hi = h2.astype(BF16)
    h2r_ref[...] = h2_hi.astype(F32)
    h2_lo = (h2 - h2_hi.astype(F32)).astype(BF16)
    z = _dot(h2_hi, wrh_ref[...]) + _dot(h2_lo, wrh_ref[...]) + _dot(h2_hi, wrl_ref[...]) + br_ref[...]

    rr = ROUTER_ROWS
    zt = z.T[0:rr, :]
    row = lax.broadcasted_iota(jnp.int32, (rr, tm), 0)
    zgrp = jnp.where(row < N_GROUPS, zt, -jnp.inf)
    mg = jnp.max(zgrp, axis=0, keepdims=True)
    pg_top = 1.0 / jnp.sum(jnp.exp(zgrp - mg), axis=0, keepdims=True)
    g_idx = jnp.min(jnp.where(zgrp == mg, row, rr), axis=0, keepdims=True)
    e_row = row - ROUTER_LANE0
    in_grp = (e_row >= 0) & (e_row < N_EXPERTS) & ((e_row >> 3) == g_idx)
    ze = jnp.where(in_grp, zt, -jnp.inf)
    m1 = jnp.max(ze, axis=0, keepdims=True)
    i1 = jnp.min(jnp.where(ze == m1, row, rr), axis=0, keepdims=True)
    ze2 = jnp.where(row == i1, -jnp.inf, ze)
    m2 = jnp.max(ze2, axis=0, keepdims=True)
    i2 = jnp.min(jnp.where(ze2 == m2, row, rr), axis=0, keepdims=True)
    e2 = jnp.exp(m2 - m1)
    w1 = pg_top / (1.0 + e2)
    w2 = pg_top * e2 / (1.0 + e2)

    hot = jnp.where((row == i1) | (row == i2), 1.0, 0.0)
    tri = jnp.where(lax.broadcasted_iota(jnp.int32, (tm, tm), 0) <= lax.broadcasted_iota(jnp.int32, (tm, tm), 1),
                    1.0, 0.0).astype(BF16)
    prefix = _dot(hot.astype(BF16), tri)

    @pl.when(i == 0)
    def _():
        carry_ref[...] = jnp.zeros_like(carry_ref)

    carry = carry_ref[...]
    base = carry[:, 0:1] + prefix - 1.0
    rank1 = jnp.sum(jnp.where(row == i1, base, 0.0), axis=0, keepdims=True)
    rank2 = jnp.sum(jnp.where(row == i2, base, 0.0), axis=0, keepdims=True)
    carry_ref[...] = carry + prefix[:, tm - 1:tm]
    counts_ref[...] = carry_ref[...]
    fields = ((i1 - ROUTER_LANE0).astype(F32), (i2 - ROUTER_LANE0).astype(F32), w1, w2, rank1, rank2)
    frow = lax.broadcasted_iota(jnp.int32, (ROUTE_FIELDS, tm), 0)
    route_t = jnp.zeros((ROUTE_FIELDS, tm), F32)
    for k, val in enumerate(fields):
        route_t = jnp.where(frow == k, val, route_t)
    route_t_ref[...] = route_t
    route_ref[...] = jnp.concatenate([route_t, jnp.zeros((LANES - ROUTE_FIELDS, tm), F32)], axis=0).T
    xz_ref[...] = jnp.zeros_like(xz_ref)


def _post_attention(x2, mod3, h1, w_z, b_z, attnT, u, pool_w, pool_scale, w_ba, w_bp, w_out, g2,
                    wr_hi, wr_lo, b_r, *, batch, seq):
    n, d = x2.shape
    tm = TM_POST
    tps = seq // tm
    cb = tm // MOBA_BLOCK
    hb = tm // POOL_HALO
    sorted_rows = _expert_tiles(n) * TE_ROWS
    assert sorted_rows % (n // tm) == 0
    zrows = sorted_rows // (n // tm)
    kern = functools.partial(_post_kernel, tm=tm, seq=seq)
    const2 = lambda i: (0, 0)
    return pl.pallas_call(
        kern,
        grid=(n // tm,),
        in_specs=[pl.BlockSpec((tm, d), lambda i: (i, 0)),
                  pl.BlockSpec((1, 1, N_MOD * d), lambda i: (i // tps, 0, 0)),
                  pl.BlockSpec((tm, d), lambda i: (i, 0)),
                  pl.BlockSpec((d, 2 * d), lambda i: (0, 1)),
                  pl.BlockSpec((1, 2 * d), const2),
                  pl.BlockSpec((1, cb, ATTN_WIDTH, MOBA_BLOCK), lambda i: (i // tps, i % tps, 0, 0)),
                  pl.BlockSpec((tm, POOL_WIDTH), lambda i: (i, 0)),
                  pl.BlockSpec((POOL_HALO, POOL_WIDTH), lambda i: (jnp.maximum(i * hb - 1, 0), 0)),
                  pl.BlockSpec((len(POOL_WINDOWS), POOL_GROUP_WIDTH, POOL_GROUP_WIDTH), lambda i: (0, 0, 0)),
                  pl.BlockSpec((1, POOL_WIDTH), const2),
                  pl.BlockSpec((ATTN_WIDTH, d), const2),
                  pl.BlockSpec((POOL_WIDTH, d), const2),
                  pl.BlockSpec((d, d), const2),
                  pl.BlockSpec((1, d), const2),
                  pl.BlockSpec((d, LANES), const2),
                  pl.BlockSpec((d, LANES), const2),
                  pl.BlockSpec((1, LANES), const2)],
        out_specs=[pl.BlockSpec((tm, d), lambda i: (i, 0)),
                   pl.BlockSpec((tm, d), lambda i: (i, 0)),
                   pl.BlockSpec((tm, LANES), lambda i: (i, 0)),
                   pl.BlockSpec((ROUTE_FIELDS, tm), lambda i: (0, i)),
                   pl.BlockSpec((ROUTER_ROWS, LANES), const2),
                   pl.BlockSpec((zrows, d), lambda i: (i, 0))],
        out_shape=[jax.ShapeDtypeStruct((n, d), F32),
                   jax.ShapeDtypeStruct((n, d), F32),
                   jax.ShapeDtypeStruct((n, LANES), F32),
                   jax.ShapeDtypeStruct((ROUTE_FIELDS, n), F32),
                   jax.ShapeDtypeStruct((ROUTER_ROWS, LANES), F32),
                   jax.ShapeDtypeStruct((zrows * (n // tm), d), F32)],
        scratch_shapes=[pltpu.VMEM((ROUTER_ROWS, LANES), F32)],
        compiler_params=_cparams("arbitrary"),
        name="merge_outproj_router",
    )(x2, mod3, h1, w_z, b_z, attnT, u, u, pool_w, pool_scale, w_ba, w_bp, w_out, g2, wr_hi, wr_lo, b_r)


def _routing_plan(route_t, counts, *, n):
    te = TE_ROWS
    n_tiles = _expert_tiles(n)
    cnt = counts[ROUTER_LANE0:ROUTER_LANE0 + N_EXPERTS, 0].astype(jnp.int32)
    padded = ((cnt + te - 1) // te) * te
    ids = jnp.arange(N_EXPERTS, dtype=jnp.int32)
    ends = jnp.sum(jnp.where(ids[None, :] <= ids[:, None], padded[None, :], 0), axis=-1)
    starts = ends - padded
    pos = []
    for k in range(EXPERT_TOPK):
        expert = route_t[k].astype(jnp.int32)
        rank = route_t[4 + k].astype(jnp.int32)
        pos.append(jnp.sum(jnp.where(expert[:, None] == ids, starts, 0), axis=-1) + rank)
    tiles = jnp.arange(n_tiles, dtype=jnp.int32)
    tile_expert = jnp.sum((ends // te)[None, :] <= tiles[:, None], axis=-1)
    tile_expert = jnp.minimum(tile_expert, N_EXPERTS - 1).astype(jnp.int32)
    n_used = (ends[N_EXPERTS - 1:] // te).astype(jnp.int32)
    later = (ids[None, :] > ids[:, None]) & (cnt[None, :] > 0)
    next_expert = jnp.min(jnp.where(later, ids[None, :], N_EXPERTS), axis=-1)
    next_expert = jnp.where(next_expert == N_EXPERTS, -1, next_expert).astype(jnp.int32)
    return pos, tile_expert, n_used, next_expert


def _expert_tiles(n):
    return (EXPERT_TOPK * n) // TE_ROWS + N_EXPERTS


def _tile_pos(pos, tm):
    return jnp.concatenate([p.reshape(-1, 1, tm) for p in pos], axis=-1)


def _dispatch_kernel(pos_ref, h_ref, xz_ref, xs_ref, sem, *, tm):
    del xz_ref
    for r in range(tm):
        for k in range(EXPERT_TOPK):
            pltpu.make_async_copy(h_ref.at[pl.ds(r, 1)], xs_ref.at[pl.ds(pos_ref[0, 0, k * tm + r], 1)],
                                  sem).start(priority=(r + k) % 2)
    for k in range(EXPERT_TOPK):
        pltpu.make_async_copy(h_ref, h_ref, sem).wait()


def _dispatch(pos, h2r, xz, *, n):
    tm = TM_DISPATCH
    rows, d = xz.shape
    assert rows == _expert_tiles(n) * TE_ROWS
    pos3 = _tile_pos(pos, tm)
    return pl.pallas_call(
        functools.partial(_dispatch_kernel, tm=tm),
        grid=(n // tm,),
        in_specs=[pl.BlockSpec((1, 1, EXPERT_TOPK * tm), lambda i: (i, 0, 0), memory_space=pltpu.SMEM),
                  pl.BlockSpec((tm, d), lambda i: (i, 0)),
                  pl.BlockSpec(memory_space=pl.ANY)],
        out_specs=pl.BlockSpec(memory_space=pl.ANY),
        out_shape=jax.ShapeDtypeStruct((rows, d), F32),
        scratch_shapes=[pltpu.SemaphoreType.DMA(())],
        input_output_aliases={2: 0},
        compiler_params=_cparams("arbitrary"),
        name="moe_dispatch",
    )(pos3, h2r, xz)


def _experts_kernel(te_ref, nu_ref, nx_ref, xs_ref, wg_ref, wu_ref, wd_ref, y_ref,
                    wg_b, wu_b, wd_b, wg_f, wu_f, wd_f, sems, turn_ref):
    t = pl.program_id(0)

    def fetch(e, slot):
        return [pltpu.make_async_copy(src.at[e], dst.at[slot], sems.at[slot])
                for src, dst in ((wg_ref, wg_f), (wu_ref, wu_f), (wd_ref, wd_f))]

    @pl.when(t < nu_ref[0])
    def _():
        e = te_ref[t]
        e_prev = te_ref[jnp.maximum(t - 1, 0)]

        @pl.when(t == 0)
        def _():
            turn_ref[0] = 0
            for cp in fetch(e, 0):
                cp.start()

        @pl.when((t == 0) | (e != e_prev))
        def _():
            slot = turn_ref[0] % 2
            for cp in fetch(e, slot):
                cp.wait()
            wg_b[...] = wg_f[slot].astype(BF16)
            wu_b[...] = wu_f[slot].astype(BF16)
            wd_b[...] = wd_f[slot].astype(BF16)
            e_next = nx_ref[e]

            @pl.when(e_next >= 0)
            def _():
                for cp in fetch(e_next, 1 - slot):
                    cp.start()

            turn_ref[0] = turn_ref[0] + 1

        x = xs_ref[...].astype(BF16)
        g = _dot(x, wg_b[...])
        u = _dot(x, wu_b[...])
        act = (g * jax.nn.sigmoid(g)) * u
        y_ref[...] = _dot(act.astype(BF16), wd_b[...])

    @pl.when(t >= nu_ref[0])
    def _():
        y_ref[...] = jnp.zeros_like(y_ref)


def _experts(tile_expert, n_used, next_expert, xs, wg, wu, wd):
    rows, d = xs.shape
    te = TE_ROWS
    hbm = pl.BlockSpec(memory_space=pl.ANY)
    grid_spec = pltpu.PrefetchScalarGridSpec(
        num_scalar_prefetch=3,
        grid=(rows // te,),
        in_specs=[pl.BlockSpec((te, d), lambda t, te_r, nu, nx: (jnp.minimum(t, nu[0] - 1), 0)),
                  hbm, hbm, hbm],
        out_specs=pl.BlockSpec((te, d), lambda t, te_r, nu, nx: (t, 0)),
        scratch_shapes=[pltpu.VMEM((d, D_EXPERT), BF16), pltpu.VMEM((d, D_EXPERT), BF16),
                        pltpu.VMEM((D_EXPERT, d), BF16),
                        pltpu.VMEM((2, d, D_EXPERT), F32), pltpu.VMEM((2, d, D_EXPERT), F32),
                        pltpu.VMEM((2, D_EXPERT, d), F32),
                        pltpu.SemaphoreType.DMA((2,)), pltpu.SMEM((1,), jnp.int32)])
    return pl.pallas_call(
        _experts_kernel,
        grid_spec=grid_spec,
        out_shape=jax.ShapeDtypeStruct((rows, d), F32),
        compiler_params=_cparams("arbitrary"),
        name="moe_experts",
    )(tile_expert, n_used, next_expert, xs, wg, wu, wd)


def _final_kernel(pos_ref, pos_next_ref, x1_ref, route_ref, mod_ref, g_ref, y_ref, o_ref, ybuf, sems, *,
                  tm, final_norm):
    d = D_MODEL
    i = pl.program_id(0)
    slot = i % 2

    def gather(p_ref, s):
        for r in range(tm):
            for k in range(EXPERT_TOPK):
                pltpu.make_async_copy(y_ref.at[pl.ds(p_ref[0, 0, k * tm + r], 1)],
                                      ybuf.at[s, k, pl.ds(r, 1)], sems.at[s]).start(priority=(r + k) % 2)

    @pl.when(i == 0)
    def _():
        gather(pos_ref, 0)

    @pl.when(i + 1 < pl.num_programs(0))
    def _():
        gather(pos_next_ref, 1 - slot)

    for k in range(EXPERT_TOPK):
        pltpu.make_async_copy(ybuf.at[slot, k], ybuf.at[slot, k], sems.at[slot]).wait()

    route = route_ref[...]
    lane = lax.broadcasted_iota(jnp.int32, route.shape, 1)
    y = jnp.zeros((tm, d), F32)
    for k in range(EXPERT_TOPK):
        wk = jnp.sum(jnp.where(lane == 2 + k, route, 0.0), axis=-1, keepdims=True)
        y = y + wk * ybuf[slot, k]
    gate2 = mod_ref[0][:, 5 * d:6 * d]
    x = x1_ref[...] + gate2 * y
    if final_norm:
        x = (x * lax.rsqrt(jnp.mean(x * x, axis=-1, keepdims=True) + EPS)) * g_ref[...]
    o_ref[...] = x


def _final(pos, x1, route, mod3, gf, y, *, seq, final_norm):
    n, d = x1.shape
    tm = TM_FINAL
    tps = seq // tm
    pos3 = _tile_pos(pos, tm)
    last = n // tm - 1
    return pl.pallas_call(
        functools.partial(_final_kernel, tm=tm, final_norm=final_norm),
        grid=(n // tm,),
        in_specs=[pl.BlockSpec((1, 1, EXPERT_TOPK * tm), lambda i: (i, 0, 0), memory_space=pltpu.SMEM),
                  pl.BlockSpec((1, 1, EXPERT_TOPK * tm), lambda i: (jnp.minimum(i + 1, last), 0, 0),
                               memory_space=pltpu.SMEM),
                  pl.BlockSpec((tm, d), lambda i: (i, 0)),
                  pl.BlockSpec((tm, LANES), lambda i: (i, 0)),
                  pl.BlockSpec((1, 1, N_MOD * d), lambda i: (i // tps, 0, 0)),
                  pl.BlockSpec((1, d), lambda i: (0, 0)),
                  pl.BlockSpec(memory_space=pl.ANY)],
        out_specs=pl.BlockSpec((tm, d), lambda i: (i, 0)),
        out_shape=jax.ShapeDtypeStruct((n, d), F32),
        scratch_shapes=[pltpu.VMEM((2, EXPERT_TOPK, tm, d), F32), pltpu.SemaphoreType.DMA((2,))],
        compiler_params=_cparams("arbitrary"),
        name="moe_combine_final_norm",
    )(pos3, pos3, x1, route, mod3, gf, y)


def _split_bf16(w):
    hi = w.astype(BF16)
    return hi, (w - hi.astype(F32)).astype(BF16)


def kernel(x, c, w_ada, b_ada, norm1_g, w_in, b_gate, rel_bias, pool_w, pool_scale, w_branch_attn,
           w_branch_pool, w_out, norm2_g, w_router_group, b_router_group, w_router_expert,
           b_router_expert, w_expert_gate, w_expert_up, w_expert_down, norm_f_g):
    batch, seq, d = x.shape
    depth = w_ada.shape[0]
    n = batch * seq
    nb = seq // MOBA_BLOCK
    qkvu = 3 * ATTN_WIDTH + POOL_WIDTH
    bias_w = _bias_vectors(rel_bias)
    c_pad = jnp.pad(c, ((0, 8 - batch), (0, 0)))

    xc = x.reshape(n, d)
    for l in range(depth):
        mod3 = _modulation(c_pad, w_ada[l], b_ada[l][None])[:batch].reshape(batch, 1, N_MOD * d)
        w_in_b = w_in[l].astype(BF16)
        g1 = norm1_g[l][None]

        assert w_in_b.shape[1] == 2 * qkvu == qkvu + 2 * d
        qT, k2, vT, u, km, h1 = _in_projection(xc, mod3, g1, w_in_b, batch=batch, seq=seq)
        km_pad = jnp.pad(km.reshape(batch, nb, ATTN_WIDTH), ((0, 0), (0, MAX_BLOCKS - nb), (0, 0)))
        attnT = _moba_attention(qT, k2.reshape(batch, seq, ATTN_WIDTH), vT, km_pad, bias_w,
                                batch=batch, seq=seq)

        w_r = jnp.concatenate([w_router_group[l], w_router_expert[l]], axis=1)
        w_r = jnp.pad(w_r, ((0, 0), (0, LANES - w_r.shape[1])))
        b_r = jnp.concatenate([b_router_group[l], b_router_expert[l]])
        b_r = jnp.pad(b_r, (0, LANES - b_r.shape[0]))[None]
        wr_hi, wr_lo = _split_bf16(w_r)
        x1, h2r, route, route_t, counts, xz = _post_attention(
            xc, mod3, h1, w_in_b, b_gate[l][None], attnT, u, pool_w[l].astype(BF16),
            pool_scale[l][None], w_branch_attn[l].astype(BF16), w_branch_pool[l].astype(BF16),
            w_out[l].astype(BF16), norm2_g[l][None], wr_hi, wr_lo, b_r, batch=batch, seq=seq)

        pos, tile_expert, n_used, next_expert = _routing_plan(route_t, counts, n=n)
        xs = _dispatch(pos, h2r, xz, n=n)
        y = _experts(tile_expert, n_used, next_expert, xs, w_expert_gate[l], w_expert_up[l],
                     w_expert_down[l])
        xc = _final(pos, x1, route, mod3, norm_f_g[None], y, seq=seq, final_norm=(l == depth - 1))
    return xc.reshape(batch, seq, d)
```

```python
import functools
import math

import numpy as np
import jax
import jax.numpy as jnp
from jax import lax
from jax.experimental import pallas as pl
from jax.experimental.pallas import tpu as pltpu

F32 = jnp.float32
BF16 = jnp.bfloat16

D_MODEL = 1024
N_HEADS = 8
HEAD_DIM = 64
ATTN_WIDTH = N_HEADS * HEAD_DIM
MOBA_BLOCK = 256
MOBA_TOPK = 3
MAX_BLOCKS = 32
FAR_GROUP = 4
HEADS_PER_STEP = 4
ZERO_ROWS = 256
ONES_ROWS = 16
LOG2E = math.log2(math.e)
POOL_WINDOWS = (2, 4, 8, 16)
POOL_WIDTH = 512
POOL_GROUP_WIDTH = 128
POOL_HALO = 16
REL_BUCKETS = 32
REL_MAX_DIST = 128
N_GROUPS = 4
EXPERTS_PER_GROUP = 8
N_EXPERTS = N_GROUPS * EXPERTS_PER_GROUP
D_EXPERT = 512
N_MOD = 6
EPS = 1e-6
NEG_INF = -1e30
LANES = 128
ROUTER_LANE0 = N_GROUPS
ROUTE_FIELDS = 8
ROUTER_ROWS = 48
VMEM_LIMIT = 56 * 1024 * 1024

TM_PROJ = 1024
TM_POST = 512
TM_DISPATCH = 2048
TE_ROWS = 512
EXPERT_TOPK = 2
TM_FINAL = 256


def _cparams(*sem):
    return pltpu.CompilerParams(dimension_semantics=sem, vmem_limit_bytes=VMEM_LIMIT)


def _dot(a, b):
    return jnp.dot(a, b, preferred_element_type=F32)


def _rms_mod(x, g, scale, shift):
    xn = x * lax.rsqrt(jnp.mean(x * x, axis=-1, keepdims=True) + EPS)
    return (xn * g) * (1.0 + scale) + shift


def _mod_kernel(c_ref, w_ref, b_ref, o_ref):
    c = c_ref[...]
    ca = c * jax.nn.sigmoid(c)
    o_ref[...] = _dot(ca.astype(BF16), w_ref[...].astype(BF16)) + b_ref[...]


def _modulation(c_pad, w_ada, b_ada):
    rows, d = c_pad.shape
    n_out = w_ada.shape[1]
    tn = 1024
    return pl.pallas_call(
        _mod_kernel,
        grid=(n_out // tn,),
        in_specs=[pl.BlockSpec((rows, d), lambda j: (0, 0)),
                  pl.BlockSpec((d, tn), lambda j: (0, j)),
                  pl.BlockSpec((1, tn), lambda j: (0, j))],
        out_specs=pl.BlockSpec((rows, tn), lambda j: (0, j)),
        out_shape=jax.ShapeDtypeStruct((rows, n_out), F32),
        compiler_params=_cparams("arbitrary"),
        name="adaln_mod",
    )(c_pad, w_ada, b_ada)


def _inproj_kernel(x_ref, mod_ref, g_ref, w_ref, qT_ref, k_ref, vT_ref, u_ref, km_ref, h_ref, *, tm):
    d = D_MODEL
    mod = mod_ref[0]
    h = _rms_mod(x_ref[...], g_ref[...], mod[:, d:2 * d], mod[:, 0:d]).astype(BF16)
    h_ref[...] = h
    proj = _dot(h, w_ref[...])
    aw = ATTN_WIDTH
    q = proj[:, 0:aw] * (HEAD_DIM ** -0.5 * LOG2E)
    k = proj[:, aw:2 * aw]
    v = proj[:, 2 * aw:3 * aw]
    u_ref[...] = proj[:, 3 * aw:3 * aw + POOL_WIDTH]
    k_ref[...] = k.astype(BF16)
    qT = q.T.astype(BF16)
    vT = v.T.astype(BF16)
    for c in range(tm // MOBA_BLOCK):
        sl = slice(c * MOBA_BLOCK, (c + 1) * MOBA_BLOCK)
        qT_ref[0, c] = qT[:, sl]
        vT_ref[0, c] = vT[:, sl]
        km_ref[0, c:c + 1, :] = jnp.mean(k[sl, :], axis=0, keepdims=True)


def _in_projection(x2, mod3, g1, w_qkvu, *, batch, seq):
    n, d = x2.shape
    tm = TM_PROJ
    tps = seq // tm
    nb = seq // MOBA_BLOCK
    cb = tm // MOBA_BLOCK
    wcols = 3 * ATTN_WIDTH + POOL_WIDTH
    assert w_qkvu.shape[1] % wcols == 0
    kern = functools.partial(_inproj_kernel, tm=tm)
    t_shape = jax.ShapeDtypeStruct((batch, nb, ATTN_WIDTH, MOBA_BLOCK), BF16)
    t_spec = pl.BlockSpec((1, cb, ATTN_WIDTH, MOBA_BLOCK), lambda i: (i // tps, i % tps, 0, 0))
    return pl.pallas_call(
        kern,
        grid=(n // tm,),
        in_specs=[pl.BlockSpec((tm, d), lambda i: (i, 0)),
                  pl.BlockSpec((1, 1, N_MOD * d), lambda i: (i // tps, 0, 0)),
                  pl.BlockSpec((1, d), lambda i: (0, 0)),
                  pl.BlockSpec((d, wcols), lambda i: (0, 0))],
        out_specs=[t_spec,
                   pl.BlockSpec((tm, ATTN_WIDTH), lambda i: (i, 0)),
                   t_spec,
                   pl.BlockSpec((tm, POOL_WIDTH), lambda i: (i, 0)),
                   pl.BlockSpec((1, cb, ATTN_WIDTH), lambda i: (i, 0, 0)),
                   pl.BlockSpec((tm, d), lambda i: (i, 0))],
        out_shape=[t_shape,
                   jax.ShapeDtypeStruct((n, ATTN_WIDTH), BF16),
                   t_shape,
                   jax.ShapeDtypeStruct((n, POOL_WIDTH), F32),
                   jax.ShapeDtypeStruct((n // tm, cb, ATTN_WIDTH), F32),
                   jax.ShapeDtypeStruct((n, d), BF16)],
        compiler_params=_cparams("arbitrary"),
        name="norm1_inproj",
    )(x2, mod3, g1, w_qkvu)


def _attn_kernel(qT_ref, k_hbm, v_hbm, km_ref, bw_ref, o_ref, xz_hbm, kaug_ref, vaug_ref, bias_ref, qnear_ref,
                 qfar_ref, sn_ref, sf0_ref, sf1_ref, k_ref, vT_ref, kv_sems, zero_ref, zero_sem, *, nb, n_steps):
    blk = MOBA_BLOCK
    hd = HEAD_DIM
    nh = HEADS_PER_STEP
    heads = range(nh)
    lane = lax.broadcasted_iota(jnp.int32, (blk, LANES), 1)

    def pair_lanes(hh):
        return slice((hh // 2) * LANES, (hh // 2 + 1) * LANES)

    def head_rows(hh):
        return slice(hh * hd, (hh + 1) * hd)

    ones = jnp.ones((ONES_ROWS, blk), BF16)

    width = nh * hd
    n_inner = pl.num_programs(1)
    step_id = pl.program_id(0) * n_inner + pl.program_id(1)

    def kv_copies(sid):
        bb, col = sid // n_inner, pl.multiple_of((sid % n_inner) * width, width)
        return (pltpu.make_async_copy(k_hbm.at[bb, :, pl.ds(col, width)], k_ref, kv_sems.at[0]),
                pltpu.make_async_copy(v_hbm.at[bb, :, pl.ds(col, width), :], vT_ref, kv_sems.at[1]))

    @pl.when(step_id == 0)
    def _():
        for cp in kv_copies(step_id):
            cp.start()

    for cp in kv_copies(step_id):
        cp.wait()

    def build_operands(j, carry):
        rows = pl.ds(pl.multiple_of(j * blk, blk), blk)
        for hh in heads:
            kp = k_ref[rows, pair_lanes(hh)]
            if hh % 2 == 0:
                kaug_ref[hh, rows, :] = jnp.where(lane < hd, kp, jnp.where(lane == hd + j, 1.0, 0.0).astype(BF16))
            else:
                kaug_ref[hh, rows, :] = jnp.where(lane >= hd, kp, jnp.where(lane == j, 1.0, 0.0).astype(BF16))
            vaug_ref[j, hh] = jnp.concatenate([vT_ref[j, head_rows(hh), :], ones], axis=0)
        return carry

    lax.fori_loop(0, nb, build_operands, 0)

    @pl.when(step_id + 1 < pl.num_programs(0) * n_inner)
    def _():
        for cp in kv_copies(step_id + 1):
            cp.start()

    @pl.when(step_id == 0)
    def _():
        zero_ref[...] = jnp.zeros_like(zero_ref)

    zrows = zero_ref.shape[0]
    zcount = xz_hbm.shape[0] // (n_steps * zrows)

    def zero_copy(q):
        return pltpu.make_async_copy(zero_ref, xz_hbm.at[pl.ds((step_id * zcount + q) * zrows, zrows)], zero_sem)

    def start_zero(q, carry):
        zero_copy(q).start()
        return carry

    def wait_zero(q, carry):
        zero_copy(q).wait()
        return carry

    lax.fori_loop(0, zcount, start_zero, 0)

    for hh in heads:
        for t in range(2):
            wide = jnp.broadcast_to(bw_ref[hh, t:t + 1, :], (blk, 2 * blk))
            bias_ref[hh, t] = pltpu.roll(wide, 0, 1, stride=1, stride_axis=0)[:, blk:2 * blk]

    rowi = lax.broadcasted_iota(jnp.int32, (MAX_BLOCKS, blk), 0)
    km_lane = lax.broadcasted_iota(jnp.int32, (MAX_BLOCKS, LANES), 1)
    zpad = jnp.zeros((LANES - hd - MAX_BLOCKS, blk), BF16)
    km_split = []
    for hh in heads:
        sub = hh % 2
        in_head = (km_lane >= sub * hd) & (km_lane < (sub + 1) * hd)
        km_h = jnp.where(in_head, km_ref[0, :, pair_lanes(hh)], 0.0)
        km_hi = km_h.astype(BF16)
        km_split.append((km_hi, (km_h - km_hi.astype(F32)).astype(BF16)))

    grp = FAR_GROUP

    def retire_begin(state, mx):
        m, acc = state
        m_new = jnp.maximum(m, mx)
        return m_new, jnp.exp2(m - m_new) * acc

    def retire_slice(src_ref, hh, m_new, acc, g, j):
        p = jnp.exp2(src_ref[hh, g * blk:(g + 1) * blk, :] - m_new)
        return acc + _dot(vaug_ref[j, hh], p.astype(BF16))

    def fresh():
        return (jnp.full((1, blk), -jnp.inf, F32), jnp.zeros((hd + ONES_ROWS, blk), F32))

    def near_blocks(i):
        return tuple(jnp.where(i - (grp - 1) + g < 0, i + 1, i - (grp - 1) + g) for g in range(grp))

    def finish(i, states):
        for hh in heads:
            _, acc = states[hh]
            o_ref[0, i, head_rows(hh), :] = (acc[0:hd] / acc[hd:hd + 1]).astype(BF16)

    def select(i):
        for hh in heads:
            qp = qT_ref[0, i, pair_lanes(hh), :]
            km_hi, km_lo = km_split[hh]
            gate = _dot(km_hi, qp) + _dot(km_lo, qp)
            g = jnp.where(rowi < i, gate, -jnp.inf)
            sel = rowi == i
            for _ in range(MOBA_TOPK):
                m = jnp.max(g, axis=0, keepdims=True)
                hit = (g == m) & (m > -jnp.inf)
                idx = jnp.min(jnp.where(hit, rowi, MAX_BLOCKS), axis=0, keepdims=True)
                pick = rowi == idx
                sel = sel | pick
                g = jnp.where(pick, -jnp.inf, g)
            near = jnp.where(sel, 0.0, NEG_INF)
            far = jnp.where(rowi > i - grp, NEG_INF, near)
            qh = qp[head_rows(hh % 2), :]
            if hh % 2 == 0:
                qnear_ref[hh] = jnp.concatenate([qh, near.astype(BF16), zpad], axis=0)
                qfar_ref[hh] = jnp.concatenate([qh, far.astype(BF16), zpad], axis=0)
            else:
                qnear_ref[hh] = jnp.concatenate([near.astype(BF16), zpad, qh], axis=0)
                qfar_ref[hh] = jnp.concatenate([far.astype(BF16), zpad, qh], axis=0)

    def park(dst_ref, q_ref, hh, g, j, bias=None):
        rows = pl.ds(pl.multiple_of(j * blk, blk), blk)
        s = _dot(kaug_ref[hh, rows, :], q_ref[hh])
        if bias is not None:
            s = s + bias
        dst_ref[hh, g * blk:(g + 1) * blk, :] = s
        return jnp.max(s, axis=0, keepdims=True)

    def park_near(i):
        idx = near_blocks(i)

        def quarter(g, hh):
            bias = None
            if g == grp - 1:
                bias = bias_ref[hh, 0]
            elif g == grp - 2:
                bias = bias_ref[hh, 1]
            return park(sn_ref, qnear_ref, hh, g, idx[g], bias)
        return quarter

    def park_far(c, dst_ref):
        return lambda g, hh: park(dst_ref, qfar_ref, hh, g, c * grp + g)

    def chunk_blocks(c):
        return tuple(c * grp + g for g in range(grp))

    def step(src_ref, states, maxes, vidx, quarter, park_first):
        begun = [retire_begin(states[hh], maxes[hh]) for hh in heads] if states is not None else None
        accs = [b[1] for b in begun] if begun else None
        new_max = [None] * nh

        def park_quarter(g):
            for hh in heads:
                cm = quarter(g, hh)
                new_max[hh] = cm if new_max[hh] is None else jnp.maximum(new_max[hh], cm)

        for g in range(grp):
            if park_first:
                park_quarter(g)
            if begun:
                for hh in heads:
                    accs[hh] = retire_slice(src_ref, hh, begun[hh][0], accs[hh], g, vidx[g])
            if not park_first and quarter is not None:
                park_quarter(g)
        new_states = tuple((begun[hh][0], accs[hh]) for hh in heads) if begun else None
        return new_states, tuple(new_max)

    def fresh_states():
        return tuple(fresh() for _ in heads)

    def early_tile(i, near_max):
        nxt = jnp.minimum(i + 1, nb - 1)
        select(nxt)
        states, next_max = step(sn_ref, fresh_states(), near_max, near_blocks(i), park_near(nxt), False)
        finish(i, states)
        return next_max

    def late_tile(i, near_max):
        sf = (sf0_ref, sf1_ref)
        n_chunks = (i - (grp - 1) + grp - 1) // grp
        nxt = jnp.minimum(i + 1, nb - 1)
        states, far_max = step(sn_ref, fresh_states(), near_max, near_blocks(i), park_far(0, sf[0]), True)

        def pair(k, st):
            states, far_max = st
            states, far_max = step(sf[0], states, far_max, chunk_blocks(2 * k), park_far(2 * k + 1, sf[1]), True)
            return step(sf[1], states, far_max, chunk_blocks(2 * k + 1), park_far(2 * k + 2, sf[0]), True)

        states, far_max = lax.fori_loop(0, (n_chunks - 1) // 2, pair, (states, far_max))

        def last(src, states, far_max):
            select(nxt)
            states, next_max = step(src, states, far_max, chunk_blocks(n_chunks - 1), park_near(nxt), True)
            finish(i, states)
            return next_max

        def even_count(st):
            states, far_max = step(sf[0], st[0], st[1], chunk_blocks(n_chunks - 2),
                                   park_far(n_chunks - 1, sf[1]), True)
            return last(sf[1], states, far_max)

        def odd_count(st):
            return last(sf[0], st[0], st[1])

        return lax.cond(n_chunks % 2 == 0, even_count, odd_count, (states, far_max))

    n_early = min(grp, nb)
    select(0)
    _, near_max = step(None, None, None, None, park_near(0), True)
    near_max = lax.fori_loop(0, n_early, early_tile, near_max)
    lax.fori_loop(n_early, nb, late_tile, near_max)
    lax.fori_loop(0, zcount, wait_zero, 0)


def _rel_bucket_table(max_rel):
    n = np.arange(max_rel)
    max_exact = REL_BUCKETS // 2
    nf = np.maximum(n, 1).astype(np.float32)
    large = max_exact + (np.log(nf / np.float32(max_exact)) / np.float32(math.log(REL_MAX_DIST / max_exact))
                         * np.float32(REL_BUCKETS - max_exact)).astype(np.int32)
    large = np.minimum(large, REL_BUCKETS - 1)
    return np.where(n < max_exact, n, large)


def _bias_vectors(rel_bias):
    blk = MOBA_BLOCK
    table = _rel_bucket_table(2 * blk)
    r = np.arange(-blk, blk)
    bt = (rel_bias.T - rel_bias[REL_BUCKETS - 1][:, None]) * LOG2E
    own = jnp.where(jnp.asarray(r >= 0)[None], bt[:, table[np.maximum(r, 0)]], NEG_INF)
    prev = bt[:, table[np.minimum(r + blk, 2 * blk - 1)]]
    return jnp.stack([own, prev], axis=1).astype(F32)


def _moba_attention(qT, k3, vT, km_pad, bias_w, *, batch, seq):
    nb = seq // MOBA_BLOCK
    blk = MOBA_BLOCK
    assert nb % FAR_GROUP == 0 and nb <= MAX_BLOCKS
    nh = HEADS_PER_STEP
    width = nh * HEAD_DIM
    assert nh % 2 == 0 and N_HEADS % nh == 0
    n_steps = batch * (N_HEADS // nh)
    sorted_rows = _expert_tiles(batch * seq) * TE_ROWS
    assert sorted_rows % (n_steps * ZERO_ROWS) == 0
    kern = functools.partial(_attn_kernel, nb=nb, n_steps=n_steps)

    hbm = pl.BlockSpec(memory_space=pl.ANY)
    return pl.pallas_call(
        kern,
        grid=(batch, N_HEADS // nh),
        in_specs=[pl.BlockSpec((1, nb, width, blk), lambda b, p: (b, 0, p, 0)),
                  hbm, hbm,
                  pl.BlockSpec((1, MAX_BLOCKS, width), lambda b, p: (b, 0, p)),
                  pl.BlockSpec((nh, 2, 2 * blk), lambda b, p: (p, 0, 0))],
        out_specs=[pl.BlockSpec((1, nb, width, blk), lambda b, p: (b, 0, p, 0)), hbm],
        out_shape=[jax.ShapeDtypeStruct((batch, nb, ATTN_WIDTH, blk), BF16),
                   jax.ShapeDtypeStruct((sorted_rows, D_MODEL), F32)],
        scratch_shapes=[pltpu.VMEM((nh, seq, LANES), BF16),
                        pltpu.VMEM((nb, nh, HEAD_DIM + ONES_ROWS, blk), BF16),
                        pltpu.VMEM((nh, 2, blk, blk), F32),
                        pltpu.VMEM((nh, LANES, blk), BF16),
                        pltpu.VMEM((nh, LANES, blk), BF16),
                        pltpu.VMEM((nh, FAR_GROUP * blk, blk), F32),
                        pltpu.VMEM((nh, FAR_GROUP * blk, blk), F32),
                        pltpu.VMEM((nh, FAR_GROUP * blk, blk), F32),
                        pltpu.VMEM((seq, width), BF16),
                        pltpu.VMEM((nb, width, blk), BF16),
                        pltpu.SemaphoreType.DMA((2,)),
                        pltpu.VMEM((ZERO_ROWS, D_MODEL), F32),
                        pltpu.SemaphoreType.DMA(())],
        compiler_params=_cparams("arbitrary", "arbitrary"),
        name="moba_attention",
    )(qT, k3, vT, km_pad, bias_w)


def _post_kernel(x_ref, mod_ref, h_ref, wz_ref, bz_ref, at_ref, u_ref, uh_ref, pw_ref, ps_ref,
                 wba_ref, wbp_ref, wo_ref, g2_ref, wrh_ref, wrl_ref, br_ref,
                 x1_ref, h2r_ref, route_ref, route_t_ref, counts_ref, carry_ref, *, tm, seq):
    d = D_MODEL
    i = pl.program_id(0)
    x = x_ref[...]
    mod = mod_ref[0]
    gate1 = mod[:, 2 * d:3 * d]
    shift2, scale2 = mod[:, 3 * d:4 * d], mod[:, 4 * d:5 * d]

    zg = 0.5 * jnp.tanh(0.5 * (_dot(h_ref[...], wz_ref[...]) + bz_ref[...])) + 0.5

    ya = jnp.concatenate(
        [lax.dot_general(at_ref[0, c], wba_ref[...], (((0,), (0,)), ((), ())),
                         preferred_element_type=F32) for c in range(tm // MOBA_BLOCK)], axis=0)

    first = (i * tm) % seq == 0
    halo = jnp.where(first, 0.0, uh_ref[...])
    ubuf = jnp.concatenate([halo, u_ref[...]], axis=0)
    rows = tm + POOL_HALO
    pos = (lax.broadcasted_iota(jnp.int32, (tm, POOL_GROUP_WIDTH), 0) + i * tm) % seq
    outs = []
    for gi, w in enumerate(POOL_WINDOWS):
        e = ubuf[:, gi * POOL_GROUP_WIDTH:(gi + 1) * POOL_GROUP_WIDTH]
        s = e
        step = 1
        while step < w:
            s = s + pltpu.roll(s, step, 0)
            step *= 2
        cnt = jnp.minimum(pos + 1, w).astype(F32)
        dlt = s[POOL_HALO:rows] / cnt - e[POOL_HALO:rows]
        outs.append(_dot(dlt.astype(BF16), pw_ref[gi]))
    pooled = jnp.concatenate(outs, axis=-1) * ps_ref[...]
    yp = _dot(pooled.astype(BF16), wbp_ref[...])

    merged = zg[:, 0:d] * ya + zg[:, d:2 * d] * yp
    x1 = x + gate1 * _dot(merged.astype(BF16), wo_ref[...])
    x1_ref[...] = x1

    h2 = _rms_mod(x1, g2_ref[...], scale2, shift2)
    h2_hi = h2.astype(BF16)
    h2r_ref[...] = h2_hi.astype(F32)
    h2_lo = (h2 - h2_hi.astype(F32)).astype(BF16)
    z = _dot(h2_hi, wrh_ref[...]) + _dot(h2_lo, wrh_ref[...]) + _dot(h2_hi, wrl_ref[...]) + br_ref[...]

    rr = ROUTER_ROWS
    zt = z.T[0:rr, :]
    row = lax.broadcasted_iota(jnp.int32, (rr, tm), 0)
    zgrp = jnp.where(row < N_GROUPS, zt, -jnp.inf)
    mg = jnp.max(zgrp, axis=0, keepdims=True)
    pg_top = 1.0 / jnp.sum(jnp.exp(zgrp - mg), axis=0, keepdims=True)
    g_idx = jnp.min(jnp.where(zgrp == mg, row, rr), axis=0, keepdims=True)
    e_row = row - ROUTER_LANE0
    in_grp = (e_row >= 0) & (e_row < N_EXPERTS) & ((e_row >> 3) == g_idx)
    ze = jnp.where(in_grp, zt, -jnp.inf)
    m1 = jnp.max(ze, axis=0, keepdims=True)
    i1 = jnp.min(jnp.where(ze == m1, row, rr), axis=0, keepdims=True)
    ze2 = jnp.where(row == i1, -jnp.inf, ze)
    m2 = jnp.max(ze2, axis=0, keepdims=True)
    i2 = jnp.min(jnp.where(ze2 == m2, row, rr), axis=0, keepdims=True)
    e2 = jnp.exp(m2 - m1)
    w1 = pg_top / (1.0 + e2)
    w2 = pg_top * e2 / (1.0 + e2)

    hot = jnp.where((row == i1) | (row == i2), 1.0, 0.0)
    tri = jnp.where(lax.broadcasted_iota(jnp.int32, (tm, tm), 0) <= lax.broadcasted_iota(jnp.int32, (tm, tm), 1),
                    1.0, 0.0).astype(BF16)
    prefix = _dot(hot.astype(BF16), tri)

    @pl.when(i == 0)
    def _():
        carry_ref[...] = jnp.zeros_like(carry_ref)

    carry = carry_ref[...]
    base = carry[:, 0:1] + prefix - 1.0
    rank1 = jnp.sum(jnp.where(row == i1, base, 0.0), axis=0, keepdims=True)
    rank2 = jnp.sum(jnp.where(row == i2, base, 0.0), axis=0, keepdims=True)
    carry_ref[...] = carry + prefix[:, tm - 1:tm]
    counts_ref[...] = carry_ref[...]
    fields = ((i1 - ROUTER_LANE0).astype(F32), (i2 - ROUTER_LANE0).astype(F32), w1, w2, rank1, rank2)
    frow = lax.broadcasted_iota(jnp.int32, (ROUTE_FIELDS, tm), 0)
    route_t = jnp.zeros((ROUTE_FIELDS, tm), F32)
    for k, val in enumerate(fields):
        route_t = jnp.where(frow == k, val, route_t)
    route_t_ref[...] = route_t
    route_ref[...] = jnp.concatenate([route_t, jnp.zeros((LANES - ROUTE_FIELDS, tm), F32)], axis=0).T


def _post_attention(x2, mod3, h1, w_z, b_z, attnT, u, pool_w, pool_scale, w_ba, w_bp, w_out, g2,
                    wr_hi, wr_lo, b_r, *, batch, seq):
    n, d = x2.shape
    tm = TM_POST
    tps = seq // tm
    cb = tm // MOBA_BLOCK
    hb = tm // POOL_HALO
    kern = functools.partial(_post_kernel, tm=tm, seq=seq)
    const2 = lambda i: (0, 0)
    return pl.pallas_call(
        kern,
        grid=(n // tm,),
        in_specs=[pl.BlockSpec((tm, d), lambda i: (i, 0)),
                  pl.BlockSpec((1, 1, N_MOD * d), lambda i: (i // tps, 0, 0)),
                  pl.BlockSpec((tm, d), lambda i: (i, 0)),
                  pl.BlockSpec((d, 2 * d), lambda i: (0, 1)),
                  pl.BlockSpec((1, 2 * d), const2),
                  pl.BlockSpec((1, cb, ATTN_WIDTH, MOBA_BLOCK), lambda i: (i // tps, i % tps, 0, 0)),
                  pl.BlockSpec((tm, POOL_WIDTH), lambda i: (i, 0)),
                  pl.BlockSpec((POOL_HALO, POOL_WIDTH), lambda i: (jnp.maximum(i * hb - 1, 0), 0)),
                  pl.BlockSpec((len(POOL_WINDOWS), POOL_GROUP_WIDTH, POOL_GROUP_WIDTH), lambda i: (0, 0, 0)),
                  pl.BlockSpec((1, POOL_WIDTH), const2),
                  pl.BlockSpec((ATTN_WIDTH, d), const2),
                  pl.BlockSpec((POOL_WIDTH, d), const2),
                  pl.BlockSpec((d, d), const2),
                  pl.BlockSpec((1, d), const2),
                  pl.BlockSpec((d, LANES), const2),
                  pl.BlockSpec((d, LANES), const2),
                  pl.BlockSpec((1, LANES), const2)],
        out_specs=[pl.BlockSpec((tm, d), lambda i: (i, 0)),
                   pl.BlockSpec((tm, d), lambda i: (i, 0)),
                   pl.BlockSpec((tm, LANES), lambda i: (i, 0)),
                   pl.BlockSpec((ROUTE_FIELDS, tm), lambda i: (0, i)),
                   pl.BlockSpec((ROUTER_ROWS, LANES), const2)],
        out_shape=[jax.ShapeDtypeStruct((n, d), F32),
                   jax.ShapeDtypeStruct((n, d), F32),
                   jax.ShapeDtypeStruct((n, LANES), F32),
                   jax.ShapeDtypeStruct((ROUTE_FIELDS, n), F32),
                   jax.ShapeDtypeStruct((ROUTER_ROWS, LANES), F32)],
        scratch_shapes=[pltpu.VMEM((ROUTER_ROWS, LANES), F32)],
        compiler_params=_cparams("arbitrary"),
        name="merge_outproj_router",
    )(x2, mod3, h1, w_z, b_z, attnT, u, u, pool_w, pool_scale, w_ba, w_bp, w_out, g2, wr_hi, wr_lo, b_r)


def _routing_plan(route_t, counts, *, n):
    te = TE_ROWS
    n_tiles = _expert_tiles(n)
    cnt = counts[ROUTER_LANE0:ROUTER_LANE0 + N_EXPERTS, 0].astype(jnp.int32)
    padded = ((cnt + te - 1) // te) * te
    ids = jnp.arange(N_EXPERTS, dtype=jnp.int32)
    ends = jnp.sum(jnp.where(ids[None, :] <= ids[:, None], padded[None, :], 0), axis=-1)
    starts = ends - padded
    pos = []
    for k in range(EXPERT_TOPK):
        expert = route_t[k].astype(jnp.int32)
        rank = route_t[4 + k].astype(jnp.int32)
        pos.append(jnp.sum(jnp.where(expert[:, None] == ids, starts, 0), axis=-1) + rank)
    tiles = jnp.arange(n_tiles, dtype=jnp.int32)
    tile_expert = jnp.sum((ends // te)[None, :] <= tiles[:, None], axis=-1)
    tile_expert = jnp.minimum(tile_expert, N_EXPERTS - 1).astype(jnp.int32)
    n_used = (ends[N_EXPERTS - 1:] // te).astype(jnp.int32)
    later = (ids[None, :] > ids[:, None]) & (cnt[None, :] > 0)
    next_expert = jnp.min(jnp.where(later, ids[None, :], N_EXPERTS), axis=-1)
    next_expert = jnp.where(next_expert == N_EXPERTS, -1, next_expert).astype(jnp.int32)
    return pos, tile_expert, n_used, next_expert


def _expert_tiles(n):
    return (EXPERT_TOPK * n) // TE_ROWS + N_EXPERTS


def _tile_pos(pos, tm):
    return jnp.concatenate([p.reshape(-1, 1, tm) for p in pos], axis=-1)


def _dispatch_kernel(pos_ref, h_ref, xz_ref, xs_ref, sem, *, tm):
    del xz_ref
    for r in range(tm):
        for k in range(EXPERT_TOPK):
            pltpu.make_async_copy(h_ref.at[pl.ds(r, 1)], xs_ref.at[pl.ds(pos_ref[0, 0, k * tm + r], 1)],
                                  sem).start(priority=(r + k) % 2)
    for k in range(EXPERT_TOPK):
        pltpu.make_async_copy(h_ref, h_ref, sem).wait()


def _dispatch(pos, h2r, xz, *, n):
    tm = TM_DISPATCH
    rows, d = xz.shape
    assert rows == _expert_tiles(n) * TE_ROWS
    pos3 = _tile_pos(pos, tm)
    return pl.pallas_call(
        functools.partial(_dispatch_kernel, tm=tm),
        grid=(n // tm,),
        in_specs=[pl.BlockSpec((1, 1, EXPERT_TOPK * tm), lambda i: (i, 0, 0), memory_space=pltpu.SMEM),
                  pl.BlockSpec((tm, d), lambda i: (i, 0)),
                  pl.BlockSpec(memory_space=pl.ANY)],
        out_specs=pl.BlockSpec(memory_space=pl.ANY),
        out_shape=jax.ShapeDtypeStruct((rows, d), F32),
        scratch_shapes=[pltpu.SemaphoreType.DMA(())],
        input_output_aliases={2: 0},
        compiler_params=_cparams("arbitrary"),
        name="moe_dispatch",
    )(pos3, h2r, xz)


def _experts_kernel(te_ref, nu_ref, nx_ref, xs_ref, wg_ref, wu_ref, wd_ref, y_ref,
                    wg_b, wu_b, wd_b, wg_f, wu_f, wd_f, sems, turn_ref):
    t = pl.program_id(0)

    def fetch(e, slot):
        return [pltpu.make_async_copy(src.at[e], dst.at[slot], sems.at[slot])
                for src, dst in ((wg_ref, wg_f), (wu_ref, wu_f), (wd_ref, wd_f))]

    @pl.when(t < nu_ref[0])
    def _():
        e = te_ref[t]
        e_prev = te_ref[jnp.maximum(t - 1, 0)]

        @pl.when(t == 0)
        def _():
            turn_ref[0] = 0
            for cp in fetch(e, 0):
                cp.start()

        @pl.when((t == 0) | (e != e_prev))
        def _():
            slot = turn_ref[0] % 2
            for cp in fetch(e, slot):
                cp.wait()
            wg_b[...] = wg_f[slot].astype(BF16)
            wu_b[...] = wu_f[slot].astype(BF16)
            wd_b[...] = wd_f[slot].astype(BF16)
            e_next = nx_ref[e]

            @pl.when(e_next >= 0)
            def _():
                for cp in fetch(e_next, 1 - slot):
                    cp.start()

            turn_ref[0] = turn_ref[0] + 1

        x = xs_ref[...].astype(BF16)
        g = _dot(x, wg_b[...])
        u = _dot(x, wu_b[...])
        act = (g * jax.nn.sigmoid(g)) * u
        y_ref[...] = _dot(act.astype(BF16), wd_b[...])

    @pl.when(t >= nu_ref[0])
    def _():
        y_ref[...] = jnp.zeros_like(y_ref)


def _experts(tile_expert, n_used, next_expert, xs, wg, wu, wd):
    rows, d = xs.shape
    te = TE_ROWS
    hbm = pl.BlockSpec(memory_space=pl.ANY)
    grid_spec = pltpu.PrefetchScalarGridSpec(
        num_scalar_prefetch=3,
        grid=(rows // te,),
        in_specs=[pl.BlockSpec((te, d), lambda t, te_r, nu, nx: (jnp.minimum(t, nu[0] - 1), 0)),
                  hbm, hbm, hbm],
        out_specs=pl.BlockSpec((te, d), lambda t, te_r, nu, nx: (t, 0)),
        scratch_shapes=[pltpu.VMEM((d, D_EXPERT), BF16), pltpu.VMEM((d, D_EXPERT), BF16),
                        pltpu.VMEM((D_EXPERT, d), BF16),
                        pltpu.VMEM((2, d, D_EXPERT), F32), pltpu.VMEM((2, d, D_EXPERT), F32),
                        pltpu.VMEM((2, D_EXPERT, d), F32),
                        pltpu.SemaphoreType.DMA((2,)), pltpu.SMEM((1,), jnp.int32)])
    return pl.pallas_call(
        _experts_kernel,
        grid_spec=grid_spec,
        out_shape=jax.ShapeDtypeStruct((rows, d), F32),
        compiler_params=_cparams("arbitrary"),
        name="moe_experts",
    )(tile_expert, n_used, next_expert, xs, wg, wu, wd)


def _final_kernel(pos_ref, pos_next_ref, x1_ref, route_ref, mod_ref, g_ref, y_ref, o_ref, ybuf, sems, *,
                  tm, final_norm):
    d = D_MODEL
    i = pl.program_id(0)
    slot = i % 2

    def gather(p_ref, s):
        for r in range(tm):
            for k in range(EXPERT_TOPK):
                pltpu.make_async_copy(y_ref.at[pl.ds(p_ref[0, 0, k * tm + r], 1)],
                                      ybuf.at[s, k, pl.ds(r, 1)], sems.at[s]).start(priority=(r + k) % 2)

    @pl.when(i == 0)
    def _():
        gather(pos_ref, 0)

    @pl.when(i + 1 < pl.num_programs(0))
    def _():
        gather(pos_next_ref, 1 - slot)

    for k in range(EXPERT_TOPK):
        pltpu.make_async_copy(ybuf.at[slot, k], ybuf.at[slot, k], sems.at[slot]).wait()

    route = route_ref[...]
    lane = lax.broadcasted_iota(jnp.int32, route.shape, 1)
    y = jnp.zeros((tm, d), F32)
    for k in range(EXPERT_TOPK):
        wk = jnp.sum(jnp.where(lane == 2 + k, route, 0.0), axis=-1, keepdims=True)
        y = y + wk * ybuf[slot, k]
    gate2 = mod_ref[0][:, 5 * d:6 * d]
    x = x1_ref[...] + gate2 * y
    if final_norm:
        x = (x * lax.rsqrt(jnp.mean(x * x, axis=-1, keepdims=True) + EPS)) * g_ref[...]
    o_ref[...] = x


def _final(pos, x1, route, mod3, gf, y, *, seq, final_norm):
    n, d = x1.shape
    tm = TM_FINAL
    tps = seq // tm
    pos3 = _tile_pos(pos, tm)
    last = n // tm - 1
    return pl.pallas_call(
        functools.partial(_final_kernel, tm=tm, final_norm=final_norm),
        grid=(n // tm,),
        in_specs=[pl.BlockSpec((1, 1, EXPERT_TOPK * tm), lambda i: (i, 0, 0), memory_space=pltpu.SMEM),
                  pl.BlockSpec((1, 1, EXPERT_TOPK * tm), lambda i: (jnp.minimum(i + 1, last), 0, 0),
                               memory_space=pltpu.SMEM),
                  pl.BlockSpec((tm, d), lambda i: (i, 0)),
                  pl.BlockSpec((tm, LANES), lambda i: (i, 0)),
                  pl.BlockSpec((1, 1, N_MOD * d), lambda i: (i // tps, 0, 0)),
                  pl.BlockSpec((1, d), lambda i: (0, 0)),
                  pl.BlockSpec(memory_space=pl.ANY)],
        out_specs=pl.BlockSpec((tm, d), lambda i: (i, 0)),
        out_shape=jax.ShapeDtypeStruct((n, d), F32),
        scratch_shapes=[pltpu.VMEM((2, EXPERT_TOPK, tm, d), F32), pltpu.SemaphoreType.DMA((2,))],
        compiler_params=_cparams("arbitrary"),
        name="moe_combine_final_norm",
    )(pos3, pos3, x1, route, mod3, gf, y)


def _split_bf16(w):
    hi = w.astype(BF16)
    return hi, (w - hi.astype(F32)).astype(BF16)


def kernel(x, c, w_ada, b_ada, norm1_g, w_in, b_gate, rel_bias, pool_w, pool_scale, w_branch_attn,
           w_branch_pool, w_out, norm2_g, w_router_group, b_router_group, w_router_expert,
           b_router_expert, w_expert_gate, w_expert_up, w_expert_down, norm_f_g):
    batch, seq, d = x.shape
    depth = w_ada.shape[0]
    n = batch * seq
    nb = seq // MOBA_BLOCK
    qkvu = 3 * ATTN_WIDTH + POOL_WIDTH
    bias_w = _bias_vectors(rel_bias)
    c_pad = jnp.pad(c, ((0, 8 - batch), (0, 0)))

    xc = x.reshape(n, d)
    for l in range(depth):
        mod3 = _modulation(c_pad, w_ada[l], b_ada[l][None])[:batch].reshape(batch, 1, N_MOD * d)
        w_in_b = w_in[l].astype(BF16)
        g1 = norm1_g[l][None]

        assert w_in_b.shape[1] == 2 * qkvu == qkvu + 2 * d
        qT, k2, vT, u, km, h1 = _in_projection(xc, mod3, g1, w_in_b, batch=batch, seq=seq)
        km_pad = jnp.pad(km.reshape(batch, nb, ATTN_WIDTH), ((0, 0), (0, MAX_BLOCKS - nb), (0, 0)))
        attnT, xz = _moba_attention(qT, k2.reshape(batch, seq, ATTN_WIDTH), vT, km_pad, bias_w,
                                    batch=batch, seq=seq)

        w_r = jnp.concatenate([w_router_group[l], w_router_expert[l]], axis=1)
        w_r = jnp.pad(w_r, ((0, 0), (0, LANES - w_r.shape[1])))
        b_r = jnp.concatenate([b_router_group[l], b_router_expert[l]])
        b_r = jnp.pad(b_r, (0, LANES - b_r.shape[0]))[None]
        wr_hi, wr_lo = _split_bf16(w_r)
        x1, h2r, route, route_t, counts = _post_attention(
            xc, mod3, h1, w_in_b, b_gate[l][None], attnT, u, pool_w[l].astype(BF16),
            pool_scale[l][None], w_branch_attn[l].astype(BF16), w_branch_pool[l].astype(BF16),
            w_out[l].astype(BF16), norm2_g[l][None], wr_hi, wr_lo, b_r, batch=batch, seq=seq)

        pos, tile_expert, n_used, next_expert = _routing_plan(route_t, counts, n=n)
        xs = _dispatch(pos, h2r, xz, n=n)
        y = _experts(tile_expert, n_used, next_expert, xs, w_expert_gate[l], w_expert_up[l],
                     w_expert_down[l])
        xc = _final(pos, x1, route, mod3, norm_f_g[None], y, seq=seq, final_norm=(l == depth - 1))
    return xc.reshape(batch, seq, d)
```

```python
import functools
import math

import numpy as np
import jax
import jax.numpy as jnp
from jax import lax
from jax.experimental import pallas as pl
from jax.experimental.pallas import tpu as pltpu

F32 = jnp.float32
BF16 = jnp.bfloat16

D_MODEL = 1024
N_HEADS = 8
HEAD_DIM = 64
ATTN_WIDTH = N_HEADS * HEAD_DIM
MOBA_BLOCK = 256
MOBA_TOPK = 3
MAX_BLOCKS = 32
FAR_GROUP = 4
HEADS_PER_STEP = 4
ZERO_ROWS = 256
ONES_ROWS = 16
LOG2E = math.log2(math.e)
POOL_WINDOWS = (2, 4, 8, 16)
POOL_WIDTH = 512
POOL_GROUP_WIDTH = 128
POOL_HALO = 16
REL_BUCKETS = 32
REL_MAX_DIST = 128
N_GROUPS = 4
EXPERTS_PER_GROUP = 8
N_EXPERTS = N_GROUPS * EXPERTS_PER_GROUP
D_EXPERT = 512
N_MOD = 6
EPS = 1e-6
NEG_INF = -1e30
LANES = 128
ROUTER_LANE0 = N_GROUPS
ROUTE_FIELDS = 8
ROUTER_ROWS = 48
VMEM_LIMIT = 56 * 1024 * 1024

TM_PROJ = 1024
TM_POST = 1024
TM_DISPATCH = 2048
TE_ROWS = 512
EXPERT_TOPK = 2
TM_FINAL = 256


def _cparams(*sem):
    return pltpu.CompilerParams(dimension_semantics=sem, vmem_limit_bytes=VMEM_LIMIT)


def _dot(a, b):
    return jnp.dot(a, b, preferred_element_type=F32)


def _rms_mod(x, g, scale, shift):
    xn = x * lax.rsqrt(jnp.mean(x * x, axis=-1, keepdims=True) + EPS)
    return (xn * g) * (1.0 + scale) + shift


def _mod_kernel(c_ref, w_ref, b_ref, o_ref):
    c = c_ref[...]
    ca = c * jax.nn.sigmoid(c)
    o_ref[...] = _dot(ca.astype(BF16), w_ref[...].astype(BF16)) + b_ref[...]


def _modulation(c_pad, w_ada, b_ada):
    rows, d = c_pad.shape
    n_out = w_ada.shape[1]
    tn = 1024
    return pl.pallas_call(
        _mod_kernel,
        grid=(n_out // tn,),
        in_specs=[pl.BlockSpec((rows, d), lambda j: (0, 0)),
                  pl.BlockSpec((d, tn), lambda j: (0, j)),
                  pl.BlockSpec((1, tn), lambda j: (0, j))],
        out_specs=pl.BlockSpec((rows, tn), lambda j: (0, j)),
        out_shape=jax.ShapeDtypeStruct((rows, n_out), F32),
        compiler_params=_cparams("arbitrary"),
        name="adaln_mod",
    )(c_pad, w_ada, b_ada)


def _inproj_kernel(x_ref, mod_ref, g_ref, w_ref, qT_ref, k_ref, vT_ref, u_ref, km_ref, h_ref, *, tm):
    d = D_MODEL
    mod = mod_ref[0]
    h = _rms_mod(x_ref[...], g_ref[...], mod[:, d:2 * d], mod[:, 0:d]).astype(BF16)
    h_ref[...] = h
    proj = _dot(h, w_ref[...])
    aw = ATTN_WIDTH
    q = proj[:, 0:aw] * (HEAD_DIM ** -0.5 * LOG2E)
    k = proj[:, aw:2 * aw]
    v = proj[:, 2 * aw:3 * aw]
    u_ref[...] = proj[:, 3 * aw:3 * aw + POOL_WIDTH]
    k_ref[...] = k.astype(BF16)
    qT = q.T.astype(BF16)
    vT = v.T.astype(BF16)
    for c in range(tm // MOBA_BLOCK):
        sl = slice(c * MOBA_BLOCK, (c + 1) * MOBA_BLOCK)
        qT_ref[0, c] = qT[:, sl]
        vT_ref[0, c] = vT[:, sl]
        km_ref[0, c:c + 1, :] = jnp.mean(k[sl, :], axis=0, keepdims=True)


def _in_projection(x2, mod3, g1, w_qkvu, *, batch, seq):
    n, d = x2.shape
    tm = TM_PROJ
    tps = seq // tm
    nb = seq // MOBA_BLOCK
    cb = tm // MOBA_BLOCK
    wcols = 3 * ATTN_WIDTH + POOL_WIDTH
    assert w_qkvu.shape[1] % wcols == 0
    kern = functools.partial(_inproj_kernel, tm=tm)
    t_shape = jax.ShapeDtypeStruct((batch, nb, ATTN_WIDTH, MOBA_BLOCK), BF16)
    t_spec = pl.BlockSpec((1, cb, ATTN_WIDTH, MOBA_BLOCK), lambda i: (i // tps, i % tps, 0, 0))
    return pl.pallas_call(
        kern,
        grid=(n // tm,),
        in_specs=[pl.BlockSpec((tm, d), lambda i: (i, 0)),
                  pl.BlockSpec((1, 1, N_MOD * d), lambda i: (i // tps, 0, 0)),
                  pl.BlockSpec((1, d), lambda i: (0, 0)),
                  pl.BlockSpec((d, wcols), lambda i: (0, 0))],
        out_specs=[t_spec,
                   pl.BlockSpec((tm, ATTN_WIDTH), lambda i: (i, 0)),
                   t_spec,
                   pl.BlockSpec((tm, POOL_WIDTH), lambda i: (i, 0)),
                   pl.BlockSpec((1, cb, ATTN_WIDTH), lambda i: (i, 0, 0)),
                   pl.BlockSpec((tm, d), lambda i: (i, 0))],
        out_shape=[t_shape,
                   jax.ShapeDtypeStruct((n, ATTN_WIDTH), BF16),
                   t_shape,
                   jax.ShapeDtypeStruct((n, POOL_WIDTH), F32),
                   jax.ShapeDtypeStruct((n // tm, cb, ATTN_WIDTH), F32),
                   jax.ShapeDtypeStruct((n, d), BF16)],
        compiler_params=_cparams("arbitrary"),
        name="norm1_inproj",
    )(x2, mod3, g1, w_qkvu)


def _attn_kernel(qT_ref, k_hbm, v_hbm, km_ref, bw_ref, o_ref, xz_hbm, kaug_ref, vaug_ref, bias_ref, qnear_ref,
                 qfar_ref, sn_ref, sf0_ref, sf1_ref, k_ref, vT_ref, kv_sems, zero_ref, zero_sem, *, nb, n_steps):
    blk = MOBA_BLOCK
    hd = HEAD_DIM
    nh = HEADS_PER_STEP
    heads = range(nh)
    lane = lax.broadcasted_iota(jnp.int32, (blk, LANES), 1)

    def pair_lanes(hh):
        return slice((hh // 2) * LANES, (hh // 2 + 1) * LANES)

    def head_rows(hh):
        return slice(hh * hd, (hh + 1) * hd)

    ones = jnp.ones((ONES_ROWS, blk), BF16)

    width = nh * hd
    n_inner = pl.num_programs(1)
    step_id = pl.program_id(0) * n_inner + pl.program_id(1)

    def kv_copies(sid):
        bb, col = sid // n_inner, pl.multiple_of((sid % n_inner) * width, width)
        return (pltpu.make_async_copy(k_hbm.at[bb, :, pl.ds(col, width)], k_ref, kv_sems.at[0]),
                pltpu.make_async_copy(v_hbm.at[bb, :, pl.ds(col, width), :], vT_ref, kv_sems.at[1]))

    @pl.when(step_id == 0)
    def _():
        for cp in kv_copies(step_id):
            cp.start()

    for cp in kv_copies(step_id):
        cp.wait()

    def build_operands(j, carry):
        rows = pl.ds(pl.multiple_of(j * blk, blk), blk)
        for hh in heads:
            kp = k_ref[rows, pair_lanes(hh)]
            if hh % 2 == 0:
                kaug_ref[hh, rows, :] = jnp.where(lane < hd, kp, jnp.where(lane == hd + j, 1.0, 0.0).astype(BF16))
            else:
                kaug_ref[hh, rows, :] = jnp.where(lane >= hd, kp, jnp.where(lane == j, 1.0, 0.0).astype(BF16))
            vaug_ref[j, hh] = jnp.concatenate([vT_ref[j, head_rows(hh), :], ones], axis=0)
        return carry

    lax.fori_loop(0, nb, build_operands, 0)

    @pl.when(step_id + 1 < pl.num_programs(0) * n_inner)
    def _():
        for cp in kv_copies(step_id + 1):
            cp.start()

    @pl.when(step_id == 0)
    def _():
        zero_ref[...] = jnp.zeros_like(zero_ref)

    zrows = zero_ref.shape[0]
    zcount = xz_hbm.shape[0] // (n_steps * zrows)

    def zero_copy(q):
        return pltpu.make_async_copy(zero_ref, xz_hbm.at[pl.ds((step_id * zcount + q) * zrows, zrows)], zero_sem)

    def start_zero(q, carry):
        zero_copy(q).start()
        return carry

    def wait_zero(q, carry):
        zero_copy(q).wait()
        return carry

    lax.fori_loop(0, zcount, start_zero, 0)

    for hh in heads:
        for t in range(2):
            wide = jnp.broadcast_to(bw_ref[hh, t:t + 1, :], (blk, 2 * blk))
            bias_ref[hh, t] = pltpu.roll(wide, 0, 1, stride=1, stride_axis=0)[:, blk:2 * blk]

    rowi = lax.broadcasted_iota(jnp.int32, (MAX_BLOCKS, blk), 0)
    km_lane = lax.broadcasted_iota(jnp.int32, (MAX_BLOCKS, LANES), 1)
    zpad = jnp.zeros((LANES - hd - MAX_BLOCKS, blk), BF16)
    km_split = []
    for hh in heads:
        sub = hh % 2
        in_head = (km_lane >= sub * hd) & (km_lane < (sub + 1) * hd)
        km_h = jnp.where(in_head, km_ref[0, :, pair_lanes(hh)], 0.0)
        km_hi = km_h.astype(BF16)
        km_split.append((km_hi, (km_h - km_hi.astype(F32)).astype(BF16)))

    grp = FAR_GROUP

    def retire_begin(state, mx):
        m, acc = state
        m_new = jnp.maximum(m, mx)
        return m_new, jnp.exp2(m - m_new) * acc

    def retire_slice(src_ref, hh, m_new, acc, g, j):
        p = jnp.exp2(src_ref[hh, g * blk:(g + 1) * blk, :] - m_new)
        return acc + _dot(vaug_ref[j, hh], p.astype(BF16))

    def fresh():
        return (jnp.full((1, blk), -jnp.inf, F32), jnp.zeros((hd + ONES_ROWS, blk), F32))

    def near_blocks(i):
        return tuple(jnp.where(i - (grp - 1) + g < 0, i + 1, i - (grp - 1) + g) for g in range(grp))

    def finish(i, states):
        for hh in heads:
            _, acc = states[hh]
            o_ref[0, i, head_rows(hh), :] = (acc[0:hd] / acc[hd:hd + 1]).astype(BF16)

    def select(i):
        for hh in heads:
            qp = qT_ref[0, i, pair_lanes(hh), :]
            km_hi, km_lo = km_split[hh]
            gate = _dot(km_hi, qp) + _dot(km_lo, qp)
            g = jnp.where(rowi < i, gate, -jnp.inf)
            sel = rowi == i
            for _ in range(MOBA_TOPK):
                m = jnp.max(g, axis=0, keepdims=True)
                hit = (g == m) & (m > -jnp.inf)
                idx = jnp.min(jnp.where(hit, rowi, MAX_BLOCKS), axis=0, keepdims=True)
                pick = rowi == idx
                sel = sel | pick
                g = jnp.where(pick, -jnp.inf, g)
            near = jnp.where(sel, 0.0, NEG_INF)
            far = jnp.where(rowi > i - grp, NEG_INF, near)
            qh = qp[head_rows(hh % 2), :]
            if hh % 2 == 0:
                qnear_ref[hh] = jnp.concatenate([qh, near.astype(BF16), zpad], axis=0)
                qfar_ref[hh] = jnp.concatenate([qh, far.astype(BF16), zpad], axis=0)
            else:
                qnear_ref[hh] = jnp.concatenate([near.astype(BF16), zpad, qh], axis=0)
                qfar_ref[hh] = jnp.concatenate([far.astype(BF16), zpad, qh], axis=0)

    def park(dst_ref, q_ref, hh, g, j, bias=None):
        rows = pl.ds(pl.multiple_of(j * blk, blk), blk)
        s = _dot(kaug_ref[hh, rows, :], q_ref[hh])
        if bias is not None:
            s = s + bias
        dst_ref[hh, g * blk:(g + 1) * blk, :] = s
        return jnp.max(s, axis=0, keepdims=True)

    def park_near(i):
        idx = near_blocks(i)

        def quarter(g, hh):
            bias = None
            if g == grp - 1:
                bias = bias_ref[hh, 0]
            elif g == grp - 2:
                bias = bias_ref[hh, 1]
            return park(sn_ref, qnear_ref, hh, g, idx[g], bias)
        return quarter

    def park_far(c, dst_ref):
        return lambda g, hh: park(dst_ref, qfar_ref, hh, g, c * grp + g)

    def chunk_blocks(c):
        return tuple(c * grp + g for g in range(grp))

    def step(src_ref, states, maxes, vidx, quarter, park_first):
        begun = [retire_begin(states[hh], maxes[hh]) for hh in heads] if states is not None else None
        accs = [b[1] for b in begun] if begun else None
        new_max = [None] * nh

        def park_quarter(g):
            for hh in heads:
                cm = quarter(g, hh)
                new_max[hh] = cm if new_max[hh] is None else jnp.maximum(new_max[hh], cm)

        for g in range(grp):
            if park_first:
                park_quarter(g)
            if begun:
                for hh in heads:
                    accs[hh] = retire_slice(src_ref, hh, begun[hh][0], accs[hh], g, vidx[g])
            if not park_first and quarter is not None:
                park_quarter(g)
        new_states = tuple((begun[hh][0], accs[hh]) for hh in heads) if begun else None
        return new_states, tuple(new_max)

    def fresh_states():
        return tuple(fresh() for _ in heads)

    def early_tile(i, near_max):
        nxt = jnp.minimum(i + 1, nb - 1)
        select(nxt)
        states, next_max = step(sn_ref, fresh_states(), near_max, near_blocks(i), park_near(nxt), False)
        finish(i, states)
        return next_max

    def late_tile(i, near_max):
        sf = (sf0_ref, sf1_ref)
        n_chunks = (i - (grp - 1) + grp - 1) // grp
        nxt = jnp.minimum(i + 1, nb - 1)
        states, far_max = step(sn_ref, fresh_states(), near_max, near_blocks(i), park_far(0, sf[0]), True)

        def pair(k, st):
            states, far_max = st
            states, far_max = step(sf[0], states, far_max, chunk_blocks(2 * k), park_far(2 * k + 1, sf[1]), True)
            return step(sf[1], states, far_max, chunk_blocks(2 * k + 1), park_far(2 * k + 2, sf[0]), True)

        states, far_max = lax.fori_loop(0, (n_chunks - 1) // 2, pair, (states, far_max))

        def last(src, states, far_max):
            select(nxt)
            states, next_max = step(src, states, far_max, chunk_blocks(n_chunks - 1), park_near(nxt), True)
            finish(i, states)
            return next_max

        def even_count(st):
            states, far_max = step(sf[0], st[0], st[1], chunk_blocks(n_chunks - 2),
                                   park_far(n_chunks - 1, sf[1]), True)
            return last(sf[1], states, far_max)

        def odd_count(st):
            return last(sf[0], st[0], st[1])

        return lax.cond(n_chunks % 2 == 0, even_count, odd_count, (states, far_max))

    n_early = min(grp, nb)
    select(0)
    _, near_max = step(None, None, None, None, park_near(0), True)
    near_max = lax.fori_loop(0, n_early, early_tile, near_max)
    lax.fori_loop(n_early, nb, late_tile, near_max)
    lax.fori_loop(0, zcount, wait_zero, 0)


def _rel_bucket_table(max_rel):
    n = np.arange(max_rel)
    max_exact = REL_BUCKETS // 2
    nf = np.maximum(n, 1).astype(np.float32)
    large = max_exact + (np.log(nf / np.float32(max_exact)) / np.float32(math.log(REL_MAX_DIST / max_exact))
                         * np.float32(REL_BUCKETS - max_exact)).astype(np.int32)
    large = np.minimum(large, REL_BUCKETS - 1)
    return np.where(n < max_exact, n, large)


def _bias_vectors(rel_bias):
    blk = MOBA_BLOCK
    table = _rel_bucket_table(2 * blk)
    r = np.arange(-blk, blk)
    bt = (rel_bias.T - rel_bias[REL_BUCKETS - 1][:, None]) * LOG2E
    own = jnp.where(jnp.asarray(r >= 0)[None], bt[:, table[np.maximum(r, 0)]], NEG_INF)
    prev = bt[:, table[np.minimum(r + blk, 2 * blk - 1)]]
    return jnp.stack([own, prev], axis=1).astype(F32)


def _moba_attention(qT, k3, vT, km_pad, bias_w, *, batch, seq):
    nb = seq // MOBA_BLOCK
    blk = MOBA_BLOCK
    assert nb % FAR_GROUP == 0 and nb <= MAX_BLOCKS
    nh = HEADS_PER_STEP
    width = nh * HEAD_DIM
    assert nh % 2 == 0 and N_HEADS % nh == 0
    n_steps = batch * (N_HEADS // nh)
    sorted_rows = _expert_tiles(batch * seq) * TE_ROWS
    assert sorted_rows % (n_steps * ZERO_ROWS) == 0
    kern = functools.partial(_attn_kernel, nb=nb, n_steps=n_steps)

    hbm = pl.BlockSpec(memory_space=pl.ANY)
    return pl.pallas_call(
        kern,
        grid=(batch, N_HEADS // nh),
        in_specs=[pl.BlockSpec((1, nb, width, blk), lambda b, p: (b, 0, p, 0)),
                  hbm, hbm,
                  pl.BlockSpec((1, MAX_BLOCKS, width), lambda b, p: (b, 0, p)),
                  pl.BlockSpec((nh, 2, 2 * blk), lambda b, p: (p, 0, 0))],
        out_specs=[pl.BlockSpec((1, nb, width, blk), lambda b, p: (b, 0, p, 0)), hbm],
        out_shape=[jax.ShapeDtypeStruct((batch, nb, ATTN_WIDTH, blk), BF16),
                   jax.ShapeDtypeStruct((sorted_rows, D_MODEL), F32)],
        scratch_shapes=[pltpu.VMEM((nh, seq, LANES), BF16),
                        pltpu.VMEM((nb, nh, HEAD_DIM + ONES_ROWS, blk), BF16),
                        pltpu.VMEM((nh, 2, blk, blk), F32),
                        pltpu.VMEM((nh, LANES, blk), BF16),
                        pltpu.VMEM((nh, LANES, blk), BF16),
                        pltpu.VMEM((nh, FAR_GROUP * blk, blk), F32),
                        pltpu.VMEM((nh, FAR_GROUP * blk, blk), F32),
                        pltpu.VMEM((nh, FAR_GROUP * blk, blk), F32),
                        pltpu.VMEM((seq, width), BF16),
                        pltpu.VMEM((nb, width, blk), BF16),
                        pltpu.SemaphoreType.DMA((2,)),
                        pltpu.VMEM((ZERO_ROWS, D_MODEL), F32),
                        pltpu.SemaphoreType.DMA(())],
        compiler_params=_cparams("arbitrary", "arbitrary"),
        name="moba_attention",
    )(qT, k3, vT, km_pad, bias_w)


def _post_kernel(x_ref, mod_ref, h_ref, wz_ref, bz_ref, at_ref, u_ref, uh_ref, pw_ref, ps_ref,
                 wba_ref, wbp_ref, wo_ref, g2_ref, wrh_ref, wrl_ref, br_ref,
                 x1_ref, h2r_ref, route_ref, route_t_ref, counts_ref, carry_ref, *, tm, seq):
    d = D_MODEL
    i = pl.program_id(0)
    x = x_ref[...]
    mod = mod_ref[0]
    gate1 = mod[:, 2 * d:3 * d]
    shift2, scale2 = mod[:, 3 * d:4 * d], mod[:, 4 * d:5 * d]

    zg = 0.5 * jnp.tanh(0.5 * (_dot(h_ref[...], wz_ref[...]) + bz_ref[...])) + 0.5

    ya = jnp.concatenate(
        [lax.dot_general(at_ref[0, c], wba_ref[...], (((0,), (0,)), ((), ())),
                         preferred_element_type=F32) for c in range(tm // MOBA_BLOCK)], axis=0)

    first = (i * tm) % seq == 0
    halo = jnp.where(first, 0.0, uh_ref[...])
    ubuf = jnp.concatenate([halo, u_ref[...]], axis=0)
    rows = tm + POOL_HALO
    pos = (lax.broadcasted_iota(jnp.int32, (tm, POOL_GROUP_WIDTH), 0) + i * tm) % seq
    outs = []
    for gi, w in enumerate(POOL_WINDOWS):
        e = ubuf[:, gi * POOL_GROUP_WIDTH:(gi + 1) * POOL_GROUP_WIDTH]
        s = e
        step = 1
        while step < w:
            s = s + pltpu.roll(s, step, 0)
            step *= 2
        cnt = jnp.minimum(pos + 1, w).astype(F32)
        dlt = s[POOL_HALO:rows] / cnt - e[POOL_HALO:rows]
        outs.append(_dot(dlt.astype(BF16), pw_ref[gi]))
    pooled = jnp.concatenate(outs, axis=-1) * ps_ref[...]
    yp = _dot(pooled.astype(BF16), wbp_ref[...])

    merged = zg[:, 0:d] * ya + zg[:, d:2 * d] * yp
    x1 = x + gate1 * _dot(merged.astype(BF16), wo_ref[...])
    x1_ref[...] = x1

    h2 = _rms_mod(x1, g2_ref[...], scale2, shift2)
    h2_hi = h2.astype(BF16)
    h2r_ref[...] = h2_hi.astype(F32)
    h2_lo = (h2 - h2_hi.astype(F32)).astype(BF16)
    z = _dot(h2_hi, wrh_ref[...]) + _dot(h2_lo, wrh_ref[...]) + _dot(h2_hi, wrl_ref[...]) + br_ref[...]

    rr = ROUTER_ROWS
    zt = z.T[0:rr, :]
    row = lax.broadcasted_iota(jnp.int32, (rr, tm), 0)
    zgrp = jnp.where(row < N_GROUPS, zt, -jnp.inf)
    mg = jnp.max(zgrp, axis=0, keepdims=True)
    pg_top = 1.0 / jnp.sum(jnp.exp(zgrp - mg), axis=0, keepdims=True)
    g_idx = jnp.min(jnp.where(zgrp == mg, row, rr), axis=0, keepdims=True)
    e_row = row - ROUTER_LANE0
    in_grp = (e_row >= 0) & (e_row < N_EXPERTS) & ((e_row >> 3) == g_idx)
    ze = jnp.where(in_grp, zt, -jnp.inf)
    m1 = jnp.max(ze, axis=0, keepdims=True)
    i1 = jnp.min(jnp.where(ze == m1, row, rr), axis=0, keepdims=True)
    ze2 = jnp.where(row == i1, -jnp.inf, ze)
    m2 = jnp.max(ze2, axis=0, keepdims=True)
    i2 = jnp.min(jnp.where(ze2 == m2, row, rr), axis=0, keepdims=True)
    e2 = jnp.exp(m2 - m1)
    w1 = pg_top / (1.0 + e2)
    w2 = pg_top * e2 / (1.0 + e2)

    hot = jnp.where((row == i1) | (row == i2), 1.0, 0.0)
    tri = jnp.where(lax.broadcasted_iota(jnp.int32, (tm, tm), 0) <= lax.broadcasted_iota(jnp.int32, (tm, tm), 1),
                    1.0, 0.0).astype(BF16)
    prefix = _dot(hot.astype(BF16), tri)

    @pl.when(i == 0)
    def _():
        carry_ref[...] = jnp.zeros_like(carry_ref)

    carry = carry_ref[...]
    base = carry[:, 0:1] + prefix - 1.0
    rank1 = jnp.sum(jnp.where(row == i1, base, 0.0), axis=0, keepdims=True)
    rank2 = jnp.sum(jnp.where(row == i2, base, 0.0), axis=0, keepdims=True)
    carry_ref[...] = carry + prefix[:, tm - 1:tm]
    counts_ref[...] = carry_ref[...]
    fields = ((i1 - ROUTER_LANE0).astype(F32), (i2 - ROUTER_LANE0).astype(F32), w1, w2, rank1, rank2)
    frow = lax.broadcasted_iota(jnp.int32, (ROUTE_FIELDS, tm), 0)
    route_t = jnp.zeros((ROUTE_FIELDS, tm), F32)
    for k, val in enumerate(fields):
        route_t = jnp.where(frow == k, val, route_t)
    route_t_ref[...] = route_t
    route_ref[...] = jnp.concatenate([route_t, jnp.zeros((LANES - ROUTE_FIELDS, tm), F32)], axis=0).T


def _post_attention(x2, mod3, h1, w_z, b_z, attnT, u, pool_w, pool_scale, w_ba, w_bp, w_out, g2,
                    wr_hi, wr_lo, b_r, *, batch, seq):
    n, d = x2.shape
    tm = TM_POST
    tps = seq // tm
    cb = tm // MOBA_BLOCK
    hb = tm // POOL_HALO
    kern = functools.partial(_post_kernel, tm=tm, seq=seq)
    const2 = lambda i: (0, 0)
    return pl.pallas_call(
        kern,
        grid=(n // tm,),
        in_specs=[pl.BlockSpec((tm, d), lambda i: (i, 0)),
                  pl.BlockSpec((1, 1, N_MOD * d), lambda i: (i // tps, 0, 0)),
                  pl.BlockSpec((tm, d), lambda i: (i, 0)),
                  pl.BlockSpec((d, 2 * d), lambda i: (0, 1)),
                  pl.BlockSpec((1, 2 * d), const2),
                  pl.BlockSpec((1, cb, ATTN_WIDTH, MOBA_BLOCK), lambda i: (i // tps, i % tps, 0, 0)),
                  pl.BlockSpec((tm, POOL_WIDTH), lambda i: (i, 0)),
                  pl.BlockSpec((POOL_HALO, POOL_WIDTH), lambda i: (jnp.maximum(i * hb - 1, 0), 0)),
                  pl.BlockSpec((len(POOL_WINDOWS), POOL_GROUP_WIDTH, POOL_GROUP_WIDTH), lambda i: (0, 0, 0)),
                  pl.BlockSpec((1, POOL_WIDTH), const2),
                  pl.BlockSpec((ATTN_WIDTH, d), const2),
                  pl.BlockSpec((POOL_WIDTH, d), const2),
                  pl.BlockSpec((d, d), const2),
                  pl.BlockSpec((1, d), const2),
                  pl.BlockSpec((d, LANES), const2),
                  pl.BlockSpec((d, LANES), const2),
                  pl.BlockSpec((1, LANES), const2)],
        out_specs=[pl.BlockSpec((tm, d), lambda i: (i, 0)),
                   pl.BlockSpec((tm, d), lambda i: (i, 0)),
                   pl.BlockSpec((tm, LANES), lambda i: (i, 0)),
                   pl.BlockSpec((ROUTE_FIELDS, tm), lambda i: (0, i)),
                   pl.BlockSpec((ROUTER_ROWS, LANES), const2)],
        out_shape=[jax.ShapeDtypeStruct((n, d), F32),
                   jax.ShapeDtypeStruct((n, d), F32),
                   jax.ShapeDtypeStruct((n, LANES), F32),
                   jax.ShapeDtypeStruct((ROUTE_FIELDS, n), F32),
                   jax.ShapeDtypeStruct((ROUTER_ROWS, LANES), F32)],
        scratch_shapes=[pltpu.VMEM((ROUTER_ROWS, LANES), F32)],
        compiler_params=_cparams("arbitrary"),
        name="merge_outproj_router",
    )(x2, mod3, h1, w_z, b_z, attnT, u, u, pool_w, pool_scale, w_ba, w_bp, w_out, g2, wr_hi, wr_lo, b_r)


def _routing_plan(route_t, counts, *, n):
    te = TE_ROWS
    n_tiles = _expert_tiles(n)
    cnt = counts[ROUTER_LANE0:ROUTER_LANE0 + N_EXPERTS, 0].astype(jnp.int32)
    padded = ((cnt + te - 1) // te) * te
    ids = jnp.arange(N_EXPERTS, dtype=jnp.int32)
    ends = jnp.sum(jnp.where(ids[None, :] <= ids[:, None], padded[None, :], 0), axis=-1)
    starts = ends - padded
    pos = []
    for k in range(EXPERT_TOPK):
        expert = route_t[k].astype(jnp.int32)
        rank = route_t[4 + k].astype(jnp.int32)
        pos.append(jnp.sum(jnp.where(expert[:, None] == ids, starts, 0), axis=-1) + rank)
    tiles = jnp.arange(n_tiles, dtype=jnp.int32)
    tile_expert = jnp.sum((ends // te)[None, :] <= tiles[:, None], axis=-1)
    tile_expert = jnp.minimum(tile_expert, N_EXPERTS - 1).astype(jnp.int32)
    n_used = (ends[N_EXPERTS - 1:] // te).astype(jnp.int32)
    later = (ids[None, :] > ids[:, None]) & (cnt[None, :] > 0)
    next_expert = jnp.min(jnp.where(later, ids[None, :], N_EXPERTS), axis=-1)
    next_expert = jnp.where(next_expert == N_EXPERTS, -1, next_expert).astype(jnp.int32)
    return pos, tile_expert, n_used, next_expert


def _expert_tiles(n):
    return (EXPERT_TOPK * n) // TE_ROWS + N_EXPERTS


def _tile_pos(pos, tm):
    return jnp.concatenate([p.reshape(-1, 1, tm) for p in pos], axis=-1)


def _dispatch_kernel(pos_ref, h_ref, xz_ref, xs_ref, sem, *, tm):
    del xz_ref
    for r in range(tm):
        for k in range(EXPERT_TOPK):
            pltpu.make_async_copy(h_ref.at[pl.ds(r, 1)], xs_ref.at[pl.ds(pos_ref[0, 0, k * tm + r], 1)],
                                  sem).start(priority=(r + k) % 2)
    for k in range(EXPERT_TOPK):
        pltpu.make_async_copy(h_ref, h_ref, sem).wait()


def _dispatch(pos, h2r, xz, *, n):
    tm = TM_DISPATCH
    rows, d = xz.shape
    assert rows == _expert_tiles(n) * TE_ROWS
    pos3 = _tile_pos(pos, tm)
    return pl.pallas_call(
        functools.partial(_dispatch_kernel, tm=tm),
        grid=(n // tm,),
        in_specs=[pl.BlockSpec((1, 1, EXPERT_TOPK * tm), lambda i: (i, 0, 0), memory_space=pltpu.SMEM),
                  pl.BlockSpec((tm, d), lambda i: (i, 0)),
                  pl.BlockSpec(memory_space=pl.ANY)],
        out_specs=pl.BlockSpec(memory_space=pl.ANY),
        out_shape=jax.ShapeDtypeStruct((rows, d), F32),
        scratch_shapes=[pltpu.SemaphoreType.DMA(())],
        input_output_aliases={2: 0},
        compiler_params=_cparams("arbitrary"),
        name="moe_dispatch",
    )(pos3, h2r, xz)


def _experts_kernel(te_ref, nu_ref, nx_ref, xs_ref, wg_ref, wu_ref, wd_ref, y_ref,
                    wg_b, wu_b, wd_b, wg_f, wu_f, wd_f, sems, turn_ref):
    t = pl.program_id(0)

    def fetch(e, slot):
        return [pltpu.make_async_copy(src.at[e], dst.at[slot], sems.at[slot])
                for src, dst in ((wg_ref, wg_f), (wu_ref, wu_f), (wd_ref, wd_f))]

    @pl.when(t < nu_ref[0])
    def _():
        e = te_ref[t]
        e_prev = te_ref[jnp.maximum(t - 1, 0)]

        @pl.when(t == 0)
        def _():
            turn_ref[0] = 0
            for cp in fetch(e, 0):
                cp.start()

        @pl.when((t == 0) | (e != e_prev))
        def _():
            slot = turn_ref[0] % 2
            for cp in fetch(e, slot):
                cp.wait()
            wg_b[...] = wg_f[slot].astype(BF16)
            wu_b[...] = wu_f[slot].astype(BF16)
            wd_b[...] = wd_f[slot].astype(BF16)
            e_next = nx_ref[e]

            @pl.when(e_next >= 0)
            def _():
                for cp in fetch(e_next, 1 - slot):
                    cp.start()

            turn_ref[0] = turn_ref[0] + 1

        x = xs_ref[...].astype(BF16)
        g = _dot(x, wg_b[...])
        u = _dot(x, wu_b[...])
        act = (g * jax.nn.sigmoid(g)) * u
        y_ref[...] = _dot(act.astype(BF16), wd_b[...])

    @pl.when(t >= nu_ref[0])
    def _():
        y_ref[...] = jnp.zeros_like(y_ref)


def _experts(tile_expert, n_used, next_expert, xs, wg, wu, wd):
    rows, d = xs.shape
    te = TE_ROWS
    hbm = pl.BlockSpec(memory_space=pl.ANY)
    grid_spec = pltpu.PrefetchScalarGridSpec(
        num_scalar_prefetch=3,
        grid=(rows // te,),
        in_specs=[pl.BlockSpec((te, d), lambda t, te_r, nu, nx: (jnp.minimum(t, nu[0] - 1), 0)),
                  hbm, hbm, hbm],
        out_specs=pl.BlockSpec((te, d), lambda t, te_r, nu, nx: (t, 0)),
        scratch_shapes=[pltpu.VMEM((d, D_EXPERT), BF16), pltpu.VMEM((d, D_EXPERT), BF16),
                        pltpu.VMEM((D_EXPERT, d), BF16),
                        pltpu.VMEM((2, d, D_EXPERT), F32), pltpu.VMEM((2, d, D_EXPERT), F32),
                        pltpu.VMEM((2, D_EXPERT, d), F32),
                        pltpu.SemaphoreType.DMA((2,)), pltpu.SMEM((1,), jnp.int32)])
    return pl.pallas_call(
        _experts_kernel,
        grid_spec=grid_spec,
        out_shape=jax.ShapeDtypeStruct((rows, d), F32),
        compiler_params=_cparams("arbitrary"),
        name="moe_experts",
    )(tile_expert, n_used, next_expert, xs, wg, wu, wd)


def _final_kernel(pos_ref, pos_next_ref, x1_ref, route_ref, mod_ref, g_ref, y_ref, o_ref, ybuf, sems, *,
                  tm, final_norm):
    d = D_MODEL
    i = pl.program_id(0)
    slot = i % 2

    def gather(p_ref, s):
        for r in range(tm):
            for k in range(EXPERT_TOPK):
                pltpu.make_async_copy(y_ref.at[pl.ds(p_ref[0, 0, k * tm + r], 1)],
                                      ybuf.at[s, k, pl.ds(r, 1)], sems.at[s]).start(priority=(r + k) % 2)

    @pl.when(i == 0)
    def _():
        gather(pos_ref, 0)

    @pl.when(i + 1 < pl.num_programs(0))
    def _():
        gather(pos_next_ref, 1 - slot)

    for k in range(EXPERT_TOPK):
        pltpu.make_async_copy(ybuf.at[slot, k], ybuf.at[slot, k], sems.at[slot]).wait()

    route = route_ref[...]
    lane = lax.broadcasted_iota(jnp.int32, route.shape, 1)
    y = jnp.zeros((tm, d), F32)
    for k in range(EXPERT_TOPK):
        wk = jnp.sum(jnp.where(lane == 2 + k, route, 0.0), axis=-1, keepdims=True)
        y = y + wk * ybuf[slot, k]
    gate2 = mod_ref[0][:, 5 * d:6 * d]
    x = x1_ref[...] + gate2 * y
    if final_norm:
        x = (x * lax.rsqrt(jnp.mean(x * x, axis=-1, keepdims=True) + EPS)) * g_ref[...]
    o_ref[...] = x


def _final(pos, x1, route, mod3, gf, y, *, seq, final_norm):
    n, d = x1.shape
    tm = TM_FINAL
    tps = seq // tm
    pos3 = _tile_pos(pos, tm)
    last = n // tm - 1
    return pl.pallas_call(
        functools.partial(_final_kernel, tm=tm, final_norm=final_norm),
        grid=(n // tm,),
        in_specs=[pl.BlockSpec((1, 1, EXPERT_TOPK * tm), lambda i: (i, 0, 0), memory_space=pltpu.SMEM),
                  pl.BlockSpec((1, 1, EXPERT_TOPK * tm), lambda i: (jnp.minimum(i + 1, last), 0, 0),
                               memory_space=pltpu.SMEM),
                  pl.BlockSpec((tm, d), lambda i: (i, 0)),
                  pl.BlockSpec((tm, LANES), lambda i: (i, 0)),
                  pl.BlockSpec((1, 1, N_MOD * d), lambda i: (i // tps, 0, 0)),
                  pl.BlockSpec((1, d), lambda i: (0, 0)),
                  pl.BlockSpec(memory_space=pl.ANY)],
        out_specs=pl.BlockSpec((tm, d), lambda i: (i, 0)),
        out_shape=jax.ShapeDtypeStruct((n, d), F32),
        scratch_shapes=[pltpu.VMEM((2, EXPERT_TOPK, tm, d), F32), pltpu.SemaphoreType.DMA((2,))],
        compiler_params=_cparams("arbitrary"),
        name="moe_combine_final_norm",
    )(pos3, pos3, x1, route, mod3, gf, y)


def _split_bf16(w):
    hi = w.astype(BF16)
    return hi, (w - hi.astype(F32)).astype(BF16)


def kernel(x, c, w_ada, b_ada, norm1_g, w_in, b_gate, rel_bias, pool_w, pool_scale, w_branch_attn,
           w_branch_pool, w_out, norm2_g, w_router_group, b_router_group, w_router_expert,
           b_router_expert, w_expert_gate, w_expert_up, w_expert_down, norm_f_g):
    batch, seq, d = x.shape
    depth = w_ada.shape[0]
    n = batch * seq
    nb = seq // MOBA_BLOCK
    qkvu = 3 * ATTN_WIDTH + POOL_WIDTH
    bias_w = _bias_vectors(rel_bias)
    c_pad = jnp.pad(c, ((0, 8 - batch), (0, 0)))

    xc = x.reshape(n, d)
    for l in range(depth):
        mod3 = _modulation(c_pad, w_ada[l], b_ada[l][None])[:batch].reshape(batch, 1, N_MOD * d)
        w_in_b = w_in[l].astype(BF16)
        g1 = norm1_g[l][None]

        assert w_in_b.shape[1] == 2 * qkvu == qkvu + 2 * d
        qT, k2, vT, u, km, h1 = _in_projection(xc, mod3, g1, w_in_b, batch=batch, seq=seq)
        km_pad = jnp.pad(km.reshape(batch, nb, ATTN_WIDTH), ((0, 0), (0, MAX_BLOCKS - nb), (0, 0)))
        attnT, xz = _moba_attention(qT, k2.reshape(batch, seq, ATTN_WIDTH), vT, km_pad, bias_w,
                                    batch=batch, seq=seq)

        w_r = jnp.concatenate([w_router_group[l], w_router_expert[l]], axis=1)
        w_r = jnp.pad(w_r, ((0, 0), (0, LANES - w_r.shape[1])))
        b_r = jnp.concatenate([b_router_group[l], b_router_expert[l]])
        b_r = jnp.pad(b_r, (0, LANES - b_r.shape[0]))[None]
        wr_hi, wr_lo = _split_bf16(w_r)
        x1, h2r, route, route_t, counts = _post_attention(
            xc, mod3, h1, w_in_b, b_gate[l][None], attnT, u, pool_w[l].astype(BF16),
            pool_scale[l][None], w_branch_attn[l].astype(BF16), w_branch_pool[l].astype(BF16),
            w_out[l].astype(BF16), norm2_g[l][None], wr_hi, wr_lo, b_r, batch=batch, seq=seq)

        pos, tile_expert, n_used, next_expert = _routing_plan(route_t, counts, n=n)
        xs = _dispatch(pos, h2r, xz, n=n)
        y = _experts(tile_expert, n_used, next_expert, xs, w_expert_gate[l], w_expert_up[l],
                     w_expert_down[l])
        xc = _final(pos, x1, route, mod3, norm_f_g[None], y, seq=seq, final_norm=(l == depth - 1))
    return xc.reshape(batch, seq, d)
```

```python
import functools
import math

import numpy as np
import jax
import jax.numpy as jnp
from jax import lax
from jax.experimental import pallas as pl
from jax.experimental.pallas import tpu as pltpu

F32 = jnp.float32
BF16 = jnp.bfloat16

D_MODEL = 1024
N_HEADS = 8
HEAD_DIM = 64
ATTN_WIDTH = N_HEADS * HEAD_DIM
MOBA_BLOCK = 256
MOBA_TOPK = 3
MAX_BLOCKS = 32
FAR_GROUP = 4
HEADS_PER_STEP = 4
ZERO_ROWS = 256
ONES_ROWS = 16
LOG2E = math.log2(math.e)
POOL_WINDOWS = (2, 4, 8, 16)
POOL_WIDTH = 512
POOL_GROUP_WIDTH = 128
POOL_HALO = 16
REL_BUCKETS = 32
REL_MAX_DIST = 128
N_GROUPS = 4
EXPERTS_PER_GROUP = 8
N_EXPERTS = N_GROUPS * EXPERTS_PER_GROUP
D_EXPERT = 512
N_MOD = 6
EPS = 1e-6
NEG_INF = -1e30
LANES = 128
ROUTER_LANE0 = N_GROUPS
ROUTE_FIELDS = 8
ROUTER_ROWS = 48
VMEM_LIMIT = 56 * 1024 * 1024

TM_PROJ = 1024
TM_POST = 1024
TM_DISPATCH = 2048
TE_ROWS = 512
EXPERT_TOPK = 2
TM_FINAL = 256


def _cparams(*sem):
    return pltpu.CompilerParams(dimension_semantics=sem, vmem_limit_bytes=VMEM_LIMIT)


def _dot(a, b):
    return jnp.dot(a, b, preferred_element_type=F32)


def _rms_mod(x, g, scale, shift):
    xn = x * lax.rsqrt(jnp.mean(x * x, axis=-1, keepdims=True) + EPS)
    return (xn * g) * (1.0 + scale) + shift


def _mod_kernel(c_ref, w_ref, b_ref, o_ref):
    c = c_ref[...]
    ca = c * jax.nn.sigmoid(c)
    o_ref[...] = _dot(ca.astype(BF16), w_ref[...].astype(BF16)) + b_ref[...]


def _modulation(c_pad, w_ada, b_ada):
    rows, d = c_pad.shape
    n_out = w_ada.shape[1]
    tn = 1024
    return pl.pallas_call(
        _mod_kernel,
        grid=(n_out // tn,),
        in_specs=[pl.BlockSpec((rows, d), lambda j: (0, 0)),
                  pl.BlockSpec((d, tn), lambda j: (0, j)),
                  pl.BlockSpec((1, tn), lambda j: (0, j))],
        out_specs=pl.BlockSpec((rows, tn), lambda j: (0, j)),
        out_shape=jax.ShapeDtypeStruct((rows, n_out), F32),
        compiler_params=_cparams("arbitrary"),
        name="adaln_mod",
    )(c_pad, w_ada, b_ada)


def _inproj_kernel(x_ref, mod_ref, g_ref, w_ref, qT_ref, k_ref, vT_ref, u_ref, km_ref, h_ref, *, tm):
    d = D_MODEL
    mod = mod_ref[0]
    h = _rms_mod(x_ref[...], g_ref[...], mod[:, d:2 * d], mod[:, 0:d]).astype(BF16)
    h_ref[...] = h
    proj = _dot(h, w_ref[...])
    aw = ATTN_WIDTH
    q = proj[:, 0:aw] * (HEAD_DIM ** -0.5 * LOG2E)
    k = proj[:, aw:2 * aw]
    v = proj[:, 2 * aw:3 * aw]
    u_ref[...] = proj[:, 3 * aw:3 * aw + POOL_WIDTH]
    k_ref[...] = k.astype(BF16)
    qT = q.T.astype(BF16)
    vT = v.T.astype(BF16)
    for c in range(tm // MOBA_BLOCK):
        sl = slice(c * MOBA_BLOCK, (c + 1) * MOBA_BLOCK)
        qT_ref[0, c] = qT[:, sl]
        vT_ref[0, c] = vT[:, sl]
        km_ref[0, c:c + 1, :] = jnp.mean(k[sl, :], axis=0, keepdims=True)


def _in_projection(x2, mod3, g1, w_qkvu, *, batch, seq):
    n, d = x2.shape
    tm = TM_PROJ
    tps = seq // tm
    nb = seq // MOBA_BLOCK
    cb = tm // MOBA_BLOCK
    wcols = 3 * ATTN_WIDTH + POOL_WIDTH
    assert w_qkvu.shape[1] % wcols == 0
    kern = functools.partial(_inproj_kernel, tm=tm)
    t_shape = jax.ShapeDtypeStruct((batch, nb, ATTN_WIDTH, MOBA_BLOCK), BF16)
    t_spec = pl.BlockSpec((1, cb, ATTN_WIDTH, MOBA_BLOCK), lambda i: (i // tps, i % tps, 0, 0))
    return pl.pallas_call(
        kern,
        grid=(n // tm,),
        in_specs=[pl.BlockSpec((tm, d), lambda i: (i, 0)),
                  pl.BlockSpec((1, 1, N_MOD * d), lambda i: (i // tps, 0, 0)),
                  pl.BlockSpec((1, d), lambda i: (0, 0)),
                  pl.BlockSpec((d, wcols), lambda i: (0, 0))],
        out_specs=[t_spec,
                   pl.BlockSpec((tm, ATTN_WIDTH), lambda i: (i, 0)),
                   t_spec,
                   pl.BlockSpec((tm, POOL_WIDTH), lambda i: (i, 0)),
                   pl.BlockSpec((1, cb, ATTN_WIDTH), lambda i: (i, 0, 0)),
                   pl.BlockSpec((tm, d), lambda i: (i, 0))],
        out_shape=[t_shape,
                   jax.ShapeDtypeStruct((n, ATTN_WIDTH), BF16),
                   t_shape,
                   jax.ShapeDtypeStruct((n, POOL_WIDTH), F32),
                   jax.ShapeDtypeStruct((n // tm, cb, ATTN_WIDTH), F32),
                   jax.ShapeDtypeStruct((n, d), BF16)],
        compiler_params=_cparams("arbitrary"),
        name="norm1_inproj",
    )(x2, mod3, g1, w_qkvu)


def _attn_kernel(qT_ref, k_hbm, v_hbm, km_ref, bw_ref, o_ref, xz_hbm, kaug_ref, vaug_ref, bias_ref, qnear_ref,
                 qfar_ref, sn_ref, sf0_ref, sf1_ref, k_ref, vT_ref, kv_sems, zero_ref, zero_sem, *, nb, n_steps):
    blk = MOBA_BLOCK
    hd = HEAD_DIM
    nh = HEADS_PER_STEP
    heads = range(nh)
    lane = lax.broadcasted_iota(jnp.int32, (blk, LANES), 1)

    def pair_lanes(hh):
        return slice((hh // 2) * LANES, (hh // 2 + 1) * LANES)

    def head_rows(hh):
        return slice(hh * hd, (hh + 1) * hd)

    ones = jnp.ones((ONES_ROWS, blk), BF16)

    width = nh * hd
    n_inner = pl.num_programs(1)
    step_id = pl.program_id(0) * n_inner + pl.program_id(1)

    def kv_copies(sid):
        bb, col = sid // n_inner, pl.multiple_of((sid % n_inner) * width, width)
        return (pltpu.make_async_copy(k_hbm.at[bb, :, pl.ds(col, width)], k_ref, kv_sems.at[0]),
                pltpu.make_async_copy(v_hbm.at[bb, :, pl.ds(col, width), :], vT_ref, kv_sems.at[1]))

    @pl.when(step_id == 0)
    def _():
        for cp in kv_copies(step_id):
            cp.start()

    for cp in kv_copies(step_id):
        cp.wait()

    def build_operands(j, carry):
        rows = pl.ds(pl.multiple_of(j * blk, blk), blk)
        for hh in heads:
            kp = k_ref[rows, pair_lanes(hh)]
            if hh % 2 == 0:
                kaug_ref[hh, rows, :] = jnp.where(lane < hd, kp, jnp.where(lane == hd + j, 1.0, 0.0).astype(BF16))
            else:
                kaug_ref[hh, rows, :] = jnp.where(lane >= hd, kp, jnp.where(lane == j, 1.0, 0.0).astype(BF16))
            vaug_ref[j, hh] = jnp.concatenate([vT_ref[j, head_rows(hh), :], ones], axis=0)
        return carry

    lax.fori_loop(0, nb, build_operands, 0)

    @pl.when(step_id + 1 < pl.num_programs(0) * n_inner)
    def _():
        for cp in kv_copies(step_id + 1):
            cp.start()

    @pl.when(step_id == 0)
    def _():
        zero_ref[...] = jnp.zeros_like(zero_ref)

    zrows = zero_ref.shape[0]
    zcount = xz_hbm.shape[0] // (n_steps * zrows)

    def zero_copy(q):
        return pltpu.make_async_copy(zero_ref, xz_hbm.at[pl.ds((step_id * zcount + q) * zrows, zrows)], zero_sem)

    def start_zero(q, carry):
        zero_copy(q).start()
        return carry

    def wait_zero(q, carry):
        zero_copy(q).wait()
        return carry

    lax.fori_loop(0, zcount, start_zero, 0)

    for hh in heads:
        for t in range(2):
            wide = jnp.broadcast_to(bw_ref[hh, t:t + 1, :], (blk, 2 * blk))
            bias_ref[hh, t] = pltpu.roll(wide, 0, 1, stride=1, stride_axis=0)[:, blk:2 * blk]

    rowi = lax.broadcasted_iota(jnp.int32, (MAX_BLOCKS, blk), 0)
    km_lane = lax.broadcasted_iota(jnp.int32, (MAX_BLOCKS, LANES), 1)
    zpad = jnp.zeros((LANES - hd - MAX_BLOCKS, blk), BF16)
    km_split = []
    for hh in heads:
        sub = hh % 2
        in_head = (km_lane >= sub * hd) & (km_lane < (sub + 1) * hd)
        km_h = jnp.where(in_head, km_ref[0, :, pair_lanes(hh)], 0.0)
        km_hi = km_h.astype(BF16)
        km_split.append((km_hi, (km_h - km_hi.astype(F32)).astype(BF16)))

    grp = FAR_GROUP

    def retire_begin(state, mx):
        m, acc = state
        m_new = jnp.maximum(m, mx)
        return m_new, jnp.exp2(m - m_new) * acc

    def retire_slice(src_ref, hh, m_new, acc, g, j):
        p = jnp.exp2(src_ref[hh, g * blk:(g + 1) * blk, :] - m_new)
        return acc + _dot(vaug_ref[j, hh], p.astype(BF16))

    def fresh():
        return (jnp.full((1, blk), -jnp.inf, F32), jnp.zeros((hd + ONES_ROWS, blk), F32))

    def near_blocks(i):
        return tuple(jnp.where(i - (grp - 1) + g < 0, i + 1, i - (grp - 1) + g) for g in range(grp))

    def finish(i, states):
        for hh in heads:
            _, acc = states[hh]
            o_ref[0, i, head_rows(hh), :] = (acc[0:hd] / acc[hd:hd + 1]).astype(BF16)

    def select(i):
        for hh in heads:
            qp = qT_ref[0, i, pair_lanes(hh), :]
            km_hi, km_lo = km_split[hh]
            gate = _dot(km_hi, qp) + _dot(km_lo, qp)
            g = jnp.where(rowi < i, gate, -jnp.inf)
            sel = rowi == i
            for _ in range(MOBA_TOPK):
                m = jnp.max(g, axis=0, keepdims=True)
                hit = (g == m) & (m > -jnp.inf)
                idx = jnp.min(jnp.where(hit, rowi, MAX_BLOCKS), axis=0, keepdims=True)
                pick = rowi == idx
                sel = sel | pick
                g = jnp.where(pick, -jnp.inf, g)
            near = jnp.where(sel, 0.0, NEG_INF)
            far = jnp.where(rowi > i - grp, NEG_INF, near)
            qh = qp[head_rows(hh % 2), :]
            if hh % 2 == 0:
                qnear_ref[hh] = jnp.concatenate([qh, near.astype(BF16), zpad], axis=0)
                qfar_ref[hh] = jnp.concatenate([qh, far.astype(BF16), zpad], axis=0)
            else:
                qnear_ref[hh] = jnp.concatenate([near.astype(BF16), zpad, qh], axis=0)
                qfar_ref[hh] = jnp.concatenate([far.astype(BF16), zpad, qh], axis=0)

    def park(dst_ref, q_ref, hh, g, j, bias=None):
        rows = pl.ds(pl.multiple_of(j * blk, blk), blk)
        s = _dot(kaug_ref[hh, rows, :], q_ref[hh])
        if bias is not None:
            s = s + bias
        dst_ref[hh, g * blk:(g + 1) * blk, :] = s
        return jnp.max(s, axis=0, keepdims=True)

    def park_near(i):
        idx = near_blocks(i)

        def quarter(g, hh):
            bias = None
            if g == grp - 1:
                bias = bias_ref[hh, 0]
            elif g == grp - 2:
                bias = bias_ref[hh, 1]
            return park(sn_ref, qnear_ref, hh, g, idx[g], bias)
        return quarter

    def park_far(c, dst_ref):
        return lambda g, hh: park(dst_ref, qfar_ref, hh, g, c * grp + g)

    def chunk_blocks(c):
        return tuple(c * grp + g for g in range(grp))

    def step(src_ref, states, maxes, vidx, quarter, park_first):
        begun = [retire_begin(states[hh], maxes[hh]) for hh in heads] if states is not None else None
        accs = [b[1] for b in begun] if begun else None
        new_max = [None] * nh

        def park_quarter(g):
            for hh in heads:
                cm = quarter(g, hh)
                new_max[hh] = cm if new_max[hh] is None else jnp.maximum(new_max[hh], cm)

        for g in range(grp):
            if park_first:
                park_quarter(g)
            if begun:
                for hh in heads:
                    accs[hh] = retire_slice(src_ref, hh, begun[hh][0], accs[hh], g, vidx[g])
            if not park_first and quarter is not None:
                park_quarter(g)
        new_states = tuple((begun[hh][0], accs[hh]) for hh in heads) if begun else None
        return new_states, tuple(new_max)

    def fresh_states():
        return tuple(fresh() for _ in heads)

    def early_tile(i, near_max):
        nxt = jnp.minimum(i + 1, nb - 1)
        select(nxt)
        states, next_max = step(sn_ref, fresh_states(), near_max, near_blocks(i), park_near(nxt), False)
        finish(i, states)
        return next_max

    def late_tile(i, near_max):
        sf = (sf0_ref, sf1_ref)
        n_chunks = (i - (grp - 1) + grp - 1) // grp
        nxt = jnp.minimum(i + 1, nb - 1)
        states, far_max = step(sn_ref, fresh_states(), near_max, near_blocks(i), park_far(0, sf[0]), True)

        def pair(k, st):
            states, far_max = st
            states, far_max = step(sf[0], states, far_max, chunk_blocks(2 * k), park_far(2 * k + 1, sf[1]), True)
            return step(sf[1], states, far_max, chunk_blocks(2 * k + 1), park_far(2 * k + 2, sf[0]), True)

        states, far_max = lax.fori_loop(0, (n_chunks - 1) // 2, pair, (states, far_max))

        def last(src, states, far_max):
            select(nxt)
            states, next_max = step(src, states, far_max, chunk_blocks(n_chunks - 1), park_near(nxt), True)
            finish(i, states)
            return next_max

        def even_count(st):
            states, far_max = step(sf[0], st[0], st[1], chunk_blocks(n_chunks - 2),
                                   park_far(n_chunks - 1, sf[1]), True)
            return last(sf[1], states, far_max)

        def odd_count(st):
            return last(sf[0], st[0], st[1])

        return lax.cond(n_chunks % 2 == 0, even_count, odd_count, (states, far_max))

    n_early = min(grp, nb)
    select(0)
    _, near_max = step(None, None, None, None, park_near(0), True)
    near_max = lax.fori_loop(0, n_early, early_tile, near_max)
    lax.fori_loop(n_early, nb, late_tile, near_max)
    lax.fori_loop(0, zcount, wait_zero, 0)


def _rel_bucket_table(max_rel):
    n = np.arange(max_rel)
    max_exact = REL_BUCKETS // 2
    nf = np.maximum(n, 1).astype(np.float32)
    large = max_exact + (np.log(nf / np.float32(max_exact)) / np.float32(math.log(REL_MAX_DIST / max_exact))
                         * np.float32(REL_BUCKETS - max_exact)).astype(np.int32)
    large = np.minimum(large, REL_BUCKETS - 1)
    return np.where(n < max_exact, n, large)


def _bias_vectors(rel_bias):
    blk = MOBA_BLOCK
    table = _rel_bucket_table(2 * blk)
    r = np.arange(-blk, blk)
    bt = (rel_bias.T - rel_bias[REL_BUCKETS - 1][:, None]) * LOG2E
    own = jnp.where(jnp.asarray(r >= 0)[None], bt[:, table[np.maximum(r, 0)]], NEG_INF)
    prev = bt[:, table[np.minimum(r + blk, 2 * blk - 1)]]
    return jnp.stack([own, prev], axis=1).astype(F32)


def _moba_attention(qT, k3, vT, km_pad, bias_w, *, batch, seq):
    nb = seq // MOBA_BLOCK
    blk = MOBA_BLOCK
    assert nb % FAR_GROUP == 0 and nb <= MAX_BLOCKS
    nh = HEADS_PER_STEP
    width = nh * HEAD_DIM
    assert nh % 2 == 0 and N_HEADS % nh == 0
    n_steps = batch * (N_HEADS // nh)
    sorted_rows = _expert_tiles(batch * seq) * TE_ROWS
    assert sorted_rows % (n_steps * ZERO_ROWS) == 0
    kern = functools.partial(_attn_kernel, nb=nb, n_steps=n_steps)

    hbm = pl.BlockSpec(memory_space=pl.ANY)
    return pl.pallas_call(
        kern,
        grid=(batch, N_HEADS // nh),
        in_specs=[pl.BlockSpec((1, nb, width, blk), lambda b, p: (b, 0, p, 0)),
                  hbm, hbm,
                  pl.BlockSpec((1, MAX_BLOCKS, width), lambda b, p: (b, 0, p)),
                  pl.BlockSpec((nh, 2, 2 * blk), lambda b, p: (p, 0, 0))],
        out_specs=[pl.BlockSpec((1, nb, width, blk), lambda b, p: (b, 0, p, 0)), hbm],
        out_shape=[jax.ShapeDtypeStruct((batch, nb, ATTN_WIDTH, blk), BF16),
                   jax.ShapeDtypeStruct((sorted_rows, D_MODEL), F32)],
        scratch_shapes=[pltpu.VMEM((nh, seq, LANES), BF16),
                        pltpu.VMEM((nb, nh, HEAD_DIM + ONES_ROWS, blk), BF16),
                        pltpu.VMEM((nh, 2, blk, blk), F32),
                        pltpu.VMEM((nh, LANES, blk), BF16),
                        pltpu.VMEM((nh, LANES, blk), BF16),
                        pltpu.VMEM((nh, FAR_GROUP * blk, blk), F32),
                        pltpu.VMEM((nh, FAR_GROUP * blk, blk), F32),
                        pltpu.VMEM((nh, FAR_GROUP * blk, blk), F32),
                        pltpu.VMEM((seq, width), BF16),
                        pltpu.VMEM((nb, width, blk), BF16),
                        pltpu.SemaphoreType.DMA((2,)),
                        pltpu.VMEM((ZERO_ROWS, D_MODEL), F32),
                        pltpu.SemaphoreType.DMA(())],
        compiler_params=_cparams("arbitrary", "arbitrary"),
        name="moba_attention",
    )(qT, k3, vT, km_pad, bias_w)


def _post_kernel(x_ref, mod_ref, h_ref, wz_ref, bz_ref, at_ref, u_ref, uh_ref, pw_ref, ps_ref,
                 wba_ref, wbp_ref, wo_ref, g2_ref, wrh_ref, wrl_ref, br_ref,
                 x1_ref, h2r_ref, route_ref, route_t_ref, counts_ref, carry_ref, *, tm, seq):
    d = D_MODEL
    i = pl.program_id(0)
    x = x_ref[...]
    mod = mod_ref[0]
    gate1 = mod[:, 2 * d:3 * d]
    shift2, scale2 = mod[:, 3 * d:4 * d], mod[:, 4 * d:5 * d]

    zg = 0.5 * jnp.tanh(0.5 * (_dot(h_ref[...], wz_ref[...]) + bz_ref[...])) + 0.5

    ya = jnp.concatenate(
        [lax.dot_general(at_ref[0, c], wba_ref[...], (((0,), (0,)), ((), ())),
                         preferred_element_type=F32) for c in range(tm // MOBA_BLOCK)], axis=0)

    first = (i * tm) % seq == 0
    halo = jnp.where(first, 0.0, uh_ref[...])
    ubuf = jnp.concatenate([halo, u_ref[...]], axis=0)
    rows = tm + POOL_HALO
    pos = (lax.broadcasted_iota(jnp.int32, (tm, POOL_GROUP_WIDTH), 0) + i * tm) % seq
    outs = []
    for gi, w in enumerate(POOL_WINDOWS):
        e = ubuf[:, gi * POOL_GROUP_WIDTH:(gi + 1) * POOL_GROUP_WIDTH]
        s = e
        step = 1
        while step < w:
            s = s + pltpu.roll(s, step, 0)
            step *= 2
        cnt = jnp.minimum(pos + 1, w).astype(F32)
        dlt = s[POOL_HALO:rows] / cnt - e[POOL_HALO:rows]
        outs.append(_dot(dlt.astype(BF16), pw_ref[gi]))
    pooled = jnp.concatenate(outs, axis=-1) * ps_ref[...]
    yp = _dot(pooled.astype(BF16), wbp_ref[...])

    merged = zg[:, 0:d] * ya + zg[:, d:2 * d] * yp
    x1 = x + gate1 * _dot(merged.astype(BF16), wo_ref[...])
    x1_ref[...] = x1

    h2 = _rms_mod(x1, g2_ref[...], scale2, shift2)
    h2_hi = h2.astype(BF16)
    h2r_ref[...] = h2_hi.astype(F32)
    h2_lo = (h2 - h2_hi.astype(F32)).astype(BF16)
    z = _dot(h2_hi, wrh_ref[...]) + _dot(h2_lo, wrh_ref[...]) + _dot(h2_hi, wrl_ref[...]) + br_ref[...]

    rr = ROUTER_ROWS
    zt = z.T[0:rr, :]
    row = lax.broadcasted_iota(jnp.int32, (rr, tm), 0)
    zgrp = jnp.where(row < N_GROUPS, zt, -jnp.inf)
    mg = jnp.max(zgrp, axis=0, keepdims=True)
    pg_top = 1.0 / jnp.sum(jnp.exp(zgrp - mg), axis=0, keepdims=True)
    g_idx = jnp.min(jnp.where(zgrp == mg, row, rr), axis=0, keepdims=True)
    e_row = row - ROUTER_LANE0
    in_grp = (e_row >= 0) & (e_row < N_EXPERTS) & ((e_row >> 3) == g_idx)
    ze = jnp.where(in_grp, zt, -jnp.inf)
    m1 = jnp.max(ze, axis=0, keepdims=True)
    i1 = jnp.min(jnp.where(ze == m1, row, rr), axis=0, keepdims=True)
    ze2 = jnp.where(row == i1, -jnp.inf, ze)
    m2 = jnp.max(ze2, axis=0, keepdims=True)
    i2 = jnp.min(jnp.where(ze2 == m2, row, rr), axis=0, keepdims=True)
    e2 = jnp.exp(m2 - m1)
    w1 = pg_top / (1.0 + e2)
    w2 = pg_top * e2 / (1.0 + e2)

    hot = jnp.where((row == i1) | (row == i2), 1.0, 0.0)
    tri = jnp.where(lax.broadcasted_iota(jnp.int32, (tm, tm), 0) <= lax.broadcasted_iota(jnp.int32, (tm, tm), 1),
                    1.0, 0.0).astype(BF16)
    prefix = _dot(hot.astype(BF16), tri)

    @pl.when(i == 0)
    def _():
        carry_ref[...] = jnp.zeros_like(carry_ref)

    carry = carry_ref[...]
    base = carry[:, 0:1] + prefix - 1.0
    rank1 = jnp.sum(jnp.where(row == i1, base, 0.0), axis=0, keepdims=True)
    rank2 = jnp.sum(jnp.where(row == i2, base, 0.0), axis=0, keepdims=True)
    carry_ref[...] = carry + prefix[:, tm - 1:tm]
    counts_ref[...] = carry_ref[...]
    fields = ((i1 - ROUTER_LANE0).astype(F32), (i2 - ROUTER_LANE0).astype(F32), w1, w2, rank1, rank2)
    frow = lax.broadcasted_iota(jnp.int32, (ROUTE_FIELDS, tm), 0)
    route_t = jnp.zeros((ROUTE_FIELDS, tm), F32)
    for k, val in enumerate(fields):
        route_t = jnp.where(frow == k, val, route_t)
    route_t_ref[...] = route_t
    route_ref[...] = jnp.concatenate([route_t, jnp.zeros((LANES - ROUTE_FIELDS, tm), F32)], axis=0).T


def _post_attention(x2, mod3, h1, w_z, b_z, attnT, u, pool_w, pool_scale, w_ba, w_bp, w_out, g2,
                    wr_hi, wr_lo, b_r, *, batch, seq):
    n, d = x2.shape
    tm = TM_POST
    tps = seq // tm
    cb = tm // MOBA_BLOCK
    hb = tm // POOL_HALO
    kern = functools.partial(_post_kernel, tm=tm, seq=seq)
    const2 = lambda i: (0, 0)
    return pl.pallas_call(
        kern,
        grid=(n // tm,),
        in_specs=[pl.BlockSpec((tm, d), lambda i: (i, 0)),
                  pl.BlockSpec((1, 1, N_MOD * d), lambda i: (i // tps, 0, 0)),
                  pl.BlockSpec((tm, d), lambda i: (i, 0)),
                  pl.BlockSpec((d, 2 * d), lambda i: (0, 1)),
                  pl.BlockSpec((1, 2 * d), const2),
                  pl.BlockSpec((1, cb, ATTN_WIDTH, MOBA_BLOCK), lambda i: (i // tps, i % tps, 0, 0)),
                  pl.BlockSpec((tm, POOL_WIDTH), lambda i: (i, 0)),
                  pl.BlockSpec((POOL_HALO, POOL_WIDTH), lambda i: (jnp.maximum(i * hb - 1, 0), 0)),
                  pl.BlockSpec((len(POOL_WINDOWS), POOL_GROUP_WIDTH, POOL_GROUP_WIDTH), lambda i: (0, 0, 0)),
                  pl.BlockSpec((1, POOL_WIDTH), const2),
                  pl.BlockSpec((ATTN_WIDTH, d), const2),
                  pl.BlockSpec((POOL_WIDTH, d), const2),
                  pl.BlockSpec((d, d), const2),
                  pl.BlockSpec((1, d), const2),
                  pl.BlockSpec((d, LANES), const2),
                  pl.BlockSpec((d, LANES), const2),
                  pl.BlockSpec((1, LANES), const2)],
        out_specs=[pl.BlockSpec((tm, d), lambda i: (i, 0)),
                   pl.BlockSpec((tm, d), lambda i: (i, 0)),
                   pl.BlockSpec((tm, LANES), lambda i: (i, 0)),
                   pl.BlockSpec((ROUTE_FIELDS, tm), lambda i: (0, i)),
                   pl.BlockSpec((ROUTER_ROWS, LANES), const2)],
        out_shape=[jax.ShapeDtypeStruct((n, d), F32),
                   jax.ShapeDtypeStruct((n, d), F32),
                   jax.ShapeDtypeStruct((n, LANES), F32),
                   jax.ShapeDtypeStruct((ROUTE_FIELDS, n), F32),
                   jax.ShapeDtypeStruct((ROUTER_ROWS, LANES), F32)],
        scratch_shapes=[pltpu.VMEM((ROUTER_ROWS, LANES), F32)],
        compiler_params=_cparams("arbitrary"),
        name="merge_outproj_router",
    )(x2, mod3, h1, w_z, b_z, attnT, u, u, pool_w, pool_scale, w_ba, w_bp, w_out, g2, wr_hi, wr_lo, b_r)


def _routing_plan(route_t, counts, *, n):
    te = TE_ROWS
    n_tiles = _expert_tiles(n)
    cnt = counts[ROUTER_LANE0:ROUTER_LANE0 + N_EXPERTS, 0].astype(jnp.int32)
    padded = ((cnt + te - 1) // te) * te
    ids = jnp.arange(N_EXPERTS, dtype=jnp.int32)
    ends = jnp.sum(jnp.where(ids[None, :] <= ids[:, None], padded[None, :], 0), axis=-1)
    starts = ends - padded
    pos = []
    for k in range(EXPERT_TOPK):
        expert = route_t[k].astype(jnp.int32)
        rank = route_t[4 + k].astype(jnp.int32)
        pos.append(jnp.sum(jnp.where(expert[:, None] == ids, starts, 0), axis=-1) + rank)
    tiles = jnp.arange(n_tiles, dtype=jnp.int32)
    tile_expert = jnp.sum((ends // te)[None, :] <= tiles[:, None], axis=-1)
    tile_expert = jnp.minimum(tile_expert, N_EXPERTS - 1).astype(jnp.int32)
    n_used = (ends[N_EXPERTS - 1:] // te).astype(jnp.int32)
    later = (ids[None, :] > ids[:, None]) & (cnt[None, :] > 0)
    next_expert = jnp.min(jnp.where(later, ids[None, :], N_EXPERTS), axis=-1)
    next_expert = jnp.where(next_expert == N_EXPERTS, -1, next_expert).astype(jnp.int32)
    return pos, tile_expert, n_used, next_expert


def _expert_tiles(n):
    return (EXPERT_TOPK * n) // TE_ROWS + N_EXPERTS


def _tile_pos(pos, tm):
    return jnp.concatenate([p.reshape(-1, 1, tm) for p in pos], axis=-1)


def _dispatch_kernel(pos_ref, h_ref, xz_ref, xs_ref, sem, *, tm):
    del xz_ref
    for r in range(tm):
        for k in range(EXPERT_TOPK):
            pltpu.make_async_copy(h_ref.at[pl.ds(r, 1)], xs_ref.at[pl.ds(pos_ref[0, 0, k * tm + r], 1)],
                                  sem).start(priority=(r + k) % 2)
    for k in range(EXPERT_TOPK):
        pltpu.make_async_copy(h_ref, h_ref, sem).wait()


def _dispatch(pos, h2r, xz, *, n):
    tm = TM_DISPATCH
    rows, d = xz.shape
    assert rows == _expert_tiles(n) * TE_ROWS
    pos3 = _tile_pos(pos, tm)
    return pl.pallas_call(
        functools.partial(_dispatch_kernel, tm=tm),
        grid=(n // tm,),
        in_specs=[pl.BlockSpec((1, 1, EXPERT_TOPK * tm), lambda i: (i, 0, 0), memory_space=pltpu.SMEM),
                  pl.BlockSpec((tm, d), lambda i: (i, 0)),
                  pl.BlockSpec(memory_space=pl.ANY)],
        out_specs=pl.BlockSpec(memory_space=pl.ANY),
        out_shape=jax.ShapeDtypeStruct((rows, d), F32),
        scratch_shapes=[pltpu.SemaphoreType.DMA(())],
        input_output_aliases={2: 0},
        compiler_params=_cparams("arbitrary"),
        name="moe_dispatch",
    )(pos3, h2r, xz)


def _experts_kernel(te_ref, nu_ref, nx_ref, xs_ref, wg_ref, wu_ref, wd_ref, y_ref,
                    wg_b, wu_b, wd_b, wg_f, wu_f, wd_f, sems, turn_ref):
    t = pl.program_id(0)

    def fetch(e, slot):
        return [pltpu.make_async_copy(src.at[e], dst.at[slot], sems.at[slot])
                for src, dst in ((wg_ref, wg_f), (wu_ref, wu_f), (wd_ref, wd_f))]

    @pl.when(t < nu_ref[0])
    def _():
        e = te_ref[t]
        e_prev = te_ref[jnp.maximum(t - 1, 0)]

        @pl.when(t == 0)
        def _():
            turn_ref[0] = 0
            for cp in fetch(e, 0):
                cp.start()

        @pl.when((t == 0) | (e != e_prev))
        def _():
            slot = turn_ref[0] % 2
            for cp in fetch(e, slot):
                cp.wait()
            wg_b[...] = wg_f[slot].astype(BF16)
            wu_b[...] = wu_f[slot].astype(BF16)
            wd_b[...] = wd_f[slot].astype(BF16)
            e_next = nx_ref[e]

            @pl.when(e_next >= 0)
            def _():
                for cp in fetch(e_next, 1 - slot):
                    cp.start()

            turn_ref[0] = turn_ref[0] + 1

        x = xs_ref[...].astype(BF16)
        g = _dot(x, wg_b[...])
        u = _dot(x, wu_b[...])
        act = (g * jax.nn.sigmoid(g)) * u
        y_ref[...] = _dot(act.astype(BF16), wd_b[...])

    @pl.when(t >= nu_ref[0])
    def _():
        y_ref[...] = jnp.zeros_like(y_ref)


def _experts(tile_expert, n_used, next_expert, xs, wg, wu, wd):
    rows, d = xs.shape
    te = TE_ROWS
    hbm = pl.BlockSpec(memory_space=pl.ANY)
    grid_spec = pltpu.PrefetchScalarGridSpec(
        num_scalar_prefetch=3,
        grid=(rows // te,),
        in_specs=[pl.BlockSpec((te, d), lambda t, te_r, nu, nx: (jnp.minimum(t, nu[0] - 1), 0)),
                  hbm, hbm, hbm],
        out_specs=pl.BlockSpec((te, d), lambda t, te_r, nu, nx: (t, 0)),
        scratch_shapes=[pltpu.VMEM((d, D_EXPERT), BF16), pltpu.VMEM((d, D_EXPERT), BF16),
                        pltpu.VMEM((D_EXPERT, d), BF16),
                        pltpu.VMEM((2, d, D_EXPERT), F32), pltpu.VMEM((2, d, D_EXPERT), F32),
                        pltpu.VMEM((2, D_EXPERT, d), F32),
                        pltpu.SemaphoreType.DMA((2,)), pltpu.SMEM((1,), jnp.int32)])
    return pl.pallas_call(
        _experts_kernel,
        grid_spec=grid_spec,
        out_shape=jax.ShapeDtypeStruct((rows, d), F32),
        compiler_params=_cparams("arbitrary"),
        name="moe_experts",
    )(tile_expert, n_used, next_expert, xs, wg, wu, wd)


def _final_kernel(pos_ref, pos_next_ref, x1_ref, route_ref, mod_ref, g_ref, y_ref, o_ref, ybuf, sems, *,
                  tm, final_norm):
    d = D_MODEL
    i = pl.program_id(0)
    slot = i % 2

    def gather(p_ref, s):
        for r in range(tm):
            for k in range(EXPERT_TOPK):
                pltpu.make_async_copy(y_ref.at[pl.ds(p_ref[0, 0, k * tm + r], 1)],
                                      ybuf.at[s, k, pl.ds(r, 1)], sems.at[s]).start(priority=(r + k) % 2)

    @pl.when(i == 0)
    def _():
        gather(pos_ref, 0)

    for s in range(2):
        @pl.when((i + 1 < pl.num_programs(0)) & (slot == 1 - s))
        def _(s=s):
            gather(pos_next_ref, s)

    for k in range(EXPERT_TOPK):
        pltpu.make_async_copy(ybuf.at[slot, k], ybuf.at[slot, k], sems.at[slot]).wait()

    route = route_ref[...]
    lane = lax.broadcasted_iota(jnp.int32, route.shape, 1)
    y = jnp.zeros((tm, d), F32)
    for k in range(EXPERT_TOPK):
        wk = jnp.sum(jnp.where(lane == 2 + k, route, 0.0), axis=-1, keepdims=True)
        y = y + wk * ybuf[slot, k]
    gate2 = mod_ref[0][:, 5 * d:6 * d]
    x = x1_ref[...] + gate2 * y
    if final_norm:
        x = (x * lax.rsqrt(jnp.mean(x * x, axis=-1, keepdims=True) + EPS)) * g_ref[...]
    o_ref[...] = x


def _final(pos, x1, route, mod3, gf, y, *, seq, final_norm):
    n, d = x1.shape
    tm = TM_FINAL
    tps = seq // tm
    pos3 = _tile_pos(pos, tm)
    last = n // tm - 1
    return pl.pallas_call(
        functools.partial(_final_kernel, tm=tm, final_norm=final_norm),
        grid=(n // tm,),
        in_specs=[pl.BlockSpec((1, 1, EXPERT_TOPK * tm), lambda i: (i, 0, 0), memory_space=pltpu.SMEM),
                  pl.BlockSpec((1, 1, EXPERT_TOPK * tm), lambda i: (jnp.minimum(i + 1, last), 0, 0),
                               memory_space=pltpu.SMEM),
                  pl.BlockSpec((tm, d), lambda i: (i, 0)),
                  pl.BlockSpec((tm, LANES), lambda i: (i, 0)),
                  pl.BlockSpec((1, 1, N_MOD * d), lambda i: (i // tps, 0, 0)),
                  pl.BlockSpec((1, d), lambda i: (0, 0)),
                  pl.BlockSpec(memory_space=pl.ANY)],
        out_specs=pl.BlockSpec((tm, d), lambda i: (i, 0)),
        out_shape=jax.ShapeDtypeStruct((n, d), F32),
        scratch_shapes=[pltpu.VMEM((2, EXPERT_TOPK, tm, d), F32), pltpu.SemaphoreType.DMA((2,))],
        compiler_params=_cparams("arbitrary"),
        name="moe_combine_final_norm",
    )(pos3, pos3, x1, route, mod3, gf, y)


def _split_bf16(w):
    hi = w.astype(BF16)
    return hi, (w - hi.astype(F32)).astype(BF16)


def kernel(x, c, w_ada, b_ada, norm1_g, w_in, b_gate, rel_bias, pool_w, pool_scale, w_branch_attn,
           w_branch_pool, w_out, norm2_g, w_router_group, b_router_group, w_router_expert,
           b_router_expert, w_expert_gate, w_expert_up, w_expert_down, norm_f_g):
    batch, seq, d = x.shape
    depth = w_ada.shape[0]
    n = batch * seq
    nb = seq // MOBA_BLOCK
    qkvu = 3 * ATTN_WIDTH + POOL_WIDTH
    bias_w = _bias_vectors(rel_bias)
    c_pad = jnp.pad(c, ((0, 8 - batch), (0, 0)))

    xc = x.reshape(n, d)
    for l in range(depth):
        mod3 = _modulation(c_pad, w_ada[l], b_ada[l][None])[:batch].reshape(batch, 1, N_MOD * d)
        w_in_b = w_in[l].astype(BF16)
        g1 = norm1_g[l][None]

        assert w_in_b.shape[1] == 2 * qkvu == qkvu + 2 * d
        qT, k2, vT, u, km, h1 = _in_projection(xc, mod3, g1, w_in_b, batch=batch, seq=seq)
        km_pad = jnp.pad(km.reshape(batch, nb, ATTN_WIDTH), ((0, 0), (0, MAX_BLOCKS - nb), (0, 0)))
        attnT, xz = _moba_attention(qT, k2.reshape(batch, seq, ATTN_WIDTH), vT, km_pad, bias_w,
                                    batch=batch, seq=seq)

        w_r = jnp.concatenate([w_router_group[l], w_router_expert[l]], axis=1)
        w_r = jnp.pad(w_r, ((0, 0), (0, LANES - w_r.shape[1])))
        b_r = jnp.concatenate([b_router_group[l], b_router_expert[l]])
        b_r = jnp.pad(b_r, (0, LANES - b_r.shape[0]))[None]
        wr_hi, wr_lo = _split_bf16(w_r)
        x1, h2r, route, route_t, counts = _post_attention(
            xc, mod3, h1, w_in_b, b_gate[l][None], attnT, u, pool_w[l].astype(BF16),
            pool_scale[l][None], w_branch_attn[l].astype(BF16), w_branch_pool[l].astype(BF16),
            w_out[l].astype(BF16), norm2_g[l][None], wr_hi, wr_lo, b_r, batch=batch, seq=seq)

        pos, tile_expert, n_used, next_expert = _routing_plan(route_t, counts, n=n)
        xs = _dispatch(pos, h2r, xz, n=n)
        y = _experts(tile_expert, n_used, next_expert, xs, w_expert_gate[l], w_expert_up[l],
                     w_expert_down[l])
        xc = _final(pos, x1, route, mod3, norm_f_g[None], y, seq=seq, final_norm=(l == depth - 1))
    return xc.reshape(batch, seq, d)
```

```python
import functools
import math

import numpy as np
import jax
import jax.numpy as jnp
from jax import lax
from jax.experimental import pallas as pl
from jax.experimental.pallas import tpu as pltpu

F32 = jnp.float32
BF16 = jnp.bfloat16

D_MODEL = 1024
N_HEADS = 8
HEAD_DIM = 64
ATTN_WIDTH = N_HEADS * HEAD_DIM
MOBA_BLOCK = 256
MOBA_TOPK = 3
MAX_BLOCKS = 32
FAR_GROUP = 4
HEADS_PER_STEP = 4
ZERO_ROWS = 256
ONES_ROWS = 16
LOG2E = math.log2(math.e)
POOL_WINDOWS = (2, 4, 8, 16)
POOL_WIDTH = 512
POOL_GROUP_WIDTH = 128
POOL_HALO = 16
REL_BUCKETS = 32
REL_MAX_DIST = 128
N_GROUPS = 4
EXPERTS_PER_GROUP = 8
N_EXPERTS = N_GROUPS * EXPERTS_PER_GROUP
D_EXPERT = 512
N_MOD = 6
EPS = 1e-6
NEG_INF = -1e30
LANES = 128
ROUTER_LANE0 = N_GROUPS
ROUTE_FIELDS = 8
ROUTER_ROWS = 48
VMEM_LIMIT = 56 * 1024 * 1024

TM_PROJ = 1024
TM_POST = 1024
TM_DISPATCH = 2048
TE_ROWS = 512
EXPERT_TOPK = 2
TM_FINAL = 256


def _cparams(*sem):
    return pltpu.CompilerParams(dimension_semantics=sem, vmem_limit_bytes=VMEM_LIMIT)


def _dot(a, b):
    return jnp.dot(a, b, preferred_element_type=F32)


def _rms_mod(x, g, scale, shift):
    xn = x * lax.rsqrt(jnp.mean(x * x, axis=-1, keepdims=True) + EPS)
    return (xn * g) * (1.0 + scale) + shift


def _mod_kernel(c_ref, w_ref, b_ref, o_ref):
    c = c_ref[...]
    ca = c * jax.nn.sigmoid(c)
    o_ref[...] = _dot(ca.astype(BF16), w_ref[...].astype(BF16)) + b_ref[...]


def _modulation(c_pad, w_ada, b_ada):
    rows, d = c_pad.shape
    n_out = w_ada.shape[1]
    tn = 1024
    return pl.pallas_call(
        _mod_kernel,
        grid=(n_out // tn,),
        in_specs=[pl.BlockSpec((rows, d), lambda j: (0, 0)),
                  pl.BlockSpec((d, tn), lambda j: (0, j)),
                  pl.BlockSpec((1, tn), lambda j: (0, j))],
        out_specs=pl.BlockSpec((rows, tn), lambda j: (0, j)),
        out_shape=jax.ShapeDtypeStruct((rows, n_out), F32),
        compiler_params=_cparams("arbitrary"),
        name="adaln_mod",
    )(c_pad, w_ada, b_ada)


def _inproj_kernel(x_ref, mod_ref, g_ref, w_ref, qT_ref, k_ref, vT_ref, u_ref, km_ref, h_ref, *, tm):
    d = D_MODEL
    mod = mod_ref[0]
    h = _rms_mod(x_ref[...], g_ref[...], mod[:, d:2 * d], mod[:, 0:d]).astype(BF16)
    h_ref[...] = h
    proj = _dot(h, w_ref[...])
    aw = ATTN_WIDTH
    q = proj[:, 0:aw] * (HEAD_DIM ** -0.5 * LOG2E)
    k = proj[:, aw:2 * aw]
    v = proj[:, 2 * aw:3 * aw]
    u_ref[...] = proj[:, 3 * aw:3 * aw + POOL_WIDTH]
    k_ref[...] = k.astype(BF16)
    qT = q.T.astype(BF16)
    vT = v.T.astype(BF16)
    for c in range(tm // MOBA_BLOCK):
        sl = slice(c * MOBA_BLOCK, (c + 1) * MOBA_BLOCK)
        qT_ref[0, c] = qT[:, sl]
        vT_ref[0, c] = vT[:, sl]
        km_ref[0, c:c + 1, :] = jnp.mean(k[sl, :], axis=0, keepdims=True)


def _in_projection(x2, mod3, g1, w_qkvu, *, batch, seq):
    n, d = x2.shape
    tm = TM_PROJ
    tps = seq // tm
    nb = seq // MOBA_BLOCK
    cb = tm // MOBA_BLOCK
    wcols = 3 * ATTN_WIDTH + POOL_WIDTH
    assert w_qkvu.shape[1] % wcols == 0
    kern = functools.partial(_inproj_kernel, tm=tm)
    t_shape = jax.ShapeDtypeStruct((batch, nb, ATTN_WIDTH, MOBA_BLOCK), BF16)
    t_spec = pl.BlockSpec((1, cb, ATTN_WIDTH, MOBA_BLOCK), lambda i: (i // tps, i % tps, 0, 0))
    return pl.pallas_call(
        kern,
        grid=(n // tm,),
        in_specs=[pl.BlockSpec((tm, d), lambda i: (i, 0)),
                  pl.BlockSpec((1, 1, N_MOD * d), lambda i: (i // tps, 0, 0)),
                  pl.BlockSpec((1, d), lambda i: (0, 0)),
                  pl.BlockSpec((d, wcols), lambda i: (0, 0))],
        out_specs=[t_spec,
                   pl.BlockSpec((tm, ATTN_WIDTH), lambda i: (i, 0)),
                   t_spec,
                   pl.BlockSpec((tm, POOL_WIDTH), lambda i: (i, 0)),
                   pl.BlockSpec((1, cb, ATTN_WIDTH), lambda i: (i, 0, 0)),
                   pl.BlockSpec((tm, d), lambda i: (i, 0))],
        out_shape=[t_shape,
                   jax.ShapeDtypeStruct((n, ATTN_WIDTH), BF16),
                   t_shape,
                   jax.ShapeDtypeStruct((n, POOL_WIDTH), F32),
                   jax.ShapeDtypeStruct((n // tm, cb, ATTN_WIDTH), F32),
                   jax.ShapeDtypeStruct((n, d), BF16)],
        compiler_params=_cparams("arbitrary"),
        name="norm1_inproj",
    )(x2, mod3, g1, w_qkvu)


def _attn_kernel(qT_ref, k_hbm, v_hbm, km_ref, bw_ref, o_ref, xz_hbm, kaug_ref, vaug_ref, bias_ref, qnear_ref,
                 qfar_ref, sn_ref, sf0_ref, sf1_ref, k_ref, vT_ref, kv_sems, zero_ref, zero_sem, *, nb, n_steps):
    blk = MOBA_BLOCK
    hd = HEAD_DIM
    nh = HEADS_PER_STEP
    heads = range(nh)
    lane = lax.broadcasted_iota(jnp.int32, (blk, LANES), 1)

    def pair_lanes(hh):
        return slice((hh // 2) * LANES, (hh // 2 + 1) * LANES)

    def head_rows(hh):
        return slice(hh * hd, (hh + 1) * hd)

    ones = jnp.ones((ONES_ROWS, blk), BF16)

    width = nh * hd
    n_inner = pl.num_programs(1)
    step_id = pl.program_id(0) * n_inner + pl.program_id(1)

    def kv_copies(sid):
        bb, col = sid // n_inner, pl.multiple_of((sid % n_inner) * width, width)
        return (pltpu.make_async_copy(k_hbm.at[bb, :, pl.ds(col, width)], k_ref, kv_sems.at[0]),
                pltpu.make_async_copy(v_hbm.at[bb, :, pl.ds(col, width), :], vT_ref, kv_sems.at[1]))

    @pl.when(step_id == 0)
    def _():
        for cp in kv_copies(step_id):
            cp.start()

    for cp in kv_copies(step_id):
        cp.wait()

    def build_operands(j, carry):
        rows = pl.ds(pl.multiple_of(j * blk, blk), blk)
        for hh in heads:
            kp = k_ref[rows, pair_lanes(hh)]
            if hh % 2 == 0:
                kaug_ref[hh, rows, :] = jnp.where(lane < hd, kp, jnp.where(lane == hd + j, 1.0, 0.0).astype(BF16))
            else:
                kaug_ref[hh, rows, :] = jnp.where(lane >= hd, kp, jnp.where(lane == j, 1.0, 0.0).astype(BF16))
            vaug_ref[j, hh] = jnp.concatenate([vT_ref[j, head_rows(hh), :], ones], axis=0)
        return carry

    lax.fori_loop(0, nb, build_operands, 0)

    @pl.when(step_id + 1 < pl.num_programs(0) * n_inner)
    def _():
        for cp in kv_copies(step_id + 1):
            cp.start()

    @pl.when(step_id == 0)
    def _():
        zero_ref[...] = jnp.zeros_like(zero_ref)

    zrows = zero_ref.shape[0]
    zcount = xz_hbm.shape[0] // (n_steps * zrows)

    def zero_copy(q):
        return pltpu.make_async_copy(zero_ref, xz_hbm.at[pl.ds((step_id * zcount + q) * zrows, zrows)], zero_sem)

    def start_zero(q, carry):
        zero_copy(q).start()
        return carry

    def wait_zero(q, carry):
        zero_copy(q).wait()
        return carry

    lax.fori_loop(0, zcount, start_zero, 0)

    for hh in heads:
        for t in range(2):
            wide = jnp.broadcast_to(bw_ref[hh, t:t + 1, :], (blk, 2 * blk))
            bias_ref[hh, t] = pltpu.roll(wide, 0, 1, stride=1, stride_axis=0)[:, blk:2 * blk]

    rowi = lax.broadcasted_iota(jnp.int32, (MAX_BLOCKS, blk), 0)
    km_lane = lax.broadcasted_iota(jnp.int32, (MAX_BLOCKS, LANES), 1)
    zpad = jnp.zeros((LANES - hd - MAX_BLOCKS, blk), BF16)
    km_split = []
    for hh in heads:
        sub = hh % 2
        in_head = (km_lane >= sub * hd) & (km_lane < (sub + 1) * hd)
        km_h = jnp.where(in_head, km_ref[0, :, pair_lanes(hh)], 0.0)
        km_hi = km_h.astype(BF16)
        km_split.append((km_hi, (km_h - km_hi.astype(F32)).astype(BF16)))

    grp = FAR_GROUP

    def retire_begin(state, mx):
        m, acc = state
        m_new = jnp.maximum(m, mx)
        return m_new, jnp.exp2(m - m_new) * acc

    def retire_slice(src_ref, hh, m_new, acc, g, j):
        p = jnp.exp2(src_ref[hh, g * blk:(g + 1) * blk, :] - m_new)
        return acc + _dot(vaug_ref[j, hh], p.astype(BF16))

    def fresh():
        return (jnp.full((1, blk), -jnp.inf, F32), jnp.zeros((hd + ONES_ROWS, blk), F32))

    def near_blocks(i):
        return tuple(jnp.where(i - (grp - 1) + g < 0, i + 1, i - (grp - 1) + g) for g in range(grp))

    def finish(i, states):
        for hh in heads:
            _, acc = states[hh]
            o_ref[0, i, head_rows(hh), :] = (acc[0:hd] / acc[hd:hd + 1]).astype(BF16)

    def select(i):
        for hh in heads:
            qp = qT_ref[0, i, pair_lanes(hh), :]
            km_hi, km_lo = km_split[hh]
            gate = _dot(km_hi, qp) + _dot(km_lo, qp)
            g = jnp.where(rowi < i, gate, -jnp.inf)
            sel = rowi == i
            for _ in range(MOBA_TOPK):
                m = jnp.max(g, axis=0, keepdims=True)
                hit = (g == m) & (m > -jnp.inf)
                idx = jnp.min(jnp.where(hit, rowi, MAX_BLOCKS), axis=0, keepdims=True)
                pick = rowi == idx
                sel = sel | pick
                g = jnp.where(pick, -jnp.inf, g)
            near = jnp.where(sel, 0.0, NEG_INF)
            far = jnp.where(rowi > i - grp, NEG_INF, near)
            qh = qp[head_rows(hh % 2), :]
            if hh % 2 == 0:
                qnear_ref[hh] = jnp.concatenate([qh, near.astype(BF16), zpad], axis=0)
                qfar_ref[hh] = jnp.concatenate([qh, far.astype(BF16), zpad], axis=0)
            else:
                qnear_ref[hh] = jnp.concatenate([near.astype(BF16), zpad, qh], axis=0)
                qfar_ref[hh] = jnp.concatenate([far.astype(BF16), zpad, qh], axis=0)

    def park(dst_ref, q_ref, hh, g, j, bias=None):
        rows = pl.ds(pl.multiple_of(j * blk, blk), blk)
        s = _dot(kaug_ref[hh, rows, :], q_ref[hh])
        if bias is not None:
            s = s + bias
        dst_ref[hh, g * blk:(g + 1) * blk, :] = s
        return jnp.max(s, axis=0, keepdims=True)

    def park_near(i):
        idx = near_blocks(i)

        def quarter(g, hh):
            bias = None
            if g == grp - 1:
                bias = bias_ref[hh, 0]
            elif g == grp - 2:
                bias = bias_ref[hh, 1]
            return park(sn_ref, qnear_ref, hh, g, idx[g], bias)
        return quarter

    def park_far(c, dst_ref):
        return lambda g, hh: park(dst_ref, qfar_ref, hh, g, c * grp + g)

    def chunk_blocks(c):
        return tuple(c * grp + g for g in range(grp))

    def step(src_ref, states, maxes, vidx, quarter, park_first):
        begun = [retire_begin(states[hh], maxes[hh]) for hh in heads] if states is not None else None
        accs = [b[1] for b in begun] if begun else None
        new_max = [None] * nh

        def park_quarter(g):
            for hh in heads:
                cm = quarter(g, hh)
                new_max[hh] = cm if new_max[hh] is None else jnp.maximum(new_max[hh], cm)

        for g in range(grp):
            if park_first:
                park_quarter(g)
            if begun:
                for hh in heads:
                    accs[hh] = retire_slice(src_ref, hh, begun[hh][0], accs[hh], g, vidx[g])
            if not park_first and quarter is not None:
                park_quarter(g)
        new_states = tuple((begun[hh][0], accs[hh]) for hh in heads) if begun else None
        return new_states, tuple(new_max)

    def fresh_states():
        return tuple(fresh() for _ in heads)

    def early_tile(i, near_max):
        nxt = jnp.minimum(i + 1, nb - 1)
        select(nxt)
        states, next_max = step(sn_ref, fresh_states(), near_max, near_blocks(i), park_near(nxt), False)
        finish(i, states)
        return next_max

    def late_tile(i, near_max):
        sf = (sf0_ref, sf1_ref)
        n_chunks = (i - (grp - 1) + grp - 1) // grp
        nxt = jnp.minimum(i + 1, nb - 1)
        states, far_max = step(sn_ref, fresh_states(), near_max, near_blocks(i), park_far(0, sf[0]), True)

        def pair(k, st):
            states, far_max = st
            states, far_max = step(sf[0], states, far_max, chunk_blocks(2 * k), park_far(2 * k + 1, sf[1]), True)
            return step(sf[1], states, far_max, chunk_blocks(2 * k + 1), park_far(2 * k + 2, sf[0]), True)

        states, far_max = lax.fori_loop(0, (n_chunks - 1) // 2, pair, (states, far_max))

        def last(src, states, far_max):
            select(nxt)
            states, next_max = step(src, states, far_max, chunk_blocks(n_chunks - 1), park_near(nxt), True)
            finish(i, states)
            return next_max

        def even_count(st):
            states, far_max = step(sf[0], st[0], st[1], chunk_blocks(n_chunks - 2),
                                   park_far(n_chunks - 1, sf[1]), True)
            return last(sf[1], states, far_max)

        def odd_count(st):
            return last(sf[0], st[0], st[1])

        return lax.cond(n_chunks % 2 == 0, even_count, odd_count, (states, far_max))

    n_early = min(grp, nb)
    select(0)
    _, near_max = step(None, None, None, None, park_near(0), True)
    near_max = lax.fori_loop(0, n_early, early_tile, near_max)
    lax.fori_loop(n_early, nb, late_tile, near_max)
    lax.fori_loop(0, zcount, wait_zero, 0)


def _rel_bucket_table(max_rel):
    n = np.arange(max_rel)
    max_exact = REL_BUCKETS // 2
    nf = np.maximum(n, 1).astype(np.float32)
    large = max_exact + (np.log(nf / np.float32(max_exact)) / np.float32(math.log(REL_MAX_DIST / max_exact))
                         * np.float32(REL_BUCKETS - max_exact)).astype(np.int32)
    large = np.minimum(large, REL_BUCKETS - 1)
    return np.where(n < max_exact, n, large)


def _bias_vectors(rel_bias):
    blk = MOBA_BLOCK
    table = _rel_bucket_table(2 * blk)
    r = np.arange(-blk, blk)
    bt = (rel_bias.T - rel_bias[REL_BUCKETS - 1][:, None]) * LOG2E
    own = jnp.where(jnp.asarray(r >= 0)[None], bt[:, table[np.maximum(r, 0)]], NEG_INF)
    prev = bt[:, table[np.minimum(r + blk, 2 * blk - 1)]]
    return jnp.stack([own, prev], axis=1).astype(F32)


def _moba_attention(qT, k3, vT, km_pad, bias_w, *, batch, seq):
    nb = seq // MOBA_BLOCK
    blk = MOBA_BLOCK
    assert nb % FAR_GROUP == 0 and nb <= MAX_BLOCKS
    nh = HEADS_PER_STEP
    width = nh * HEAD_DIM
    assert nh % 2 == 0 and N_HEADS % nh == 0
    n_steps = batch * (N_HEADS // nh)
    sorted_rows = _expert_tiles(batch * seq) * TE_ROWS
    assert sorted_rows % (n_steps * ZERO_ROWS) == 0
    kern = functools.partial(_attn_kernel, nb=nb, n_steps=n_steps)

    hbm = pl.BlockSpec(memory_space=pl.ANY)
    return pl.pallas_call(
        kern,
        grid=(batch, N_HEADS // nh),
        in_specs=[pl.BlockSpec((1, nb, width, blk), lambda b, p: (b, 0, p, 0)),
                  hbm, hbm,
                  pl.BlockSpec((1, MAX_BLOCKS, width), lambda b, p: (b, 0, p)),
                  pl.BlockSpec((nh, 2, 2 * blk), lambda b, p: (p, 0, 0))],
        out_specs=[pl.BlockSpec((1, nb, width, blk), lambda b, p: (b, 0, p, 0)), hbm],
        out_shape=[jax.ShapeDtypeStruct((batch, nb, ATTN_WIDTH, blk), BF16),
                   jax.ShapeDtypeStruct((sorted_rows, D_MODEL), F32)],
        scratch_shapes=[pltpu.VMEM((nh, seq, LANES), BF16),
                        pltpu.VMEM((nb, nh, HEAD_DIM + ONES_ROWS, blk), BF16),
                        pltpu.VMEM((nh, 2, blk, blk), F32),
                        pltpu.VMEM((nh, LANES, blk), BF16),
                        pltpu.VMEM((nh, LANES, blk), BF16),
                        pltpu.VMEM((nh, FAR_GROUP * blk, blk), F32),
                        pltpu.VMEM((nh, FAR_GROUP * blk, blk), F32),
                        pltpu.VMEM((nh, FAR_GROUP * blk, blk), F32),
                        pltpu.VMEM((seq, width), BF16),
                        pltpu.VMEM((nb, width, blk), BF16),
                        pltpu.SemaphoreType.DMA((2,)),
                        pltpu.VMEM((ZERO_ROWS, D_MODEL), F32),
                        pltpu.SemaphoreType.DMA(())],
        compiler_params=_cparams("arbitrary", "arbitrary"),
        name="moba_attention",
    )(qT, k3, vT, km_pad, bias_w)


def _post_kernel(x_ref, mod_ref, h_ref, wz_ref, bz_ref, at_ref, u_ref, uh_ref, pw_ref, ps_ref,
                 wba_ref, wbp_ref, wo_ref, g2_ref, wrh_ref, wrl_ref, br_ref,
                 x1_ref, h2r_ref, route_ref, route_t_ref, counts_ref, carry_ref, *, tm, seq):
    d = D_MODEL
    i = pl.program_id(0)
    x = x_ref[...]
    mod = mod_ref[0]
    gate1 = mod[:, 2 * d:3 * d]
    shift2, scale2 = mod[:, 3 * d:4 * d], mod[:, 4 * d:5 * d]

    zg = 0.5 * jnp.tanh(0.5 * (_dot(h_ref[...], wz_ref[...]) + bz_ref[...])) + 0.5

    ya = jnp.concatenate(
        [lax.dot_general(at_ref[0, c], wba_ref[...], (((0,), (0,)), ((), ())),
                         preferred_element_type=F32) for c in range(tm // MOBA_BLOCK)], axis=0)

    first = (i * tm) % seq == 0
    halo = jnp.where(first, 0.0, uh_ref[...])
    ubuf = jnp.concatenate([halo, u_ref[...]], axis=0)
    rows = tm + POOL_HALO
    pos = (lax.broadcasted_iota(jnp.int32, (tm, POOL_GROUP_WIDTH), 0) + i * tm) % seq
    outs = []
    for gi, w in enumerate(POOL_WINDOWS):
        e = ubuf[:, gi * POOL_GROUP_WIDTH:(gi + 1) * POOL_GROUP_WIDTH]
        s = e
        step = 1
        while step < w:
            s = s + pltpu.roll(s, step, 0)
            step *= 2
        cnt = jnp.minimum(pos + 1, w).astype(F32)
        dlt = s[POOL_HALO:rows] / cnt - e[POOL_HALO:rows]
        outs.append(_dot(dlt.astype(BF16), pw_ref[gi]))
    pooled = jnp.concatenate(outs, axis=-1) * ps_ref[...]
    yp = _dot(pooled.astype(BF16), wbp_ref[...])

    merged = zg[:, 0:d] * ya + zg[:, d:2 * d] * yp
    x1 = x + gate1 * _dot(merged.astype(BF16), wo_ref[...])
    x1_ref[...] = x1

    h2 = _rms_mod(x1, g2_ref[...], scale2, shift2)
    h2_hi = h2.astype(BF16)
    h2r_ref[...] = h2_hi.astype(F32)
    h2_lo = (h2 - h2_hi.astype(F32)).astype(BF16)
    z = _dot(h2_hi, wrh_ref[...]) + _dot(h2_lo, wrh_ref[...]) + _dot(h2_hi, wrl_ref[...]) + br_ref[...]

    rr = ROUTER_ROWS
    zt = z.T[0:rr, :]
    row = lax.broadcasted_iota(jnp.int32, (rr, tm), 0)
    zgrp = jnp.where(row < N_GROUPS, zt, -jnp.inf)
    mg = jnp.max(zgrp, axis=0, keepdims=True)
    pg_top = 1.0 / jnp.sum(jnp.exp(zgrp - mg), axis=0, keepdims=True)
    g_idx = jnp.min(jnp.where(zgrp == mg, row, rr), axis=0, keepdims=True)
    e_row = row - ROUTER_LANE0
    in_grp = (e_row >= 0) & (e_row < N_EXPERTS) & ((e_row >> 3) == g_idx)
    ze = jnp.where(in_grp, zt, -jnp.inf)
    m1 = jnp.max(ze, axis=0, keepdims=True)
    i1 = jnp.min(jnp.where(ze == m1, row, rr), axis=0, keepdims=True)
    ze2 = jnp.where(row == i1, -jnp.inf, ze)
    m2 = jnp.max(ze2, axis=0, keepdims=True)
    i2 = jnp.min(jnp.where(ze2 == m2, row, rr), axis=0, keepdims=True)
    e2 = jnp.exp(m2 - m1)
    w1 = pg_top / (1.0 + e2)
    w2 = pg_top * e2 / (1.0 + e2)

    hot = jnp.where((row == i1) | (row == i2), 1.0, 0.0)
    tri = jnp.where(lax.broadcasted_iota(jnp.int32, (tm, tm), 0) <= lax.broadcasted_iota(jnp.int32, (tm, tm), 1),
                    1.0, 0.0).astype(BF16)
    prefix = _dot(hot.astype(BF16), tri)

    @pl.when(i == 0)
    def _():
        carry_ref[...] = jnp.zeros_like(carry_ref)

    carry = carry_ref[...]
    base = carry[:, 0:1] + prefix - 1.0
    rank1 = jnp.sum(jnp.where(row == i1, base, 0.0), axis=0, keepdims=True)
    rank2 = jnp.sum(jnp.where(row == i2, base, 0.0), axis=0, keepdims=True)
    carry_ref[...] = carry + prefix[:, tm - 1:tm]
    counts_ref[...] = carry_ref[...]
    fields = ((i1 - ROUTER_LANE0).astype(F32), (i2 - ROUTER_LANE0).astype(F32), w1, w2, rank1, rank2)
    frow = lax.broadcasted_iota(jnp.int32, (ROUTE_FIELDS, tm), 0)
    route_t = jnp.zeros((ROUTE_FIELDS, tm), F32)
    for k, val in enumerate(fields):
        route_t = jnp.where(frow == k, val, route_t)
    route_t_ref[...] = route_t
    route_ref[...] = jnp.concatenate([route_t, jnp.zeros((LANES - ROUTE_FIELDS, tm), F32)], axis=0).T


def _post_attention(x2, mod3, h1, w_z, b_z, attnT, u, pool_w, pool_scale, w_ba, w_bp, w_out, g2,
                    wr_hi, wr_lo, b_r, *, batch, seq):
    n, d = x2.shape
    tm = TM_POST
    tps = seq // tm
    cb = tm // MOBA_BLOCK
    hb = tm // POOL_HALO
    kern = functools.partial(_post_kernel, tm=tm, seq=seq)
    const2 = lambda i: (0, 0)
    return pl.pallas_call(
        kern,
        grid=(n // tm,),
        in_specs=[pl.BlockSpec((tm, d), lambda i: (i, 0)),
                  pl.BlockSpec((1, 1, N_MOD * d), lambda i: (i // tps, 0, 0)),
                  pl.BlockSpec((tm, d), lambda i: (i, 0)),
                  pl.BlockSpec((d, 2 * d), lambda i: (0, 1)),
                  pl.BlockSpec((1, 2 * d), const2),
                  pl.BlockSpec((1, cb, ATTN_WIDTH, MOBA_BLOCK), lambda i: (i // tps, i % tps, 0, 0)),
                  pl.BlockSpec((tm, POOL_WIDTH), lambda i: (i, 0)),
                  pl.BlockSpec((POOL_HALO, POOL_WIDTH), lambda i: (jnp.maximum(i * hb - 1, 0), 0)),
                  pl.BlockSpec((len(POOL_WINDOWS), POOL_GROUP_WIDTH, POOL_GROUP_WIDTH), lambda i: (0, 0, 0)),
                  pl.BlockSpec((1, POOL_WIDTH), const2),
                  pl.BlockSpec((ATTN_WIDTH, d), const2),
                  pl.BlockSpec((POOL_WIDTH, d), const2),
                  pl.BlockSpec((d, d), const2),
                  pl.BlockSpec((1, d), const2),
                  pl.BlockSpec((d, LANES), const2),
                  pl.BlockSpec((d, LANES), const2),
                  pl.BlockSpec((1, LANES), const2)],
        out_specs=[pl.BlockSpec((tm, d), lambda i: (i, 0)),
                   pl.BlockSpec((tm, d), lambda i: (i, 0)),
                   pl.BlockSpec((tm, LANES), lambda i: (i, 0)),
                   pl.BlockSpec((ROUTE_FIELDS, tm), lambda i: (0, i)),
                   pl.BlockSpec((ROUTER_ROWS, LANES), const2)],
        out_shape=[jax.ShapeDtypeStruct((n, d), F32),
                   jax.ShapeDtypeStruct((n, d), F32),
                   jax.ShapeDtypeStruct((n, LANES), F32),
                   jax.ShapeDtypeStruct((ROUTE_FIELDS, n), F32),
                   jax.ShapeDtypeStruct((ROUTER_ROWS, LANES), F32)],
        scratch_shapes=[pltpu.VMEM((ROUTER_ROWS, LANES), F32)],
        compiler_params=_cparams("arbitrary"),
        name="merge_outproj_router",
    )(x2, mod3, h1, w_z, b_z, attnT, u, u, pool_w, pool_scale, w_ba, w_bp, w_out, g2, wr_hi, wr_lo, b_r)


def _routing_plan(route_t, counts, *, n):
    te = TE_ROWS
    n_tiles = _expert_tiles(n)
    cnt = counts[ROUTER_LANE0:ROUTER_LANE0 + N_EXPERTS, 0].astype(jnp.int32)
    padded = ((cnt + te - 1) // te) * te
    ids = jnp.arange(N_EXPERTS, dtype=jnp.int32)
    ends = jnp.sum(jnp.where(ids[None, :] <= ids[:, None], padded[None, :], 0), axis=-1)
    starts = ends - padded
    pos = []
    for k in range(EXPERT_TOPK):
        expert = route_t[k].astype(jnp.int32)
        rank = route_t[4 + k].astype(jnp.int32)
        pos.append(jnp.sum(jnp.where(expert[:, None] == ids, starts, 0), axis=-1) + rank)
    tiles = jnp.arange(n_tiles, dtype=jnp.int32)
    tile_expert = jnp.sum((ends // te)[None, :] <= tiles[:, None], axis=-1)
    tile_expert = jnp.minimum(tile_expert, N_EXPERTS - 1).astype(jnp.int32)
    n_used = (ends[N_EXPERTS - 1:] // te).astype(jnp.int32)
    later = (ids[None, :] > ids[:, None]) & (cnt[None, :] > 0)
    next_expert = jnp.min(jnp.where(later, ids[None, :], N_EXPERTS), axis=-1)
    next_expert = jnp.where(next_expert == N_EXPERTS, -1, next_expert).astype(jnp.int32)
    return pos, tile_expert, n_used, next_expert


def _expert_tiles(n):
    return (EXPERT_TOPK * n) // TE_ROWS + N_EXPERTS


def _tile_pos(pos, tm):
    return jnp.concatenate([p.reshape(-1, 1, tm) for p in pos], axis=-1)


def _dispatch_kernel(pos_ref, h_ref, xz_ref, xs_ref, sem, *, tm):
    del xz_ref
    for r in range(tm):
        for k in range(EXPERT_TOPK):
            pltpu.make_async_copy(h_ref.at[pl.ds(r, 1)], xs_ref.at[pl.ds(pos_ref[0, 0, k * tm + r], 1)],
                                  sem).start(priority=(r + k) % 2)
    for k in range(EXPERT_TOPK):
        pltpu.make_async_copy(h_ref, h_ref, sem).wait()


def _dispatch(pos, h2r, xz, *, n):
    tm = TM_DISPATCH
    rows, d = xz.shape
    assert rows == _expert_tiles(n) * TE_ROWS
    pos3 = _tile_pos(pos, tm)
    return pl.pallas_call(
        functools.partial(_dispatch_kernel, tm=tm),
        grid=(n // tm,),
        in_specs=[pl.BlockSpec((1, 1, EXPERT_TOPK * tm), lambda i: (i, 0, 0), memory_space=pltpu.SMEM),
                  pl.BlockSpec((tm, d), lambda i: (i, 0)),
                  pl.BlockSpec(memory_space=pl.ANY)],
        out_specs=pl.BlockSpec(memory_space=pl.ANY),
        out_shape=jax.ShapeDtypeStruct((rows, d), F32),
        scratch_shapes=[pltpu.SemaphoreType.DMA(())],
        input_output_aliases={2: 0},
        compiler_params=_cparams("arbitrary"),
        name="moe_dispatch",
    )(pos3, h2r, xz)


def _experts_kernel(te_ref, nu_ref, nx_ref, xs_ref, wg_ref, wu_ref, wd_ref, y_ref,
                    wg_b0, wu_b0, wd_b0, wg_b1, wu_b1, wd_b1, wg_f, wu_f, wd_f, sems, turn_ref):
    t = pl.program_id(0)
    n_tiles = pl.num_programs(0)
    bf = ((wg_b0, wu_b0, wd_b0), (wg_b1, wu_b1, wd_b1))

    def fetch(e, slot):
        return [pltpu.make_async_copy(src.at[e], dst.at[slot], sems.at[slot])
                for src, dst in ((wg_ref, wg_f), (wu_ref, wu_f), (wd_ref, wd_f))]

    def cast(e, slot):
        for cp in fetch(e, slot):
            cp.wait()
        for dst, src in zip(bf[slot], (wg_f, wu_f, wd_f)):
            dst[...] = src[slot].astype(BF16)

    def prefetch(e, slot):
        @pl.when(e >= 0)
        def _():
            for cp in fetch(e, slot):
                cp.start()

    @pl.when(t < nu_ref[0])
    def _():
        e = te_ref[t]
        e_after = te_ref[jnp.minimum(t + 1, n_tiles - 1)]
        last = (t + 1 < nu_ref[0]) & (e_after != e)

        @pl.when(t == 0)
        def _():
            turn_ref[0] = 0
            for cp in fetch(e, 0):
                cp.start()
            cast(e, 0)
            prefetch(nx_ref[e], 1)

        def tile(slot, hand_over):
            def body():
                wg_b, wu_b, wd_b = bf[slot]
                if hand_over:
                    cast(e_after, 1 - slot)
                x = xs_ref[...].astype(BF16)
                g = _dot(x, wg_b[...])
                u = _dot(x, wu_b[...])
                act = (g * jax.nn.sigmoid(g)) * u
                y_ref[...] = _dot(act.astype(BF16), wd_b[...])
                if hand_over:
                    prefetch(nx_ref[e_after], slot)
                    turn_ref[0] = turn_ref[0] + 1
            return body

        turn = turn_ref[0] % 2
        for slot in range(2):
            @pl.when(turn == slot)
            def _(slot=slot):
                lax.cond(last, tile(slot, True), tile(slot, False))

    @pl.when(t >= nu_ref[0])
    def _():
        y_ref[...] = jnp.zeros_like(y_ref)


def _experts(tile_expert, n_used, next_expert, xs, wg, wu, wd):
    rows, d = xs.shape
    te = TE_ROWS
    hbm = pl.BlockSpec(memory_space=pl.ANY)
    grid_spec = pltpu.PrefetchScalarGridSpec(
        num_scalar_prefetch=3,
        grid=(rows // te,),
        in_specs=[pl.BlockSpec((te, d), lambda t, te_r, nu, nx: (jnp.minimum(t, nu[0] - 1), 0)),
                  hbm, hbm, hbm],
        out_specs=pl.BlockSpec((te, d), lambda t, te_r, nu, nx: (t, 0)),
        scratch_shapes=[pltpu.VMEM((d, D_EXPERT), BF16), pltpu.VMEM((d, D_EXPERT), BF16),
                        pltpu.VMEM((D_EXPERT, d), BF16),
                        pltpu.VMEM((d, D_EXPERT), BF16), pltpu.VMEM((d, D_EXPERT), BF16),
                        pltpu.VMEM((D_EXPERT, d), BF16),
                        pltpu.VMEM((2, d, D_EXPERT), F32), pltpu.VMEM((2, d, D_EXPERT), F32),
                        pltpu.VMEM((2, D_EXPERT, d), F32),
                        pltpu.SemaphoreType.DMA((2,)), pltpu.SMEM((1,), jnp.int32)])
    return pl.pallas_call(
        _experts_kernel,
        grid_spec=grid_spec,
        out_shape=jax.ShapeDtypeStruct((rows, d), F32),
        compiler_params=_cparams("arbitrary"),
        name="moe_experts",
    )(tile_expert, n_used, next_expert, xs, wg, wu, wd)


def _final_kernel(pos_ref, pos_next_ref, x1_ref, route_ref, mod_ref, g_ref, y_ref, o_ref, ybuf, sems, *,
                  tm, final_norm):
    d = D_MODEL
    i = pl.program_id(0)
    slot = i % 2

    def gather(p_ref, s):
        for r in range(tm):
            for k in range(EXPERT_TOPK):
                pltpu.make_async_copy(y_ref.at[pl.ds(p_ref[0, 0, k * tm + r], 1)],
                                      ybuf.at[s, k, pl.ds(r, 1)], sems.at[s]).start(priority=(r + k) % 2)

    @pl.when(i == 0)
    def _():
        gather(pos_ref, 0)

    for s in range(2):
        @pl.when((i + 1 < pl.num_programs(0)) & (slot == 1 - s))
        def _(s=s):
            gather(pos_next_ref, s)

    for k in range(EXPERT_TOPK):
        pltpu.make_async_copy(ybuf.at[slot, k], ybuf.at[slot, k], sems.at[slot]).wait()

    route = route_ref[...]
    lane = lax.broadcasted_iota(jnp.int32, route.shape, 1)
    y = jnp.zeros((tm, d), F32)
    for k in range(EXPERT_TOPK):
        wk = jnp.sum(jnp.where(lane == 2 + k, route, 0.0), axis=-1, keepdims=True)
        y = y + wk * ybuf[slot, k]
    gate2 = mod_ref[0][:, 5 * d:6 * d]
    x = x1_ref[...] + gate2 * y
    if final_norm:
        x = (x * lax.rsqrt(jnp.mean(x * x, axis=-1, keepdims=True) + EPS)) * g_ref[...]
    o_ref[...] = x


def _final(pos, x1, route, mod3, gf, y, *, seq, final_norm):
    n, d = x1.shape
    tm = TM_FINAL
    tps = seq // tm
    pos3 = _tile_pos(pos, tm)
    last = n // tm - 1
    return pl.pallas_call(
        functools.partial(_final_kernel, tm=tm, final_norm=final_norm),
        grid=(n // tm,),
        in_specs=[pl.BlockSpec((1, 1, EXPERT_TOPK * tm), lambda i: (i, 0, 0), memory_space=pltpu.SMEM),
                  pl.BlockSpec((1, 1, EXPERT_TOPK * tm), lambda i: (jnp.minimum(i + 1, last), 0, 0),
                               memory_space=pltpu.SMEM),
                  pl.BlockSpec((tm, d), lambda i: (i, 0)),
                  pl.BlockSpec((tm, LANES), lambda i: (i, 0)),
                  pl.BlockSpec((1, 1, N_MOD * d), lambda i: (i // tps, 0, 0)),
                  pl.BlockSpec((1, d), lambda i: (0, 0)),
                  pl.BlockSpec(memory_space=pl.ANY)],
        out_specs=pl.BlockSpec((tm, d), lambda i: (i, 0)),
        out_shape=jax.ShapeDtypeStruct((n, d), F32),
        scratch_shapes=[pltpu.VMEM((2, EXPERT_TOPK, tm, d), F32), pltpu.SemaphoreType.DMA((2,))],
        compiler_params=_cparams("arbitrary"),
        name="moe_combine_final_norm",
    )(pos3, pos3, x1, route, mod3, gf, y)


def _split_bf16(w):
    hi = w.astype(BF16)
    return hi, (w - hi.astype(F32)).astype(BF16)


def kernel(x, c, w_ada, b_ada, norm1_g, w_in, b_gate, rel_bias, pool_w, pool_scale, w_branch_attn,
           w_branch_pool, w_out, norm2_g, w_router_group, b_router_group, w_router_expert,
           b_router_expert, w_expert_gate, w_expert_up, w_expert_down, norm_f_g):
    batch, seq, d = x.shape
    depth = w_ada.shape[0]
    n = batch * seq
    nb = seq // MOBA_BLOCK
    qkvu = 3 * ATTN_WIDTH + POOL_WIDTH
    bias_w = _bias_vectors(rel_bias)
    c_pad = jnp.pad(c, ((0, 8 - batch), (0, 0)))

    xc = x.reshape(n, d)
    for l in range(depth):
        mod3 = _modulation(c_pad, w_ada[l], b_ada[l][None])[:batch].reshape(batch, 1, N_MOD * d)
        w_in_b = w_in[l].astype(BF16)
        g1 = norm1_g[l][None]

        assert w_in_b.shape[1] == 2 * qkvu == qkvu + 2 * d
        qT, k2, vT, u, km, h1 = _in_projection(xc, mod3, g1, w_in_b, batch=batch, seq=seq)
        km_pad = jnp.pad(km.reshape(batch, nb, ATTN_WIDTH), ((0, 0), (0, MAX_BLOCKS - nb), (0, 0)))
        attnT, xz = _moba_attention(qT, k2.reshape(batch, seq, ATTN_WIDTH), vT, km_pad, bias_w,
                                    batch=batch, seq=seq)

        w_r = jnp.concatenate([w_router_group[l], w_router_expert[l]], axis=1)
        w_r = jnp.pad(w_r, ((0, 0), (0, LANES - w_r.shape[1])))
        b_r = jnp.concatenate([b_router_group[l], b_router_expert[l]])
        b_r = jnp.pad(b_r, (0, LANES - b_r.shape[0]))[None]
        wr_hi, wr_lo = _split_bf16(w_r)
        x1, h2r, route, route_t, counts = _post_attention(
            xc, mod3, h1, w_in_b, b_gate[l][None], attnT, u, pool_w[l].astype(BF16),
            pool_scale[l][None], w_branch_attn[l].astype(BF16), w_branch_pool[l].astype(BF16),
            w_out[l].astype(BF16), norm2_g[l][None], wr_hi, wr_lo, b_r, batch=batch, seq=seq)

        pos, tile_expert, n_used, next_expert = _routing_plan(route_t, counts, n=n)
        xs = _dispatch(pos, h2r, xz, n=n)
        y = _experts(tile_expert, n_used, next_expert, xs, w_expert_gate[l], w_expert_up[l],
                     w_expert_down[l])
        xc = _final(pos, x1, route, mod3, norm_f_g[None], y, seq=seq, final_norm=(l == depth - 1))
    return xc.reshape(batch, seq, d)
```

```python
import functools
import math

import numpy as np
import jax
import jax.numpy as jnp
from jax import lax
from jax.experimental import pallas as pl
from jax.experimental.pallas import tpu as pltpu

F32 = jnp.float32
BF16 = jnp.bfloat16

D_MODEL = 1024
N_HEADS = 8
HEAD_DIM = 64
ATTN_WIDTH = N_HEADS * HEAD_DIM
MOBA_BLOCK = 256
MOBA_TOPK = 3
MAX_BLOCKS = 32
FAR_GROUP = 4
HEADS_PER_STEP = 4
ZERO_ROWS = 256
ONES_ROWS = 16
LOG2E = math.log2(math.e)
POOL_WINDOWS = (2, 4, 8, 16)
POOL_WIDTH = 512
POOL_GROUP_WIDTH = 128
POOL_HALO = 16
REL_BUCKETS = 32
REL_MAX_DIST = 128
N_GROUPS = 4
EXPERTS_PER_GROUP = 8
N_EXPERTS = N_GROUPS * EXPERTS_PER_GROUP
D_EXPERT = 512
N_MOD = 6
EPS = 1e-6
NEG_INF = -1e30
LANES = 128
ROUTER_LANE0 = N_GROUPS
ROUTE_FIELDS = 8
ROUTER_ROWS = 48
VMEM_LIMIT = 56 * 1024 * 1024

TM_PROJ = 1024
TM_POST = 1024
TM_DISPATCH = 2048
TE_ROWS = 512
X_RING = 3
EXPERT_TOPK = 2
TM_FINAL = 256


def _cparams(*sem):
    return pltpu.CompilerParams(dimension_semantics=sem, vmem_limit_bytes=VMEM_LIMIT)


def _dot(a, b):
    return jnp.dot(a, b, preferred_element_type=F32)


def _rms_mod(x, g, scale, shift):
    xn = x * lax.rsqrt(jnp.mean(x * x, axis=-1, keepdims=True) + EPS)
    return (xn * g) * (1.0 + scale) + shift


def _mod_kernel(c_ref, w_ref, b_ref, o_ref):
    c = c_ref[...]
    ca = c * jax.nn.sigmoid(c)
    o_ref[...] = _dot(ca.astype(BF16), w_ref[...].astype(BF16)) + b_ref[...]


def _modulation(c_pad, w_ada, b_ada):
    rows, d = c_pad.shape
    n_out = w_ada.shape[1]
    tn = 1024
    return pl.pallas_call(
        _mod_kernel,
        grid=(n_out // tn,),
        in_specs=[pl.BlockSpec((rows, d), lambda j: (0, 0)),
                  pl.BlockSpec((d, tn), lambda j: (0, j)),
                  pl.BlockSpec((1, tn), lambda j: (0, j))],
        out_specs=pl.BlockSpec((rows, tn), lambda j: (0, j)),
        out_shape=jax.ShapeDtypeStruct((rows, n_out), F32),
        compiler_params=_cparams("arbitrary"),
        name="adaln_mod",
    )(c_pad, w_ada, b_ada)


def _inproj_kernel(x_ref, mod_ref, g_ref, w_ref, qT_ref, k_ref, vT_ref, u_ref, km_ref, h_ref, *, tm):
    d = D_MODEL
    mod = mod_ref[0]
    h = _rms_mod(x_ref[...], g_ref[...], mod[:, d:2 * d], mod[:, 0:d]).astype(BF16)
    h_ref[...] = h
    proj = _dot(h, w_ref[...])
    aw = ATTN_WIDTH
    q = proj[:, 0:aw] * (HEAD_DIM ** -0.5 * LOG2E)
    k = proj[:, aw:2 * aw]
    v = proj[:, 2 * aw:3 * aw]
    u_ref[...] = proj[:, 3 * aw:3 * aw + POOL_WIDTH]
    k_ref[...] = k.astype(BF16)
    qT = q.T.astype(BF16)
    vT = v.T.astype(BF16)
    for c in range(tm // MOBA_BLOCK):
        sl = slice(c * MOBA_BLOCK, (c + 1) * MOBA_BLOCK)
        qT_ref[0, c] = qT[:, sl]
        vT_ref[0, c] = vT[:, sl]
        km_ref[0, c:c + 1, :] = jnp.mean(k[sl, :], axis=0, keepdims=True)


def _in_projection(x2, mod3, g1, w_qkvu, *, batch, seq):
    n, d = x2.shape
    tm = TM_PROJ
    tps = seq // tm
    nb = seq // MOBA_BLOCK
    cb = tm // MOBA_BLOCK
    wcols = 3 * ATTN_WIDTH + POOL_WIDTH
    assert w_qkvu.shape[1] % wcols == 0
    kern = functools.partial(_inproj_kernel, tm=tm)
    t_shape = jax.ShapeDtypeStruct((batch, nb, ATTN_WIDTH, MOBA_BLOCK), BF16)
    t_spec = pl.BlockSpec((1, cb, ATTN_WIDTH, MOBA_BLOCK), lambda i: (i // tps, i % tps, 0, 0))
    return pl.pallas_call(
        kern,
        grid=(n // tm,),
        in_specs=[pl.BlockSpec((tm, d), lambda i: (i, 0)),
                  pl.BlockSpec((1, 1, N_MOD * d), lambda i: (i // tps, 0, 0)),
                  pl.BlockSpec((1, d), lambda i: (0, 0)),
                  pl.BlockSpec((d, wcols), lambda i: (0, 0))],
        out_specs=[t_spec,
                   pl.BlockSpec((tm, ATTN_WIDTH), lambda i: (i, 0)),
                   t_spec,
                   pl.BlockSpec((tm, POOL_WIDTH), lambda i: (i, 0)),
                   pl.BlockSpec((1, cb, ATTN_WIDTH), lambda i: (i, 0, 0)),
                   pl.BlockSpec((tm, d), lambda i: (i, 0))],
        out_shape=[t_shape,
                   jax.ShapeDtypeStruct((n, ATTN_WIDTH), BF16),
                   t_shape,
                   jax.ShapeDtypeStruct((n, POOL_WIDTH), F32),
                   jax.ShapeDtypeStruct((n // tm, cb, ATTN_WIDTH), F32),
                   jax.ShapeDtypeStruct((n, d), BF16)],
        compiler_params=_cparams("arbitrary"),
        name="norm1_inproj",
    )(x2, mod3, g1, w_qkvu)


def _attn_kernel(qT_ref, k_hbm, v_hbm, km_ref, bw_ref, o_ref, xz_hbm, kaug_ref, vaug_ref, bias_ref, qnear_ref,
                 qfar_ref, sn_ref, sf0_ref, sf1_ref, k_ref, vT_ref, kv_sems, zero_ref, zero_sem, *, nb, n_steps):
    blk = MOBA_BLOCK
    hd = HEAD_DIM
    nh = HEADS_PER_STEP
    heads = range(nh)
    lane = lax.broadcasted_iota(jnp.int32, (blk, LANES), 1)

    def pair_lanes(hh):
        return slice((hh // 2) * LANES, (hh // 2 + 1) * LANES)

    def head_rows(hh):
        return slice(hh * hd, (hh + 1) * hd)

    ones = jnp.ones((ONES_ROWS, blk), BF16)

    width = nh * hd
    n_inner = pl.num_programs(1)
    step_id = pl.program_id(0) * n_inner + pl.program_id(1)

    def kv_copies(sid):
        bb, col = sid // n_inner, pl.multiple_of((sid % n_inner) * width, width)
        return (pltpu.make_async_copy(k_hbm.at[bb, :, pl.ds(col, width)], k_ref, kv_sems.at[0]),
                pltpu.make_async_copy(v_hbm.at[bb, :, pl.ds(col, width), :], vT_ref, kv_sems.at[1]))

    @pl.when(step_id == 0)
    def _():
        for cp in kv_copies(step_id):
            cp.start()

    for cp in kv_copies(step_id):
        cp.wait()

    def build_operands(j, carry):
        rows = pl.ds(pl.multiple_of(j * blk, blk), blk)
        for hh in heads:
            kp = k_ref[rows, pair_lanes(hh)]
            if hh % 2 == 0:
                kaug_ref[hh, rows, :] = jnp.where(lane < hd, kp, jnp.where(lane == hd + j, 1.0, 0.0).astype(BF16))
            else:
                kaug_ref[hh, rows, :] = jnp.where(lane >= hd, kp, jnp.where(lane == j, 1.0, 0.0).astype(BF16))
            vaug_ref[j, hh] = jnp.concatenate([vT_ref[j, head_rows(hh), :], ones], axis=0)
        return carry

    lax.fori_loop(0, nb, build_operands, 0)

    @pl.when(step_id + 1 < pl.num_programs(0) * n_inner)
    def _():
        for cp in kv_copies(step_id + 1):
            cp.start()

    @pl.when(step_id == 0)
    def _():
        zero_ref[...] = jnp.zeros_like(zero_ref)

    zrows = zero_ref.shape[0]
    zcount = xz_hbm.shape[0] // (n_steps * zrows)

    def zero_copy(q):
        return pltpu.make_async_copy(zero_ref, xz_hbm.at[pl.ds((step_id * zcount + q) * zrows, zrows)], zero_sem)

    def start_zero(q, carry):
        zero_copy(q).start()
        return carry

    def wait_zero(q, carry):
        zero_copy(q).wait()
        return carry

    lax.fori_loop(0, zcount, start_zero, 0)

    for hh in heads:
        for t in range(2):
            wide = jnp.broadcast_to(bw_ref[hh, t:t + 1, :], (blk, 2 * blk))
            bias_ref[hh, t] = pltpu.roll(wide, 0, 1, stride=1, stride_axis=0)[:, blk:2 * blk]

    rowi = lax.broadcasted_iota(jnp.int32, (MAX_BLOCKS, blk), 0)
    km_lane = lax.broadcasted_iota(jnp.int32, (MAX_BLOCKS, LANES), 1)
    zpad = jnp.zeros((LANES - hd - MAX_BLOCKS, blk), BF16)
    km_split = []
    for hh in heads:
        sub = hh % 2
        in_head = (km_lane >= sub * hd) & (km_lane < (sub + 1) * hd)
        km_h = jnp.where(in_head, km_ref[0, :, pair_lanes(hh)], 0.0)
        km_hi = km_h.astype(BF16)
        km_split.append((km_hi, (km_h - km_hi.astype(F32)).astype(BF16)))

    grp = FAR_GROUP

    def retire_begin(state, mx):
        m, acc = state
        m_new = jnp.maximum(m, mx)
        return m_new, jnp.exp2(m - m_new) * acc

    def retire_slice(src_ref, hh, m_new, acc, g, j):
        p = jnp.exp2(src_ref[hh, g * blk:(g + 1) * blk, :] - m_new)
        return acc + _dot(vaug_ref[j, hh], p.astype(BF16))

    def fresh():
        return (jnp.full((1, blk), -jnp.inf, F32), jnp.zeros((hd + ONES_ROWS, blk), F32))

    def near_blocks(i):
        return tuple(jnp.where(i - (grp - 1) + g < 0, i + 1, i - (grp - 1) + g) for g in range(grp))

    def finish(i, states):
        for hh in heads:
            _, acc = states[hh]
            o_ref[0, i, head_rows(hh), :] = (acc[0:hd] / acc[hd:hd + 1]).astype(BF16)

    def select(i):
        for hh in heads:
            qp = qT_ref[0, i, pair_lanes(hh), :]
            km_hi, km_lo = km_split[hh]
            gate = _dot(km_hi, qp) + _dot(km_lo, qp)
            g = jnp.where(rowi < i, gate, -jnp.inf)
            sel = rowi == i
            for _ in range(MOBA_TOPK):
                m = jnp.max(g, axis=0, keepdims=True)
                hit = (g == m) & (m > -jnp.inf)
                idx = jnp.min(jnp.where(hit, rowi, MAX_BLOCKS), axis=0, keepdims=True)
                pick = rowi == idx
                sel = sel | pick
                g = jnp.where(pick, -jnp.inf, g)
            near = jnp.where(sel, 0.0, NEG_INF)
            far = jnp.where(rowi > i - grp, NEG_INF, near)
            qh = qp[head_rows(hh % 2), :]
            if hh % 2 == 0:
                qnear_ref[hh] = jnp.concatenate([qh, near.astype(BF16), zpad], axis=0)
                qfar_ref[hh] = jnp.concatenate([qh, far.astype(BF16), zpad], axis=0)
            else:
                qnear_ref[hh] = jnp.concatenate([near.astype(BF16), zpad, qh], axis=0)
                qfar_ref[hh] = jnp.concatenate([far.astype(BF16), zpad, qh], axis=0)

    def park(dst_ref, q_ref, hh, g, j, bias=None):
        rows = pl.ds(pl.multiple_of(j * blk, blk), blk)
        s = _dot(kaug_ref[hh, rows, :], q_ref[hh])
        if bias is not None:
            s = s + bias
        dst_ref[hh, g * blk:(g + 1) * blk, :] = s
        return jnp.max(s, axis=0, keepdims=True)

    def park_near(i):
        idx = near_blocks(i)

        def quarter(g, hh):
            bias = None
            if g == grp - 1:
                bias = bias_ref[hh, 0]
            elif g == grp - 2:
                bias = bias_ref[hh, 1]
            return park(sn_ref, qnear_ref, hh, g, idx[g], bias)
        return quarter

    def park_far(c, dst_ref):
        return lambda g, hh: park(dst_ref, qfar_ref, hh, g, c * grp + g)

    def chunk_blocks(c):
        return tuple(c * grp + g for g in range(grp))

    def step(src_ref, states, maxes, vidx, quarter, park_first):
        begun = [retire_begin(states[hh], maxes[hh]) for hh in heads] if states is not None else None
        accs = [b[1] for b in begun] if begun else None
        new_max = [None] * nh

        def park_quarter(g):
            for hh in heads:
                cm = quarter(g, hh)
                new_max[hh] = cm if new_max[hh] is None else jnp.maximum(new_max[hh], cm)

        for g in range(grp):
            if park_first:
                park_quarter(g)
            if begun:
                for hh in heads:
                    accs[hh] = retire_slice(src_ref, hh, begun[hh][0], accs[hh], g, vidx[g])
            if not park_first and quarter is not None:
                park_quarter(g)
        new_states = tuple((begun[hh][0], accs[hh]) for hh in heads) if begun else None
        return new_states, tuple(new_max)

    def fresh_states():
        return tuple(fresh() for _ in heads)

    def early_tile(i, near_max):
        nxt = jnp.minimum(i + 1, nb - 1)
        select(nxt)
        states, next_max = step(sn_ref, fresh_states(), near_max, near_blocks(i), park_near(nxt), False)
        finish(i, states)
        return next_max

    def late_tile(i, near_max):
        sf = (sf0_ref, sf1_ref)
        n_chunks = (i - (grp - 1) + grp - 1) // grp
        nxt = jnp.minimum(i + 1, nb - 1)
        states, far_max = step(sn_ref, fresh_states(), near_max, near_blocks(i), park_far(0, sf[0]), True)

        def pair(k, st):
            states, far_max = st
            states, far_max = step(sf[0], states, far_max, chunk_blocks(2 * k), park_far(2 * k + 1, sf[1]), True)
            return step(sf[1], states, far_max, chunk_blocks(2 * k + 1), park_far(2 * k + 2, sf[0]), True)

        states, far_max = lax.fori_loop(0, (n_chunks - 1) // 2, pair, (states, far_max))

        def last(src, states, far_max):
            select(nxt)
            states, next_max = step(src, states, far_max, chunk_blocks(n_chunks - 1), park_near(nxt), True)
            finish(i, states)
            return next_max

        def even_count(st):
            states, far_max = step(sf[0], st[0], st[1], chunk_blocks(n_chunks - 2),
                                   park_far(n_chunks - 1, sf[1]), True)
            return last(sf[1], states, far_max)

        def odd_count(st):
            return last(sf[0], st[0], st[1])

        return lax.cond(n_chunks % 2 == 0, even_count, odd_count, (states, far_max))

    n_early = min(grp, nb)
    select(0)
    _, near_max = step(None, None, None, None, park_near(0), True)
    near_max = lax.fori_loop(0, n_early, early_tile, near_max)
    lax.fori_loop(n_early, nb, late_tile, near_max)
    lax.fori_loop(0, zcount, wait_zero, 0)


def _rel_bucket_table(max_rel):
    n = np.arange(max_rel)
    max_exact = REL_BUCKETS // 2
    nf = np.maximum(n, 1).astype(np.float32)
    large = max_exact + (np.log(nf / np.float32(max_exact)) / np.float32(math.log(REL_MAX_DIST / max_exact))
                         * np.float32(REL_BUCKETS - max_exact)).astype(np.int32)
    large = np.minimum(large, REL_BUCKETS - 1)
    return np.where(n < max_exact, n, large)


def _bias_vectors(rel_bias):
    blk = MOBA_BLOCK
    table = _rel_bucket_table(2 * blk)
    r = np.arange(-blk, blk)
    bt = (rel_bias.T - rel_bias[REL_BUCKETS - 1][:, None]) * LOG2E
    own = jnp.where(jnp.asarray(r >= 0)[None], bt[:, table[np.maximum(r, 0)]], NEG_INF)
    prev = bt[:, table[np.minimum(r + blk, 2 * blk - 1)]]
    return jnp.stack([own, prev], axis=1).astype(F32)


def _moba_attention(qT, k3, vT, km_pad, bias_w, *, batch, seq):
    nb = seq // MOBA_BLOCK
    blk = MOBA_BLOCK
    assert nb % FAR_GROUP == 0 and nb <= MAX_BLOCKS
    nh = HEADS_PER_STEP
    width = nh * HEAD_DIM
    assert nh % 2 == 0 and N_HEADS % nh == 0
    n_steps = batch * (N_HEADS // nh)
    sorted_rows = _expert_tiles(batch * seq) * TE_ROWS
    assert sorted_rows % (n_steps * ZERO_ROWS) == 0
    kern = functools.partial(_attn_kernel, nb=nb, n_steps=n_steps)

    hbm = pl.BlockSpec(memory_space=pl.ANY)
    return pl.pallas_call(
        kern,
        grid=(batch, N_HEADS // nh),
        in_specs=[pl.BlockSpec((1, nb, width, blk), lambda b, p: (b, 0, p, 0)),
                  hbm, hbm,
                  pl.BlockSpec((1, MAX_BLOCKS, width), lambda b, p: (b, 0, p)),
                  pl.BlockSpec((nh, 2, 2 * blk), lambda b, p: (p, 0, 0))],
        out_specs=[pl.BlockSpec((1, nb, width, blk), lambda b, p: (b, 0, p, 0)), hbm],
        out_shape=[jax.ShapeDtypeStruct((batch, nb, ATTN_WIDTH, blk), BF16),
                   jax.ShapeDtypeStruct((sorted_rows, D_MODEL), F32)],
        scratch_shapes=[pltpu.VMEM((nh, seq, LANES), BF16),
                        pltpu.VMEM((nb, nh, HEAD_DIM + ONES_ROWS, blk), BF16),
                        pltpu.VMEM((nh, 2, blk, blk), F32),
                        pltpu.VMEM((nh, LANES, blk), BF16),
                        pltpu.VMEM((nh, LANES, blk), BF16),
                        pltpu.VMEM((nh, FAR_GROUP * blk, blk), F32),
                        pltpu.VMEM((nh, FAR_GROUP * blk, blk), F32),
                        pltpu.VMEM((nh, FAR_GROUP * blk, blk), F32),
                        pltpu.VMEM((seq, width), BF16),
                        pltpu.VMEM((nb, width, blk), BF16),
                        pltpu.SemaphoreType.DMA((2,)),
                        pltpu.VMEM((ZERO_ROWS, D_MODEL), F32),
                        pltpu.SemaphoreType.DMA(())],
        compiler_params=_cparams("arbitrary", "arbitrary"),
        name="moba_attention",
    )(qT, k3, vT, km_pad, bias_w)


def _post_kernel(x_ref, mod_ref, h_ref, wz_ref, bz_ref, at_ref, u_ref, uh_ref, pw_ref, ps_ref,
                 wba_ref, wbp_ref, wo_ref, g2_ref, wrh_ref, wrl_ref, br_ref,
                 x1_ref, h2r_ref, route_ref, route_t_ref, counts_ref, carry_ref, *, tm, seq):
    d = D_MODEL
    i = pl.program_id(0)
    x = x_ref[...]
    mod = mod_ref[0]
    gate1 = mod[:, 2 * d:3 * d]
    shift2, scale2 = mod[:, 3 * d:4 * d], mod[:, 4 * d:5 * d]

    zg = 0.5 * jnp.tanh(0.5 * (_dot(h_ref[...], wz_ref[...]) + bz_ref[...])) + 0.5

    ya = jnp.concatenate(
        [lax.dot_general(at_ref[0, c], wba_ref[...], (((0,), (0,)), ((), ())),
                         preferred_element_type=F32) for c in range(tm // MOBA_BLOCK)], axis=0)

    first = (i * tm) % seq == 0
    halo = jnp.where(first, 0.0, uh_ref[...])
    ubuf = jnp.concatenate([halo, u_ref[...]], axis=0)
    rows = tm + POOL_HALO
    pos = (lax.broadcasted_iota(jnp.int32, (tm, POOL_GROUP_WIDTH), 0) + i * tm) % seq
    outs = []
    for gi, w in enumerate(POOL_WINDOWS):
        e = ubuf[:, gi * POOL_GROUP_WIDTH:(gi + 1) * POOL_GROUP_WIDTH]
        s = e
        step = 1
        while step < w:
            s = s + pltpu.roll(s, step, 0)
            step *= 2
        cnt = jnp.minimum(pos + 1, w).astype(F32)
        dlt = s[POOL_HALO:rows] / cnt - e[POOL_HALO:rows]
        outs.append(_dot(dlt.astype(BF16), pw_ref[gi]))
    pooled = jnp.concatenate(outs, axis=-1) * ps_ref[...]
    yp = _dot(pooled.astype(BF16), wbp_ref[...])

    merged = zg[:, 0:d] * ya + zg[:, d:2 * d] * yp
    x1 = x + gate1 * _dot(merged.astype(BF16), wo_ref[...])
    x1_ref[...] = x1

    h2 = _rms_mod(x1, g2_ref[...], scale2, shift2)
    h2_hi = h2.astype(BF16)
    h2r_ref[...] = h2_hi.astype(F32)
    h2_lo = (h2 - h2_hi.astype(F32)).astype(BF16)
    z = _dot(h2_hi, wrh_ref[...]) + _dot(h2_lo, wrh_ref[...]) + _dot(h2_hi, wrl_ref[...]) + br_ref[...]

    rr = ROUTER_ROWS
    zt = z.T[0:rr, :]
    row = lax.broadcasted_iota(jnp.int32, (rr, tm), 0)
    zgrp = jnp.where(row < N_GROUPS, zt, -jnp.inf)
    mg = jnp.max(zgrp, axis=0, keepdims=True)
    pg_top = 1.0 / jnp.sum(jnp.exp(zgrp - mg), axis=0, keepdims=True)
    g_idx = jnp.min(jnp.where(zgrp == mg, row, rr), axis=0, keepdims=True)
    e_row = row - ROUTER_LANE0
    in_grp = (e_row >= 0) & (e_row < N_EXPERTS) & ((e_row >> 3) == g_idx)
    ze = jnp.where(in_grp, zt, -jnp.inf)
    m1 = jnp.max(ze, axis=0, keepdims=True)
    i1 = jnp.min(jnp.where(ze == m1, row, rr), axis=0, keepdims=True)
    ze2 = jnp.where(row == i1, -jnp.inf, ze)
    m2 = jnp.max(ze2, axis=0, keepdims=True)
    i2 = jnp.min(jnp.where(ze2 == m2, row, rr), axis=0, keepdims=True)
    e2 = jnp.exp(m2 - m1)
    w1 = pg_top / (1.0 + e2)
    w2 = pg_top * e2 / (1.0 + e2)

    hot = jnp.where((row == i1) | (row == i2), 1.0, 0.0)
    tri = jnp.where(lax.broadcasted_iota(jnp.int32, (tm, tm), 0) <= lax.broadcasted_iota(jnp.int32, (tm, tm), 1),
                    1.0, 0.0).astype(BF16)
    prefix = _dot(hot.astype(BF16), tri)

    @pl.when(i == 0)
    def _():
        carry_ref[...] = jnp.zeros_like(carry_ref)

    carry = carry_ref[...]
    base = carry[:, 0:1] + prefix - 1.0
    rank1 = jnp.sum(jnp.where(row == i1, base, 0.0), axis=0, keepdims=True)
    rank2 = jnp.sum(jnp.where(row == i2, base, 0.0), axis=0, keepdims=True)
    carry_ref[...] = carry + prefix[:, tm - 1:tm]
    counts_ref[...] = carry_ref[...]
    fields = ((i1 - ROUTER_LANE0).astype(F32), (i2 - ROUTER_LANE0).astype(F32), w1, w2, rank1, rank2)
    frow = lax.broadcasted_iota(jnp.int32, (ROUTE_FIELDS, tm), 0)
    route_t = jnp.zeros((ROUTE_FIELDS, tm), F32)
    for k, val in enumerate(fields):
        route_t = jnp.where(frow == k, val, route_t)
    route_t_ref[...] = route_t
    route_ref[...] = jnp.concatenate([route_t, jnp.zeros((LANES - ROUTE_FIELDS, tm), F32)], axis=0).T


def _post_attention(x2, mod3, h1, w_z, b_z, attnT, u, pool_w, pool_scale, w_ba, w_bp, w_out, g2,
                    wr_hi, wr_lo, b_r, *, batch, seq):
    n, d = x2.shape
    tm = TM_POST
    tps = seq // tm
    cb = tm // MOBA_BLOCK
    hb = tm // POOL_HALO
    kern = functools.partial(_post_kernel, tm=tm, seq=seq)
    const2 = lambda i: (0, 0)
    return pl.pallas_call(
        kern,
        grid=(n // tm,),
        in_specs=[pl.BlockSpec((tm, d), lambda i: (i, 0)),
                  pl.BlockSpec((1, 1, N_MOD * d), lambda i: (i // tps, 0, 0)),
                  pl.BlockSpec((tm, d), lambda i: (i, 0)),
                  pl.BlockSpec((d, 2 * d), lambda i: (0, 1)),
                  pl.BlockSpec((1, 2 * d), const2),
                  pl.BlockSpec((1, cb, ATTN_WIDTH, MOBA_BLOCK), lambda i: (i // tps, i % tps, 0, 0)),
                  pl.BlockSpec((tm, POOL_WIDTH), lambda i: (i, 0)),
                  pl.BlockSpec((POOL_HALO, POOL_WIDTH), lambda i: (jnp.maximum(i * hb - 1, 0), 0)),
                  pl.BlockSpec((len(POOL_WINDOWS), POOL_GROUP_WIDTH, POOL_GROUP_WIDTH), lambda i: (0, 0, 0)),
                  pl.BlockSpec((1, POOL_WIDTH), const2),
                  pl.BlockSpec((ATTN_WIDTH, d), const2),
                  pl.BlockSpec((POOL_WIDTH, d), const2),
                  pl.BlockSpec((d, d), const2),
                  pl.BlockSpec((1, d), const2),
                  pl.BlockSpec((d, LANES), const2),
                  pl.BlockSpec((d, LANES), const2),
                  pl.BlockSpec((1, LANES), const2)],
        out_specs=[pl.BlockSpec((tm, d), lambda i: (i, 0)),
                   pl.BlockSpec((tm, d), lambda i: (i, 0)),
                   pl.BlockSpec((tm, LANES), lambda i: (i, 0)),
                   pl.BlockSpec((ROUTE_FIELDS, tm), lambda i: (0, i)),
                   pl.BlockSpec((ROUTER_ROWS, LANES), const2)],
        out_shape=[jax.ShapeDtypeStruct((n, d), F32),
                   jax.ShapeDtypeStruct((n, d), F32),
                   jax.ShapeDtypeStruct((n, LANES), F32),
                   jax.ShapeDtypeStruct((ROUTE_FIELDS, n), F32),
                   jax.ShapeDtypeStruct((ROUTER_ROWS, LANES), F32)],
        scratch_shapes=[pltpu.VMEM((ROUTER_ROWS, LANES), F32)],
        compiler_params=_cparams("arbitrary"),
        name="merge_outproj_router",
    )(x2, mod3, h1, w_z, b_z, attnT, u, u, pool_w, pool_scale, w_ba, w_bp, w_out, g2, wr_hi, wr_lo, b_r)


def _routing_plan(route_t, counts, *, n):
    te = TE_ROWS
    n_tiles = _expert_tiles(n)
    cnt = counts[ROUTER_LANE0:ROUTER_LANE0 + N_EXPERTS, 0].astype(jnp.int32)
    padded = ((cnt + te - 1) // te) * te
    ids = jnp.arange(N_EXPERTS, dtype=jnp.int32)
    ends = jnp.sum(jnp.where(ids[None, :] <= ids[:, None], padded[None, :], 0), axis=-1)
    starts = ends - padded
    pos = []
    for k in range(EXPERT_TOPK):
        expert = route_t[k].astype(jnp.int32)
        rank = route_t[4 + k].astype(jnp.int32)
        pos.append(jnp.sum(jnp.where(expert[:, None] == ids, starts, 0), axis=-1) + rank)
    tiles = jnp.arange(n_tiles, dtype=jnp.int32)
    tile_expert = jnp.sum((ends // te)[None, :] <= tiles[:, None], axis=-1)
    tile_expert = jnp.minimum(tile_expert, N_EXPERTS - 1).astype(jnp.int32)
    n_used = (ends[N_EXPERTS - 1:] // te).astype(jnp.int32)
    later = (ids[None, :] > ids[:, None]) & (cnt[None, :] > 0)
    next_expert = jnp.min(jnp.where(later, ids[None, :], N_EXPERTS), axis=-1)
    next_expert = jnp.where(next_expert == N_EXPERTS, -1, next_expert).astype(jnp.int32)
    return pos, tile_expert, n_used, next_expert


def _expert_tiles(n):
    return (EXPERT_TOPK * n) // TE_ROWS + N_EXPERTS


def _tile_pos(pos, tm):
    return jnp.concatenate([p.reshape(-1, 1, tm) for p in pos], axis=-1)


def _dispatch_kernel(pos_ref, h_ref, xz_ref, xs_ref, sem, *, tm):
    del xz_ref
    for r in range(tm):
        for k in range(EXPERT_TOPK):
            pltpu.make_async_copy(h_ref.at[pl.ds(r, 1)], xs_ref.at[pl.ds(pos_ref[0, 0, k * tm + r], 1)],
                                  sem).start(priority=(r + k) % 2)
    for k in range(EXPERT_TOPK):
        pltpu.make_async_copy(h_ref, h_ref, sem).wait()


def _dispatch(pos, h2r, xz, *, n):
    tm = TM_DISPATCH
    rows, d = xz.shape
    assert rows == _expert_tiles(n) * TE_ROWS
    pos3 = _tile_pos(pos, tm)
    return pl.pallas_call(
        functools.partial(_dispatch_kernel, tm=tm),
        grid=(n // tm,),
        in_specs=[pl.BlockSpec((1, 1, EXPERT_TOPK * tm), lambda i: (i, 0, 0), memory_space=pltpu.SMEM),
                  pl.BlockSpec((tm, d), lambda i: (i, 0)),
                  pl.BlockSpec(memory_space=pl.ANY)],
        out_specs=pl.BlockSpec(memory_space=pl.ANY),
        out_shape=jax.ShapeDtypeStruct((rows, d), F32),
        scratch_shapes=[pltpu.SemaphoreType.DMA(())],
        input_output_aliases={2: 0},
        compiler_params=_cparams("arbitrary"),
        name="moe_dispatch",
    )(pos3, h2r, xz)


def _experts_kernel(te_ref, nu_ref, nx_ref, xs_ref, wg_ref, wu_ref, wd_ref, y_ref,
                    wg_b, wu_b, wd_b, wg_f, wu_f, wd_f, sems, turn_ref, xbuf, xsems):
    t = pl.program_id(0)
    n_used = nu_ref[0]

    def fetch(e, slot):
        return [pltpu.make_async_copy(src.at[e], dst.at[slot], sems.at[slot])
                for src, dst in ((wg_ref, wg_f), (wu_ref, wu_f), (wd_ref, wd_f))]

    def rows_copy(tile, slot):
        r0 = pl.multiple_of(tile * TE_ROWS, TE_ROWS)
        return pltpu.make_async_copy(xs_ref.at[pl.ds(r0, TE_ROWS)], xbuf.at[slot], xsems.at[slot])

    @pl.when(t < n_used)
    def _():
        e = te_ref[t]
        e_prev = te_ref[jnp.maximum(t - 1, 0)]

        @pl.when(t == 0)
        def _():
            turn_ref[0] = 0
            for cp in fetch(e, 0):
                cp.start()
            for j in range(X_RING - 1):
                @pl.when(j < n_used)
                def _(j=j):
                    rows_copy(j, j).start()

        ahead = t + (X_RING - 1)

        @pl.when(ahead < n_used)
        def _():
            rows_copy(ahead, ahead % X_RING).start()

        xslot = t % X_RING
        rows_copy(t, xslot).wait()

        @pl.when((t == 0) | (e != e_prev))
        def _():
            slot = turn_ref[0] % 2
            for cp in fetch(e, slot):
                cp.wait()
            wg_b[...] = wg_f[slot].astype(BF16)
            wu_b[...] = wu_f[slot].astype(BF16)
            wd_b[...] = wd_f[slot].astype(BF16)
            e_next = nx_ref[e]

            @pl.when(e_next >= 0)
            def _():
                for cp in fetch(e_next, 1 - slot):
                    cp.start()

            turn_ref[0] = turn_ref[0] + 1

        x = xbuf[xslot].astype(BF16)
        g = _dot(x, wg_b[...])
        u = _dot(x, wu_b[...])
        act = (g * jax.nn.sigmoid(g)) * u
        y_ref[...] = _dot(act.astype(BF16), wd_b[...])

    @pl.when(t >= nu_ref[0])
    def _():
        y_ref[...] = jnp.zeros_like(y_ref)


def _experts(tile_expert, n_used, next_expert, xs, wg, wu, wd):
    rows, d = xs.shape
    te = TE_ROWS
    hbm = pl.BlockSpec(memory_space=pl.ANY)
    grid_spec = pltpu.PrefetchScalarGridSpec(
        num_scalar_prefetch=3,
        grid=(rows // te,),
        in_specs=[hbm, hbm, hbm, hbm],
        out_specs=pl.BlockSpec((te, d), lambda t, te_r, nu, nx: (t, 0)),
        scratch_shapes=[pltpu.VMEM((d, D_EXPERT), BF16), pltpu.VMEM((d, D_EXPERT), BF16),
                        pltpu.VMEM((D_EXPERT, d), BF16),
                        pltpu.VMEM((2, d, D_EXPERT), F32), pltpu.VMEM((2, d, D_EXPERT), F32),
                        pltpu.VMEM((2, D_EXPERT, d), F32),
                        pltpu.SemaphoreType.DMA((2,)), pltpu.SMEM((1,), jnp.int32),
                        pltpu.VMEM((X_RING, te, d), F32), pltpu.SemaphoreType.DMA((X_RING,))])
    return pl.pallas_call(
        _experts_kernel,
        grid_spec=grid_spec,
        out_shape=jax.ShapeDtypeStruct((rows, d), F32),
        compiler_params=_cparams("arbitrary"),
        name="moe_experts",
    )(tile_expert, n_used, next_expert, xs, wg, wu, wd)


def _final_kernel(pos_ref, pos_next_ref, x1_ref, route_ref, mod_ref, g_ref, y_ref, o_ref, ybuf, sems, *,
                  tm, final_norm):
    d = D_MODEL
    i = pl.program_id(0)
    slot = i % 2

    def gather(p_ref, s):
        for r in range(tm):
            for k in range(EXPERT_TOPK):
                pltpu.make_async_copy(y_ref.at[pl.ds(p_ref[0, 0, k * tm + r], 1)],
                                      ybuf.at[s, k, pl.ds(r, 1)], sems.at[s]).start(priority=(r + k) % 2)

    @pl.when(i == 0)
    def _():
        gather(pos_ref, 0)

    for s in range(2):
        @pl.when((i + 1 < pl.num_programs(0)) & (slot == 1 - s))
        def _(s=s):
            gather(pos_next_ref, s)

    for k in range(EXPERT_TOPK):
        pltpu.make_async_copy(ybuf.at[slot, k], ybuf.at[slot, k], sems.at[slot]).wait()

    route = route_ref[...]
    lane = lax.broadcasted_iota(jnp.int32, route.shape, 1)
    y = jnp.zeros((tm, d), F32)
    for k in range(EXPERT_TOPK):
        wk = jnp.sum(jnp.where(lane == 2 + k, route, 0.0), axis=-1, keepdims=True)
        y = y + wk * ybuf[slot, k]
    gate2 = mod_ref[0][:, 5 * d:6 * d]
    x = x1_ref[...] + gate2 * y
    if final_norm:
        x = (x * lax.rsqrt(jnp.mean(x * x, axis=-1, keepdims=True) + EPS)) * g_ref[...]
    o_ref[...] = x


def _final(pos, x1, route, mod3, gf, y, *, seq, final_norm):
    n, d = x1.shape
    tm = TM_FINAL
    tps = seq // tm
    pos3 = _tile_pos(pos, tm)
    last = n // tm - 1
    return pl.pallas_call(
        functools.partial(_final_kernel, tm=tm, final_norm=final_norm),
        grid=(n // tm,),
        in_specs=[pl.BlockSpec((1, 1, EXPERT_TOPK * tm), lambda i: (i, 0, 0), memory_space=pltpu.SMEM),
                  pl.BlockSpec((1, 1, EXPERT_TOPK * tm), lambda i: (jnp.minimum(i + 1, last), 0, 0),
                               memory_space=pltpu.SMEM),
                  pl.BlockSpec((tm, d), lambda i: (i, 0)),
                  pl.BlockSpec((tm, LANES), lambda i: (i, 0)),
                  pl.BlockSpec((1, 1, N_MOD * d), lambda i: (i // tps, 0, 0)),
                  pl.BlockSpec((1, d), lambda i: (0, 0)),
                  pl.BlockSpec(memory_space=pl.ANY)],
        out_specs=pl.BlockSpec((tm, d), lambda i: (i, 0)),
        out_shape=jax.ShapeDtypeStruct((n, d), F32),
        scratch_shapes=[pltpu.VMEM((2, EXPERT_TOPK, tm, d), F32), pltpu.SemaphoreType.DMA((2,))],
        compiler_params=_cparams("arbitrary"),
        name="moe_combine_final_norm",
    )(pos3, pos3, x1, route, mod3, gf, y)


def _split_bf16(w):
    hi = w.astype(BF16)
    return hi, (w - hi.astype(F32)).astype(BF16)


def kernel(x, c, w_ada, b_ada, norm1_g, w_in, b_gate, rel_bias, pool_w, pool_scale, w_branch_attn,
           w_branch_pool, w_out, norm2_g, w_router_group, b_router_group, w_router_expert,
           b_router_expert, w_expert_gate, w_expert_up, w_expert_down, norm_f_g):
    batch, seq, d = x.shape
    depth = w_ada.shape[0]
    n = batch * seq
    nb = seq // MOBA_BLOCK
    qkvu = 3 * ATTN_WIDTH + POOL_WIDTH
    bias_w = _bias_vectors(rel_bias)
    c_pad = jnp.pad(c, ((0, 8 - batch), (0, 0)))

    xc = x.reshape(n, d)
    for l in range(depth):
        mod3 = _modulation(c_pad, w_ada[l], b_ada[l][None])[:batch].reshape(batch, 1, N_MOD * d)
        w_in_b = w_in[l].astype(BF16)
        g1 = norm1_g[l][None]

        assert w_in_b.shape[1] == 2 * qkvu == qkvu + 2 * d
        qT, k2, vT, u, km, h1 = _in_projection(xc, mod3, g1, w_in_b, batch=batch, seq=seq)
        km_pad = jnp.pad(km.reshape(batch, nb, ATTN_WIDTH), ((0, 0), (0, MAX_BLOCKS - nb), (0, 0)))
        attnT, xz = _moba_attention(qT, k2.reshape(batch, seq, ATTN_WIDTH), vT, km_pad, bias_w,
                                    batch=batch, seq=seq)

        w_r = jnp.concatenate([w_router_group[l], w_router_expert[l]], axis=1)
        w_r = jnp.pad(w_r, ((0, 0), (0, LANES - w_r.shape[1])))
        b_r = jnp.concatenate([b_router_group[l], b_router_expert[l]])
        b_r = jnp.pad(b_r, (0, LANES - b_r.shape[0]))[None]
        wr_hi, wr_lo = _split_bf16(w_r)
        x1, h2r, route, route_t, counts = _post_attention(
            xc, mod3, h1, w_in_b, b_gate[l][None], attnT, u, pool_w[l].astype(BF16),
            pool_scale[l][None], w_branch_attn[l].astype(BF16), w_branch_pool[l].astype(BF16),
            w_out[l].astype(BF16), norm2_g[l][None], wr_hi, wr_lo, b_r, batch=batch, seq=seq)

        pos, tile_expert, n_used, next_expert = _routing_plan(route_t, counts, n=n)
        xs = _dispatch(pos, h2r, xz, n=n)
        y = _experts(tile_expert, n_used, next_expert, xs, w_expert_gate[l], w_expert_up[l],
                     w_expert_down[l])
        xc = _final(pos, x1, route, mod3, norm_f_g[None], y, seq=seq, final_norm=(l == depth - 1))
    return xc.reshape(batch, seq, d)
```

```python
import functools
import math

import numpy as np
import jax
import jax.numpy as jnp
from jax import lax
from jax.experimental import pallas as pl
from jax.experimental.pallas import tpu as pltpu

F32 = jnp.float32
BF16 = jnp.bfloat16

D_MODEL = 1024
N_HEADS = 8
HEAD_DIM = 64
ATTN_WIDTH = N_HEADS * HEAD_DIM
MOBA_BLOCK = 256
MOBA_TOPK = 3
MAX_BLOCKS = 32
FAR_GROUP = 4
HEADS_PER_STEP = 4
ZERO_ROWS = 256
ONES_ROWS = 16
LOG2E = math.log2(math.e)
POOL_WINDOWS = (2, 4, 8, 16)
POOL_WIDTH = 512
POOL_GROUP_WIDTH = 128
POOL_HALO = 16
REL_BUCKETS = 32
REL_MAX_DIST = 128
N_GROUPS = 4
EXPERTS_PER_GROUP = 8
N_EXPERTS = N_GROUPS * EXPERTS_PER_GROUP
D_EXPERT = 512
N_MOD = 6
EPS = 1e-6
NEG_INF = -1e30
LANES = 128
ROUTER_LANE0 = N_GROUPS
ROUTE_FIELDS = 8
ROUTER_ROWS = 48
VMEM_LIMIT = 56 * 1024 * 1024

TM_PROJ = 1024
TM_POST = 1024
TM_DISPATCH = 2048
TE_ROWS = 512
X_RING = 4
EXPERT_TOPK = 2
TM_FINAL = 256


def _cparams(*sem):
    return pltpu.CompilerParams(dimension_semantics=sem, vmem_limit_bytes=VMEM_LIMIT)


def _dot(a, b):
    return jnp.dot(a, b, preferred_element_type=F32)


def _rms_mod(x, g, scale, shift):
    xn = x * lax.rsqrt(jnp.mean(x * x, axis=-1, keepdims=True) + EPS)
    return (xn * g) * (1.0 + scale) + shift


def _mod_kernel(c_ref, w_ref, b_ref, o_ref):
    c = c_ref[...]
    ca = c * jax.nn.sigmoid(c)
    o_ref[...] = _dot(ca.astype(BF16), w_ref[...].astype(BF16)) + b_ref[...]


def _modulation(c_pad, w_ada, b_ada):
    rows, d = c_pad.shape
    n_out = w_ada.shape[1]
    tn = 1024
    return pl.pallas_call(
        _mod_kernel,
        grid=(n_out // tn,),
        in_specs=[pl.BlockSpec((rows, d), lambda j: (0, 0)),
                  pl.BlockSpec((d, tn), lambda j: (0, j)),
                  pl.BlockSpec((1, tn), lambda j: (0, j))],
        out_specs=pl.BlockSpec((rows, tn), lambda j: (0, j)),
        out_shape=jax.ShapeDtypeStruct((rows, n_out), F32),
        compiler_params=_cparams("arbitrary"),
        name="adaln_mod",
    )(c_pad, w_ada, b_ada)


def _inproj_kernel(x_ref, mod_ref, g_ref, w_ref, qT_ref, k_ref, vT_ref, u_ref, km_ref, h_ref, *, tm):
    d = D_MODEL
    mod = mod_ref[0]
    h = _rms_mod(x_ref[...], g_ref[...], mod[:, d:2 * d], mod[:, 0:d]).astype(BF16)
    h_ref[...] = h
    proj = _dot(h, w_ref[...])
    aw = ATTN_WIDTH
    q = proj[:, 0:aw] * (HEAD_DIM ** -0.5 * LOG2E)
    k = proj[:, aw:2 * aw]
    v = proj[:, 2 * aw:3 * aw]
    u_ref[...] = proj[:, 3 * aw:3 * aw + POOL_WIDTH]
    k_ref[...] = k.astype(BF16)
    qT = q.T.astype(BF16)
    vT = v.T.astype(BF16)
    for c in range(tm // MOBA_BLOCK):
        sl = slice(c * MOBA_BLOCK, (c + 1) * MOBA_BLOCK)
        qT_ref[0, c] = qT[:, sl]
        vT_ref[0, c] = vT[:, sl]
        km_ref[0, c:c + 1, :] = jnp.mean(k[sl, :], axis=0, keepdims=True)


def _in_projection(x2, mod3, g1, w_qkvu, *, batch, seq):
    n, d = x2.shape
    tm = TM_PROJ
    tps = seq // tm
    nb = seq // MOBA_BLOCK
    cb = tm // MOBA_BLOCK
    wcols = 3 * ATTN_WIDTH + POOL_WIDTH
    assert w_qkvu.shape[1] % wcols == 0
    kern = functools.partial(_inproj_kernel, tm=tm)
    t_shape = jax.ShapeDtypeStruct((batch, nb, ATTN_WIDTH, MOBA_BLOCK), BF16)
    t_spec = pl.BlockSpec((1, cb, ATTN_WIDTH, MOBA_BLOCK), lambda i: (i // tps, i % tps, 0, 0))
    return pl.pallas_call(
        kern,
        grid=(n // tm,),
        in_specs=[pl.BlockSpec((tm, d), lambda i: (i, 0)),
                  pl.BlockSpec((1, 1, N_MOD * d), lambda i: (i // tps, 0, 0)),
                  pl.BlockSpec((1, d), lambda i: (0, 0)),
                  pl.BlockSpec((d, wcols), lambda i: (0, 0))],
        out_specs=[t_spec,
                   pl.BlockSpec((tm, ATTN_WIDTH), lambda i: (i, 0)),
                   t_spec,
                   pl.BlockSpec((tm, POOL_WIDTH), lambda i: (i, 0)),
                   pl.BlockSpec((1, cb, ATTN_WIDTH), lambda i: (i, 0, 0)),
                   pl.BlockSpec((tm, d), lambda i: (i, 0))],
        out_shape=[t_shape,
                   jax.ShapeDtypeStruct((n, ATTN_WIDTH), BF16),
                   t_shape,
                   jax.ShapeDtypeStruct((n, POOL_WIDTH), F32),
                   jax.ShapeDtypeStruct((n // tm, cb, ATTN_WIDTH), F32),
                   jax.ShapeDtypeStruct((n, d), BF16)],
        compiler_params=_cparams("arbitrary"),
        name="norm1_inproj",
    )(x2, mod3, g1, w_qkvu)


def _attn_kernel(qT_ref, k_hbm, v_hbm, km_ref, bw_ref, o_ref, xz_hbm, kaug_ref, vaug_ref, bias_ref, qnear_ref,
                 qfar_ref, sn_ref, sf0_ref, sf1_ref, k_ref, vT_ref, kv_sems, zero_ref, zero_sem, *, nb, n_steps):
    blk = MOBA_BLOCK
    hd = HEAD_DIM
    nh = HEADS_PER_STEP
    heads = range(nh)
    lane = lax.broadcasted_iota(jnp.int32, (blk, LANES), 1)

    def pair_lanes(hh):
        return slice((hh // 2) * LANES, (hh // 2 + 1) * LANES)

    def head_rows(hh):
        return slice(hh * hd, (hh + 1) * hd)

    ones = jnp.ones((ONES_ROWS, blk), BF16)

    width = nh * hd
    n_inner = pl.num_programs(1)
    step_id = pl.program_id(0) * n_inner + pl.program_id(1)

    def kv_copies(sid):
        bb, col = sid // n_inner, pl.multiple_of((sid % n_inner) * width, width)
        return (pltpu.make_async_copy(k_hbm.at[bb, :, pl.ds(col, width)], k_ref, kv_sems.at[0]),
                pltpu.make_async_copy(v_hbm.at[bb, :, pl.ds(col, width), :], vT_ref, kv_sems.at[1]))

    @pl.when(step_id == 0)
    def _():
        for cp in kv_copies(step_id):
            cp.start()

    for cp in kv_copies(step_id):
        cp.wait()

    def build_operands(j, carry):
        rows = pl.ds(pl.multiple_of(j * blk, blk), blk)
        for hh in heads:
            kp = k_ref[rows, pair_lanes(hh)]
            if hh % 2 == 0:
                kaug_ref[hh, rows, :] = jnp.where(lane < hd, kp, jnp.where(lane == hd + j, 1.0, 0.0).astype(BF16))
            else:
                kaug_ref[hh, rows, :] = jnp.where(lane >= hd, kp, jnp.where(lane == j, 1.0, 0.0).astype(BF16))
            vaug_ref[j, hh] = jnp.concatenate([vT_ref[j, head_rows(hh), :], ones], axis=0)
        return carry

    lax.fori_loop(0, nb, build_operands, 0)

    @pl.when(step_id + 1 < pl.num_programs(0) * n_inner)
    def _():
        for cp in kv_copies(step_id + 1):
            cp.start()

    @pl.when(step_id == 0)
    def _():
        zero_ref[...] = jnp.zeros_like(zero_ref)

    zrows = zero_ref.shape[0]
    zcount = xz_hbm.shape[0] // (n_steps * zrows)

    def zero_copy(q):
        return pltpu.make_async_copy(zero_ref, xz_hbm.at[pl.ds((step_id * zcount + q) * zrows, zrows)], zero_sem)

    def start_zero(q, carry):
        zero_copy(q).start()
        return carry

    def wait_zero(q, carry):
        zero_copy(q).wait()
        return carry

    lax.fori_loop(0, zcount, start_zero, 0)

    for hh in heads:
        for t in range(2):
            wide = jnp.broadcast_to(bw_ref[hh, t:t + 1, :], (blk, 2 * blk))
            bias_ref[hh, t] = pltpu.roll(wide, 0, 1, stride=1, stride_axis=0)[:, blk:2 * blk]

    rowi = lax.broadcasted_iota(jnp.int32, (MAX_BLOCKS, blk), 0)
    km_lane = lax.broadcasted_iota(jnp.int32, (MAX_BLOCKS, LANES), 1)
    zpad = jnp.zeros((LANES - hd - MAX_BLOCKS, blk), BF16)
    km_split = []
    for hh in heads:
        sub = hh % 2
        in_head = (km_lane >= sub * hd) & (km_lane < (sub + 1) * hd)
        km_h = jnp.where(in_head, km_ref[0, :, pair_lanes(hh)], 0.0)
        km_hi = km_h.astype(BF16)
        km_split.append((km_hi, (km_h - km_hi.astype(F32)).astype(BF16)))

    grp = FAR_GROUP

    def retire_begin(state, mx):
        m, acc = state
        m_new = jnp.maximum(m, mx)
        return m_new, jnp.exp2(m - m_new) * acc

    def retire_slice(src_ref, hh, m_new, acc, g, j):
        p = jnp.exp2(src_ref[hh, g * blk:(g + 1) * blk, :] - m_new)
        return acc + _dot(vaug_ref[j, hh], p.astype(BF16))

    def fresh():
        return (jnp.full((1, blk), -jnp.inf, F32), jnp.zeros((hd + ONES_ROWS, blk), F32))

    def near_blocks(i):
        return tuple(jnp.where(i - (grp - 1) + g < 0, i + 1, i - (grp - 1) + g) for g in range(grp))

    def finish(i, states):
        for hh in heads:
            _, acc = states[hh]
            o_ref[0, i, head_rows(hh), :] = (acc[0:hd] / acc[hd:hd + 1]).astype(BF16)

    def select(i):
        for hh in heads:
            qp = qT_ref[0, i, pair_lanes(hh), :]
            km_hi, km_lo = km_split[hh]
            gate = _dot(km_hi, qp) + _dot(km_lo, qp)
            g = jnp.where(rowi < i, gate, -jnp.inf)
            sel = rowi == i
            for _ in range(MOBA_TOPK):
                m = jnp.max(g, axis=0, keepdims=True)
                hit = (g == m) & (m > -jnp.inf)
                idx = jnp.min(jnp.where(hit, rowi, MAX_BLOCKS), axis=0, keepdims=True)
                pick = rowi == idx
                sel = sel | pick
                g = jnp.where(pick, -jnp.inf, g)
            near = jnp.where(sel, 0.0, NEG_INF)
            far = jnp.where(rowi > i - grp, NEG_INF, near)
            qh = qp[head_rows(hh % 2), :]
            if hh % 2 == 0:
                qnear_ref[hh] = jnp.concatenate([qh, near.astype(BF16), zpad], axis=0)
                qfar_ref[hh] = jnp.concatenate([qh, far.astype(BF16), zpad], axis=0)
            else:
                qnear_ref[hh] = jnp.concatenate([near.astype(BF16), zpad, qh], axis=0)
                qfar_ref[hh] = jnp.concatenate([far.astype(BF16), zpad, qh], axis=0)

    def park(dst_ref, q_ref, hh, g, j, bias=None):
        rows = pl.ds(pl.multiple_of(j * blk, blk), blk)
        s = _dot(kaug_ref[hh, rows, :], q_ref[hh])
        if bias is not None:
            s = s + bias
        dst_ref[hh, g * blk:(g + 1) * blk, :] = s
        return jnp.max(s, axis=0, keepdims=True)

    def park_near(i):
        idx = near_blocks(i)

        def quarter(g, hh):
            bias = None
            if g == grp - 1:
                bias = bias_ref[hh, 0]
            elif g == grp - 2:
                bias = bias_ref[hh, 1]
            return park(sn_ref, qnear_ref, hh, g, idx[g], bias)
        return quarter

    def park_far(c, dst_ref):
        return lambda g, hh: park(dst_ref, qfar_ref, hh, g, c * grp + g)

    def chunk_blocks(c):
        return tuple(c * grp + g for g in range(grp))

    def step(src_ref, states, maxes, vidx, quarter, park_first):
        begun = [retire_begin(states[hh], maxes[hh]) for hh in heads] if states is not None else None
        accs = [b[1] for b in begun] if begun else None
        new_max = [None] * nh

        def park_quarter(g):
            for hh in heads:
                cm = quarter(g, hh)
                new_max[hh] = cm if new_max[hh] is None else jnp.maximum(new_max[hh], cm)

        for g in range(grp):
            if park_first:
                park_quarter(g)
            if begun:
                for hh in heads:
                    accs[hh] = retire_slice(src_ref, hh, begun[hh][0], accs[hh], g, vidx[g])
            if not park_first and quarter is not None:
                park_quarter(g)
        new_states = tuple((begun[hh][0], accs[hh]) for hh in heads) if begun else None
        return new_states, tuple(new_max)

    def fresh_states():
        return tuple(fresh() for _ in heads)

    def early_tile(i, near_max):
        nxt = jnp.minimum(i + 1, nb - 1)
        select(nxt)
        states, next_max = step(sn_ref, fresh_states(), near_max, near_blocks(i), park_near(nxt), False)
        finish(i, states)
        return next_max

    def late_tile(i, near_max):
        sf = (sf0_ref, sf1_ref)
        n_chunks = (i - (grp - 1) + grp - 1) // grp
        nxt = jnp.minimum(i + 1, nb - 1)
        states, far_max = step(sn_ref, fresh_states(), near_max, near_blocks(i), park_far(0, sf[0]), True)

        def pair(k, st):
            states, far_max = st
            states, far_max = step(sf[0], states, far_max, chunk_blocks(2 * k), park_far(2 * k + 1, sf[1]), True)
            return step(sf[1], states, far_max, chunk_blocks(2 * k + 1), park_far(2 * k + 2, sf[0]), True)

        states, far_max = lax.fori_loop(0, (n_chunks - 1) // 2, pair, (states, far_max))

        def last(src, states, far_max):
            select(nxt)
            states, next_max = step(src, states, far_max, chunk_blocks(n_chunks - 1), park_near(nxt), True)
            finish(i, states)
            return next_max

        def even_count(st):
            states, far_max = step(sf[0], st[0], st[1], chunk_blocks(n_chunks - 2),
                                   park_far(n_chunks - 1, sf[1]), True)
            return last(sf[1], states, far_max)

        def odd_count(st):
            return last(sf[0], st[0], st[1])

        return lax.cond(n_chunks % 2 == 0, even_count, odd_count, (states, far_max))

    n_early = min(grp, nb)
    select(0)
    _, near_max = step(None, None, None, None, park_near(0), True)
    near_max = lax.fori_loop(0, n_early, early_tile, near_max)
    lax.fori_loop(n_early, nb, late_tile, near_max)
    lax.fori_loop(0, zcount, wait_zero, 0)


def _rel_bucket_table(max_rel):
    n = np.arange(max_rel)
    max_exact = REL_BUCKETS // 2
    nf = np.maximum(n, 1).astype(np.float32)
    large = max_exact + (np.log(nf / np.float32(max_exact)) / np.float32(math.log(REL_MAX_DIST / max_exact))
                         * np.float32(REL_BUCKETS - max_exact)).astype(np.int32)
    large = np.minimum(large, REL_BUCKETS - 1)
    return np.where(n < max_exact, n, large)


def _bias_vectors(rel_bias):
    blk = MOBA_BLOCK
    table = _rel_bucket_table(2 * blk)
    r = np.arange(-blk, blk)
    bt = (rel_bias.T - rel_bias[REL_BUCKETS - 1][:, None]) * LOG2E
    own = jnp.where(jnp.asarray(r >= 0)[None], bt[:, table[np.maximum(r, 0)]], NEG_INF)
    prev = bt[:, table[np.minimum(r + blk, 2 * blk - 1)]]
    return jnp.stack([own, prev], axis=1).astype(F32)


def _moba_attention(qT, k3, vT, km_pad, bias_w, *, batch, seq):
    nb = seq // MOBA_BLOCK
    blk = MOBA_BLOCK
    assert nb % FAR_GROUP == 0 and nb <= MAX_BLOCKS
    nh = HEADS_PER_STEP
    width = nh * HEAD_DIM
    assert nh % 2 == 0 and N_HEADS % nh == 0
    n_steps = batch * (N_HEADS // nh)
    sorted_rows = _expert_tiles(batch * seq) * TE_ROWS
    assert sorted_rows % (n_steps * ZERO_ROWS) == 0
    kern = functools.partial(_attn_kernel, nb=nb, n_steps=n_steps)

    hbm = pl.BlockSpec(memory_space=pl.ANY)
    return pl.pallas_call(
        kern,
        grid=(batch, N_HEADS // nh),
        in_specs=[pl.BlockSpec((1, nb, width, blk), lambda b, p: (b, 0, p, 0)),
                  hbm, hbm,
                  pl.BlockSpec((1, MAX_BLOCKS, width), lambda b, p: (b, 0, p)),
                  pl.BlockSpec((nh, 2, 2 * blk), lambda b, p: (p, 0, 0))],
        out_specs=[pl.BlockSpec((1, nb, width, blk), lambda b, p: (b, 0, p, 0)), hbm],
        out_shape=[jax.ShapeDtypeStruct((batch, nb, ATTN_WIDTH, blk), BF16),
                   jax.ShapeDtypeStruct((sorted_rows, D_MODEL), F32)],
        scratch_shapes=[pltpu.VMEM((nh, seq, LANES), BF16),
                        pltpu.VMEM((nb, nh, HEAD_DIM + ONES_ROWS, blk), BF16),
                        pltpu.VMEM((nh, 2, blk, blk), F32),
                        pltpu.VMEM((nh, LANES, blk), BF16),
                        pltpu.VMEM((nh, LANES, blk), BF16),
                        pltpu.VMEM((nh, FAR_GROUP * blk, blk), F32),
                        pltpu.VMEM((nh, FAR_GROUP * blk, blk), F32),
                        pltpu.VMEM((nh, FAR_GROUP * blk, blk), F32),
                        pltpu.VMEM((seq, width), BF16),
                        pltpu.VMEM((nb, width, blk), BF16),
                        pltpu.SemaphoreType.DMA((2,)),
                        pltpu.VMEM((ZERO_ROWS, D_MODEL), F32),
                        pltpu.SemaphoreType.DMA(())],
        compiler_params=_cparams("arbitrary", "arbitrary"),
        name="moba_attention",
    )(qT, k3, vT, km_pad, bias_w)


def _post_kernel(x_ref, mod_ref, h_ref, wz_ref, bz_ref, at_ref, u_ref, uh_ref, pw_ref, ps_ref,
                 wba_ref, wbp_ref, wo_ref, g2_ref, wrh_ref, wrl_ref, br_ref,
                 x1_ref, h2r_ref, route_ref, route_t_ref, counts_ref, carry_ref, *, tm, seq):
    d = D_MODEL
    i = pl.program_id(0)
    x = x_ref[...]
    mod = mod_ref[0]
    gate1 = mod[:, 2 * d:3 * d]
    shift2, scale2 = mod[:, 3 * d:4 * d], mod[:, 4 * d:5 * d]

    zg = 0.5 * jnp.tanh(0.5 * (_dot(h_ref[...], wz_ref[...]) + bz_ref[...])) + 0.5

    ya = jnp.concatenate(
        [lax.dot_general(at_ref[0, c], wba_ref[...], (((0,), (0,)), ((), ())),
                         preferred_element_type=F32) for c in range(tm // MOBA_BLOCK)], axis=0)

    first = (i * tm) % seq == 0
    halo = jnp.where(first, 0.0, uh_ref[...])
    ubuf = jnp.concatenate([halo, u_ref[...]], axis=0)
    rows = tm + POOL_HALO
    pos = (lax.broadcasted_iota(jnp.int32, (tm, POOL_GROUP_WIDTH), 0) + i * tm) % seq
    outs = []
    for gi, w in enumerate(POOL_WINDOWS):
        e = ubuf[:, gi * POOL_GROUP_WIDTH:(gi + 1) * POOL_GROUP_WIDTH]
        s = e
        step = 1
        while step < w:
            s = s + pltpu.roll(s, step, 0)
            step *= 2
        cnt = jnp.minimum(pos + 1, w).astype(F32)
        dlt = s[POOL_HALO:rows] / cnt - e[POOL_HALO:rows]
        outs.append(_dot(dlt.astype(BF16), pw_ref[gi]))
    pooled = jnp.concatenate(outs, axis=-1) * ps_ref[...]
    yp = _dot(pooled.astype(BF16), wbp_ref[...])

    merged = zg[:, 0:d] * ya + zg[:, d:2 * d] * yp
    x1 = x + gate1 * _dot(merged.astype(BF16), wo_ref[...])
    x1_ref[...] = x1

    h2 = _rms_mod(x1, g2_ref[...], scale2, shift2)
    h2_hi = h2.astype(BF16)
    h2r_ref[...] = h2_hi.astype(F32)
    h2_lo = (h2 - h2_hi.astype(F32)).astype(BF16)
    z = _dot(h2_hi, wrh_ref[...]) + _dot(h2_lo, wrh_ref[...]) + _dot(h2_hi, wrl_ref[...]) + br_ref[...]

    rr = ROUTER_ROWS
    zt = z.T[0:rr, :]
    row = lax.broadcasted_iota(jnp.int32, (rr, tm), 0)
    zgrp = jnp.where(row < N_GROUPS, zt, -jnp.inf)
    mg = jnp.max(zgrp, axis=0, keepdims=True)
    pg_top = 1.0 / jnp.sum(jnp.exp(zgrp - mg), axis=0, keepdims=True)
    g_idx = jnp.min(jnp.where(zgrp == mg, row, rr), axis=0, keepdims=True)
    e_row = row - ROUTER_LANE0
    in_grp = (e_row >= 0) & (e_row < N_EXPERTS) & ((e_row >> 3) == g_idx)
    ze = jnp.where(in_grp, zt, -jnp.inf)
    m1 = jnp.max(ze, axis=0, keepdims=True)
    i1 = jnp.min(jnp.where(ze == m1, row, rr), axis=0, keepdims=True)
    ze2 = jnp.where(row == i1, -jnp.inf, ze)
    m2 = jnp.max(ze2, axis=0, keepdims=True)
    i2 = jnp.min(jnp.where(ze2 == m2, row, rr), axis=0, keepdims=True)
    e2 = jnp.exp(m2 - m1)
    w1 = pg_top / (1.0 + e2)
    w2 = pg_top * e2 / (1.0 + e2)

    hot = jnp.where((row == i1) | (row == i2), 1.0, 0.0)
    tri = jnp.where(lax.broadcasted_iota(jnp.int32, (tm, tm), 0) <= lax.broadcasted_iota(jnp.int32, (tm, tm), 1),
                    1.0, 0.0).astype(BF16)
    prefix = _dot(hot.astype(BF16), tri)

    @pl.when(i == 0)
    def _():
        carry_ref[...] = jnp.zeros_like(carry_ref)

    carry = carry_ref[...]
    base = carry[:, 0:1] + prefix - 1.0
    rank1 = jnp.sum(jnp.where(row == i1, base, 0.0), axis=0, keepdims=True)
    rank2 = jnp.sum(jnp.where(row == i2, base, 0.0), axis=0, keepdims=True)
    carry_ref[...] = carry + prefix[:, tm - 1:tm]
    counts_ref[...] = carry_ref[...]
    fields = ((i1 - ROUTER_LANE0).astype(F32), (i2 - ROUTER_LANE0).astype(F32), w1, w2, rank1, rank2)
    frow = lax.broadcasted_iota(jnp.int32, (ROUTE_FIELDS, tm), 0)
    route_t = jnp.zeros((ROUTE_FIELDS, tm), F32)
    for k, val in enumerate(fields):
        route_t = jnp.where(frow == k, val, route_t)
    route_t_ref[...] = route_t
    route_ref[...] = jnp.concatenate([route_t, jnp.zeros((LANES - ROUTE_FIELDS, tm), F32)], axis=0).T


def _post_attention(x2, mod3, h1, w_z, b_z, attnT, u, pool_w, pool_scale, w_ba, w_bp, w_out, g2,
                    wr_hi, wr_lo, b_r, *, batch, seq):
    n, d = x2.shape
    tm = TM_POST
    tps = seq // tm
    cb = tm // MOBA_BLOCK
    hb = tm // POOL_HALO
    kern = functools.partial(_post_kernel, tm=tm, seq=seq)
    const2 = lambda i: (0, 0)
    return pl.pallas_call(
        kern,
        grid=(n // tm,),
        in_specs=[pl.BlockSpec((tm, d), lambda i: (i, 0)),
                  pl.BlockSpec((1, 1, N_MOD * d), lambda i: (i // tps, 0, 0)),
                  pl.BlockSpec((tm, d), lambda i: (i, 0)),
                  pl.BlockSpec((d, 2 * d), lambda i: (0, 1)),
                  pl.BlockSpec((1, 2 * d), const2),
                  pl.BlockSpec((1, cb, ATTN_WIDTH, MOBA_BLOCK), lambda i: (i // tps, i % tps, 0, 0)),
                  pl.BlockSpec((tm, POOL_WIDTH), lambda i: (i, 0)),
                  pl.BlockSpec((POOL_HALO, POOL_WIDTH), lambda i: (jnp.maximum(i * hb - 1, 0), 0)),
                  pl.BlockSpec((len(POOL_WINDOWS), POOL_GROUP_WIDTH, POOL_GROUP_WIDTH), lambda i: (0, 0, 0)),
                  pl.BlockSpec((1, POOL_WIDTH), const2),
                  pl.BlockSpec((ATTN_WIDTH, d), const2),
                  pl.BlockSpec((POOL_WIDTH, d), const2),
                  pl.BlockSpec((d, d), const2),
                  pl.BlockSpec((1, d), const2),
                  pl.BlockSpec((d, LANES), const2),
                  pl.BlockSpec((d, LANES), const2),
                  pl.BlockSpec((1, LANES), const2)],
        out_specs=[pl.BlockSpec((tm, d), lambda i: (i, 0)),
                   pl.BlockSpec((tm, d), lambda i: (i, 0)),
                   pl.BlockSpec((tm, LANES), lambda i: (i, 0)),
                   pl.BlockSpec((ROUTE_FIELDS, tm), lambda i: (0, i)),
                   pl.BlockSpec((ROUTER_ROWS, LANES), const2)],
        out_shape=[jax.ShapeDtypeStruct((n, d), F32),
                   jax.ShapeDtypeStruct((n, d), F32),
                   jax.ShapeDtypeStruct((n, LANES), F32),
                   jax.ShapeDtypeStruct((ROUTE_FIELDS, n), F32),
                   jax.ShapeDtypeStruct((ROUTER_ROWS, LANES), F32)],
        scratch_shapes=[pltpu.VMEM((ROUTER_ROWS, LANES), F32)],
        compiler_params=_cparams("arbitrary"),
        name="merge_outproj_router",
    )(x2, mod3, h1, w_z, b_z, attnT, u, u, pool_w, pool_scale, w_ba, w_bp, w_out, g2, wr_hi, wr_lo, b_r)


def _routing_plan(route_t, counts, *, n):
    te = TE_ROWS
    n_tiles = _expert_tiles(n)
    cnt = counts[ROUTER_LANE0:ROUTER_LANE0 + N_EXPERTS, 0].astype(jnp.int32)
    padded = ((cnt + te - 1) // te) * te
    ids = jnp.arange(N_EXPERTS, dtype=jnp.int32)
    ends = jnp.sum(jnp.where(ids[None, :] <= ids[:, None], padded[None, :], 0), axis=-1)
    starts = ends - padded
    pos = []
    for k in range(EXPERT_TOPK):
        expert = route_t[k].astype(jnp.int32)
        rank = route_t[4 + k].astype(jnp.int32)
        pos.append(jnp.sum(jnp.where(expert[:, None] == ids, starts, 0), axis=-1) + rank)
    tiles = jnp.arange(n_tiles, dtype=jnp.int32)
    tile_expert = jnp.sum((ends // te)[None, :] <= tiles[:, None], axis=-1)
    tile_expert = jnp.minimum(tile_expert, N_EXPERTS - 1).astype(jnp.int32)
    n_used = (ends[N_EXPERTS - 1:] // te).astype(jnp.int32)
    later = (ids[None, :] > ids[:, None]) & (cnt[None, :] > 0)
    next_expert = jnp.min(jnp.where(later, ids[None, :], N_EXPERTS), axis=-1)
    next_expert = jnp.where(next_expert == N_EXPERTS, -1, next_expert).astype(jnp.int32)
    return pos, tile_expert, n_used, next_expert


def _expert_tiles(n):
    return (EXPERT_TOPK * n) // TE_ROWS + N_EXPERTS


def _tile_pos(pos, tm):
    return jnp.concatenate([p.reshape(-1, 1, tm) for p in pos], axis=-1)


def _dispatch_kernel(pos_ref, h_ref, xz_ref, xs_ref, sem, *, tm):
    del xz_ref
    for r in range(tm):
        for k in range(EXPERT_TOPK):
            pltpu.make_async_copy(h_ref.at[pl.ds(r, 1)], xs_ref.at[pl.ds(pos_ref[0, 0, k * tm + r], 1)],
                                  sem).start(priority=(r + k) % 2)
    for k in range(EXPERT_TOPK):
        pltpu.make_async_copy(h_ref, h_ref, sem).wait()


def _dispatch(pos, h2r, xz, *, n):
    tm = TM_DISPATCH
    rows, d = xz.shape
    assert rows == _expert_tiles(n) * TE_ROWS
    pos3 = _tile_pos(pos, tm)
    return pl.pallas_call(
        functools.partial(_dispatch_kernel, tm=tm),
        grid=(n // tm,),
        in_specs=[pl.BlockSpec((1, 1, EXPERT_TOPK * tm), lambda i: (i, 0, 0), memory_space=pltpu.SMEM),
                  pl.BlockSpec((tm, d), lambda i: (i, 0)),
                  pl.BlockSpec(memory_space=pl.ANY)],
        out_specs=pl.BlockSpec(memory_space=pl.ANY),
        out_shape=jax.ShapeDtypeStruct((rows, d), F32),
        scratch_shapes=[pltpu.SemaphoreType.DMA(())],
        input_output_aliases={2: 0},
        compiler_params=_cparams("arbitrary"),
        name="moe_dispatch",
    )(pos3, h2r, xz)


def _experts_kernel(te_ref, nu_ref, nx_ref, xs_ref, wg_ref, wu_ref, wd_ref, y_ref,
                    wg_b, wu_b, wd_b, wg_f, wu_f, wd_f, sems, turn_ref, xbuf, xsems):
    t = pl.program_id(0)
    n_used = nu_ref[0]

    def fetch(e, slot):
        return [pltpu.make_async_copy(src.at[e], dst.at[slot], sems.at[slot])
                for src, dst in ((wg_ref, wg_f), (wu_ref, wu_f), (wd_ref, wd_f))]

    def rows_copy(tile, slot):
        r0 = pl.multiple_of(tile * TE_ROWS, TE_ROWS)
        return pltpu.make_async_copy(xs_ref.at[pl.ds(r0, TE_ROWS)], xbuf.at[slot], xsems.at[slot])

    @pl.when(t < n_used)
    def _():
        e = te_ref[t]
        e_prev = te_ref[jnp.maximum(t - 1, 0)]

        @pl.when(t == 0)
        def _():
            turn_ref[0] = 0
            for cp in fetch(e, 0):
                cp.start()
            for j in range(X_RING - 1):
                @pl.when(j < n_used)
                def _(j=j):
                    rows_copy(j, j).start()

        ahead = t + (X_RING - 1)

        @pl.when(ahead < n_used)
        def _():
            rows_copy(ahead, ahead % X_RING).start()

        xslot = t % X_RING
        rows_copy(t, xslot).wait()

        @pl.when((t == 0) | (e != e_prev))
        def _():
            slot = turn_ref[0] % 2
            for cp in fetch(e, slot):
                cp.wait()
            wg_b[...] = wg_f[slot].astype(BF16)
            wu_b[...] = wu_f[slot].astype(BF16)
            wd_b[...] = wd_f[slot].astype(BF16)
            e_next = nx_ref[e]

            @pl.when(e_next >= 0)
            def _():
                for cp in fetch(e_next, 1 - slot):
                    cp.start()

            turn_ref[0] = turn_ref[0] + 1

        x = xbuf[xslot].astype(BF16)
        g = _dot(x, wg_b[...])
        u = _dot(x, wu_b[...])
        act = (g * jax.nn.sigmoid(g)) * u
        y_ref[...] = _dot(act.astype(BF16), wd_b[...])

    @pl.when(t >= nu_ref[0])
    def _():
        y_ref[...] = jnp.zeros_like(y_ref)


def _experts(tile_expert, n_used, next_expert, xs, wg, wu, wd):
    rows, d = xs.shape
    te = TE_ROWS
    hbm = pl.BlockSpec(memory_space=pl.ANY)
    grid_spec = pltpu.PrefetchScalarGridSpec(
        num_scalar_prefetch=3,
        grid=(rows // te,),
        in_specs=[hbm, hbm, hbm, hbm],
        out_specs=pl.BlockSpec((te, d), lambda t, te_r, nu, nx: (t, 0)),
        scratch_shapes=[pltpu.VMEM((d, D_EXPERT), BF16), pltpu.VMEM((d, D_EXPERT), BF16),
                        pltpu.VMEM((D_EXPERT, d), BF16),
                        pltpu.VMEM((2, d, D_EXPERT), F32), pltpu.VMEM((2, d, D_EXPERT), F32),
                        pltpu.VMEM((2, D_EXPERT, d), F32),
                        pltpu.SemaphoreType.DMA((2,)), pltpu.SMEM((1,), jnp.int32),
                        pltpu.VMEM((X_RING, te, d), F32), pltpu.SemaphoreType.DMA((X_RING,))])
    return pl.pallas_call(
        _experts_kernel,
        grid_spec=grid_spec,
        out_shape=jax.ShapeDtypeStruct((rows, d), F32),
        compiler_params=_cparams("arbitrary"),
        name="moe_experts",
    )(tile_expert, n_used, next_expert, xs, wg, wu, wd)


def _final_kernel(pos_ref, pos_next_ref, x1_ref, route_ref, mod_ref, g_ref, y_ref, o_ref, ybuf, sems, *,
                  tm, final_norm):
    d = D_MODEL
    i = pl.program_id(0)
    slot = i % 2

    def gather(p_ref, s):
        for r in range(tm):
            for k in range(EXPERT_TOPK):
                pltpu.make_async_copy(y_ref.at[pl.ds(p_ref[0, 0, k * tm + r], 1)],
                                      ybuf.at[s, k, pl.ds(r, 1)], sems.at[s]).start(priority=(r + k) % 2)

    @pl.when(i == 0)
    def _():
        gather(pos_ref, 0)

    for s in range(2):
        @pl.when((i + 1 < pl.num_programs(0)) & (slot == 1 - s))
        def _(s=s):
            gather(pos_next_ref, s)

    for k in range(EXPERT_TOPK):
        pltpu.make_async_copy(ybuf.at[slot, k], ybuf.at[slot, k], sems.at[slot]).wait()

    route = route_ref[...]
    lane = lax.broadcasted_iota(jnp.int32, route.shape, 1)
    y = jnp.zeros((tm, d), F32)
    for k in range(EXPERT_TOPK):
        wk = jnp.sum(jnp.where(lane == 2 + k, route, 0.0), axis=-1, keepdims=True)
        y = y + wk * ybuf[slot, k]
    gate2 = mod_ref[0][:, 5 * d:6 * d]
    x = x1_ref[...] + gate2 * y
    if final_norm:
        x = (x * lax.rsqrt(jnp.mean(x * x, axis=-1, keepdims=True) + EPS)) * g_ref[...]
    o_ref[...] = x


def _final(pos, x1, route, mod3, gf, y, *, seq, final_norm):
    n, d = x1.shape
    tm = TM_FINAL
    tps = seq // tm
    pos3 = _tile_pos(pos, tm)
    last = n // tm - 1
    return pl.pallas_call(
        functools.partial(_final_kernel, tm=tm, final_norm=final_norm),
        grid=(n // tm,),
        in_specs=[pl.BlockSpec((1, 1, EXPERT_TOPK * tm), lambda i: (i, 0, 0), memory_space=pltpu.SMEM),
                  pl.BlockSpec((1, 1, EXPERT_TOPK * tm), lambda i: (jnp.minimum(i + 1, last), 0, 0),
                               memory_space=pltpu.SMEM),
                  pl.BlockSpec((tm, d), lambda i: (i, 0)),
                  pl.BlockSpec((tm, LANES), lambda i: (i, 0)),
                  pl.BlockSpec((1, 1, N_MOD * d), lambda i: (i // tps, 0, 0)),
                  pl.BlockSpec((1, d), lambda i: (0, 0)),
                  pl.BlockSpec(memory_space=pl.ANY)],
        out_specs=pl.BlockSpec((tm, d), lambda i: (i, 0)),
        out_shape=jax.ShapeDtypeStruct((n, d), F32),
        scratch_shapes=[pltpu.VMEM((2, EXPERT_TOPK, tm, d), F32), pltpu.SemaphoreType.DMA((2,))],
        compiler_params=_cparams("arbitrary"),
        name="moe_combine_final_norm",
    )(pos3, pos3, x1, route, mod3, gf, y)


def _split_bf16(w):
    hi = w.astype(BF16)
    return hi, (w - hi.astype(F32)).astype(BF16)


def kernel(x, c, w_ada, b_ada, norm1_g, w_in, b_gate, rel_bias, pool_w, pool_scale, w_branch_attn,
           w_branch_pool, w_out, norm2_g, w_router_group, b_router_group, w_router_expert,
           b_router_expert, w_expert_gate, w_expert_up, w_expert_down, norm_f_g):
    batch, seq, d = x.shape
    depth = w_ada.shape[0]
    n = batch * seq
    nb = seq // MOBA_BLOCK
    qkvu = 3 * ATTN_WIDTH + POOL_WIDTH
    bias_w = _bias_vectors(rel_bias)
    c_pad = jnp.pad(c, ((0, 8 - batch), (0, 0)))

    xc = x.reshape(n, d)
    for l in range(depth):
        mod3 = _modulation(c_pad, w_ada[l], b_ada[l][None])[:batch].reshape(batch, 1, N_MOD * d)
        w_in_b = w_in[l].astype(BF16)
        g1 = norm1_g[l][None]

        assert w_in_b.shape[1] == 2 * qkvu == qkvu + 2 * d
        qT, k2, vT, u, km, h1 = _in_projection(xc, mod3, g1, w_in_b, batch=batch, seq=seq)
        km_pad = jnp.pad(km.reshape(batch, nb, ATTN_WIDTH), ((0, 0), (0, MAX_BLOCKS - nb), (0, 0)))
        attnT, xz = _moba_attention(qT, k2.reshape(batch, seq, ATTN_WIDTH), vT, km_pad, bias_w,
                                    batch=batch, seq=seq)

        w_r = jnp.concatenate([w_router_group[l], w_router_expert[l]], axis=1)
        w_r = jnp.pad(w_r, ((0, 0), (0, LANES - w_r.shape[1])))
        b_r = jnp.concatenate([b_router_group[l], b_router_expert[l]])
        b_r = jnp.pad(b_r, (0, LANES - b_r.shape[0]))[None]
        wr_hi, wr_lo = _split_bf16(w_r)
        x1, h2r, route, route_t, counts = _post_attention(
            xc, mod3, h1, w_in_b, b_gate[l][None], attnT, u, pool_w[l].astype(BF16),
            pool_scale[l][None], w_branch_attn[l].astype(BF16), w_branch_pool[l].astype(BF16),
            w_out[l].astype(BF16), norm2_g[l][None], wr_hi, wr_lo, b_r, batch=batch, seq=seq)

        pos, tile_expert, n_used, next_expert = _routing_plan(route_t, counts, n=n)
        xs = _dispatch(pos, h2r, xz, n=n)
        y = _experts(tile_expert, n_used, next_expert, xs, w_expert_gate[l], w_expert_up[l],
                     w_expert_down[l])
        xc = _final(pos, x1, route, mod3, norm_f_g[None], y, seq=seq, final_norm=(l == depth - 1))
    return xc.reshape(batch, seq, d)
```
